```python
import math
import jax, jax.numpy as jnp
from jax import lax
import numpy as np

D_MODEL = 1024
BATCH = 8
SEQ = 2048
DEPTH = 4

CHUNK = 64
N_META = 16
HEAD_DIM = 64
RW_HEADS = 8
RW_WIDTH = RW_HEADS * HEAD_DIM
FX_HEADS = 8
FX_WIDTH = FX_HEADS * HEAD_DIM
DECAY_LORA = 64
AAA_LORA = 64
GATE_LORA = 160
D_FF = 2816
CONV_W = 3
Q_BLOCK = 128
NORM_EPS = 1e-6
GN_EPS = HEAD_DIM * 1e-5

RW_SIZES = (RW_WIDTH, RW_WIDTH, RW_WIDTH, DECAY_LORA, AAA_LORA, GATE_LORA)
RW_COLS = 3 * RW_WIDTH + DECAY_LORA + AAA_LORA + GATE_LORA
FX_SIZES = (FX_WIDTH, FX_WIDTH, FX_WIDTH, FX_WIDTH, FX_HEADS)
FX_COLS = 4 * FX_WIDTH + FX_HEADS
N_IN = RW_COLS + FX_COLS
MIX_WIDTH = RW_WIDTH + FX_WIDTH

kernel_name = "hybrid_rwkv7_fox_convffn_trunk"


def _rmsnorm(x, g, eps=NORM_EPS):
    xf = x.astype(jnp.float32)
    y = xf * lax.rsqrt(jnp.mean(xf * xf, axis=-1, keepdims=True) + eps)
    return (y * g.astype(jnp.float32)).astype(x.dtype)


def _split(h, sizes):
    out = []
    o = 0
    for s in sizes:
        out.append(h[..., o:o + s])
        o += s
    return out


def _heads(t, n):
    return t.reshape(t.shape[:-1] + (n, HEAD_DIM))


def _rwkv7_scan(r, w, k, v, a, b):
    B, L, H, N = r.shape

    def step(S, inp):
        r_t, w_t, k_t, v_t, a_t, b_t = inp
        sa = jnp.einsum('bhvk,bhk->bhv', S, a_t)
        S = S * w_t[:, :, None, :] + sa[..., None] * b_t[:, :, None, :] + v_t[..., None] * k_t[:, :, None, :]
        y = jnp.einsum('bhvk,bhk->bhv', S, r_t)
        return S, y

    xs = tuple(jnp.swapaxes(t, 0, 1) for t in (r, w, k, v, a, b))
    S0 = jnp.zeros((B, H, N, N), jnp.float32)
    _, y = lax.scan(step, S0, xs)
    return jnp.swapaxes(y, 0, 1)


def _rwkv7_mixer(h, mu, w0, w_up, a0, a_up, g_up, k_k, k_a, r_k, gn_w, gn_b):
    f32 = jnp.float32
    B, L, _ = h.shape
    dtype = h.dtype
    prev = jnp.pad(h, ((0, 0), (1, 0), (0, 0)))[:, :-1]
    h = h + mu.astype(dtype) * (prev - h)
    r, k, v, wd, ad, gd = _split(h, RW_SIZES)
    w_log = -jax.nn.softplus(-(w0 + jnp.tanh(wd) @ w_up).astype(f32)) - 0.5
    decay = jnp.exp(-jnp.exp(w_log))
    a = jax.nn.sigmoid((a0 + ad @ a_up).astype(f32))
    g = jax.nn.sigmoid(gd) @ g_up
    r = _heads(r.astype(f32), RW_HEADS)
    k = _heads(k.astype(f32), RW_HEADS)
    v = _heads(v.astype(f32), RW_HEADS)
    a = _heads(a, RW_HEADS)
    decay = _heads(decay, RW_HEADS)
    kk = k * _heads(k_k.astype(f32), RW_HEADS)
    kk = kk / jnp.maximum(jnp.sqrt(jnp.sum(kk * kk, axis=-1, keepdims=True)), 1e-12)
    k = k * (1.0 + (a - 1.0) * _heads(k_a.astype(f32), RW_HEADS))
    y = _rwkv7_scan(r, decay, k, v, -kk, kk * a)
    mean = jnp.mean(y, axis=-1, keepdims=True)
    var = jnp.mean(jnp.square(y - mean), axis=-1, keepdims=True)
    y = ((y - mean) * lax.rsqrt(var + GN_EPS)).reshape(B, L, RW_WIDTH)
    y = y * gn_w.astype(f32) + gn_b.astype(f32)
    bonus = jnp.sum(r * k * r_k.astype(f32), axis=-1, keepdims=True) * v
    y = y + bonus.reshape(B, L, RW_WIDTH)
    return (y * g.astype(f32)).astype(dtype)


def _fox_mixer(h, b_f, q_g, k_g):
    f32 = jnp.float32
    B, L, _ = h.shape
    dtype = h.dtype
    q, k, v, og, fl = _split(h, FX_SIZES)
    q = _rmsnorm(_heads(q, FX_HEADS), q_g).astype(f32) * (HEAD_DIM ** -0.5)
    k = _rmsnorm(_heads(k, FX_HEADS), k_g).astype(f32)
    v = _heads(v, FX_HEADS).astype(f32)
    logf = jax.nn.log_sigmoid((fl + b_f).astype(f32))
    c = jnp.cumsum(logf, axis=1)
    q = q.transpose(0, 2, 1, 3)
    k = k.transpose(0, 2, 1, 3)
    v = v.transpose(0, 2, 1, 3)
    c = c.transpose(0, 2, 1)
    n_blk = -(-L // Q_BLOCK)
    pad = n_blk * Q_BLOCK - L
    qb = jnp.pad(q, ((0, 0), (0, 0), (0, pad), (0, 0))).reshape(B, FX_HEADS, n_blk, Q_BLOCK, HEAD_DIM).transpose(2, 0, 1, 3, 4)
    cb = jnp.pad(c, ((0, 0), (0, 0), (0, pad))).reshape(B, FX_HEADS, n_blk, Q_BLOCK).transpose(2, 0, 1, 3)
    qpos = jnp.arange(n_blk * Q_BLOCK).reshape(n_blk, Q_BLOCK)
    kpos = jnp.arange(L)

    def block(args):
        q_i, c_i, pos_i = args
        s = jnp.einsum('bhqd,bhkd->bhqk', q_i, k) + c_i[..., None] - c[:, :, None, :]
        s = jnp.where(kpos[None, :] <= pos_i[:, None], s, -jnp.inf)
        p = jax.nn.softmax(s, axis=-1)
        return jnp.einsum('bhqk,bhkd->bhqd', p, v)

    o = lax.map(block, (qb, cb, qpos))
    o = o.transpose(1, 0, 3, 2, 4).reshape(B, n_blk * Q_BLOCK, FX_WIDTH)[:, :L]
    o = o * jax.nn.sigmoid(og.astype(f32))
    return o.astype(dtype)


def _conv_ffn(x, w_in, conv_w, conv_b, w_out):
    h = x @ w_in
    L = h.shape[1]
    hp = jnp.pad(h, ((0, 0), (CONV_W - 1, 0), (0, 0)))
    y = conv_b
    for i in range(CONV_W):
        y = y + conv_w[i] * hp[:, i:i + L]
    u, gt = jnp.split(y, 2, axis=-1)
    return (jax.nn.silu(gt) * u) @ w_out


def setup_inputs(seed: int = 0) -> dict:
    key = jax.random.key(seed)
    ks = jax.random.split(key, 26)
    f32 = jnp.float32
    nrm = lambda k, s: jax.random.normal(k, s, f32)
    uni = lambda k, s: jax.random.uniform(k, s, f32)
    D = D_MODEL
    conv_center = jnp.zeros((CONV_W, 1), f32).at[CONV_W - 1].set(1.0)
    return {
        "x": nrm(ks[0], (BATCH, SEQ, D)),
        "meta": nrm(ks[1], (N_META, D)),
        "norm1_g": 1.0 + 0.02 * nrm(ks[2], (DEPTH, D)),
        "w_in": nrm(ks[3], (DEPTH, D, N_IN)) * D ** -0.5,
        "rw_mu": uni(ks[4], (DEPTH, RW_COLS)),
        "rw_w0": -6.0 + 5.0 * uni(ks[5], (DEPTH, RW_WIDTH)),
        "rw_w_up": 0.5 * nrm(ks[6], (DEPTH, DECAY_LORA, RW_WIDTH)) * DECAY_LORA ** -0.5,
        "rw_a0": 0.02 * nrm(ks[7], (DEPTH, RW_WIDTH)),
        "rw_a_up": nrm(ks[8], (DEPTH, AAA_LORA, RW_WIDTH)) * AAA_LORA ** -0.5,
        "rw_g_up": nrm(ks[9], (DEPTH, GATE_LORA, RW_WIDTH)) * GATE_LORA ** -0.5,
        "rw_k_k": 0.85 + 0.02 * nrm(ks[10], (DEPTH, RW_WIDTH)),
        "rw_k_a": 1.0 + 0.02 * nrm(ks[11], (DEPTH, RW_WIDTH)),
        "rw_r_k": 0.1 * nrm(ks[12], (DEPTH, RW_HEADS, HEAD_DIM)),
        "rw_gn_w": 1.0 + 0.02 * nrm(ks[13], (DEPTH, RW_WIDTH)),
        "rw_gn_b": 0.02 * nrm(ks[14], (DEPTH, RW_WIDTH)),
        "fx_b_f": 1.0 + 3.0 * uni(ks[15], (DEPTH, FX_HEADS)),
        "fx_q_g": 1.0 + 0.02 * nrm(ks[16], (DEPTH, HEAD_DIM)),
        "fx_k_g": 1.0 + 0.02 * nrm(ks[17], (DEPTH, HEAD_DIM)),
        "w_o": nrm(ks[18], (DEPTH, MIX_WIDTH, D)) * MIX_WIDTH ** -0.5,
        "norm2_g": 1.0 + 0.02 * nrm(ks[19], (DEPTH, D)),
        "ffn_w_in": nrm(ks[20], (DEPTH, D, 2 * D_FF)) * D ** -0.5,
        "ffn_conv_w": 0.2 * nrm(ks[21], (DEPTH, CONV_W, 2 * D_FF)) + conv_center[None],
        "ffn_conv_b": 0.02 * nrm(ks[22], (DEPTH, 2 * D_FF)),
        "ffn_w_out": nrm(ks[23], (DEPTH, D_FF, D)) * D_FF ** -0.5,
    }


def reference(x, meta, norm1_g, w_in, rw_mu, rw_w0, rw_w_up, rw_a0, rw_a_up, rw_g_up,
              rw_k_k, rw_k_a, rw_r_k, rw_gn_w, rw_gn_b, fx_b_f, fx_q_g, fx_k_g, w_o,
              norm2_g, ffn_w_in, ffn_conv_w, ffn_conv_b, ffn_w_out):
    B = x.shape[0]
    meta_b = jnp.broadcast_to(meta[None].astype(x.dtype), (B, N_META, D_MODEL))
    h = jnp.concatenate([meta_b, x], axis=1)
    for l in range(DEPTH):
        p = _rmsnorm(h, norm1_g[l]) @ w_in[l]
        y_rw = _rwkv7_mixer(p[..., :RW_COLS], rw_mu[l], rw_w0[l], rw_w_up[l], rw_a0[l], rw_a_up[l],
                            rw_g_up[l], rw_k_k[l], rw_k_a[l], rw_r_k[l], rw_gn_w[l], rw_gn_b[l])
        y_fx = _fox_mixer(p[..., RW_COLS:], fx_b_f[l], fx_q_g[l], fx_k_g[l])
        h = h + jnp.concatenate([y_rw, y_fx], axis=-1) @ w_o[l]
        h = h + _conv_ffn(_rmsnorm(h, norm2_g[l]), ffn_w_in[l], ffn_conv_w[l], ffn_conv_b[l], ffn_w_out[l])
    return h[:, N_META:]
```

```python
import functools

import jax
import jax.numpy as jnp
from jax import lax
from jax.experimental import pallas as pl
from jax.experimental.pallas import tpu as pltpu

F32 = jnp.float32
BF16 = jnp.bfloat16

N_META = 16
HEAD_DIM = 64
N_HEADS = 8
WIDTH = N_HEADS * HEAD_DIM
DECAY_LORA = 64
AAA_LORA = 64
GATE_LORA = 160
GATE_PAD = 256
RW_PCOLS = 3 * WIDTH + 128 + GATE_PAD
FX_PCOLS = 4 * WIDTH + 128
CONV_W = 3
NORM_EPS = 1e-6
GN_EPS = HEAD_DIM * 1e-5
LANES = 128
MASK_VALUE = -1e30
VMEM_LIMIT = 56 * 1024 * 1024


def _pick(n, cands):
    for c in cands:
        if n % c == 0:
            return c
    raise ValueError(f"no tile for {n} in {cands}")


def _cparams(sem):
    return pltpu.CompilerParams(dimension_semantics=sem, vmem_limit_bytes=VMEM_LIMIT)


def _sigmoid(x):
    return 1.0 / (1.0 + jnp.exp(-x))


def _softplus(x):
    return jnp.maximum(x, 0.0) + jnp.log(1.0 + jnp.exp(-jnp.abs(x)))


def _split2(x):
    hi = x.astype(BF16)
    lo = (x - hi.astype(F32)).astype(BF16)
    return hi, lo


def _dot_ones(x, ones_bf16):
    hi, lo = _split2(x)
    return (jnp.dot(hi, ones_bf16, preferred_element_type=F32)
            + jnp.dot(lo, ones_bf16, preferred_element_type=F32))


def _dot_f32(a, b):
    a_hi, a_lo = _split2(a)
    b_hi, b_lo = _split2(b)
    return (jnp.dot(a_hi, b_hi, preferred_element_type=F32)
            + jnp.dot(a_lo, b_hi, preferred_element_type=F32)
            + jnp.dot(a_hi, b_lo, preferred_element_type=F32))


def _group_ones(n, group):
    i = jnp.arange(n) // group
    return (i[:, None] == i[None, :]).astype(BF16)


def _norm_matmul_body(x_ref, g_ref, w_ref, o_ref, *, n_chunk):
    x = x_ref[...]
    ms = jnp.mean(x * x, axis=-1, keepdims=True)
    xn = (x * lax.rsqrt(ms + NORM_EPS) * g_ref[...]).astype(BF16)
    n = o_ref.shape[-1]
    for c in range(0, n, n_chunk):
        e = min(c + n_chunk, n)
        o_ref[:, c:e] = jnp.dot(xn, w_ref[:, c:e], preferred_element_type=F32)


def _norm_matmul(x2, g, w_bf16, tm):
    m, d = x2.shape
    n = w_bf16.shape[1]
    return pl.pallas_call(
        functools.partial(_norm_matmul_body, n_chunk=512),
        out_shape=jax.ShapeDtypeStruct((m, n), F32),
        grid=(m // tm,),
        in_specs=[pl.BlockSpec((tm, d), lambda i: (i, 0)),
                  pl.BlockSpec((1, d), lambda i: (0, 0)),
                  pl.BlockSpec((d, n), lambda i: (0, 0))],
        out_specs=pl.BlockSpec((tm, n), lambda i: (i, 0)),
        compiler_params=_cparams(("arbitrary",)),
        name="norm_matmul",
    )(x2, g.reshape(1, d), w_bf16)


def _rwkv_prep_body(p_ref, mu_ref, wcomb_ref, w0a0_ref, gup_ref, kk_ref, ka_ref, rk_ref, ones_ref,
                    r_o, w_o, k_o, v_o, a_o, b_o, g_o, bonus_o, carry_ref):
    t2 = p_ref.shape[0]

    @pl.when(pl.program_id(1) == 0)
    def _():
        carry_ref[...] = jnp.zeros_like(carry_ref)

    p = p_ref[...]
    rows = lax.broadcasted_iota(jnp.int32, p.shape, 0)
    prev = jnp.where(rows == 0, carry_ref[...], pltpu.roll(p, 1, axis=0))
    carry_ref[...] = p[t2 - 1:t2, :]
    x = p + mu_ref[...] * (prev - p)

    r = x[:, 0:WIDTH]
    k = x[:, WIDTH:2 * WIDTH]
    v = x[:, 2 * WIDTH:3 * WIDTH]
    wa = x[:, 3 * WIDTH:3 * WIDTH + 128]
    gd = x[:, 3 * WIDTH + 128:]

    lane = lax.broadcasted_iota(jnp.int32, wa.shape, 1)
    xa = jnp.where(lane < DECAY_LORA, jnp.tanh(wa), wa)
    pre = _dot_f32(xa, wcomb_ref[...]) + w0a0_ref[...]
    w_log = -_softplus(-pre[:, :WIDTH]) - 0.5
    decay = jnp.exp(-jnp.exp(w_log))
    a = _sigmoid(pre[:, WIDTH:])
    g = _dot_f32(_sigmoid(gd), gup_ref[...])

    ones = ones_ref[...]
    kk = k * kk_ref[...]
    ss = _dot_ones(kk * kk, ones)
    kk = kk / jnp.maximum(jnp.sqrt(ss), 1e-12)
    k2 = k * (1.0 + (a - 1.0) * ka_ref[...])
    bonus = _dot_ones(r * k2 * rk_ref[...], ones) * v

    r_o[...] = r
    w_o[...] = decay
    k_o[...] = k2
    v_o[...] = v
    a_o[...] = -kk
    b_o[...] = kk * a
    g_o[...] = g
    bonus_o[...] = bonus


def _rwkv_prep(p_rw, mu_p, wcomb, w0a0, gup_p, k_k, k_a, r_k, ones512, t2):
    b, l, pc = p_rw.shape
    full = lambda shape: pl.BlockSpec(shape, lambda bi, ti: (0,) * len(shape))
    out_sd = jax.ShapeDtypeStruct((l, b * WIDTH), F32)
    out_spec = pl.BlockSpec((t2, WIDTH), lambda bi, ti: (ti, bi))
    return pl.pallas_call(
        _rwkv_prep_body,
        out_shape=[out_sd] * 8,
        grid=(b, l // t2),
        in_specs=[pl.BlockSpec((None, t2, pc), lambda bi, ti: (bi, ti, 0)),
                  full((1, pc)), full(wcomb.shape), full((1, 2 * WIDTH)), full(gup_p.shape),
                  full((1, WIDTH)), full((1, WIDTH)), full((1, WIDTH)), full((WIDTH, WIDTH))],
        out_specs=[out_spec] * 8,
        scratch_shapes=[pltpu.VMEM((1, pc), F32)],
        compiler_params=_cparams(("arbitrary", "arbitrary")),
        name="rwkv_prep",
    )(p_rw, mu_p, wcomb, w0a0, gup_p, k_k, k_a, r_k, ones512)


def _rwkv_scan_body(r_ref, w_ref, k_ref, v_ref, a_ref, b_ref, sel_ref, ones_ref, y_ref, s_ref, *, sub):
    t_len, n_pairs, _ = r_ref.shape

    @pl.when(pl.program_id(0) == 0)
    def _():
        s_ref[...] = jnp.zeros_like(s_ref)

    sel = sel_ref[...]
    ones = ones_ref[...]

    def step(t, carry):
        for p0 in range(0, n_pairs, sub):
            ps = slice(p0, p0 + sub)
            bc = lambda ref: ref[t, ps, :][:, None, :]
            s = s_ref[ps]
            sa = _dot_ones((s * bc(a_ref)).reshape(sub * HEAD_DIM, LANES), ones)
            vcol = _dot_ones((bc(v_ref) * sel[None]).reshape(sub * HEAD_DIM, LANES), ones)
            s = (s * bc(w_ref)
                 + sa.reshape(sub, HEAD_DIM, LANES) * bc(b_ref)
                 + vcol.reshape(sub, HEAD_DIM, LANES) * bc(k_ref))
            s_ref[ps] = s
            q = (s * bc(r_ref)).reshape(sub * HEAD_DIM, LANES)
            ycol = _dot_ones(q, ones).reshape(sub, HEAD_DIM, LANES)
            y_ref[t, ps, :] = jnp.sum(ycol * sel[None], axis=1)
        return carry

    lax.fori_loop(0, t_len, step, 0)


def _rwkv_scan(r, w, k, v, a, b, t_len, sub):
    l, n_pairs, _ = r.shape
    lane = jnp.arange(LANES)
    sel = (lane[None, :] % HEAD_DIM == jnp.arange(HEAD_DIM)[:, None]).astype(F32)
    ones = _group_ones(LANES, HEAD_DIM)
    blk = pl.BlockSpec((t_len, n_pairs, LANES), lambda ti: (ti, 0, 0))
    return pl.pallas_call(
        functools.partial(_rwkv_scan_body, sub=sub),
        out_shape=jax.ShapeDtypeStruct((l, n_pairs, LANES), F32),
        grid=(l // t_len,),
        in_specs=[blk] * 6 + [pl.BlockSpec((HEAD_DIM, LANES), lambda ti: (0, 0)),
                              pl.BlockSpec((LANES, LANES), lambda ti: (0, 0))],
        out_specs=blk,
        scratch_shapes=[pltpu.VMEM((n_pairs, HEAD_DIM, LANES), F32)],
        compiler_params=_cparams(("arbitrary",)),
        name="rwkv_scan",
    )(r, w, k, v, a, b, sel, ones)


def _fox_prep_body(qk_ref, fl_ref, gain_ref, bf_ref, ones_ref, qk_o, c_o, carry_ref):
    t2 = qk_ref.shape[0]

    @pl.when(pl.program_id(1) == 0)
    def _():
        carry_ref[...] = jnp.zeros_like(carry_ref)

    ones = ones_ref[...]
    for h0 in (0, WIDTH):
        x = qk_ref[:, h0:h0 + WIDTH]
        ms = _dot_ones(x * x, ones) * (1.0 / HEAD_DIM)
        qk_o[:, h0:h0 + WIDTH] = (x * lax.rsqrt(ms + NORM_EPS) * gain_ref[:, h0:h0 + WIDTH]).astype(qk_o.dtype)

    logf = -_softplus(-(fl_ref[...] + bf_ref[...]))
    ri = lax.broadcasted_iota(jnp.int32, (t2, t2), 0)
    ci = lax.broadcasted_iota(jnp.int32, (t2, t2), 1)
    tri = (ci <= ri).astype(BF16)
    f1 = logf.astype(BF16)
    rem = logf - f1.astype(F32)
    f2 = rem.astype(BF16)
    f3 = (rem - f2.astype(F32)).astype(BF16)
    c = (jnp.dot(tri, f1, preferred_element_type=F32)
         + jnp.dot(tri, f2, preferred_element_type=F32)
         + jnp.dot(tri, f3, preferred_element_type=F32)) + carry_ref[...]
    c_o[...] = c
    carry_ref[...] = c[t2 - 1:t2, :]


def _fox_prep(p_fx, gain, bf_p, ones512, t2):
    b, l, _ = p_fx.shape
    full = lambda shape: pl.BlockSpec(shape, lambda bi, ti: (0,) * len(shape))
    return pl.pallas_call(
        _fox_prep_body,
        out_shape=[jax.ShapeDtypeStruct((b, l, 2 * WIDTH), BF16),
                   jax.ShapeDtypeStruct((b, l, LANES), F32)],
        grid=(b, l // t2),
        in_specs=[pl.BlockSpec((None, t2, 2 * WIDTH), lambda bi, ti: (bi, ti, 0)),
                  pl.BlockSpec((None, t2, LANES), lambda bi, ti: (bi, ti, 4 * WIDTH // LANES)),
                  full((1, 2 * WIDTH)), full((1, LANES)), full((WIDTH, WIDTH))],
        out_specs=[pl.BlockSpec((None, t2, 2 * WIDTH), lambda bi, ti: (bi, ti, 0)),
                   pl.BlockSpec((None, t2, LANES), lambda bi, ti: (bi, ti, 0))],
        scratch_shapes=[pltpu.VMEM((1, LANES), F32)],
        compiler_params=_cparams(("arbitrary", "arbitrary")),
        name="fox_prep",
    )(p_fx, p_fx, gain, bf_p, ones512)


def _fox_attn_body(q_ref, k_ref, v_ref, cc_ref, cr_ref, o_ref, m_ref, l_ref, acc_ref):
    n_heads, tq, _ = q_ref.shape
    tk = k_ref.shape[1]
    qi = pl.program_id(1)
    kj = pl.program_id(2)

    @pl.when(kj == 0)
    def _():
        m_ref[...] = jnp.full_like(m_ref, MASK_VALUE)
        l_ref[...] = jnp.zeros_like(l_ref)
        acc_ref[...] = jnp.zeros_like(acc_ref)

    @pl.when(kj <= qi)
    def _():
        rowpos = qi * tq + lax.broadcasted_iota(jnp.int32, (tq, tk), 0)
        colpos = kj * tk + lax.broadcasted_iota(jnp.int32, (tq, tk), 1)
        causal = colpos <= rowpos
        for h in range(n_heads):
            s = lax.dot_general(q_ref[h], k_ref[h], (((1,), (1,)), ((), ())),
                                preferred_element_type=F32)
            s = s + (cc_ref[h] - cr_ref[h])
            s = jnp.where(causal, s, MASK_VALUE)
            m_prev = m_ref[h]
            m_new = jnp.maximum(m_prev, jnp.max(s, axis=1, keepdims=True))
            alpha = jnp.exp(m_prev - m_new)
            p = jnp.exp(s - m_new)
            l_ref[h] = alpha * l_ref[h] + jnp.sum(p, axis=1, keepdims=True)
            acc_ref[h] = alpha * acc_ref[h] + jnp.dot(p.astype(BF16), v_ref[h],
                                                      preferred_element_type=F32)
            m_ref[h] = m_new

    @pl.when(kj == qi)
    def _():
        o_ref[...] = acc_ref[...] / l_ref[...]


def _fox_attn(q, k, v, c_col, c_row, tq):
    b, h, lp, d = q.shape
    nq = lp // tq
    kv_map = lambda bi, qi, kj: (bi, 0, jnp.minimum(kj, qi), 0)
    return pl.pallas_call(
        _fox_attn_body,
        out_shape=jax.ShapeDtypeStruct((b, h, lp, d), F32),
        grid=(b, nq, nq),
        in_specs=[pl.BlockSpec((None, h, tq, d), lambda bi, qi, kj: (bi, 0, qi, 0)),
                  pl.BlockSpec((None, h, tq, d), kv_map),
                  pl.BlockSpec((None, h, tq, d), kv_map),
                  pl.BlockSpec((None, h, tq, 1), lambda bi, qi, kj: (bi, 0, qi, 0)),
                  pl.BlockSpec((None, h, 1, tq), lambda bi, qi, kj: (bi, 0, 0, jnp.minimum(kj, qi)))],
        out_specs=pl.BlockSpec((None, h, tq, d), lambda bi, qi, kj: (bi, 0, qi, 0)),
        scratch_shapes=[pltpu.VMEM((h, tq, 1), F32), pltpu.VMEM((h, tq, 1), F32),
                        pltpu.VMEM((h, tq, d), F32)],
        compiler_params=_cparams(("arbitrary", "arbitrary", "arbitrary")),
        name="fox_attn",
    )(q, k, v, c_col, c_row)


def _mix_out_body(h_ref, y_ref, bonus_ref, g_ref, o_ref, og_ref, gnw_ref, gnb_ref, ones_ref, wo_ref, out_ref):
    ones = ones_ref[...]
    y = y_ref[...]
    mean = _dot_ones(y, ones) * (1.0 / HEAD_DIM)
    d = y - mean
    var = _dot_ones(d * d, ones) * (1.0 / HEAD_DIM)
    yn = d * lax.rsqrt(var + GN_EPS) * gnw_ref[...] + gnb_ref[...]
    y_rw = ((yn + bonus_ref[...]) * g_ref[...]).astype(BF16)
    y_fx = (o_ref[...] * _sigmoid(og_ref[...])).astype(BF16)
    out_ref[...] = (h_ref[...]
                    + jnp.dot(y_rw, wo_ref[0:WIDTH, :], preferred_element_type=F32)
                    + jnp.dot(y_fx, wo_ref[WIDTH:2 * WIDTH, :], preferred_element_type=F32))


def _mix_out(h, y, bonus, g, o_fx, p_fx, gn_w, gn_b, ones512, wo_bf16, tm):
    b, l, d = h.shape
    full = lambda shape: pl.BlockSpec(shape, lambda bi, ti: (0,) * len(shape))
    tmaj = pl.BlockSpec((tm, WIDTH), lambda bi, ti: (ti, bi))
    return pl.pallas_call(
        _mix_out_body,
        out_shape=jax.ShapeDtypeStruct((b, l, d), F32),
        grid=(b, l // tm),
        in_specs=[pl.BlockSpec((None, tm, d), lambda bi, ti: (bi, ti, 0)),
                  tmaj, tmaj, tmaj,
                  pl.BlockSpec((None, tm, WIDTH), lambda bi, ti: (bi, ti, 0)),
                  pl.BlockSpec((None, tm, WIDTH), lambda bi, ti: (bi, ti, 3)),
                  full((1, WIDTH)), full((1, WIDTH)), full((WIDTH, WIDTH)), full((2 * WIDTH, d))],
        out_specs=pl.BlockSpec((None, tm, d), lambda bi, ti: (bi, ti, 0)),
        compiler_params=_cparams(("arbitrary", "arbitrary")),
        name="mix_out",
    )(h, y, bonus, g, o_fx, p_fx, gn_w, gn_b, ones512, wo_bf16)


def _conv_ffn_body(h_ref, g_ref, wu_ref, wg_ref, cwu_ref, cwg_ref, cbu_ref, cbg_ref, wout_ref,
                   out_ref, xn_ref, cu_ref, cg_ref):
    tm = h_ref.shape[0]
    ti = pl.program_id(1)
    j = pl.program_id(2)

    @pl.when(j == 0)
    def _():
        x = h_ref[...]
        ms = jnp.mean(x * x, axis=-1, keepdims=True)
        xn_ref[...] = (x * lax.rsqrt(ms + NORM_EPS) * g_ref[...]).astype(BF16)
        out_ref[...] = x

    @pl.when(ti == 0)
    def _():
        cu_ref[j] = jnp.zeros(cu_ref.shape[1:], F32)
        cg_ref[j] = jnp.zeros(cg_ref.shape[1:], F32)

    xn = xn_ref[...]

    def conv(w_ref, cw_ref, cb_ref, carry_ref):
        hcur = jnp.dot(xn, w_ref[...], preferred_element_type=F32)
        rows = lax.broadcasted_iota(jnp.int32, hcur.shape, 0)
        carry = carry_ref[j]
        h1 = jnp.where(rows == 0, carry[7:8, :], pltpu.roll(hcur, 1, axis=0))
        h2 = pltpu.roll(hcur, 2, axis=0)
        h2 = jnp.where(rows == 0, carry[6:7, :], jnp.where(rows == 1, carry[7:8, :], h2))
        carry_ref[j] = hcur[tm - 8:tm, :]
        cw = cw_ref[...]
        return cb_ref[...] + cw[0:1, :] * h2 + cw[1:2, :] * h1 + cw[2:3, :] * hcur

    u = conv(wu_ref, cwu_ref, cbu_ref, cu_ref)
    gt = conv(wg_ref, cwg_ref, cbg_ref, cg_ref)
    act = (gt * _sigmoid(gt) * u).astype(BF16)
    out_ref[...] += jnp.dot(act, wout_ref[...], preferred_element_type=F32)


def _conv_ffn(h, g, w_in_bf16, conv_w, conv_b, w_out_bf16, tm, tf):
    b, l, d = h.shape
    d_ff = w_out_bf16.shape[0]
    nj = d_ff // tf
    cw = jnp.pad(conv_w, ((0, 8 - CONV_W), (0, 0)))
    cb = conv_b.reshape(1, 2 * d_ff)
    return pl.pallas_call(
        _conv_ffn_body,
        out_shape=jax.ShapeDtypeStruct((b, l, d), F32),
        grid=(b, l // tm, nj),
        in_specs=[pl.BlockSpec((None, tm, d), lambda bi, ti, j: (bi, ti, 0)),
                  pl.BlockSpec((1, d), lambda bi, ti, j: (0, 0)),
                  pl.BlockSpec((d, tf), lambda bi, ti, j: (0, j)),
                  pl.BlockSpec((d, tf), lambda bi, ti, j: (0, nj + j)),
                  pl.BlockSpec((8, tf), lambda bi, ti, j: (0, j)),
                  pl.BlockSpec((8, tf), lambda bi, ti, j: (0, nj + j)),
                  pl.BlockSpec((1, tf), lambda bi, ti, j: (0, j)),
                  pl.BlockSpec((1, tf), lambda bi, ti, j: (0, nj + j)),
                  pl.BlockSpec((tf, d), lambda bi, ti, j: (j, 0))],
        out_specs=pl.BlockSpec((None, tm, d), lambda bi, ti, j: (bi, ti, 0)),
        scratch_shapes=[pltpu.VMEM((tm, d), BF16),
                        pltpu.VMEM((nj, 8, tf), F32), pltpu.VMEM((nj, 8, tf), F32)],
        compiler_params=_cparams(("arbitrary", "arbitrary", "arbitrary")),
        name="conv_ffn",
    )(h, g.reshape(1, d), w_in_bf16, w_in_bf16, cw, cw, cb, cb, w_out_bf16)


def _pad_cols(w, n):
    return jnp.pad(w, ((0, 0), (0, n - w.shape[1])))


def _layer(h, prm):
    (norm1_g, w_in, rw_mu, rw_w0, rw_w_up, rw_a0, rw_a_up, rw_g_up, rw_k_k, rw_k_a, rw_r_k,
     rw_gn_w, rw_gn_b, fx_b_f, fx_q_g, fx_k_g, w_o, norm2_g, ffn_w_in, ffn_conv_w, ffn_conv_b,
     ffn_w_out) = prm
    b, l, d = h.shape
    m = b * l
    rw_cols = 3 * WIDTH + DECAY_LORA + AAA_LORA + GATE_LORA
    lora0 = 3 * WIDTH

    w_rw = _pad_cols(w_in[:, :rw_cols], RW_PCOLS).astype(BF16)
    w_fx = _pad_cols(w_in[:, rw_cols:], FX_PCOLS).astype(BF16)
    mu_p = _pad_cols(rw_mu.reshape(1, rw_cols), RW_PCOLS)
    wcomb = jnp.zeros((128, 2 * WIDTH), F32)
    wcomb = wcomb.at[:DECAY_LORA, :WIDTH].set(rw_w_up).at[DECAY_LORA:, WIDTH:].set(rw_a_up)
    w0a0 = jnp.concatenate([rw_w0, rw_a0]).reshape(1, 2 * WIDTH)
    gup_p = jnp.pad(rw_g_up, ((0, GATE_PAD - GATE_LORA), (0, 0)))
    ones512 = _group_ones(WIDTH, HEAD_DIM)
    gain = jnp.concatenate([jnp.tile(fx_q_g, N_HEADS) * (HEAD_DIM ** -0.5),
                            jnp.tile(fx_k_g, N_HEADS)]).reshape(1, 2 * WIDTH)
    bf_p = jnp.pad(fx_b_f, (0, LANES - N_HEADS)).reshape(1, LANES)
    del lora0

    tm = _pick(m, (688, 344, 144, 72, 48, 16, 8))
    t2 = _pick(l, (344, 144, 72, 48, 16, 8))
    t_scan = _pick(l, (48, 24, 16, 8))

    h2 = h.reshape(m, d)
    p_rw = _norm_matmul(h2, norm1_g, w_rw, tm).reshape(b, l, RW_PCOLS)
    p_fx = _norm_matmul(h2, norm1_g, w_fx, tm).reshape(b, l, FX_PCOLS)

    r, w, k, v, a, bb, g, bonus = _rwkv_prep(
        p_rw, mu_p, wcomb, w0a0, gup_p, rw_k_k.reshape(1, WIDTH), rw_k_a.reshape(1, WIDTH),
        rw_r_k.reshape(1, WIDTH), ones512, t2)
    n_pairs = b * WIDTH // LANES
    tm3 = lambda z: z.reshape(l, n_pairs, LANES)
    y = _rwkv_scan(tm3(r), tm3(w), tm3(k), tm3(v), tm3(a), tm3(bb), t_scan, sub=min(4, n_pairs))
    y = y.reshape(l, b * WIDTH)

    qk, c = _fox_prep(p_fx, gain, bf_p, ones512, t2)
    tq = 256 if l > 256 else 128
    lp = -(-l // tq) * tq
    def heads(z):
        z = z.reshape(b, l, N_HEADS, HEAD_DIM).transpose(0, 2, 1, 3)
        return jnp.pad(z, ((0, 0), (0, 0), (0, lp - l), (0, 0)))
    qh = heads(qk[..., :WIDTH])
    kh = heads(qk[..., WIDTH:])
    vh = heads(p_fx[..., 2 * WIDTH:3 * WIDTH].astype(BF16))
    ch = jnp.pad(c[..., :N_HEADS].transpose(0, 2, 1), ((0, 0), (0, 0), (0, lp - l)))
    o = _fox_attn(qh, kh, vh, ch[..., None], ch[:, :, None, :], tq)
    o_fx = o[:, :, :l].transpose(0, 2, 1, 3).reshape(b, l, WIDTH)

    h = _mix_out(h, y, bonus, g, o_fx, p_fx, rw_gn_w.reshape(1, WIDTH), rw_gn_b.reshape(1, WIDTH),
                 ones512, w_o.astype(BF16), t2)

    d_ff = ffn_w_out.shape[0]
    tf = _pick(d_ff, (1408, 256, 128))
    return _conv_ffn(h, norm2_g, ffn_w_in.astype(BF16), ffn_conv_w, ffn_conv_b,
                     ffn_w_out.astype(BF16), t2, tf)


def kernel(x, meta, norm1_g, w_in, rw_mu, rw_w0, rw_w_up, rw_a0, rw_a_up, rw_g_up, rw_k_k, rw_k_a,
           rw_r_k, rw_gn_w, rw_gn_b, fx_b_f, fx_q_g, fx_k_g, w_o, norm2_g, ffn_w_in, ffn_conv_w,
           ffn_conv_b, ffn_w_out):
    b = x.shape[0]
    params = (norm1_g, w_in, rw_mu, rw_w0, rw_w_up, rw_a0, rw_a_up, rw_g_up, rw_k_k, rw_k_a, rw_r_k,
              rw_gn_w, rw_gn_b, fx_b_f, fx_q_g, fx_k_g, w_o, norm2_g, ffn_w_in, ffn_conv_w,
              ffn_conv_b, ffn_w_out)
    meta_b = jnp.broadcast_to(meta[None].astype(x.dtype), (b, N_META, x.shape[-1]))
    h = jnp.concatenate([meta_b, x], axis=1)
    for layer in range(norm1_g.shape[0]):
        h = _layer(h, tuple(p[layer] for p in params))
    return h[:, N_META:]
```

```python
import functools

import jax
import jax.numpy as jnp
from jax import lax
from jax.experimental import pallas as pl
from jax.experimental.pallas import tpu as pltpu

F32 = jnp.float32
BF16 = jnp.bfloat16

N_META = 16
HEAD_DIM = 64
N_HEADS = 8
WIDTH = N_HEADS * HEAD_DIM
DECAY_LORA = 64
AAA_LORA = 64
GATE_LORA = 160
GATE_PAD = 256
RW_PCOLS = 3 * WIDTH + 128 + GATE_PAD
FX_PCOLS = 4 * WIDTH + 128
CONV_W = 3
NORM_EPS = 1e-6
GN_EPS = HEAD_DIM * 1e-5
LANES = 128
PAIRS = WIDTH // LANES
CHUNK = 16
SB = 128
MASK_VALUE = -1e30
VMEM_LIMIT = 56 * 1024 * 1024


def _pick(n, cands):
    for c in cands:
        if n % c == 0:
            return c
    raise ValueError(f"no tile for {n} in {cands}")


def _cparams(sem):
    return pltpu.CompilerParams(dimension_semantics=sem, vmem_limit_bytes=VMEM_LIMIT)


def _sigmoid(x):
    return 1.0 / (1.0 + jnp.exp(-x))


def _softplus(x):
    return jnp.maximum(x, 0.0) + jnp.log(1.0 + jnp.exp(-jnp.abs(x)))


def _split2(x):
    hi = x.astype(BF16)
    lo = (x - hi.astype(F32)).astype(BF16)
    return hi, lo


def _split3(x):
    p1 = x.astype(BF16)
    rem = x - p1.astype(F32)
    p2 = rem.astype(BF16)
    p3 = (rem - p2.astype(F32)).astype(BF16)
    return p1, p2, p3


def _dot_ones(x, ones_bf16):
    hi, lo = _split2(x)
    return (jnp.dot(hi, ones_bf16, preferred_element_type=F32)
            + jnp.dot(lo, ones_bf16, preferred_element_type=F32))


def _mm3(a_hi, a_lo, b_hi, b_lo, dims=((1,), (0,))):
    dn = (dims, ((), ()))
    m_axis = 1 - dims[0][0]
    m = a_hi.shape[m_axis]
    d = lax.dot_general(jnp.concatenate([a_hi, a_lo], axis=m_axis), b_hi, dn,
                        preferred_element_type=F32)
    return d[:m] + d[m:] + lax.dot_general(a_hi, b_lo, dn, preferred_element_type=F32)


def _dot_f32(a, b):
    return _mm3(*_split2(a), *_split2(b))


def _group_ones(n, group):
    i = jnp.arange(n) // group
    return (i[:, None] == i[None, :]).astype(BF16)


def _norm_matmul_body(x_ref, g_ref, w_ref, o_ref, *, n_chunk):
    x = x_ref[...]
    ms = jnp.mean(x * x, axis=-1, keepdims=True)
    xn = (x * lax.rsqrt(ms + NORM_EPS) * g_ref[...]).astype(BF16)
    n = o_ref.shape[-1]
    for c in range(0, n, n_chunk):
        e = min(c + n_chunk, n)
        o_ref[:, c:e] = jnp.dot(xn, w_ref[:, c:e], preferred_element_type=F32)


def _norm_matmul(x2, g, w_bf16, tm):
    m, d = x2.shape
    n = w_bf16.shape[1]
    return pl.pallas_call(
        functools.partial(_norm_matmul_body, n_chunk=512),
        out_shape=jax.ShapeDtypeStruct((m, n), F32),
        grid=(m // tm,),
        in_specs=[pl.BlockSpec((tm, d), lambda i: (i, 0)),
                  pl.BlockSpec((1, d), lambda i: (0, 0)),
                  pl.BlockSpec((d, n), lambda i: (0, 0))],
        out_specs=pl.BlockSpec((tm, n), lambda i: (i, 0)),
        compiler_params=_cparams(("arbitrary",)),
        name="norm_matmul",
    )(x2, g.reshape(1, d), w_bf16)


def _rwkv_prep_body(p_ref, mu_ref, wcomb_ref, w0a0_ref, gup_ref, kk_ref, ka_ref, rk_ref, ones_ref,
                    r_o, lw_o, k_o, v_o, a_o, b_o, g_o, bonus_o, carry_ref):
    t2 = p_ref.shape[0]

    @pl.when(pl.program_id(1) == 0)
    def _():
        carry_ref[...] = jnp.zeros_like(carry_ref)

    p = p_ref[...]
    rows = lax.broadcasted_iota(jnp.int32, p.shape, 0)
    prev = jnp.where(rows == 0, carry_ref[...], pltpu.roll(p, 1, axis=0))
    carry_ref[...] = p[t2 - 1:t2, :]
    x = p + mu_ref[...] * (prev - p)

    r = x[:, 0:WIDTH]
    k = x[:, WIDTH:2 * WIDTH]
    v = x[:, 2 * WIDTH:3 * WIDTH]
    wa = x[:, 3 * WIDTH:3 * WIDTH + 128]
    gd = x[:, 3 * WIDTH + 128:]

    lane = lax.broadcasted_iota(jnp.int32, wa.shape, 1)
    xa = jnp.where(lane < DECAY_LORA, jnp.tanh(wa), wa)
    pre = _dot_f32(xa, wcomb_ref[...]) + w0a0_ref[...]
    w_log = -_softplus(-pre[:, :WIDTH]) - 0.5
    log_decay = -jnp.exp(w_log)
    a = _sigmoid(pre[:, WIDTH:])
    g = _dot_f32(_sigmoid(gd), gup_ref[...])

    ones = ones_ref[...]
    kk = k * kk_ref[...]
    ss = _dot_ones(kk * kk, ones)
    kk = kk / jnp.maximum(jnp.sqrt(ss), 1e-12)
    k2 = k * (1.0 + (a - 1.0) * ka_ref[...])
    bonus = _dot_ones(r * k2 * rk_ref[...], ones) * v

    for j in range(PAIRS):
        ls = slice(j * LANES, (j + 1) * LANES)
        r_o[j] = r[:, ls]
        lw_o[j] = log_decay[:, ls]
        k_o[j] = k2[:, ls]
        v_o[j] = v[:, ls]
        a_o[j] = -kk[:, ls]
        b_o[j] = (kk * a)[:, ls]
    g_o[...] = g
    bonus_o[...] = bonus


def _rwkv_prep(p_rw, mu_p, wcomb, w0a0, gup_p, k_k, k_a, r_k, ones512, t2):
    b, l, pc = p_rw.shape
    full = lambda shape: pl.BlockSpec(shape, lambda bi, ti: (0,) * len(shape))
    pair_sd = jax.ShapeDtypeStruct((b, PAIRS, l, LANES), F32)
    pair_spec = pl.BlockSpec((None, PAIRS, t2, LANES), lambda bi, ti: (bi, 0, ti, 0))
    row_sd = jax.ShapeDtypeStruct((b, l, WIDTH), F32)
    row_spec = pl.BlockSpec((None, t2, WIDTH), lambda bi, ti: (bi, ti, 0))
    return pl.pallas_call(
        _rwkv_prep_body,
        out_shape=[pair_sd] * 6 + [row_sd] * 2,
        grid=(b, l // t2),
        in_specs=[pl.BlockSpec((None, t2, pc), lambda bi, ti: (bi, ti, 0)),
                  full((1, pc)), full(wcomb.shape), full((1, 2 * WIDTH)), full(gup_p.shape),
                  full((1, WIDTH)), full((1, WIDTH)), full((1, WIDTH)), full((WIDTH, WIDTH))],
        out_specs=[pair_spec] * 6 + [row_spec] * 2,
        scratch_shapes=[pltpu.VMEM((1, pc), F32)],
        compiler_params=_cparams(("arbitrary", "arbitrary")),
        name="rwkv_prep",
    )(p_rw, mu_p, wcomb, w0a0, gup_p, k_k, k_a, r_k, ones512)


def _rwkv_chunk_body(r_ref, lw_ref, k_ref, v_ref, a_ref, b_ref, y_ref, h_ref):
    npg = r_ref.shape[0]
    n_chunks = SB // CHUNK

    @pl.when(pl.program_id(1) == 0)
    def _():
        h_ref[...] = jnp.zeros_like(h_ref)

    ti = lax.broadcasted_iota(jnp.int32, (SB, SB), 0)
    si = lax.broadcasted_iota(jnp.int32, (SB, SB), 1)
    same = (ti // CHUNK) == (si // CHUNK)
    incl = same & (si <= ti)
    strict = same & (si < ti)
    eye = ti == si
    same_head = (ti < HEAD_DIM) == (si < HEAD_DIM)
    head0 = lax.broadcasted_iota(jnp.int32, (SB, LANES), 1) < HEAD_DIM
    tri_ones = jnp.concatenate([incl.astype(BF16), same.astype(BF16)], axis=0)
    ones = jnp.ones((LANES, LANES), BF16)
    eye_f = eye.astype(F32)

    def keep(mask, x):
        return jnp.where(mask, x, jnp.zeros_like(x))

    pairs = range(npg)
    units = [(p, h) for p in pairs for h in range(2)]
    cat0 = lambda *xs: jnp.concatenate(xs, axis=0)
    mm = lambda a, b, **kw: _mm3(*a, *b, **kw)

    lw = [lw_ref[p] for p in pairs]
    cc = [sum(jnp.dot(tri_ones, piece, preferred_element_type=F32) for piece in _split3(lw[p]))
          for p in pairs]
    cs = [cc[p][:SB] for p in pairs]
    ce = [cc[p][SB:] for p in pairs]
    g = [jnp.exp(cs[p]) for p in pairs]
    ginv = [jnp.exp(-cs[p]) for p in pairs]
    gend = [jnp.exp(ce[p] - cs[p]) for p in pairs]
    g_c = [jnp.exp(ce[p]) for p in pairs]
    at = [a_ref[p] * jnp.exp(cs[p] - lw[p]) for p in pairs]
    rt = [r_ref[p] * g[p] for p in pairs]
    v_s = [_split2(v_ref[p]) for p in pairs]
    at_s = [_split2(at[p]) for p in pairs]
    rt_s = [_split2(rt[p]) for p in pairs]
    bt_s = [_split2(b_ref[p] * ginv[p]) for p in pairs]
    kt_s = [_split2(k_ref[p] * ginv[p]) for p in pairs]
    bh_s = [_split2(b_ref[p] * gend[p]) for p in pairs]
    kh_s = [_split2(k_ref[p] * gend[p]) for p in pairs]

    gram = [mm([cat0(keep(head0, at_s[p][i]), keep(head0, rt_s[p][i]),
                     keep(~head0, at_s[p][i]), keep(~head0, rt_s[p][i])) for i in range(2)],
               [cat0(bt_s[p][i], kt_s[p][i]) for i in range(2)], dims=((1,), (1,)))
            for p in pairs]
    gb = {(p, h): gram[p][2 * SB * h:2 * SB * (h + 1)] for p, h in units}
    mab = {u: keep(strict, gb[u][:SB, :SB]) for u in units}
    mkk_s = {u: _split2(cat0(keep(strict, gb[u][:SB, SB:]), keep(incl, gb[u][SB:, SB:]))) for u in units}
    mrb_s = {u: _split2(keep(incl, gb[u][SB:, :SB])) for u in units}

    n1 = {u: _split2(mab[u]) for u in units}
    n2 = {u: _split2(mm(n1[u], n1[u])) for u in units}
    n4 = {u: _split2(mm(n2[u], n2[u])) for u in units}
    n8 = {u: _split2(mm(n4[u], n4[u])) for u in units}
    t = {u: eye_f + mab[u] for u in units}
    for npow in (n2, n4, n8):
        t = {u: t[u] + mm(npow[u], _split2(t[u])) for u in units}
    wk = {u: mm(mkk_s[u], v_s[u[0]]) for u in units}
    x = {u: mm(_split2(t[u]), _split2(jnp.concatenate([at[u[0]], wk[u][:SB]], axis=1))) for u in units}
    ry = {u: mm(mrb_s[u], _split2(x[u])) for u in units}
    both = lambda f: [jnp.where(head0, f((p, 0)), f((p, 1))) for p in pairs]
    abar_s = [_split2(z) for z in both(lambda u: x[u][:, :LANES])]
    vbar = both(lambda u: x[u][:, LANES:])
    rbar_s = [_split2(z) for z in both(lambda u: rt[u[0]] + ry[u][:, :LANES])]
    ybar = both(lambda u: wk[u][SB:] + ry[u][:, LANES:])

    gcol = [sum(jnp.dot(piece, ones, preferred_element_type=F32)
                for piece in _split3(cat0(*[eye_f * g_c[p][c * CHUNK:c * CHUNK + 1, :]
                                            for c in range(n_chunks)])))
            for p in pairs]

    hbd = [h_ref[p] for p in pairs]
    for c in range(n_chunks):
        sl = slice(c * CHUNK, (c + 1) * CHUNK)
        yu = [mm([cat0(rbar_s[p][i][sl], abar_s[p][i][sl]) for i in range(2)], _split2(hbd[p]))
              for p in pairs]
        for p in pairs:
            y_ref[p, sl, :] = yu[p][:CHUNK] + ybar[p][sl]
        u_s = [_split2(yu[p][CHUNK:] + vbar[p][sl]) for p in pairs]
        upd = [mm([cat0(bh_s[p][i][sl], kh_s[p][i][sl]) for i in range(2)],
                  [cat0(u_s[p][i], v_s[p][i][sl]) for i in range(2)], dims=((0,), (0,)))
               for p in pairs]
        hbd = [gcol[p][c * SB:(c + 1) * SB] * hbd[p] + keep(same_head, upd[p]) for p in pairs]
    for p in pairs:
        h_ref[p] = hbd[p]


def _rwkv_chunk(r, lw, k, v, a, b, npg):
    n_pairs, l, _ = r.shape
    blk = pl.BlockSpec((npg, SB, LANES), lambda gi, ti: (gi, ti, 0))
    return pl.pallas_call(
        _rwkv_chunk_body,
        out_shape=jax.ShapeDtypeStruct((n_pairs, l, LANES), F32),
        grid=(n_pairs // npg, l // SB),
        in_specs=[blk] * 6,
        out_specs=blk,
        scratch_shapes=[pltpu.VMEM((npg, LANES, LANES), F32)],
        compiler_params=_cparams(("arbitrary", "arbitrary")),
        name="rwkv_chunk",
    )(r, lw, k, v, a, b)


def _fox_prep_body(qk_ref, fl_ref, gain_ref, bf_ref, ones_ref, qk_o, c_o, carry_ref):
    t2 = qk_ref.shape[0]

    @pl.when(pl.program_id(1) == 0)
    def _():
        carry_ref[...] = jnp.zeros_like(carry_ref)

    ones = ones_ref[...]
    for h0 in (0, WIDTH):
        x = qk_ref[:, h0:h0 + WIDTH]
        ms = _dot_ones(x * x, ones) * (1.0 / HEAD_DIM)
        qk_o[:, h0:h0 + WIDTH] = (x * lax.rsqrt(ms + NORM_EPS) * gain_ref[:, h0:h0 + WIDTH]).astype(qk_o.dtype)

    logf = -_softplus(-(fl_ref[...] + bf_ref[...]))
    ri = lax.broadcasted_iota(jnp.int32, (t2, t2), 0)
    ci = lax.broadcasted_iota(jnp.int32, (t2, t2), 1)
    tri = (ci <= ri).astype(BF16)
    c = sum(jnp.dot(tri, piece, preferred_element_type=F32) for piece in _split3(logf)) + carry_ref[...]
    c_o[...] = c
    carry_ref[...] = c[t2 - 1:t2, :]


def _fox_prep(p_fx, gain, bf_p, ones512, t2):
    b, l, _ = p_fx.shape
    full = lambda shape: pl.BlockSpec(shape, lambda bi, ti: (0,) * len(shape))
    return pl.pallas_call(
        _fox_prep_body,
        out_shape=[jax.ShapeDtypeStruct((b, l, 2 * WIDTH), BF16),
                   jax.ShapeDtypeStruct((b, l, LANES), F32)],
        grid=(b, l // t2),
        in_specs=[pl.BlockSpec((None, t2, 2 * WIDTH), lambda bi, ti: (bi, ti, 0)),
                  pl.BlockSpec((None, t2, LANES), lambda bi, ti: (bi, ti, 4 * WIDTH // LANES)),
                  full((1, 2 * WIDTH)), full((1, LANES)), full((WIDTH, WIDTH))],
        out_specs=[pl.BlockSpec((None, t2, 2 * WIDTH), lambda bi, ti: (bi, ti, 0)),
                   pl.BlockSpec((None, t2, LANES), lambda bi, ti: (bi, ti, 0))],
        scratch_shapes=[pltpu.VMEM((1, LANES), F32)],
        compiler_params=_cparams(("arbitrary", "arbitrary")),
        name="fox_prep",
    )(p_fx, p_fx, gain, bf_p, ones512)


def _fox_attn_body(q_ref, k_ref, v_ref, cc_ref, cr_ref, o_ref, m_ref, l_ref, acc_ref):
    n_heads, tq, _ = q_ref.shape
    tk = k_ref.shape[1]
    qi = pl.program_id(1)
    kj = pl.program_id(2)

    @pl.when(kj == 0)
    def _():
        m_ref[...] = jnp.full_like(m_ref, MASK_VALUE)
        l_ref[...] = jnp.zeros_like(l_ref)
        acc_ref[...] = jnp.zeros_like(acc_ref)

    @pl.when(kj <= qi)
    def _():
        rowpos = qi * tq + lax.broadcasted_iota(jnp.int32, (tq, tk), 0)
        colpos = kj * tk + lax.broadcasted_iota(jnp.int32, (tq, tk), 1)
        causal = colpos <= rowpos
        for h in range(n_heads):
            s = lax.dot_general(q_ref[h], k_ref[h], (((1,), (1,)), ((), ())),
                                preferred_element_type=F32)
            s = s + (cc_ref[h] - cr_ref[h])
            s = jnp.where(causal, s, MASK_VALUE)
            m_prev = m_ref[h]
            m_new = jnp.maximum(m_prev, jnp.max(s, axis=1, keepdims=True))
            alpha = jnp.exp(m_prev - m_new)
            p = jnp.exp(s - m_new)
            l_ref[h] = alpha * l_ref[h] + jnp.sum(p, axis=1, keepdims=True)
            acc_ref[h] = alpha * acc_ref[h] + jnp.dot(p.astype(BF16), v_ref[h],
                                                      preferred_element_type=F32)
            m_ref[h] = m_new

    @pl.when(kj == qi)
    def _():
        o_ref[...] = acc_ref[...] / l_ref[...]


def _fox_attn(q, k, v, c_col, c_row, tq):
    b, h, lp, d = q.shape
    nq = lp // tq
    kv_map = lambda bi, qi, kj: (bi, 0, jnp.minimum(kj, qi), 0)
    return pl.pallas_call(
        _fox_attn_body,
        out_shape=jax.ShapeDtypeStruct((b, h, lp, d), F32),
        grid=(b, nq, nq),
        in_specs=[pl.BlockSpec((None, h, tq, d), lambda bi, qi, kj: (bi, 0, qi, 0)),
                  pl.BlockSpec((None, h, tq, d), kv_map),
                  pl.BlockSpec((None, h, tq, d), kv_map),
                  pl.BlockSpec((None, h, tq, 1), lambda bi, qi, kj: (bi, 0, qi, 0)),
                  pl.BlockSpec((None, h, 1, tq), lambda bi, qi, kj: (bi, 0, 0, jnp.minimum(kj, qi)))],
        out_specs=pl.BlockSpec((None, h, tq, d), lambda bi, qi, kj: (bi, 0, qi, 0)),
        scratch_shapes=[pltpu.VMEM((h, tq, 1), F32), pltpu.VMEM((h, tq, 1), F32),
                        pltpu.VMEM((h, tq, d), F32)],
        compiler_params=_cparams(("arbitrary", "arbitrary", "arbitrary")),
        name="fox_attn",
    )(q, k, v, c_col, c_row)


def _mix_out_body(h_ref, y_ref, bonus_ref, g_ref, o_ref, og_ref, gnw_ref, gnb_ref, ones_ref, wo_ref, out_ref):
    ones = ones_ref[...]
    y = jnp.concatenate([y_ref[j] for j in range(PAIRS)], axis=1)
    mean = _dot_ones(y, ones) * (1.0 / HEAD_DIM)
    d = y - mean
    var = _dot_ones(d * d, ones) * (1.0 / HEAD_DIM)
    yn = d * lax.rsqrt(var + GN_EPS) * gnw_ref[...] + gnb_ref[...]
    y_rw = ((yn + bonus_ref[...]) * g_ref[...]).astype(BF16)
    y_fx = (o_ref[...] * _sigmoid(og_ref[...])).astype(BF16)
    out_ref[...] = (h_ref[...]
                    + jnp.dot(y_rw, wo_ref[0:WIDTH, :], preferred_element_type=F32)
                    + jnp.dot(y_fx, wo_ref[WIDTH:2 * WIDTH, :], preferred_element_type=F32))


def _mix_out(h, y, bonus, g, o_fx, p_fx, gn_w, gn_b, ones512, wo_bf16, tm):
    b, l, d = h.shape
    full = lambda shape: pl.BlockSpec(shape, lambda bi, ti: (0,) * len(shape))
    rows = pl.BlockSpec((None, tm, WIDTH), lambda bi, ti: (bi, ti, 0))
    return pl.pallas_call(
        _mix_out_body,
        out_shape=jax.ShapeDtypeStruct((b, l, d), F32),
        grid=(b, l // tm),
        in_specs=[pl.BlockSpec((None, tm, d), lambda bi, ti: (bi, ti, 0)),
                  pl.BlockSpec((None, PAIRS, tm, LANES), lambda bi, ti: (bi, 0, ti, 0)),
                  rows, rows, rows,
                  pl.BlockSpec((None, tm, WIDTH), lambda bi, ti: (bi, ti, 3)),
                  full((1, WIDTH)), full((1, WIDTH)), full((WIDTH, WIDTH)), full((2 * WIDTH, d))],
        out_specs=pl.BlockSpec((None, tm, d), lambda bi, ti: (bi, ti, 0)),
        compiler_params=_cparams(("arbitrary", "arbitrary")),
        name="mix_out",
    )(h, y, bonus, g, o_fx, p_fx, gn_w, gn_b, ones512, wo_bf16)


def _conv_ffn_body(h_ref, g_ref, wu_ref, wg_ref, cwu_ref, cwg_ref, cbu_ref, cbg_ref, wout_ref,
                   out_ref, xn_ref, cu_ref, cg_ref):
    tm = h_ref.shape[0]
    ti = pl.program_id(1)
    j = pl.program_id(2)

    @pl.when(j == 0)
    def _():
        x = h_ref[...]
        ms = jnp.mean(x * x, axis=-1, keepdims=True)
        xn_ref[...] = (x * lax.rsqrt(ms + NORM_EPS) * g_ref[...]).astype(BF16)
        out_ref[...] = x

    @pl.when(ti == 0)
    def _():
        cu_ref[j] = jnp.zeros(cu_ref.shape[1:], F32)
        cg_ref[j] = jnp.zeros(cg_ref.shape[1:], F32)

    xn = xn_ref[...]

    def conv(w_ref, cw_ref, cb_ref, carry_ref):
        hcur = jnp.dot(xn, w_ref[...], preferred_element_type=F32)
        rows = lax.broadcasted_iota(jnp.int32, hcur.shape, 0)
        carry = carry_ref[j]
        h1 = jnp.where(rows == 0, carry[7:8, :], pltpu.roll(hcur, 1, axis=0))
        h2 = pltpu.roll(hcur, 2, axis=0)
        h2 = jnp.where(rows == 0, carry[6:7, :], jnp.where(rows == 1, carry[7:8, :], h2))
        carry_ref[j] = hcur[tm - 8:tm, :]
        cw = cw_ref[...]
        return cb_ref[...] + cw[0:1, :] * h2 + cw[1:2, :] * h1 + cw[2:3, :] * hcur

    u = conv(wu_ref, cwu_ref, cbu_ref, cu_ref)
    gt = conv(wg_ref, cwg_ref, cbg_ref, cg_ref)
    act = (gt * _sigmoid(gt) * u).astype(BF16)
    out_ref[...] += jnp.dot(act, wout_ref[...], preferred_element_type=F32)


def _conv_ffn(h, g, w_in_bf16, conv_w, conv_b, w_out_bf16, tm, tf):
    b, l, d = h.shape
    d_ff = w_out_bf16.shape[0]
    nj = d_ff // tf
    cw = jnp.pad(conv_w, ((0, 8 - CONV_W), (0, 0)))
    cb = conv_b.reshape(1, 2 * d_ff)
    return pl.pallas_call(
        _conv_ffn_body,
        out_shape=jax.ShapeDtypeStruct((b, l, d), F32),
        grid=(b, l // tm, nj),
        in_specs=[pl.BlockSpec((None, tm, d), lambda bi, ti, j: (bi, ti, 0)),
                  pl.BlockSpec((1, d), lambda bi, ti, j: (0, 0)),
                  pl.BlockSpec((d, tf), lambda bi, ti, j: (0, j)),
                  pl.BlockSpec((d, tf), lambda bi, ti, j: (0, nj + j)),
                  pl.BlockSpec((8, tf), lambda bi, ti, j: (0, j)),
                  pl.BlockSpec((8, tf), lambda bi, ti, j: (0, nj + j)),
                  pl.BlockSpec((1, tf), lambda bi, ti, j: (0, j)),
                  pl.BlockSpec((1, tf), lambda bi, ti, j: (0, nj + j)),
                  pl.BlockSpec((tf, d), lambda bi, ti, j: (j, 0))],
        out_specs=pl.BlockSpec((None, tm, d), lambda bi, ti, j: (bi, ti, 0)),
        scratch_shapes=[pltpu.VMEM((tm, d), BF16),
                        pltpu.VMEM((nj, 8, tf), F32), pltpu.VMEM((nj, 8, tf), F32)],
        compiler_params=_cparams(("arbitrary", "arbitrary", "arbitrary")),
        name="conv_ffn",
    )(h, g.reshape(1, d), w_in_bf16, w_in_bf16, cw, cw, cb, cb, w_out_bf16)


def _pad_cols(w, n):
    return jnp.pad(w, ((0, 0), (0, n - w.shape[1])))


def _layer(h, prm):
    (norm1_g, w_in, rw_mu, rw_w0, rw_w_up, rw_a0, rw_a_up, rw_g_up, rw_k_k, rw_k_a, rw_r_k,
     rw_gn_w, rw_gn_b, fx_b_f, fx_q_g, fx_k_g, w_o, norm2_g, ffn_w_in, ffn_conv_w, ffn_conv_b,
     ffn_w_out) = prm
    b, l, d = h.shape
    m = b * l
    rw_cols = 3 * WIDTH + DECAY_LORA + AAA_LORA + GATE_LORA

    w_rw = _pad_cols(w_in[:, :rw_cols], RW_PCOLS).astype(BF16)
    w_fx = _pad_cols(w_in[:, rw_cols:], FX_PCOLS).astype(BF16)
    mu_p = _pad_cols(rw_mu.reshape(1, rw_cols), RW_PCOLS)
    wcomb = jnp.zeros((128, 2 * WIDTH), F32)
    wcomb = wcomb.at[:DECAY_LORA, :WIDTH].set(rw_w_up).at[DECAY_LORA:, WIDTH:].set(rw_a_up)
    w0a0 = jnp.concatenate([rw_w0, rw_a0]).reshape(1, 2 * WIDTH)
    gup_p = jnp.pad(rw_g_up, ((0, GATE_PAD - GATE_LORA), (0, 0)))
    ones512 = _group_ones(WIDTH, HEAD_DIM)
    gain = jnp.concatenate([jnp.tile(fx_q_g, N_HEADS) * (HEAD_DIM ** -0.5),
                            jnp.tile(fx_k_g, N_HEADS)]).reshape(1, 2 * WIDTH)
    bf_p = jnp.pad(fx_b_f, (0, LANES - N_HEADS)).reshape(1, LANES)

    tm = _pick(m, (1024, 512, 256, 128))
    t2 = _pick(l, (544, 384, 256, 128))

    h2 = h.reshape(m, d)
    p_rw = _norm_matmul(h2, norm1_g, w_rw, tm).reshape(b, l, RW_PCOLS)
    p_fx = _norm_matmul(h2, norm1_g, w_fx, tm).reshape(b, l, FX_PCOLS)

    r, lw, k, v, a, bb, g, bonus = _rwkv_prep(
        p_rw, mu_p, wcomb, w0a0, gup_p, rw_k_k.reshape(1, WIDTH), rw_k_a.reshape(1, WIDTH),
        rw_r_k.reshape(1, WIDTH), ones512, t2)
    pm = lambda z: z.reshape(b * PAIRS, l, LANES)
    y = _rwkv_chunk(pm(r), pm(lw), pm(k), pm(v), pm(a), pm(bb), npg=4)
    y = y.reshape(b, PAIRS, l, LANES)

    qk, c = _fox_prep(p_fx, gain, bf_p, ones512, t2)
    tq = 256 if l > 256 else 128
    lp = -(-l // tq) * tq
    def heads(z):
        z = z.reshape(b, l, N_HEADS, HEAD_DIM).transpose(0, 2, 1, 3)
        return jnp.pad(z, ((0, 0), (0, 0), (0, lp - l), (0, 0)))
    qh = heads(qk[..., :WIDTH])
    kh = heads(qk[..., WIDTH:])
    vh = heads(p_fx[..., 2 * WIDTH:3 * WIDTH].astype(BF16))
    ch = jnp.pad(c[..., :N_HEADS].transpose(0, 2, 1), ((0, 0), (0, 0), (0, lp - l)))
    o = _fox_attn(qh, kh, vh, ch[..., None], ch[:, :, None, :], tq)
    o_fx = o[:, :, :l].transpose(0, 2, 1, 3).reshape(b, l, WIDTH)

    h = _mix_out(h, y, bonus, g, o_fx, p_fx, rw_gn_w.reshape(1, WIDTH), rw_gn_b.reshape(1, WIDTH),
                 ones512, w_o.astype(BF16), t2)

    d_ff = ffn_w_out.shape[0]
    tf = _pick(d_ff, (1408, 256, 128))
    return _conv_ffn(h, norm2_g, ffn_w_in.astype(BF16), ffn_conv_w, ffn_conv_b,
                     ffn_w_out.astype(BF16), t2, tf)


def kernel(x, meta, norm1_g, w_in, rw_mu, rw_w0, rw_w_up, rw_a0, rw_a_up, rw_g_up, rw_k_k, rw_k_a,
           rw_r_k, rw_gn_w, rw_gn_b, fx_b_f, fx_q_g, fx_k_g, w_o, norm2_g, ffn_w_in, ffn_conv_w,
           ffn_conv_b, ffn_w_out):
    b, seq, d = x.shape
    params = (norm1_g, w_in, rw_mu, rw_w0, rw_w_up, rw_a0, rw_a_up, rw_g_up, rw_k_k, rw_k_a, rw_r_k,
              rw_gn_w, rw_gn_b, fx_b_f, fx_q_g, fx_k_g, w_o, norm2_g, ffn_w_in, ffn_conv_w,
              ffn_conv_b, ffn_w_out)
    l = N_META + seq
    lp = -(-l // SB) * SB
    meta_b = jnp.broadcast_to(meta[None].astype(x.dtype), (b, N_META, d))
    h = jnp.concatenate([meta_b, x, jnp.zeros((b, lp - l, d), x.dtype)], axis=1)
    for layer in range(norm1_g.shape[0]):
        h = _layer(h, tuple(p[layer] for p in params))
    return h[:, N_META:l]
```

```python
import functools

import jax
import jax.numpy as jnp
from jax import lax
from jax.experimental import pallas as pl
from jax.experimental.pallas import tpu as pltpu

F32 = jnp.float32
BF16 = jnp.bfloat16

N_META = 16
HEAD_DIM = 64
N_HEADS = 8
WIDTH = N_HEADS * HEAD_DIM
DECAY_LORA = 64
AAA_LORA = 64
GATE_LORA = 160
GATE_PAD = 256
RW_PCOLS = 3 * WIDTH + 128 + GATE_PAD
FX_PCOLS = 4 * WIDTH + 128
CONV_W = 3
NORM_EPS = 1e-6
GN_EPS = HEAD_DIM * 1e-5
LANES = 128
PAIRS = WIDTH // LANES
CHUNK = 16
SB = 128
ATTN_BLOCK = 256
MASK_VALUE = -1e30
VMEM_LIMIT = 56 * 1024 * 1024


def _pick(n, cands):
    for c in cands:
        if n % c == 0:
            return c
    raise ValueError(f"no tile for {n} in {cands}")


def _cparams(sem):
    return pltpu.CompilerParams(dimension_semantics=sem, vmem_limit_bytes=VMEM_LIMIT)


def _sigmoid(x):
    return 1.0 / (1.0 + jnp.exp(-x))


def _softplus(x):
    return jnp.maximum(x, 0.0) + jnp.log(1.0 + jnp.exp(-jnp.abs(x)))


def _split2(x):
    hi = x.astype(BF16)
    lo = (x - hi.astype(F32)).astype(BF16)
    return hi, lo


def _split3(x):
    p1 = x.astype(BF16)
    rem = x - p1.astype(F32)
    p2 = rem.astype(BF16)
    p3 = (rem - p2.astype(F32)).astype(BF16)
    return p1, p2, p3


def _dot_ones(x, ones_bf16):
    hi, lo = _split2(x)
    return (jnp.dot(hi, ones_bf16, preferred_element_type=F32)
            + jnp.dot(lo, ones_bf16, preferred_element_type=F32))


def _mm3(a_hi, a_lo, b_hi, b_lo, dims=((1,), (0,))):
    dn = (dims, ((), ()))
    m_axis = 1 - dims[0][0]
    m = a_hi.shape[m_axis]
    d = lax.dot_general(jnp.concatenate([a_hi, a_lo], axis=m_axis), b_hi, dn,
                        preferred_element_type=F32)
    return d[:m] + d[m:] + lax.dot_general(a_hi, b_lo, dn, preferred_element_type=F32)


def _dot_f32(a, b):
    return _mm3(*_split2(a), *_split2(b))


def _group_ones(n, group):
    i = jnp.arange(n) // group
    return (i[:, None] == i[None, :]).astype(BF16)


def _norm_matmul_body(x_ref, g_ref, w_ref, o_ref, *, n_chunk):
    x = x_ref[...]
    ms = jnp.mean(x * x, axis=-1, keepdims=True)
    xn = (x * lax.rsqrt(ms + NORM_EPS) * g_ref[...]).astype(BF16)
    n = o_ref.shape[-1]
    for c in range(0, n, n_chunk):
        e = min(c + n_chunk, n)
        o_ref[:, c:e] = jnp.dot(xn, w_ref[:, c:e], preferred_element_type=F32)


def _norm_matmul(x2, g, w_bf16, tm):
    m, d = x2.shape
    n = w_bf16.shape[1]
    return pl.pallas_call(
        functools.partial(_norm_matmul_body, n_chunk=512),
        out_shape=jax.ShapeDtypeStruct((m, n), F32),
        grid=(m // tm,),
        in_specs=[pl.BlockSpec((tm, d), lambda i: (i, 0)),
                  pl.BlockSpec((1, d), lambda i: (0, 0)),
                  pl.BlockSpec((d, n), lambda i: (0, 0))],
        out_specs=pl.BlockSpec((tm, n), lambda i: (i, 0)),
        compiler_params=_cparams(("arbitrary",)),
        name="norm_matmul",
    )(x2, g.reshape(1, d), w_bf16)


def _rwkv_prep_body(p_ref, mu_ref, wcomb_ref, w0a0_ref, gup_ref, kk_ref, ka_ref, rk_ref, ones_ref,
                    r_o, lw_o, k_o, v_o, a_o, b_o, g_o, bonus_o, carry_ref):
    t2 = p_ref.shape[0]

    @pl.when(pl.program_id(1) == 0)
    def _():
        carry_ref[...] = jnp.zeros_like(carry_ref)

    p = p_ref[...]
    rows = lax.broadcasted_iota(jnp.int32, p.shape, 0)
    prev = jnp.where(rows == 0, carry_ref[...], pltpu.roll(p, 1, axis=0))
    carry_ref[...] = p[t2 - 1:t2, :]
    x = p + mu_ref[...] * (prev - p)

    r = x[:, 0:WIDTH]
    k = x[:, WIDTH:2 * WIDTH]
    v = x[:, 2 * WIDTH:3 * WIDTH]
    wa = x[:, 3 * WIDTH:3 * WIDTH + 128]
    gd = x[:, 3 * WIDTH + 128:]

    lane = lax.broadcasted_iota(jnp.int32, wa.shape, 1)
    xa = jnp.where(lane < DECAY_LORA, jnp.tanh(wa), wa)
    pre = _dot_f32(xa, wcomb_ref[...]) + w0a0_ref[...]
    w_log = -_softplus(-pre[:, :WIDTH]) - 0.5
    log_decay = -jnp.exp(w_log)
    a = _sigmoid(pre[:, WIDTH:])
    g = _dot_f32(_sigmoid(gd), gup_ref[...])

    ones = ones_ref[...]
    kk = k * kk_ref[...]
    ss = _dot_ones(kk * kk, ones)
    kk = kk / jnp.maximum(jnp.sqrt(ss), 1e-12)
    k2 = k * (1.0 + (a - 1.0) * ka_ref[...])
    bonus = _dot_ones(r * k2 * rk_ref[...], ones) * v

    for j in range(PAIRS):
        ls = slice(j * LANES, (j + 1) * LANES)
        r_o[j] = r[:, ls]
        lw_o[j] = log_decay[:, ls]
        k_o[j] = k2[:, ls]
        v_o[j] = v[:, ls]
        a_o[j] = -kk[:, ls]
        b_o[j] = (kk * a)[:, ls]
    g_o[...] = g
    bonus_o[...] = bonus


def _rwkv_prep(p_rw, mu_p, wcomb, w0a0, gup_p, k_k, k_a, r_k, ones512, t2):
    b, l, pc = p_rw.shape
    full = lambda shape: pl.BlockSpec(shape, lambda bi, ti: (0,) * len(shape))
    pair_sd = jax.ShapeDtypeStruct((b, PAIRS, l, LANES), F32)
    pair_spec = pl.BlockSpec((None, PAIRS, t2, LANES), lambda bi, ti: (bi, 0, ti, 0))
    row_sd = jax.ShapeDtypeStruct((b, l, WIDTH), F32)
    row_spec = pl.BlockSpec((None, t2, WIDTH), lambda bi, ti: (bi, ti, 0))
    return pl.pallas_call(
        _rwkv_prep_body,
        out_shape=[pair_sd] * 6 + [row_sd] * 2,
        grid=(b, l // t2),
        in_specs=[pl.BlockSpec((None, t2, pc), lambda bi, ti: (bi, ti, 0)),
                  full((1, pc)), full(wcomb.shape), full((1, 2 * WIDTH)), full(gup_p.shape),
                  full((1, WIDTH)), full((1, WIDTH)), full((1, WIDTH)), full((WIDTH, WIDTH))],
        out_specs=[pair_spec] * 6 + [row_spec] * 2,
        scratch_shapes=[pltpu.VMEM((1, pc), F32)],
        compiler_params=_cparams(("arbitrary", "arbitrary")),
        name="rwkv_prep",
    )(p_rw, mu_p, wcomb, w0a0, gup_p, k_k, k_a, r_k, ones512)


def _rwkv_chunk_body(r_ref, lw_ref, k_ref, v_ref, a_ref, b_ref, y_ref, h_ref):
    npg = r_ref.shape[0]
    n_chunks = SB // CHUNK

    @pl.when(pl.program_id(1) == 0)
    def _():
        h_ref[...] = jnp.zeros_like(h_ref)

    ti = lax.broadcasted_iota(jnp.int32, (SB, SB), 0)
    si = lax.broadcasted_iota(jnp.int32, (SB, SB), 1)
    same = (ti // CHUNK) == (si // CHUNK)
    incl = same & (si <= ti)
    strict = same & (si < ti)
    eye = ti == si
    same_head = (ti < HEAD_DIM) == (si < HEAD_DIM)
    head0 = lax.broadcasted_iota(jnp.int32, (SB, LANES), 1) < HEAD_DIM
    tri_ones = jnp.concatenate([incl.astype(BF16), same.astype(BF16)], axis=0)
    ones = jnp.ones((LANES, LANES), BF16)
    eye_f = eye.astype(F32)

    def keep(mask, x):
        return jnp.where(mask, x, jnp.zeros_like(x))

    pairs = range(npg)
    units = [(p, h) for p in pairs for h in range(2)]
    cat0 = lambda *xs: jnp.concatenate(xs, axis=0)
    mm = lambda a, b, **kw: _mm3(*a, *b, **kw)

    lw = [lw_ref[p] for p in pairs]
    cc = [sum(jnp.dot(tri_ones, piece, preferred_element_type=F32) for piece in _split3(lw[p]))
          for p in pairs]
    cs = [cc[p][:SB] for p in pairs]
    ce = [cc[p][SB:] for p in pairs]
    g = [jnp.exp(cs[p]) for p in pairs]
    ginv = [jnp.exp(-cs[p]) for p in pairs]
    gend = [jnp.exp(ce[p] - cs[p]) for p in pairs]
    g_c = [jnp.exp(ce[p]) for p in pairs]
    at = [a_ref[p] * jnp.exp(cs[p] - lw[p]) for p in pairs]
    rt = [r_ref[p] * g[p] for p in pairs]
    v_s = [_split2(v_ref[p]) for p in pairs]
    at_s = [_split2(at[p]) for p in pairs]
    rt_s = [_split2(rt[p]) for p in pairs]
    bt_s = [_split2(b_ref[p] * ginv[p]) for p in pairs]
    kt_s = [_split2(k_ref[p] * ginv[p]) for p in pairs]
    bh_s = [_split2(b_ref[p] * gend[p]) for p in pairs]
    kh_s = [_split2(k_ref[p] * gend[p]) for p in pairs]

    gram = [mm([cat0(keep(head0, at_s[p][i]), keep(head0, rt_s[p][i]),
                     keep(~head0, at_s[p][i]), keep(~head0, rt_s[p][i])) for i in range(2)],
               [cat0(bt_s[p][i], kt_s[p][i]) for i in range(2)], dims=((1,), (1,)))
            for p in pairs]
    gb = {(p, h): gram[p][2 * SB * h:2 * SB * (h + 1)] for p, h in units}
    mab = {u: keep(strict, gb[u][:SB, :SB]) for u in units}
    mkk_s = {u: _split2(cat0(keep(strict, gb[u][:SB, SB:]), keep(incl, gb[u][SB:, SB:]))) for u in units}
    mrb_s = {u: _split2(keep(incl, gb[u][SB:, :SB])) for u in units}

    n1 = {u: _split2(mab[u]) for u in units}
    n2 = {u: _split2(mm(n1[u], n1[u])) for u in units}
    n4 = {u: _split2(mm(n2[u], n2[u])) for u in units}
    n8 = {u: _split2(mm(n4[u], n4[u])) for u in units}
    t = {u: eye_f + mab[u] for u in units}
    for npow in (n2, n4, n8):
        t = {u: t[u] + mm(npow[u], _split2(t[u])) for u in units}
    wk = {u: mm(mkk_s[u], v_s[u[0]]) for u in units}
    x = {u: mm(_split2(t[u]), _split2(jnp.concatenate([at[u[0]], wk[u][:SB]], axis=1))) for u in units}
    ry = {u: mm(mrb_s[u], _split2(x[u])) for u in units}
    both = lambda f: [jnp.where(head0, f((p, 0)), f((p, 1))) for p in pairs]
    abar_s = [_split2(z) for z in both(lambda u: x[u][:, :LANES])]
    vbar = both(lambda u: x[u][:, LANES:])
    rbar_s = [_split2(z) for z in both(lambda u: rt[u[0]] + ry[u][:, :LANES])]
    ybar = both(lambda u: wk[u][SB:] + ry[u][:, LANES:])

    gcol = [sum(jnp.dot(piece, ones, preferred_element_type=F32)
                for piece in _split3(cat0(*[eye_f * g_c[p][c * CHUNK:c * CHUNK + 1, :]
                                            for c in range(n_chunks)])))
            for p in pairs]

    hbd = [h_ref[p] for p in pairs]
    for c in range(n_chunks):
        sl = slice(c * CHUNK, (c + 1) * CHUNK)
        yu = [mm([cat0(rbar_s[p][i][sl], abar_s[p][i][sl]) for i in range(2)], _split2(hbd[p]))
              for p in pairs]
        for p in pairs:
            y_ref[p, sl, :] = yu[p][:CHUNK] + ybar[p][sl]
        u_s = [_split2(yu[p][CHUNK:] + vbar[p][sl]) for p in pairs]
        upd = [mm([cat0(bh_s[p][i][sl], kh_s[p][i][sl]) for i in range(2)],
                  [cat0(u_s[p][i], v_s[p][i][sl]) for i in range(2)], dims=((0,), (0,)))
               for p in pairs]
        hbd = [gcol[p][c * SB:(c + 1) * SB] * hbd[p] + keep(same_head, upd[p]) for p in pairs]
    for p in pairs:
        h_ref[p] = hbd[p]


def _rwkv_chunk(r, lw, k, v, a, b, npg):
    n_pairs, l, _ = r.shape
    blk = pl.BlockSpec((npg, SB, LANES), lambda gi, ti: (gi, ti, 0))
    return pl.pallas_call(
        _rwkv_chunk_body,
        out_shape=jax.ShapeDtypeStruct((n_pairs, l, LANES), F32),
        grid=(n_pairs // npg, l // SB),
        in_specs=[blk] * 6,
        out_specs=blk,
        scratch_shapes=[pltpu.VMEM((npg, LANES, LANES), F32)],
        compiler_params=_cparams(("arbitrary", "arbitrary")),
        name="rwkv_chunk",
    )(r, lw, k, v, a, b)


def _fox_prep_body(qkv_ref, fl_ref, gain_ref, bf_ref, ones_ref, qkv_o, c_o, carry_ref):
    t2 = qkv_ref.shape[0]

    @pl.when(pl.program_id(1) == 0)
    def _():
        carry_ref[...] = jnp.zeros_like(carry_ref)

    ones = ones_ref[...]
    for h0 in (0, WIDTH):
        x = qkv_ref[:, h0:h0 + WIDTH]
        ms = _dot_ones(x * x, ones) * (1.0 / HEAD_DIM)
        qkv_o[:, h0:h0 + WIDTH] = (x * lax.rsqrt(ms + NORM_EPS) * gain_ref[:, h0:h0 + WIDTH]).astype(qkv_o.dtype)
    qkv_o[:, 2 * WIDTH:] = qkv_ref[:, 2 * WIDTH:].astype(qkv_o.dtype)

    logf = -_softplus(-(fl_ref[...] + bf_ref[...]))
    ri = lax.broadcasted_iota(jnp.int32, (t2, t2), 0)
    ci = lax.broadcasted_iota(jnp.int32, (t2, t2), 1)
    tri = (ci <= ri).astype(BF16)
    c = sum(jnp.dot(tri, piece, preferred_element_type=F32) for piece in _split3(logf)) + carry_ref[...]
    c_o[...] = c
    carry_ref[...] = c[t2 - 1:t2, :]


def _fox_prep(p_fx, gain, bf_p, ones512, t2):
    b, l, _ = p_fx.shape
    full = lambda shape: pl.BlockSpec(shape, lambda bi, ti: (0,) * len(shape))
    return pl.pallas_call(
        _fox_prep_body,
        out_shape=[jax.ShapeDtypeStruct((b, l, 3 * WIDTH), BF16),
                   jax.ShapeDtypeStruct((b, l, LANES), F32)],
        grid=(b, l // t2),
        in_specs=[pl.BlockSpec((None, t2, 3 * WIDTH), lambda bi, ti: (bi, ti, 0)),
                  pl.BlockSpec((None, t2, LANES), lambda bi, ti: (bi, ti, 4 * WIDTH // LANES)),
                  full((1, 2 * WIDTH)), full((1, LANES)), full((WIDTH, WIDTH))],
        out_specs=[pl.BlockSpec((None, t2, 3 * WIDTH), lambda bi, ti: (bi, ti, 0)),
                   pl.BlockSpec((None, t2, LANES), lambda bi, ti: (bi, ti, 0))],
        scratch_shapes=[pltpu.VMEM((1, LANES), F32)],
        compiler_params=_cparams(("arbitrary", "arbitrary")),
        name="fox_prep",
    )(p_fx, p_fx, gain, bf_p, ones512)


def _fox_attn_body(q_ref, k_ref, v_ref, nc_ref, o_ref, m_ref, l_ref, acc_ref, *, q_row0):
    tq = q_ref.shape[0]
    tk = k_ref.shape[0]
    qi = pl.program_id(1)
    kj = pl.program_id(2)
    row_first = q_row0 + qi * tq
    col_first = kj * tk
    last_kj = (row_first + tq - 1) // tk

    @pl.when(kj == 0)
    def _():
        m_ref[...] = jnp.full_like(m_ref, MASK_VALUE)
        l_ref[...] = jnp.zeros_like(l_ref)
        acc_ref[...] = jnp.zeros_like(acc_ref)

    head0 = lax.broadcasted_iota(jnp.int32, (1, LANES), 1) < HEAD_DIM
    ones = jnp.ones((tk, LANES), BF16)

    def block(masked):
        if masked:
            rowpos = row_first + lax.broadcasted_iota(jnp.int32, (tq, tk), 0)
            colpos = col_first + lax.broadcasted_iota(jnp.int32, (tq, tk), 1)
            causal = colpos <= rowpos
        for j in range(PAIRS):
            ls = slice(j * LANES, (j + 1) * LANES)
            q = q_ref[:, ls]
            k = k_ref[:, ls]
            v = v_ref[:, ls]
            zero = jnp.zeros_like(q)
            q2 = jnp.concatenate([jnp.where(head0, q, zero), jnp.where(head0, zero, q)], axis=0)
            s = lax.dot_general(q2, k, (((1,), (1,)), ((), ())), preferred_element_type=F32)
            s0 = s[:tq] + nc_ref[2 * j:2 * j + 1, :]
            s1 = s[tq:] + nc_ref[2 * j + 1:2 * j + 2, :]
            if masked:
                s0 = jnp.where(causal, s0, MASK_VALUE)
                s1 = jnp.where(causal, s1, MASK_VALUE)
            zv = jnp.zeros_like(v)
            vsel = (jnp.where(head0, v, zv), jnp.where(head0, zv, v))
            alphas, pvs = [], []
            for hh, sh in enumerate((s0, s1)):
                m_prev = m_ref[2 * j + hh]
                m_new = jnp.maximum(m_prev, jnp.max(sh, axis=1, keepdims=True))
                p = jnp.exp(sh - jnp.concatenate([m_new] * (tk // LANES), axis=1))
                alpha = jnp.exp(m_prev - m_new)
                pv = jnp.dot(p.astype(BF16), jnp.concatenate([vsel[hh], ones], axis=1),
                             preferred_element_type=F32)
                l_ref[2 * j + hh] = alpha * l_ref[2 * j + hh] + pv[:, LANES:]
                m_ref[2 * j + hh] = m_new
                alphas.append(alpha)
                pvs.append(pv[:, :LANES])
            acc_ref[j] = jnp.where(head0, alphas[0], alphas[1]) * acc_ref[j] + pvs[0] + pvs[1]

    needed = col_first <= row_first + tq - 1
    crosses = col_first + tk - 1 > row_first

    @pl.when(needed & crosses)
    def _():
        block(True)

    @pl.when(needed & jnp.logical_not(crosses))
    def _():
        block(False)

    @pl.when(kj == last_kj)
    def _():
        for j in range(PAIRS):
            o_ref[:, j * LANES:(j + 1) * LANES] = acc_ref[j] / jnp.where(head0, l_ref[2 * j], l_ref[2 * j + 1])


def _fox_attn(qkv, nc, q_row0, n_rows, tq, tk):
    b, l, _ = qkv.shape
    nq = n_rows // tq
    qb0 = q_row0 // tq
    kv_last = lambda qi: (q_row0 + (qi + 1) * tq - 1) // tk
    nk = kv_last(nq - 1) + 1
    kcol, vcol = 1, 2
    return pl.pallas_call(
        functools.partial(_fox_attn_body, q_row0=q_row0),
        out_shape=jax.ShapeDtypeStruct((b, n_rows, WIDTH), F32),
        grid=(b, nq, nk),
        in_specs=[pl.BlockSpec((None, tq, WIDTH), lambda bi, qi, kj: (bi, qb0 + qi, 0)),
                  pl.BlockSpec((None, tk, WIDTH), lambda bi, qi, kj: (bi, jnp.minimum(kj, kv_last(qi)), kcol)),
                  pl.BlockSpec((None, tk, WIDTH), lambda bi, qi, kj: (bi, jnp.minimum(kj, kv_last(qi)), vcol)),
                  pl.BlockSpec((None, N_HEADS, tk), lambda bi, qi, kj: (bi, 0, jnp.minimum(kj, kv_last(qi))))],
        out_specs=pl.BlockSpec((None, tq, WIDTH), lambda bi, qi, kj: (bi, qi, 0)),
        scratch_shapes=[pltpu.VMEM((N_HEADS, tq, LANES), F32), pltpu.VMEM((N_HEADS, tq, LANES), F32),
                        pltpu.VMEM((PAIRS, tq, LANES), F32)],
        compiler_params=_cparams(("arbitrary", "arbitrary", "arbitrary")),
        name="fox_attn",
    )(qkv, qkv, qkv, nc)


def _mix_out_body(h_ref, y_ref, bonus_ref, g_ref, o_ref, og_ref, gnw_ref, gnb_ref, ones_ref, wo_ref, out_ref):
    ones = ones_ref[...]
    y = jnp.concatenate([y_ref[j] for j in range(PAIRS)], axis=1)
    mean = _dot_ones(y, ones) * (1.0 / HEAD_DIM)
    d = y - mean
    var = _dot_ones(d * d, ones) * (1.0 / HEAD_DIM)
    yn = d * lax.rsqrt(var + GN_EPS) * gnw_ref[...] + gnb_ref[...]
    y_rw = ((yn + bonus_ref[...]) * g_ref[...]).astype(BF16)
    y_fx = (o_ref[...] * _sigmoid(og_ref[...])).astype(BF16)
    out_ref[...] = (h_ref[...]
                    + jnp.dot(y_rw, wo_ref[0:WIDTH, :], preferred_element_type=F32)
                    + jnp.dot(y_fx, wo_ref[WIDTH:2 * WIDTH, :], preferred_element_type=F32))


def _mix_out(h, y, bonus, g, o_fx, p_fx, gn_w, gn_b, ones512, wo_bf16, tm):
    b, l, d = h.shape
    full = lambda shape: pl.BlockSpec(shape, lambda bi, ti: (0,) * len(shape))
    rows = pl.BlockSpec((None, tm, WIDTH), lambda bi, ti: (bi, ti, 0))
    return pl.pallas_call(
        _mix_out_body,
        out_shape=jax.ShapeDtypeStruct((b, l, d), F32),
        grid=(b, l // tm),
        in_specs=[pl.BlockSpec((None, tm, d), lambda bi, ti: (bi, ti, 0)),
                  pl.BlockSpec((None, PAIRS, tm, LANES), lambda bi, ti: (bi, 0, ti, 0)),
                  rows, rows, rows,
                  pl.BlockSpec((None, tm, WIDTH), lambda bi, ti: (bi, ti, 3)),
                  full((1, WIDTH)), full((1, WIDTH)), full((WIDTH, WIDTH)), full((2 * WIDTH, d))],
        out_specs=pl.BlockSpec((None, tm, d), lambda bi, ti: (bi, ti, 0)),
        compiler_params=_cparams(("arbitrary", "arbitrary")),
        name="mix_out",
    )(h, y, bonus, g, o_fx, p_fx, gn_w, gn_b, ones512, wo_bf16)


def _conv_ffn_body(h_ref, g_ref, wu_ref, wg_ref, cwu_ref, cwg_ref, cbu_ref, cbg_ref, wout_ref,
                   out_ref, xn_ref, cu_ref, cg_ref):
    tm = h_ref.shape[0]
    ti = pl.program_id(1)
    j = pl.program_id(2)

    @pl.when(j == 0)
    def _():
        x = h_ref[...]
        ms = jnp.mean(x * x, axis=-1, keepdims=True)
        xn_ref[...] = (x * lax.rsqrt(ms + NORM_EPS) * g_ref[...]).astype(BF16)
        out_ref[...] = x

    @pl.when(ti == 0)
    def _():
        cu_ref[j] = jnp.zeros(cu_ref.shape[1:], F32)
        cg_ref[j] = jnp.zeros(cg_ref.shape[1:], F32)

    xn = xn_ref[...]

    def conv(w_ref, cw_ref, cb_ref, carry_ref):
        hcur = jnp.dot(xn, w_ref[...], preferred_element_type=F32)
        rows = lax.broadcasted_iota(jnp.int32, hcur.shape, 0)
        carry = carry_ref[j]
        h1 = jnp.where(rows == 0, carry[7:8, :], pltpu.roll(hcur, 1, axis=0))
        h2 = pltpu.roll(hcur, 2, axis=0)
        h2 = jnp.where(rows == 0, carry[6:7, :], jnp.where(rows == 1, carry[7:8, :], h2))
        carry_ref[j] = hcur[tm - 8:tm, :]
        cw = cw_ref[...]
        return cb_ref[...] + cw[0:1, :] * h2 + cw[1:2, :] * h1 + cw[2:3, :] * hcur

    u = conv(wu_ref, cwu_ref, cbu_ref, cu_ref)
    gt = conv(wg_ref, cwg_ref, cbg_ref, cg_ref)
    act = (gt * _sigmoid(gt) * u).astype(BF16)
    out_ref[...] += jnp.dot(act, wout_ref[...], preferred_element_type=F32)


def _conv_ffn(h, g, w_in_bf16, conv_w, conv_b, w_out_bf16, tm, tf):
    b, l, d = h.shape
    d_ff = w_out_bf16.shape[0]
    nj = d_ff // tf
    cw = jnp.pad(conv_w, ((0, 8 - CONV_W), (0, 0)))
    cb = conv_b.reshape(1, 2 * d_ff)
    return pl.pallas_call(
        _conv_ffn_body,
        out_shape=jax.ShapeDtypeStruct((b, l, d), F32),
        grid=(b, l // tm, nj),
        in_specs=[pl.BlockSpec((None, tm, d), lambda bi, ti, j: (bi, ti, 0)),
                  pl.BlockSpec((1, d), lambda bi, ti, j: (0, 0)),
                  pl.BlockSpec((d, tf), lambda bi, ti, j: (0, j)),
                  pl.BlockSpec((d, tf), lambda bi, ti, j: (0, nj + j)),
                  pl.BlockSpec((8, tf), lambda bi, ti, j: (0, j)),
                  pl.BlockSpec((8, tf), lambda bi, ti, j: (0, nj + j)),
                  pl.BlockSpec((1, tf), lambda bi, ti, j: (0, j)),
                  pl.BlockSpec((1, tf), lambda bi, ti, j: (0, nj + j)),
                  pl.BlockSpec((tf, d), lambda bi, ti, j: (j, 0))],
        out_specs=pl.BlockSpec((None, tm, d), lambda bi, ti, j: (bi, ti, 0)),
        scratch_shapes=[pltpu.VMEM((tm, d), BF16),
                        pltpu.VMEM((nj, 8, tf), F32), pltpu.VMEM((nj, 8, tf), F32)],
        compiler_params=_cparams(("arbitrary", "arbitrary", "arbitrary")),
        name="conv_ffn",
    )(h, g.reshape(1, d), w_in_bf16, w_in_bf16, cw, cw, cb, cb, w_out_bf16)


def _pad_cols(w, n):
    return jnp.pad(w, ((0, 0), (0, n - w.shape[1])))


def _layer(h, prm):
    (norm1_g, w_in, rw_mu, rw_w0, rw_w_up, rw_a0, rw_a_up, rw_g_up, rw_k_k, rw_k_a, rw_r_k,
     rw_gn_w, rw_gn_b, fx_b_f, fx_q_g, fx_k_g, w_o, norm2_g, ffn_w_in, ffn_conv_w, ffn_conv_b,
     ffn_w_out) = prm
    b, l, d = h.shape
    m = b * l
    rw_cols = 3 * WIDTH + DECAY_LORA + AAA_LORA + GATE_LORA

    w_rw = _pad_cols(w_in[:, :rw_cols], RW_PCOLS).astype(BF16)
    w_fx = _pad_cols(w_in[:, rw_cols:], FX_PCOLS).astype(BF16)
    mu_p = _pad_cols(rw_mu.reshape(1, rw_cols), RW_PCOLS)
    wcomb = jnp.zeros((128, 2 * WIDTH), F32)
    wcomb = wcomb.at[:DECAY_LORA, :WIDTH].set(rw_w_up).at[DECAY_LORA:, WIDTH:].set(rw_a_up)
    w0a0 = jnp.concatenate([rw_w0, rw_a0]).reshape(1, 2 * WIDTH)
    gup_p = jnp.pad(rw_g_up, ((0, GATE_PAD - GATE_LORA), (0, 0)))
    ones512 = _group_ones(WIDTH, HEAD_DIM)
    gain = jnp.concatenate([jnp.tile(fx_q_g, N_HEADS) * (HEAD_DIM ** -0.5),
                            jnp.tile(fx_k_g, N_HEADS)]).reshape(1, 2 * WIDTH)
    bf_p = jnp.pad(fx_b_f, (0, LANES - N_HEADS)).reshape(1, LANES)

    tm = _pick(m, (1024, 512, 256, 128))
    t2 = _pick(l, (544, 384, 256, 128))

    h2 = h.reshape(m, d)
    p_rw = _norm_matmul(h2, norm1_g, w_rw, tm).reshape(b, l, RW_PCOLS)
    p_fx = _norm_matmul(h2, norm1_g, w_fx, tm).reshape(b, l, FX_PCOLS)

    r, lw, k, v, a, bb, g, bonus = _rwkv_prep(
        p_rw, mu_p, wcomb, w0a0, gup_p, rw_k_k.reshape(1, WIDTH), rw_k_a.reshape(1, WIDTH),
        rw_r_k.reshape(1, WIDTH), ones512, t2)
    pm = lambda z: z.reshape(b * PAIRS, l, LANES)
    y = _rwkv_chunk(pm(r), pm(lw), pm(k), pm(v), pm(a), pm(bb), npg=4)
    y = y.reshape(b, PAIRS, l, LANES)

    qkv, c = _fox_prep(p_fx, gain, bf_p, ones512, t2)
    nc = -c[..., :N_HEADS].transpose(0, 2, 1)
    n_main = l // ATTN_BLOCK * ATTN_BLOCK
    parts = []
    if n_main:
        parts.append(_fox_attn(qkv, nc, 0, n_main, ATTN_BLOCK, ATTN_BLOCK))
    if l > n_main:
        parts.append(_fox_attn(qkv, nc, n_main, l - n_main, l - n_main, l))
    o_fx = jnp.concatenate(parts, axis=1)

    h = _mix_out(h, y, bonus, g, o_fx, p_fx, rw_gn_w.reshape(1, WIDTH), rw_gn_b.reshape(1, WIDTH),
                 ones512, w_o.astype(BF16), t2)

    d_ff = ffn_w_out.shape[0]
    tf = _pick(d_ff, (1408, 256, 128))
    return _conv_ffn(h, norm2_g, ffn_w_in.astype(BF16), ffn_conv_w, ffn_conv_b,
                     ffn_w_out.astype(BF16), t2, tf)


def kernel(x, meta, norm1_g, w_in, rw_mu, rw_w0, rw_w_up, rw_a0, rw_a_up, rw_g_up, rw_k_k, rw_k_a,
           rw_r_k, rw_gn_w, rw_gn_b, fx_b_f, fx_q_g, fx_k_g, w_o, norm2_g, ffn_w_in, ffn_conv_w,
           ffn_conv_b, ffn_w_out):
    b, seq, d = x.shape
    params = (norm1_g, w_in, rw_mu, rw_w0, rw_w_up, rw_a0, rw_a_up, rw_g_up, rw_k_k, rw_k_a, rw_r_k,
              rw_gn_w, rw_gn_b, fx_b_f, fx_q_g, fx_k_g, w_o, norm2_g, ffn_w_in, ffn_conv_w,
              ffn_conv_b, ffn_w_out)
    l = N_META + seq
    lp = -(-l // SB) * SB
    meta_b = jnp.broadcast_to(meta[None].astype(x.dtype), (b, N_META, d))
    h = jnp.concatenate([meta_b, x, jnp.zeros((b, lp - l, d), x.dtype)], axis=1)
    for layer in range(norm1_g.shape[0]):
        h = _layer(h, tuple(p[layer] for p in params))
    return h[:, N_META:l]
```

```python
import functools

import jax
import jax.numpy as jnp
from jax import lax
from jax.experimental import pallas as pl
from jax.experimental.pallas import tpu as pltpu

F32 = jnp.float32
BF16 = jnp.bfloat16

N_META = 16
HEAD_DIM = 64
N_HEADS = 8
WIDTH = N_HEADS * HEAD_DIM
DECAY_LORA = 64
AAA_LORA = 64
GATE_LORA = 160
GATE_PAD = 256
RW_PCOLS = 3 * WIDTH + 128 + GATE_PAD
FX_PCOLS = 4 * WIDTH + 128
CONV_W = 3
NORM_EPS = 1e-6
GN_EPS = HEAD_DIM * 1e-5
LANES = 128
PAIRS = WIDTH // LANES
CHUNK = 16
SB = 128
ATTN_BLOCK = 256
MASK_VALUE = -1e30
VMEM_LIMIT = 56 * 1024 * 1024


def _pick(n, cands):
    for c in cands:
        if n % c == 0:
            return c
    raise ValueError(f"no tile for {n} in {cands}")


def _cparams(sem):
    return pltpu.CompilerParams(dimension_semantics=sem, vmem_limit_bytes=VMEM_LIMIT)


def _sigmoid(x):
    return 1.0 / (1.0 + jnp.exp(-x))


def _softplus(x):
    return jnp.maximum(x, 0.0) + jnp.log(1.0 + jnp.exp(-jnp.abs(x)))


def _split2(x):
    hi = x.astype(BF16)
    lo = (x - hi.astype(F32)).astype(BF16)
    return hi, lo


def _split3(x):
    p1 = x.astype(BF16)
    rem = x - p1.astype(F32)
    p2 = rem.astype(BF16)
    p3 = (rem - p2.astype(F32)).astype(BF16)
    return p1, p2, p3


def _dot_ones(x, ones_bf16):
    hi, lo = _split2(x)
    return (jnp.dot(hi, ones_bf16, preferred_element_type=F32)
            + jnp.dot(lo, ones_bf16, preferred_element_type=F32))


def _mm3(a_hi, a_lo, b_hi, b_lo, dims=((1,), (0,))):
    dn = (dims, ((), ()))
    m_axis = 1 - dims[0][0]
    m = a_hi.shape[m_axis]
    d = lax.dot_general(jnp.concatenate([a_hi, a_lo], axis=m_axis), b_hi, dn,
                        preferred_element_type=F32)
    return d[:m] + d[m:] + lax.dot_general(a_hi, b_lo, dn, preferred_element_type=F32)


def _dot_f32(a, b):
    return _mm3(*_split2(a), *_split2(b))


def _group_ones(n, group):
    i = jnp.arange(n) // group
    return (i[:, None] == i[None, :]).astype(BF16)


def _norm_matmul_body(x_ref, g_ref, w_ref, o_ref, *, n_chunk):
    x = x_ref[...]
    ms = jnp.mean(x * x, axis=-1, keepdims=True)
    xn = (x * lax.rsqrt(ms + NORM_EPS) * g_ref[...]).astype(BF16)
    n = o_ref.shape[-1]
    for c in range(0, n, n_chunk):
        e = min(c + n_chunk, n)
        o_ref[:, c:e] = jnp.dot(xn, w_ref[:, c:e], preferred_element_type=F32)


def _norm_matmul(x2, g, w_bf16, tm):
    m, d = x2.shape
    n = w_bf16.shape[1]
    return pl.pallas_call(
        functools.partial(_norm_matmul_body, n_chunk=512),
        out_shape=jax.ShapeDtypeStruct((m, n), F32),
        grid=(m // tm,),
        in_specs=[pl.BlockSpec((tm, d), lambda i: (i, 0)),
                  pl.BlockSpec((1, d), lambda i: (0, 0)),
                  pl.BlockSpec((d, n), lambda i: (0, 0))],
        out_specs=pl.BlockSpec((tm, n), lambda i: (i, 0)),
        compiler_params=_cparams(("arbitrary",)),
        name="norm_matmul",
    )(x2, g.reshape(1, d), w_bf16)


def _rwkv_prep_body(p_ref, mu_ref, wcomb_ref, w0a0_ref, gup_ref, kk_ref, ka_ref, rk_ref, ones_ref,
                    r_o, lw_o, k_o, v_o, a_o, b_o, g_o, bonus_o, carry_ref):
    t2 = p_ref.shape[0]

    @pl.when(pl.program_id(1) == 0)
    def _():
        carry_ref[...] = jnp.zeros_like(carry_ref)

    p = p_ref[...]
    rows = lax.broadcasted_iota(jnp.int32, p.shape, 0)
    prev = jnp.where(rows == 0, carry_ref[...], pltpu.roll(p, 1, axis=0))
    carry_ref[...] = p[t2 - 1:t2, :]
    x = p + mu_ref[...] * (prev - p)

    r = x[:, 0:WIDTH]
    k = x[:, WIDTH:2 * WIDTH]
    v = x[:, 2 * WIDTH:3 * WIDTH]
    wa = x[:, 3 * WIDTH:3 * WIDTH + 128]
    gd = x[:, 3 * WIDTH + 128:]

    lane = lax.broadcasted_iota(jnp.int32, wa.shape, 1)
    xa = jnp.where(lane < DECAY_LORA, jnp.tanh(wa), wa)
    pre = _dot_f32(xa, wcomb_ref[...]) + w0a0_ref[...]
    w_log = -_softplus(-pre[:, :WIDTH]) - 0.5
    log_decay = -jnp.exp(w_log)
    a = _sigmoid(pre[:, WIDTH:])
    g = _dot_f32(_sigmoid(gd), gup_ref[...])

    ones = ones_ref[...]
    kk = k * kk_ref[...]
    ss = _dot_ones(kk * kk, ones)
    kk = kk / jnp.maximum(jnp.sqrt(ss), 1e-12)
    k2 = k * (1.0 + (a - 1.0) * ka_ref[...])
    bonus = _dot_ones(r * k2 * rk_ref[...], ones) * v

    for j in range(PAIRS):
        ls = slice(j * LANES, (j + 1) * LANES)
        r_o[j] = r[:, ls]
        lw_o[j] = log_decay[:, ls]
        k_o[j] = k2[:, ls]
        v_o[j] = v[:, ls]
        a_o[j] = -kk[:, ls]
        b_o[j] = (kk * a)[:, ls]
    g_o[...] = g
    bonus_o[...] = bonus


def _rwkv_prep(p_rw, mu_p, wcomb, w0a0, gup_p, k_k, k_a, r_k, ones512, t2):
    b, l, pc = p_rw.shape
    full = lambda shape: pl.BlockSpec(shape, lambda bi, ti: (0,) * len(shape))
    pair_sd = jax.ShapeDtypeStruct((b, PAIRS, l, LANES), F32)
    pair_spec = pl.BlockSpec((None, PAIRS, t2, LANES), lambda bi, ti: (bi, 0, ti, 0))
    row_sd = jax.ShapeDtypeStruct((b, l, WIDTH), F32)
    row_spec = pl.BlockSpec((None, t2, WIDTH), lambda bi, ti: (bi, ti, 0))
    return pl.pallas_call(
        _rwkv_prep_body,
        out_shape=[pair_sd] * 6 + [row_sd] * 2,
        grid=(b, l // t2),
        in_specs=[pl.BlockSpec((None, t2, pc), lambda bi, ti: (bi, ti, 0)),
                  full((1, pc)), full(wcomb.shape), full((1, 2 * WIDTH)), full(gup_p.shape),
                  full((1, WIDTH)), full((1, WIDTH)), full((1, WIDTH)), full((WIDTH, WIDTH))],
        out_specs=[pair_spec] * 6 + [row_spec] * 2,
        scratch_shapes=[pltpu.VMEM((1, pc), F32)],
        compiler_params=_cparams(("arbitrary", "arbitrary")),
        name="rwkv_prep",
    )(p_rw, mu_p, wcomb, w0a0, gup_p, k_k, k_a, r_k, ones512)


def _rwkv_chunk_body(r_ref, lw_ref, k_ref, v_ref, a_ref, b_ref, y_ref, h_ref):
    npg = r_ref.shape[0]
    n_chunks = SB // CHUNK

    @pl.when(pl.program_id(1) == 0)
    def _():
        h_ref[...] = jnp.zeros_like(h_ref)

    ti = lax.broadcasted_iota(jnp.int32, (SB, SB), 0)
    si = lax.broadcasted_iota(jnp.int32, (SB, SB), 1)
    same = (ti // CHUNK) == (si // CHUNK)
    incl = same & (si <= ti)
    strict = same & (si < ti)
    eye = ti == si
    same_head = (ti < HEAD_DIM) == (si < HEAD_DIM)
    head0 = lax.broadcasted_iota(jnp.int32, (SB, LANES), 1) < HEAD_DIM
    tri_ones = jnp.concatenate([incl.astype(BF16), same.astype(BF16)], axis=0)
    eye_f = eye.astype(F32)
    bf = lambda z: z.astype(BF16)
    dot = functools.partial(jnp.dot, preferred_element_type=F32)

    def keep(mask, x):
        return jnp.where(mask, x, jnp.zeros_like(x))

    pairs = range(npg)
    units = [(p, h) for p in pairs for h in range(2)]
    cat0 = lambda *xs: jnp.concatenate(xs, axis=0)
    mm = lambda a, b, **kw: _mm3(*a, *b, **kw)

    lw = [lw_ref[p] for p in pairs]
    cc = [sum(jnp.dot(tri_ones, piece, preferred_element_type=F32) for piece in _split3(lw[p]))
          for p in pairs]
    cs = [cc[p][:SB] for p in pairs]
    ce = [cc[p][SB:] for p in pairs]
    g = [jnp.exp(cs[p]) for p in pairs]
    ginv = [jnp.exp(-cs[p]) for p in pairs]
    gend = [jnp.exp(ce[p] - cs[p]) for p in pairs]
    g_c = [jnp.exp(ce[p]) for p in pairs]
    at = [a_ref[p] * jnp.exp(cs[p] - lw[p]) for p in pairs]
    rt = [r_ref[p] * g[p] for p in pairs]
    v_b = [bf(v_ref[p]) for p in pairs]
    at_b = [bf(at[p]) for p in pairs]
    rt_b = [bf(rt[p]) for p in pairs]
    bk_t = [cat0(bf(b_ref[p] * ginv[p]), bf(k_ref[p] * ginv[p])) for p in pairs]
    bh_b = [bf(b_ref[p] * gend[p]) for p in pairs]
    kh_b = [bf(k_ref[p] * gend[p]) for p in pairs]

    gram = [lax.dot_general(cat0(keep(head0, at_b[p]), keep(head0, rt_b[p]),
                                 keep(~head0, at_b[p]), keep(~head0, rt_b[p])), bk_t[p],
                            (((1,), (1,)), ((), ())), preferred_element_type=F32)
            for p in pairs]
    gb = {(p, h): gram[p][2 * SB * h:2 * SB * (h + 1)] for p, h in units}
    mab = {u: keep(strict, gb[u][:SB, :SB]) for u in units}
    mkk_b = {u: bf(cat0(keep(strict, gb[u][:SB, SB:]), keep(incl, gb[u][SB:, SB:]))) for u in units}
    mrb_b = {u: bf(keep(incl, gb[u][SB:, :SB])) for u in units}

    n1 = {u: _split2(mab[u]) for u in units}
    n2 = {u: _split2(mm(n1[u], n1[u])) for u in units}
    n4 = {u: _split2(mm(n2[u], n2[u])) for u in units}
    n8 = {u: _split2(mm(n4[u], n4[u])) for u in units}
    t = {u: eye_f + mab[u] for u in units}
    for npow in (n2, n4, n8):
        t = {u: t[u] + mm(npow[u], _split2(t[u])) for u in units}
    wk = {u: dot(mkk_b[u], v_b[u[0]]) for u in units}
    x = {u: dot(bf(t[u]), jnp.concatenate([at_b[u[0]], bf(wk[u][:SB])], axis=1)) for u in units}
    ry = {u: dot(mrb_b[u], bf(x[u])) for u in units}
    both = lambda f: [jnp.where(head0, f((p, 0)), f((p, 1))) for p in pairs]
    abar_b = [bf(z) for z in both(lambda u: x[u][:, :LANES])]
    vbar = both(lambda u: x[u][:, LANES:])
    rbar_b = [bf(z) for z in both(lambda u: rt[u[0]] + ry[u][:, :LANES])]
    ybar = both(lambda u: wk[u][SB:] + ry[u][:, LANES:])

    g_ct = [g_c[p].T for p in pairs]

    hbd = [h_ref[p] for p in pairs]
    for c in range(n_chunks):
        sl = slice(c * CHUNK, (c + 1) * CHUNK)
        yu = [dot(cat0(rbar_b[p][sl], abar_b[p][sl]), bf(hbd[p])) for p in pairs]
        for p in pairs:
            y_ref[p, sl, :] = yu[p][:CHUNK] + ybar[p][sl]
        upd = [lax.dot_general(cat0(bh_b[p][sl], kh_b[p][sl]),
                               cat0(bf(yu[p][CHUNK:] + vbar[p][sl]), v_b[p][sl]),
                               (((0,), (0,)), ((), ())), preferred_element_type=F32)
               for p in pairs]
        hbd = [jnp.broadcast_to(g_ct[p][:, c * CHUNK:c * CHUNK + 1], (LANES, LANES)) * hbd[p]
               + keep(same_head, upd[p]) for p in pairs]
    for p in pairs:
        h_ref[p] = hbd[p]


def _rwkv_chunk(r, lw, k, v, a, b, npg):
    n_pairs, l, _ = r.shape
    blk = pl.BlockSpec((npg, SB, LANES), lambda gi, ti: (gi, ti, 0))
    return pl.pallas_call(
        _rwkv_chunk_body,
        out_shape=jax.ShapeDtypeStruct((n_pairs, l, LANES), F32),
        grid=(n_pairs // npg, l // SB),
        in_specs=[blk] * 6,
        out_specs=blk,
        scratch_shapes=[pltpu.VMEM((npg, LANES, LANES), F32)],
        compiler_params=_cparams(("arbitrary", "arbitrary")),
        name="rwkv_chunk",
    )(r, lw, k, v, a, b)


def _fox_prep_body(qkv_ref, fl_ref, gain_ref, bf_ref, ones_ref, qkv_o, c_o, carry_ref):
    t2 = qkv_ref.shape[0]

    @pl.when(pl.program_id(1) == 0)
    def _():
        carry_ref[...] = jnp.zeros_like(carry_ref)

    ones = ones_ref[...]
    for h0 in (0, WIDTH):
        x = qkv_ref[:, h0:h0 + WIDTH]
        ms = _dot_ones(x * x, ones) * (1.0 / HEAD_DIM)
        qkv_o[:, h0:h0 + WIDTH] = (x * lax.rsqrt(ms + NORM_EPS) * gain_ref[:, h0:h0 + WIDTH]).astype(qkv_o.dtype)
    qkv_o[:, 2 * WIDTH:] = qkv_ref[:, 2 * WIDTH:].astype(qkv_o.dtype)

    logf = -_softplus(-(fl_ref[...] + bf_ref[...]))
    ri = lax.broadcasted_iota(jnp.int32, (t2, t2), 0)
    ci = lax.broadcasted_iota(jnp.int32, (t2, t2), 1)
    tri = (ci <= ri).astype(BF16)
    c = sum(jnp.dot(tri, piece, preferred_element_type=F32) for piece in _split3(logf)) + carry_ref[...]
    c_o[...] = c
    carry_ref[...] = c[t2 - 1:t2, :]


def _fox_prep(p_fx, gain, bf_p, ones512, t2):
    b, l, _ = p_fx.shape
    full = lambda shape: pl.BlockSpec(shape, lambda bi, ti: (0,) * len(shape))
    return pl.pallas_call(
        _fox_prep_body,
        out_shape=[jax.ShapeDtypeStruct((b, l, 3 * WIDTH), BF16),
                   jax.ShapeDtypeStruct((b, l, LANES), F32)],
        grid=(b, l // t2),
        in_specs=[pl.BlockSpec((None, t2, 3 * WIDTH), lambda bi, ti: (bi, ti, 0)),
                  pl.BlockSpec((None, t2, LANES), lambda bi, ti: (bi, ti, 4 * WIDTH // LANES)),
                  full((1, 2 * WIDTH)), full((1, LANES)), full((WIDTH, WIDTH))],
        out_specs=[pl.BlockSpec((None, t2, 3 * WIDTH), lambda bi, ti: (bi, ti, 0)),
                   pl.BlockSpec((None, t2, LANES), lambda bi, ti: (bi, ti, 0))],
        scratch_shapes=[pltpu.VMEM((1, LANES), F32)],
        compiler_params=_cparams(("arbitrary", "arbitrary")),
        name="fox_prep",
    )(p_fx, p_fx, gain, bf_p, ones512)


def _fox_attn_body(q_ref, k_ref, v_ref, nc_ref, o_ref, m_ref, l_ref, acc_ref, *, q_row0):
    tq = q_ref.shape[0]
    tk = k_ref.shape[0]
    qi = pl.program_id(1)
    kj = pl.program_id(2)
    row_first = q_row0 + qi * tq
    col_first = kj * tk
    last_kj = (row_first + tq - 1) // tk

    @pl.when(kj == 0)
    def _():
        m_ref[...] = jnp.full_like(m_ref, MASK_VALUE)
        l_ref[...] = jnp.zeros_like(l_ref)
        acc_ref[...] = jnp.zeros_like(acc_ref)

    head0 = lax.broadcasted_iota(jnp.int32, (1, LANES), 1) < HEAD_DIM
    ones = jnp.ones((tk, LANES), BF16)

    def block(masked):
        if masked:
            rowpos = row_first + lax.broadcasted_iota(jnp.int32, (tq, tk), 0)
            colpos = col_first + lax.broadcasted_iota(jnp.int32, (tq, tk), 1)
            causal = colpos <= rowpos
        for j in range(PAIRS):
            ls = slice(j * LANES, (j + 1) * LANES)
            q = q_ref[:, ls]
            k = k_ref[:, ls]
            v = v_ref[:, ls]
            zero = jnp.zeros_like(q)
            q2 = jnp.concatenate([jnp.where(head0, q, zero), jnp.where(head0, zero, q)], axis=0)
            s = lax.dot_general(q2, k, (((1,), (1,)), ((), ())), preferred_element_type=F32)
            s0 = s[:tq] + nc_ref[2 * j:2 * j + 1, :]
            s1 = s[tq:] + nc_ref[2 * j + 1:2 * j + 2, :]
            if masked:
                s0 = jnp.where(causal, s0, MASK_VALUE)
                s1 = jnp.where(causal, s1, MASK_VALUE)
            zv = jnp.zeros_like(v)
            vsel = (jnp.where(head0, v, zv), jnp.where(head0, zv, v))
            alphas, pvs = [], []
            for hh, sh in enumerate((s0, s1)):
                m_prev = m_ref[2 * j + hh]
                m_new = jnp.maximum(m_prev, jnp.max(sh, axis=1, keepdims=True))
                p = jnp.exp(sh - jnp.concatenate([m_new] * (tk // LANES), axis=1))
                alpha = jnp.exp(m_prev - m_new)
                pv = jnp.dot(p.astype(BF16), jnp.concatenate([vsel[hh], ones], axis=1),
                             preferred_element_type=F32)
                l_ref[2 * j + hh] = alpha * l_ref[2 * j + hh] + pv[:, LANES:]
                m_ref[2 * j + hh] = m_new
                alphas.append(alpha)
                pvs.append(pv[:, :LANES])
            acc_ref[j] = jnp.where(head0, alphas[0], alphas[1]) * acc_ref[j] + pvs[0] + pvs[1]

    needed = col_first <= row_first + tq - 1
    crosses = col_first + tk - 1 > row_first

    @pl.when(needed & crosses)
    def _():
        block(True)

    @pl.when(needed & jnp.logical_not(crosses))
    def _():
        block(False)

    @pl.when(kj == last_kj)
    def _():
        for j in range(PAIRS):
            o_ref[:, j * LANES:(j + 1) * LANES] = acc_ref[j] / jnp.where(head0, l_ref[2 * j], l_ref[2 * j + 1])


def _fox_attn(qkv, nc, q_row0, n_rows, tq, tk):
    b, l, _ = qkv.shape
    nq = n_rows // tq
    qb0 = q_row0 // tq
    kv_last = lambda qi: (q_row0 + (qi + 1) * tq - 1) // tk
    nk = kv_last(nq - 1) + 1
    kcol, vcol = 1, 2
    return pl.pallas_call(
        functools.partial(_fox_attn_body, q_row0=q_row0),
        out_shape=jax.ShapeDtypeStruct((b, n_rows, WIDTH), F32),
        grid=(b, nq, nk),
        in_specs=[pl.BlockSpec((None, tq, WIDTH), lambda bi, qi, kj: (bi, qb0 + qi, 0)),
                  pl.BlockSpec((None, tk, WIDTH), lambda bi, qi, kj: (bi, jnp.minimum(kj, kv_last(qi)), kcol)),
                  pl.BlockSpec((None, tk, WIDTH), lambda bi, qi, kj: (bi, jnp.minimum(kj, kv_last(qi)), vcol)),
                  pl.BlockSpec((None, N_HEADS, tk), lambda bi, qi, kj: (bi, 0, jnp.minimum(kj, kv_last(qi))))],
        out_specs=pl.BlockSpec((None, tq, WIDTH), lambda bi, qi, kj: (bi, qi, 0)),
        scratch_shapes=[pltpu.VMEM((N_HEADS, tq, LANES), F32), pltpu.VMEM((N_HEADS, tq, LANES), F32),
                        pltpu.VMEM((PAIRS, tq, LANES), F32)],
        compiler_params=_cparams(("arbitrary", "arbitrary", "arbitrary")),
        name="fox_attn",
    )(qkv, qkv, qkv, nc)


def _mix_out_body(h_ref, y_ref, bonus_ref, g_ref, o_ref, og_ref, gnw_ref, gnb_ref, ones_ref, wo_ref, out_ref):
    ones = ones_ref[...]
    y = jnp.concatenate([y_ref[j] for j in range(PAIRS)], axis=1)
    mean = _dot_ones(y, ones) * (1.0 / HEAD_DIM)
    d = y - mean
    var = _dot_ones(d * d, ones) * (1.0 / HEAD_DIM)
    yn = d * lax.rsqrt(var + GN_EPS) * gnw_ref[...] + gnb_ref[...]
    y_rw = ((yn + bonus_ref[...]) * g_ref[...]).astype(BF16)
    y_fx = (o_ref[...] * _sigmoid(og_ref[...])).astype(BF16)
    out_ref[...] = (h_ref[...]
                    + jnp.dot(y_rw, wo_ref[0:WIDTH, :], preferred_element_type=F32)
                    + jnp.dot(y_fx, wo_ref[WIDTH:2 * WIDTH, :], preferred_element_type=F32))


def _mix_out(h, y, bonus, g, o_fx, p_fx, gn_w, gn_b, ones512, wo_bf16, tm):
    b, l, d = h.shape
    full = lambda shape: pl.BlockSpec(shape, lambda bi, ti: (0,) * len(shape))
    rows = pl.BlockSpec((None, tm, WIDTH), lambda bi, ti: (bi, ti, 0))
    return pl.pallas_call(
        _mix_out_body,
        out_shape=jax.ShapeDtypeStruct((b, l, d), F32),
        grid=(b, l // tm),
        in_specs=[pl.BlockSpec((None, tm, d), lambda bi, ti: (bi, ti, 0)),
                  pl.BlockSpec((None, PAIRS, tm, LANES), lambda bi, ti: (bi, 0, ti, 0)),
                  rows, rows, rows,
                  pl.BlockSpec((None, tm, WIDTH), lambda bi, ti: (bi, ti, 3)),
                  full((1, WIDTH)), full((1, WIDTH)), full((WIDTH, WIDTH)), full((2 * WIDTH, d))],
        out_specs=pl.BlockSpec((None, tm, d), lambda bi, ti: (bi, ti, 0)),
        compiler_params=_cparams(("arbitrary", "arbitrary")),
        name="mix_out",
    )(h, y, bonus, g, o_fx, p_fx, gn_w, gn_b, ones512, wo_bf16)


def _conv_ffn_body(h_ref, g_ref, wu_ref, wg_ref, cwu_ref, cwg_ref, cbu_ref, cbg_ref, wout_ref,
                   out_ref, xn_ref, cu_ref, cg_ref):
    tm = h_ref.shape[0]
    ti = pl.program_id(1)
    j = pl.program_id(2)

    @pl.when(j == 0)
    def _():
        x = h_ref[...]
        ms = jnp.mean(x * x, axis=-1, keepdims=True)
        xn_ref[...] = (x * lax.rsqrt(ms + NORM_EPS) * g_ref[...]).astype(BF16)
        out_ref[...] = x

    @pl.when(ti == 0)
    def _():
        cu_ref[j] = jnp.zeros(cu_ref.shape[1:], F32)
        cg_ref[j] = jnp.zeros(cg_ref.shape[1:], F32)

    xn = xn_ref[...]

    def conv(w_ref, cw_ref, cb_ref, carry_ref):
        hcur = jnp.dot(xn, w_ref[...], preferred_element_type=F32)
        rows = lax.broadcasted_iota(jnp.int32, hcur.shape, 0)
        carry = carry_ref[j]
        h1 = jnp.where(rows == 0, carry[7:8, :], pltpu.roll(hcur, 1, axis=0))
        h2 = pltpu.roll(hcur, 2, axis=0)
        h2 = jnp.where(rows == 0, carry[6:7, :], jnp.where(rows == 1, carry[7:8, :], h2))
        carry_ref[j] = hcur[tm - 8:tm, :]
        cw = cw_ref[...]
        return cb_ref[...] + cw[0:1, :] * h2 + cw[1:2, :] * h1 + cw[2:3, :] * hcur

    u = conv(wu_ref, cwu_ref, cbu_ref, cu_ref)
    gt = conv(wg_ref, cwg_ref, cbg_ref, cg_ref)
    act = (gt * _sigmoid(gt) * u).astype(BF16)
    out_ref[...] += jnp.dot(act, wout_ref[...], preferred_element_type=F32)


def _conv_ffn(h, g, w_in_bf16, conv_w, conv_b, w_out_bf16, tm, tf):
    b, l, d = h.shape
    d_ff = w_out_bf16.shape[0]
    nj = d_ff // tf
    cw = jnp.pad(conv_w, ((0, 8 - CONV_W), (0, 0)))
    cb = conv_b.reshape(1, 2 * d_ff)
    return pl.pallas_call(
        _conv_ffn_body,
        out_shape=jax.ShapeDtypeStruct((b, l, d), F32),
        grid=(b, l // tm, nj),
        in_specs=[pl.BlockSpec((None, tm, d), lambda bi, ti, j: (bi, ti, 0)),
                  pl.BlockSpec((1, d), lambda bi, ti, j: (0, 0)),
                  pl.BlockSpec((d, tf), lambda bi, ti, j: (0, j)),
                  pl.BlockSpec((d, tf), lambda bi, ti, j: (0, nj + j)),
                  pl.BlockSpec((8, tf), lambda bi, ti, j: (0, j)),
                  pl.BlockSpec((8, tf), lambda bi, ti, j: (0, nj + j)),
                  pl.BlockSpec((1, tf), lambda bi, ti, j: (0, j)),
                  pl.BlockSpec((1, tf), lambda bi, ti, j: (0, nj + j)),
                  pl.BlockSpec((tf, d), lambda bi, ti, j: (j, 0))],
        out_specs=pl.BlockSpec((None, tm, d), lambda bi, ti, j: (bi, ti, 0)),
        scratch_shapes=[pltpu.VMEM((tm, d), BF16),
                        pltpu.VMEM((nj, 8, tf), F32), pltpu.VMEM((nj, 8, tf), F32)],
        compiler_params=_cparams(("arbitrary", "arbitrary", "arbitrary")),
        name="conv_ffn",
    )(h, g.reshape(1, d), w_in_bf16, w_in_bf16, cw, cw, cb, cb, w_out_bf16)


def _pad_cols(w, n):
    return jnp.pad(w, ((0, 0), (0, n - w.shape[1])))


def _layer(h, prm):
    (norm1_g, w_in, rw_mu, rw_w0, rw_w_up, rw_a0, rw_a_up, rw_g_up, rw_k_k, rw_k_a, rw_r_k,
     rw_gn_w, rw_gn_b, fx_b_f, fx_q_g, fx_k_g, w_o, norm2_g, ffn_w_in, ffn_conv_w, ffn_conv_b,
     ffn_w_out) = prm
    b, l, d = h.shape
    m = b * l
    rw_cols = 3 * WIDTH + DECAY_LORA + AAA_LORA + GATE_LORA

    w_rw = _pad_cols(w_in[:, :rw_cols], RW_PCOLS).astype(BF16)
    w_fx = _pad_cols(w_in[:, rw_cols:], FX_PCOLS).astype(BF16)
    mu_p = _pad_cols(rw_mu.reshape(1, rw_cols), RW_PCOLS)
    wcomb = jnp.zeros((128, 2 * WIDTH), F32)
    wcomb = wcomb.at[:DECAY_LORA, :WIDTH].set(rw_w_up).at[DECAY_LORA:, WIDTH:].set(rw_a_up)
    w0a0 = jnp.concatenate([rw_w0, rw_a0]).reshape(1, 2 * WIDTH)
    gup_p = jnp.pad(rw_g_up, ((0, GATE_PAD - GATE_LORA), (0, 0)))
    ones512 = _group_ones(WIDTH, HEAD_DIM)
    gain = jnp.concatenate([jnp.tile(fx_q_g, N_HEADS) * (HEAD_DIM ** -0.5),
                            jnp.tile(fx_k_g, N_HEADS)]).reshape(1, 2 * WIDTH)
    bf_p = jnp.pad(fx_b_f, (0, LANES - N_HEADS)).reshape(1, LANES)

    tm = _pick(m, (1024, 512, 256, 128))
    t2 = _pick(l, (544, 384, 256, 128))

    h2 = h.reshape(m, d)
    p_rw = _norm_matmul(h2, norm1_g, w_rw, tm).reshape(b, l, RW_PCOLS)
    p_fx = _norm_matmul(h2, norm1_g, w_fx, tm).reshape(b, l, FX_PCOLS)

    r, lw, k, v, a, bb, g, bonus = _rwkv_prep(
        p_rw, mu_p, wcomb, w0a0, gup_p, rw_k_k.reshape(1, WIDTH), rw_k_a.reshape(1, WIDTH),
        rw_r_k.reshape(1, WIDTH), ones512, t2)
    pm = lambda z: z.reshape(b * PAIRS, l, LANES)
    y = _rwkv_chunk(pm(r), pm(lw), pm(k), pm(v), pm(a), pm(bb), npg=8)
    y = y.reshape(b, PAIRS, l, LANES)

    qkv, c = _fox_prep(p_fx, gain, bf_p, ones512, t2)
    nc = -c[..., :N_HEADS].transpose(0, 2, 1)
    n_main = l // ATTN_BLOCK * ATTN_BLOCK
    parts = []
    if n_main:
        parts.append(_fox_attn(qkv, nc, 0, n_main, ATTN_BLOCK, ATTN_BLOCK))
    if l > n_main:
        parts.append(_fox_attn(qkv, nc, n_main, l - n_main, l - n_main, l))
    o_fx = jnp.concatenate(parts, axis=1)

    h = _mix_out(h, y, bonus, g, o_fx, p_fx, rw_gn_w.reshape(1, WIDTH), rw_gn_b.reshape(1, WIDTH),
                 ones512, w_o.astype(BF16), t2)

    d_ff = ffn_w_out.shape[0]
    tf = _pick(d_ff, (1408, 256, 128))
    return _conv_ffn(h, norm2_g, ffn_w_in.astype(BF16), ffn_conv_w, ffn_conv_b,
                     ffn_w_out.astype(BF16), t2, tf)


def kernel(x, meta, norm1_g, w_in, rw_mu, rw_w0, rw_w_up, rw_a0, rw_a_up, rw_g_up, rw_k_k, rw_k_a,
           rw_r_k, rw_gn_w, rw_gn_b, fx_b_f, fx_q_g, fx_k_g, w_o, norm2_g, ffn_w_in, ffn_conv_w,
           ffn_conv_b, ffn_w_out):
    b, seq, d = x.shape
    params = (norm1_g, w_in, rw_mu, rw_w0, rw_w_up, rw_a0, rw_a_up, rw_g_up, rw_k_k, rw_k_a, rw_r_k,
              rw_gn_w, rw_gn_b, fx_b_f, fx_q_g, fx_k_g, w_o, norm2_g, ffn_w_in, ffn_conv_w,
              ffn_conv_b, ffn_w_out)
    l = N_META + seq
    lp = -(-l // SB) * SB
    meta_b = jnp.broadcast_to(meta[None].astype(x.dtype), (b, N_META, d))
    h = jnp.concatenate([meta_b, x, jnp.zeros((b, lp - l, d), x.dtype)], axis=1)
    for layer in range(norm1_g.shape[0]):
        h = _layer(h, tuple(p[layer] for p in params))
    return h[:, N_META:l]
```

```python
import functools

import jax
import jax.numpy as jnp
from jax import lax
from jax.experimental import pallas as pl
from jax.experimental.pallas import tpu as pltpu

F32 = jnp.float32
BF16 = jnp.bfloat16

N_META = 16
HEAD_DIM = 64
N_HEADS = 8
WIDTH = N_HEADS * HEAD_DIM
DECAY_LORA = 64
AAA_LORA = 64
GATE_LORA = 160
GATE_PAD = 256
RW_PCOLS = 3 * WIDTH + 128 + GATE_PAD
FX_PCOLS = 4 * WIDTH + 128
CONV_W = 3
NORM_EPS = 1e-6
GN_EPS = HEAD_DIM * 1e-5
LANES = 128
PAIRS = WIDTH // LANES
CHUNK = 16
SB = 128
ATTN_BLOCK = 256
BIAS_LANES = 3
LOG2E = 1.4426950408889634
MASK_VALUE = -1e30
VMEM_LIMIT = 56 * 1024 * 1024


def _pick(n, cands):
    for c in cands:
        if n % c == 0:
            return c
    raise ValueError(f"no tile for {n} in {cands}")


def _cparams(sem):
    return pltpu.CompilerParams(dimension_semantics=sem, vmem_limit_bytes=VMEM_LIMIT)


def _sigmoid(x):
    return 1.0 / (1.0 + jnp.exp(-x))


def _softplus(x):
    return jnp.maximum(x, 0.0) + jnp.log(1.0 + jnp.exp(-jnp.abs(x)))


def _split2(x):
    hi = x.astype(BF16)
    lo = (x - hi.astype(F32)).astype(BF16)
    return hi, lo


def _split3(x):
    p1 = x.astype(BF16)
    rem = x - p1.astype(F32)
    p2 = rem.astype(BF16)
    p3 = (rem - p2.astype(F32)).astype(BF16)
    return p1, p2, p3


def _dot_ones(x, ones_bf16):
    hi, lo = _split2(x)
    return (jnp.dot(hi, ones_bf16, preferred_element_type=F32)
            + jnp.dot(lo, ones_bf16, preferred_element_type=F32))


def _mm3(a_hi, a_lo, b_hi, b_lo, dims=((1,), (0,))):
    dn = (dims, ((), ()))
    m_axis = 1 - dims[0][0]
    m = a_hi.shape[m_axis]
    d = lax.dot_general(jnp.concatenate([a_hi, a_lo], axis=m_axis), b_hi, dn,
                        preferred_element_type=F32)
    return d[:m] + d[m:] + lax.dot_general(a_hi, b_lo, dn, preferred_element_type=F32)


def _dot_f32(a, b):
    return _mm3(*_split2(a), *_split2(b))


def _group_ones(n, group):
    i = jnp.arange(n) // group
    return (i[:, None] == i[None, :]).astype(BF16)


def _norm_matmul_body(x_ref, g_ref, w_ref, o_ref, *, n_chunk):
    x = x_ref[...]
    ms = jnp.mean(x * x, axis=-1, keepdims=True)
    xn = (x * lax.rsqrt(ms + NORM_EPS) * g_ref[...]).astype(BF16)
    n = o_ref.shape[-1]
    for c in range(0, n, n_chunk):
        e = min(c + n_chunk, n)
        o_ref[:, c:e] = jnp.dot(xn, w_ref[:, c:e], preferred_element_type=F32)


def _norm_matmul(x2, g, w_bf16, tm):
    m, d = x2.shape
    n = w_bf16.shape[1]
    return pl.pallas_call(
        functools.partial(_norm_matmul_body, n_chunk=512),
        out_shape=jax.ShapeDtypeStruct((m, n), F32),
        grid=(m // tm,),
        in_specs=[pl.BlockSpec((tm, d), lambda i: (i, 0)),
                  pl.BlockSpec((1, d), lambda i: (0, 0)),
                  pl.BlockSpec((d, n), lambda i: (0, 0))],
        out_specs=pl.BlockSpec((tm, n), lambda i: (i, 0)),
        compiler_params=_cparams(("arbitrary",)),
        name="norm_matmul",
    )(x2, g.reshape(1, d), w_bf16)


def _rwkv_prep_body(p_ref, mu_ref, wcomb_ref, w0a0_ref, gup_ref, kk_ref, ka_ref, rk_ref, ones_ref,
                    r_o, lw_o, k_o, v_o, a_o, b_o, g_o, bonus_o, carry_ref):
    t2 = p_ref.shape[0]

    @pl.when(pl.program_id(1) == 0)
    def _():
        carry_ref[...] = jnp.zeros_like(carry_ref)

    p = p_ref[...]
    rows = lax.broadcasted_iota(jnp.int32, p.shape, 0)
    prev = jnp.where(rows == 0, carry_ref[...], pltpu.roll(p, 1, axis=0))
    carry_ref[...] = p[t2 - 1:t2, :]
    x = p + mu_ref[...] * (prev - p)

    r = x[:, 0:WIDTH]
    k = x[:, WIDTH:2 * WIDTH]
    v = x[:, 2 * WIDTH:3 * WIDTH]
    wa = x[:, 3 * WIDTH:3 * WIDTH + 128]
    gd = x[:, 3 * WIDTH + 128:]

    lane = lax.broadcasted_iota(jnp.int32, wa.shape, 1)
    xa = jnp.where(lane < DECAY_LORA, jnp.tanh(wa), wa)
    pre = _dot_f32(xa, wcomb_ref[...]) + w0a0_ref[...]
    w_log = -_softplus(-pre[:, :WIDTH]) - 0.5
    log_decay = -jnp.exp(w_log)
    a = _sigmoid(pre[:, WIDTH:])
    g = _dot_f32(_sigmoid(gd), gup_ref[...])

    ones = ones_ref[...]
    kk = k * kk_ref[...]
    ss = _dot_ones(kk * kk, ones)
    kk = kk / jnp.maximum(jnp.sqrt(ss), 1e-12)
    k2 = k * (1.0 + (a - 1.0) * ka_ref[...])
    bonus = _dot_ones(r * k2 * rk_ref[...], ones) * v

    for j in range(PAIRS):
        ls = slice(j * LANES, (j + 1) * LANES)
        r_o[j] = r[:, ls]
        lw_o[j] = log_decay[:, ls]
        k_o[j] = k2[:, ls]
        v_o[j] = v[:, ls]
        a_o[j] = -kk[:, ls]
        b_o[j] = (kk * a)[:, ls]
    g_o[...] = g
    bonus_o[...] = bonus


def _rwkv_prep(p_rw, mu_p, wcomb, w0a0, gup_p, k_k, k_a, r_k, ones512, t2):
    b, l, pc = p_rw.shape
    full = lambda shape: pl.BlockSpec(shape, lambda bi, ti: (0,) * len(shape))
    pair_sd = jax.ShapeDtypeStruct((b, PAIRS, l, LANES), F32)
    pair_spec = pl.BlockSpec((None, PAIRS, t2, LANES), lambda bi, ti: (bi, 0, ti, 0))
    row_sd = jax.ShapeDtypeStruct((b, l, WIDTH), F32)
    row_spec = pl.BlockSpec((None, t2, WIDTH), lambda bi, ti: (bi, ti, 0))
    return pl.pallas_call(
        _rwkv_prep_body,
        out_shape=[pair_sd] * 6 + [row_sd] * 2,
        grid=(b, l // t2),
        in_specs=[pl.BlockSpec((None, t2, pc), lambda bi, ti: (bi, ti, 0)),
                  full((1, pc)), full(wcomb.shape), full((1, 2 * WIDTH)), full(gup_p.shape),
                  full((1, WIDTH)), full((1, WIDTH)), full((1, WIDTH)), full((WIDTH, WIDTH))],
        out_specs=[pair_spec] * 6 + [row_spec] * 2,
        scratch_shapes=[pltpu.VMEM((1, pc), F32)],
        compiler_params=_cparams(("arbitrary", "arbitrary")),
        name="rwkv_prep",
    )(p_rw, mu_p, wcomb, w0a0, gup_p, k_k, k_a, r_k, ones512)


def _rwkv_chunk_body(r_ref, lw_ref, k_ref, v_ref, a_ref, b_ref, y_ref, h_ref):
    npg = r_ref.shape[0]
    n_chunks = SB // CHUNK

    @pl.when(pl.program_id(1) == 0)
    def _():
        h_ref[...] = jnp.zeros_like(h_ref)

    ti = lax.broadcasted_iota(jnp.int32, (SB, SB), 0)
    si = lax.broadcasted_iota(jnp.int32, (SB, SB), 1)
    same = (ti // CHUNK) == (si // CHUNK)
    incl = same & (si <= ti)
    strict = same & (si < ti)
    eye = ti == si
    same_head = (ti < HEAD_DIM) == (si < HEAD_DIM)
    head0 = lax.broadcasted_iota(jnp.int32, (SB, LANES), 1) < HEAD_DIM
    tri_ones = jnp.concatenate([incl.astype(BF16), same.astype(BF16)], axis=0)
    eye_f = eye.astype(F32)
    bf = lambda z: z.astype(BF16)
    dot = functools.partial(jnp.dot, preferred_element_type=F32)

    def keep(mask, x):
        return jnp.where(mask, x, jnp.zeros_like(x))

    pairs = range(npg)
    units = [(p, h) for p in pairs for h in range(2)]
    cat0 = lambda *xs: jnp.concatenate(xs, axis=0)
    mm = lambda a, b, **kw: _mm3(*a, *b, **kw)

    lw = [lw_ref[p] for p in pairs]
    cc = [sum(jnp.dot(tri_ones, piece, preferred_element_type=F32) for piece in _split3(lw[p]))
          for p in pairs]
    cs = [cc[p][:SB] for p in pairs]
    ce = [cc[p][SB:] for p in pairs]
    g = [jnp.exp(cs[p]) for p in pairs]
    ginv = [jnp.exp(-cs[p]) for p in pairs]
    gend = [jnp.exp(ce[p] - cs[p]) for p in pairs]
    g_c = [jnp.exp(ce[p]) for p in pairs]
    at = [a_ref[p] * jnp.exp(cs[p] - lw[p]) for p in pairs]
    rt = [r_ref[p] * g[p] for p in pairs]
    v_b = [bf(v_ref[p]) for p in pairs]
    at_b = [bf(at[p]) for p in pairs]
    rt_b = [bf(rt[p]) for p in pairs]
    bk_t = [cat0(bf(b_ref[p] * ginv[p]), bf(k_ref[p] * ginv[p])) for p in pairs]
    bh_b = [bf(b_ref[p] * gend[p]) for p in pairs]
    kh_b = [bf(k_ref[p] * gend[p]) for p in pairs]

    gram = [lax.dot_general(cat0(keep(head0, at_b[p]), keep(head0, rt_b[p]),
                                 keep(~head0, at_b[p]), keep(~head0, rt_b[p])), bk_t[p],
                            (((1,), (1,)), ((), ())), preferred_element_type=F32)
            for p in pairs]
    gb = {(p, h): gram[p][2 * SB * h:2 * SB * (h + 1)] for p, h in units}
    mab = {u: keep(strict, gb[u][:SB, :SB]) for u in units}
    mkk_b = {u: bf(cat0(keep(strict, gb[u][:SB, SB:]), keep(incl, gb[u][SB:, SB:]))) for u in units}
    mrb_b = {u: bf(keep(incl, gb[u][SB:, :SB])) for u in units}

    n1 = {u: _split2(mab[u]) for u in units}
    n2 = {u: _split2(mm(n1[u], n1[u])) for u in units}
    n4 = {u: _split2(mm(n2[u], n2[u])) for u in units}
    n8 = {u: _split2(mm(n4[u], n4[u])) for u in units}
    t = {u: eye_f + mab[u] for u in units}
    for npow in (n2, n4, n8):
        t = {u: t[u] + mm(npow[u], _split2(t[u])) for u in units}
    wk = {u: dot(mkk_b[u], v_b[u[0]]) for u in units}
    x = {u: dot(bf(t[u]), jnp.concatenate([at_b[u[0]], bf(wk[u][:SB])], axis=1)) for u in units}
    ry = {u: dot(mrb_b[u], bf(x[u])) for u in units}
    both = lambda f: [jnp.where(head0, f((p, 0)), f((p, 1))) for p in pairs]
    abar_b = [bf(z) for z in both(lambda u: x[u][:, :LANES])]
    vbar = both(lambda u: x[u][:, LANES:])
    rbar_b = [bf(z) for z in both(lambda u: rt[u[0]] + ry[u][:, :LANES])]
    ybar = both(lambda u: wk[u][SB:] + ry[u][:, LANES:])

    g_ct = [g_c[p].T for p in pairs]

    hbd = [h_ref[p] for p in pairs]
    for c in range(n_chunks):
        sl = slice(c * CHUNK, (c + 1) * CHUNK)
        yu = [dot(cat0(rbar_b[p][sl], abar_b[p][sl]), bf(hbd[p])) for p in pairs]
        for p in pairs:
            y_ref[p, sl, :] = yu[p][:CHUNK] + ybar[p][sl]
        upd = [lax.dot_general(cat0(bh_b[p][sl], kh_b[p][sl]),
                               cat0(bf(yu[p][CHUNK:] + vbar[p][sl]), v_b[p][sl]),
                               (((0,), (0,)), ((), ())), preferred_element_type=F32)
               for p in pairs]
        hbd = [jnp.broadcast_to(g_ct[p][:, c * CHUNK:c * CHUNK + 1], (LANES, LANES)) * hbd[p]
               + keep(same_head, upd[p]) for p in pairs]
    for p in pairs:
        h_ref[p] = hbd[p]


def _rwkv_chunk(r, lw, k, v, a, b, npg):
    n_pairs, l, _ = r.shape
    blk = pl.BlockSpec((npg, SB, LANES), lambda gi, ti: (gi, ti, 0))
    return pl.pallas_call(
        _rwkv_chunk_body,
        out_shape=jax.ShapeDtypeStruct((n_pairs, l, LANES), F32),
        grid=(n_pairs // npg, l // SB),
        in_specs=[blk] * 6,
        out_specs=blk,
        scratch_shapes=[pltpu.VMEM((npg, LANES, LANES), F32)],
        compiler_params=_cparams(("arbitrary", "arbitrary")),
        name="rwkv_chunk",
    )(r, lw, k, v, a, b)


def _fox_prep_body(qkv_ref, fl_ref, gain_ref, bf_ref, ones_ref, place_ref, out_ref, carry_ref):
    t2 = qkv_ref.shape[0]

    @pl.when(pl.program_id(1) == 0)
    def _():
        carry_ref[...] = jnp.zeros_like(carry_ref)

    ones = ones_ref[...]
    for h0 in (0, WIDTH):
        x = qkv_ref[:, h0:h0 + WIDTH]
        ms = _dot_ones(x * x, ones) * (1.0 / HEAD_DIM)
        out_ref[:, h0:h0 + WIDTH] = (x * lax.rsqrt(ms + NORM_EPS) * gain_ref[:, h0:h0 + WIDTH]).astype(BF16)
    out_ref[:, 2 * WIDTH:3 * WIDTH] = qkv_ref[:, 2 * WIDTH:].astype(BF16)

    logf = -_softplus(-(fl_ref[...] + bf_ref[...]))
    ri = lax.broadcasted_iota(jnp.int32, (t2, t2), 0)
    ci = lax.broadcasted_iota(jnp.int32, (t2, t2), 1)
    tri = (ci <= ri).astype(BF16)
    c = sum(jnp.dot(tri, piece, preferred_element_type=F32) for piece in _split3(logf)) + carry_ref[...]
    carry_ref[...] = c[t2 - 1:t2, :]
    pieces = jnp.concatenate(_split3(c * (-LOG2E)), axis=1)
    out_ref[:, 3 * WIDTH:] = jnp.dot(pieces, place_ref[...], preferred_element_type=F32).astype(BF16)


def _bias_placement():
    src = jnp.arange(3 * LANES)
    piece, head = src // LANES, src % LANES
    dst = (head // 2) * LANES + BIAS_LANES * (head % 2) + piece
    hit = (dst[:, None] == jnp.arange(WIDTH)[None, :]) & (head[:, None] < N_HEADS)
    return hit.astype(BF16)


def _fox_prep(p_fx, gain, bf_p, ones512, t2):
    b, l, _ = p_fx.shape
    full = lambda shape: pl.BlockSpec(shape, lambda bi, ti: (0,) * len(shape))
    return pl.pallas_call(
        _fox_prep_body,
        out_shape=jax.ShapeDtypeStruct((b, l, 4 * WIDTH), BF16),
        grid=(b, l // t2),
        in_specs=[pl.BlockSpec((None, t2, 3 * WIDTH), lambda bi, ti: (bi, ti, 0)),
                  pl.BlockSpec((None, t2, LANES), lambda bi, ti: (bi, ti, 4 * WIDTH // LANES)),
                  full((1, 2 * WIDTH)), full((1, LANES)), full((WIDTH, WIDTH)), full((3 * LANES, WIDTH))],
        out_specs=pl.BlockSpec((None, t2, 4 * WIDTH), lambda bi, ti: (bi, ti, 0)),
        scratch_shapes=[pltpu.VMEM((1, LANES), F32)],
        compiler_params=_cparams(("arbitrary", "arbitrary")),
        name="fox_prep",
    )(p_fx, p_fx, gain, bf_p, ones512, _bias_placement())


def _fox_attn_body(q_ref, k_ref, v_ref, kb_ref, o_ref, m_ref, acc_ref, *, q_row0):
    tq = q_ref.shape[0]
    tk = k_ref.shape[0]
    qi = pl.program_id(1)
    kj = pl.program_id(2)
    row_first = q_row0 + qi * tq
    col_first = kj * tk
    last_kj = (row_first + tq - 1) // tk

    @pl.when(kj == 0)
    def _():
        m_ref[...] = jnp.full_like(m_ref, MASK_VALUE)
        acc_ref[...] = jnp.zeros_like(acc_ref)

    lane = lax.broadcasted_iota(jnp.int32, (1, LANES), 1)
    head0 = lane < HEAD_DIM
    row2 = lax.broadcasted_iota(jnp.int32, (2 * tq, LANES), 0)
    lane2 = lax.broadcasted_iota(jnp.int32, (2 * tq, LANES), 1)
    first_lane = jnp.where(row2 < tq, 0, BIAS_LANES)
    bias_on = (lane2 >= first_lane) & (lane2 < first_lane + BIAS_LANES)
    bias_q = jnp.where(bias_on, 1.0, 0.0).astype(BF16)

    def block(masked):
        if masked:
            rowpos = row_first + lax.broadcasted_iota(jnp.int32, (tq, tk), 0)
            colpos = col_first + lax.broadcasted_iota(jnp.int32, (tq, tk), 1)
            causal = colpos <= rowpos
        heads = range(N_HEADS)
        lanes = [slice(j * LANES, (j + 1) * LANES) for j in range(PAIRS)]
        s2 = []
        for ls in lanes:
            q = q_ref[:, ls]
            zero = jnp.zeros_like(q)
            q2 = jnp.concatenate([jnp.where(head0, q, zero), jnp.where(head0, zero, q)], axis=0)
            s2.append(lax.dot_general(jnp.concatenate([q2, bias_q], axis=1),
                                      jnp.concatenate([k_ref[:, ls], kb_ref[:, ls]], axis=1),
                                      (((1,), (1,)), ((), ())), preferred_element_type=F32))
        s = [s2[h // 2][(h % 2) * tq:(h % 2 + 1) * tq] for h in heads]
        if masked:
            s = [jnp.where(causal, sh, MASK_VALUE) for sh in s]
        m_prev = [m_ref[h] for h in heads]
        m_new = [jnp.maximum(m_prev[h], jnp.max(s[h], axis=1, keepdims=True)) for h in heads]
        p = [jnp.exp2(s[h] - jnp.concatenate([m_new[h]] * (tk // LANES), axis=1)).astype(BF16)
             for h in heads]
        v1 = []
        for ls in lanes:
            v = v_ref[:, ls]
            one = jnp.ones_like(v)
            v1 += [jnp.where(head0, v, one), jnp.where(head0, one, v)]
        pv = [jnp.dot(p[h], v1[h], preferred_element_type=F32) for h in heads]
        for h in heads:
            acc_ref[h] = jnp.exp2(m_prev[h] - m_new[h]) * acc_ref[h] + pv[h]
            m_ref[h] = m_new[h]

    needed = col_first <= row_first + tq - 1
    crosses = col_first + tk - 1 > row_first

    @pl.when(needed & crosses)
    def _():
        block(True)

    @pl.when(needed & jnp.logical_not(crosses))
    def _():
        block(False)

    @pl.when(kj == last_kj)
    def _():
        for j in range(PAIRS):
            a0, a1 = acc_ref[2 * j], acc_ref[2 * j + 1]
            o_ref[:, j * LANES:(j + 1) * LANES] = jnp.where(
                head0, a0 / pltpu.roll(a0, HEAD_DIM, axis=1), a1 / pltpu.roll(a1, HEAD_DIM, axis=1))


def _fox_attn(qkvb, q_row0, n_rows, tq, tk):
    b, l, _ = qkvb.shape
    nq = n_rows // tq
    qb0 = q_row0 // tq
    kv_last = lambda qi: (q_row0 + (qi + 1) * tq - 1) // tk
    nk = kv_last(nq - 1) + 1
    kv_spec = lambda col: pl.BlockSpec((None, tk, WIDTH),
                                       lambda bi, qi, kj: (bi, jnp.minimum(kj, kv_last(qi)), col))
    return pl.pallas_call(
        functools.partial(_fox_attn_body, q_row0=q_row0),
        out_shape=jax.ShapeDtypeStruct((b, n_rows, WIDTH), F32),
        grid=(b, nq, nk),
        in_specs=[pl.BlockSpec((None, tq, WIDTH), lambda bi, qi, kj: (bi, qb0 + qi, 0)),
                  kv_spec(1), kv_spec(2), kv_spec(3)],
        out_specs=pl.BlockSpec((None, tq, WIDTH), lambda bi, qi, kj: (bi, qi, 0)),
        scratch_shapes=[pltpu.VMEM((N_HEADS, tq, LANES), F32), pltpu.VMEM((N_HEADS, tq, LANES), F32)],
        compiler_params=_cparams(("arbitrary", "arbitrary", "arbitrary")),
        name="fox_attn",
    )(qkvb, qkvb, qkvb, qkvb)


def _mix_out_body(h_ref, y_ref, bonus_ref, g_ref, o_ref, og_ref, gnw_ref, gnb_ref, ones_ref, wo_ref, out_ref):
    ones = ones_ref[...]
    y = jnp.concatenate([y_ref[j] for j in range(PAIRS)], axis=1)
    mean = _dot_ones(y, ones) * (1.0 / HEAD_DIM)
    d = y - mean
    var = _dot_ones(d * d, ones) * (1.0 / HEAD_DIM)
    yn = d * lax.rsqrt(var + GN_EPS) * gnw_ref[...] + gnb_ref[...]
    y_rw = ((yn + bonus_ref[...]) * g_ref[...]).astype(BF16)
    y_fx = (o_ref[...] * _sigmoid(og_ref[...])).astype(BF16)
    out_ref[...] = (h_ref[...]
                    + jnp.dot(y_rw, wo_ref[0:WIDTH, :], preferred_element_type=F32)
                    + jnp.dot(y_fx, wo_ref[WIDTH:2 * WIDTH, :], preferred_element_type=F32))


def _mix_out(h, y, bonus, g, o_fx, p_fx, gn_w, gn_b, ones512, wo_bf16, tm):
    b, l, d = h.shape
    full = lambda shape: pl.BlockSpec(shape, lambda bi, ti: (0,) * len(shape))
    rows = pl.BlockSpec((None, tm, WIDTH), lambda bi, ti: (bi, ti, 0))
    return pl.pallas_call(
        _mix_out_body,
        out_shape=jax.ShapeDtypeStruct((b, l, d), F32),
        grid=(b, l // tm),
        in_specs=[pl.BlockSpec((None, tm, d), lambda bi, ti: (bi, ti, 0)),
                  pl.BlockSpec((None, PAIRS, tm, LANES), lambda bi, ti: (bi, 0, ti, 0)),
                  rows, rows, rows,
                  pl.BlockSpec((None, tm, WIDTH), lambda bi, ti: (bi, ti, 3)),
                  full((1, WIDTH)), full((1, WIDTH)), full((WIDTH, WIDTH)), full((2 * WIDTH, d))],
        out_specs=pl.BlockSpec((None, tm, d), lambda bi, ti: (bi, ti, 0)),
        compiler_params=_cparams(("arbitrary", "arbitrary")),
        name="mix_out",
    )(h, y, bonus, g, o_fx, p_fx, gn_w, gn_b, ones512, wo_bf16)


def _conv_ffn_body(h_ref, g_ref, wu_ref, wg_ref, cwu_ref, cwg_ref, cbu_ref, cbg_ref, wout_ref,
                   out_ref, xn_ref, cu_ref, cg_ref):
    tm = h_ref.shape[0]
    ti = pl.program_id(1)
    j = pl.program_id(2)

    @pl.when(j == 0)
    def _():
        x = h_ref[...]
        ms = jnp.mean(x * x, axis=-1, keepdims=True)
        xn_ref[...] = (x * lax.rsqrt(ms + NORM_EPS) * g_ref[...]).astype(BF16)
        out_ref[...] = x

    @pl.when(ti == 0)
    def _():
        cu_ref[j] = jnp.zeros(cu_ref.shape[1:], F32)
        cg_ref[j] = jnp.zeros(cg_ref.shape[1:], F32)

    xn = xn_ref[...]

    def conv(w_ref, cw_ref, cb_ref, carry_ref):
        hcur = jnp.dot(xn, w_ref[...], preferred_element_type=F32)
        rows = lax.broadcasted_iota(jnp.int32, hcur.shape, 0)
        carry = carry_ref[j]
        h1 = jnp.where(rows == 0, carry[7:8, :], pltpu.roll(hcur, 1, axis=0))
        h2 = pltpu.roll(hcur, 2, axis=0)
        h2 = jnp.where(rows == 0, carry[6:7, :], jnp.where(rows == 1, carry[7:8, :], h2))
        carry_ref[j] = hcur[tm - 8:tm, :]
        cw = cw_ref[...]
        return cb_ref[...] + cw[0:1, :] * h2 + cw[1:2, :] * h1 + cw[2:3, :] * hcur

    u = conv(wu_ref, cwu_ref, cbu_ref, cu_ref)
    gt = conv(wg_ref, cwg_ref, cbg_ref, cg_ref)
    act = (gt * _sigmoid(gt) * u).astype(BF16)
    out_ref[...] += jnp.dot(act, wout_ref[...], preferred_element_type=F32)


def _conv_ffn(h, g, w_in_bf16, conv_w, conv_b, w_out_bf16, tm, tf):
    b, l, d = h.shape
    d_ff = w_out_bf16.shape[0]
    nj = d_ff // tf
    cw = jnp.pad(conv_w, ((0, 8 - CONV_W), (0, 0)))
    cb = conv_b.reshape(1, 2 * d_ff)
    return pl.pallas_call(
        _conv_ffn_body,
        out_shape=jax.ShapeDtypeStruct((b, l, d), F32),
        grid=(b, l // tm, nj),
        in_specs=[pl.BlockSpec((None, tm, d), lambda bi, ti, j: (bi, ti, 0)),
                  pl.BlockSpec((1, d), lambda bi, ti, j: (0, 0)),
                  pl.BlockSpec((d, tf), lambda bi, ti, j: (0, j)),
                  pl.BlockSpec((d, tf), lambda bi, ti, j: (0, nj + j)),
                  pl.BlockSpec((8, tf), lambda bi, ti, j: (0, j)),
                  pl.BlockSpec((8, tf), lambda bi, ti, j: (0, nj + j)),
                  pl.BlockSpec((1, tf), lambda bi, ti, j: (0, j)),
                  pl.BlockSpec((1, tf), lambda bi, ti, j: (0, nj + j)),
                  pl.BlockSpec((tf, d), lambda bi, ti, j: (j, 0))],
        out_specs=pl.BlockSpec((None, tm, d), lambda bi, ti, j: (bi, ti, 0)),
        scratch_shapes=[pltpu.VMEM((tm, d), BF16),
                        pltpu.VMEM((nj, 8, tf), F32), pltpu.VMEM((nj, 8, tf), F32)],
        compiler_params=_cparams(("arbitrary", "arbitrary", "arbitrary")),
        name="conv_ffn",
    )(h, g.reshape(1, d), w_in_bf16, w_in_bf16, cw, cw, cb, cb, w_out_bf16)


def _pad_cols(w, n):
    return jnp.pad(w, ((0, 0), (0, n - w.shape[1])))


def _layer(h, prm):
    (norm1_g, w_in, rw_mu, rw_w0, rw_w_up, rw_a0, rw_a_up, rw_g_up, rw_k_k, rw_k_a, rw_r_k,
     rw_gn_w, rw_gn_b, fx_b_f, fx_q_g, fx_k_g, w_o, norm2_g, ffn_w_in, ffn_conv_w, ffn_conv_b,
     ffn_w_out) = prm
    b, l, d = h.shape
    m = b * l
    rw_cols = 3 * WIDTH + DECAY_LORA + AAA_LORA + GATE_LORA

    w_rw = _pad_cols(w_in[:, :rw_cols], RW_PCOLS).astype(BF16)
    w_fx = _pad_cols(w_in[:, rw_cols:], FX_PCOLS).astype(BF16)
    mu_p = _pad_cols(rw_mu.reshape(1, rw_cols), RW_PCOLS)
    wcomb = jnp.zeros((128, 2 * WIDTH), F32)
    wcomb = wcomb.at[:DECAY_LORA, :WIDTH].set(rw_w_up).at[DECAY_LORA:, WIDTH:].set(rw_a_up)
    w0a0 = jnp.concatenate([rw_w0, rw_a0]).reshape(1, 2 * WIDTH)
    gup_p = jnp.pad(rw_g_up, ((0, GATE_PAD - GATE_LORA), (0, 0)))
    ones512 = _group_ones(WIDTH, HEAD_DIM)
    gain = jnp.concatenate([jnp.tile(fx_q_g, N_HEADS) * (HEAD_DIM ** -0.5 * LOG2E),
                            jnp.tile(fx_k_g, N_HEADS)]).reshape(1, 2 * WIDTH)
    bf_p = jnp.pad(fx_b_f, (0, LANES - N_HEADS)).reshape(1, LANES)

    tm = _pick(m, (1024, 512, 256, 128))
    t2 = _pick(l, (544, 384, 256, 128))

    h2 = h.reshape(m, d)
    p_rw = _norm_matmul(h2, norm1_g, w_rw, tm).reshape(b, l, RW_PCOLS)
    p_fx = _norm_matmul(h2, norm1_g, w_fx, tm).reshape(b, l, FX_PCOLS)

    r, lw, k, v, a, bb, g, bonus = _rwkv_prep(
        p_rw, mu_p, wcomb, w0a0, gup_p, rw_k_k.reshape(1, WIDTH), rw_k_a.reshape(1, WIDTH),
        rw_r_k.reshape(1, WIDTH), ones512, t2)
    pm = lambda z: z.reshape(b * PAIRS, l, LANES)
    y = _rwkv_chunk(pm(r), pm(lw), pm(k), pm(v), pm(a), pm(bb), npg=8)
    y = y.reshape(b, PAIRS, l, LANES)

    qkvb = _fox_prep(p_fx, gain, bf_p, ones512, t2)
    n_main = l // ATTN_BLOCK * ATTN_BLOCK
    parts = []
    if n_main:
        parts.append(_fox_attn(qkvb, 0, n_main, ATTN_BLOCK, ATTN_BLOCK))
    if l > n_main:
        parts.append(_fox_attn(qkvb, n_main, l - n_main, l - n_main, l))
    o_fx = jnp.concatenate(parts, axis=1)

    h = _mix_out(h, y, bonus, g, o_fx, p_fx, rw_gn_w.reshape(1, WIDTH), rw_gn_b.reshape(1, WIDTH),
                 ones512, w_o.astype(BF16), t2)

    d_ff = ffn_w_out.shape[0]
    tf = _pick(d_ff, (1408, 256, 128))
    return _conv_ffn(h, norm2_g, ffn_w_in.astype(BF16), ffn_conv_w, ffn_conv_b,
                     ffn_w_out.astype(BF16), t2, tf)


def kernel(x, meta, norm1_g, w_in, rw_mu, rw_w0, rw_w_up, rw_a0, rw_a_up, rw_g_up, rw_k_k, rw_k_a,
           rw_r_k, rw_gn_w, rw_gn_b, fx_b_f, fx_q_g, fx_k_g, w_o, norm2_g, ffn_w_in, ffn_conv_w,
           ffn_conv_b, ffn_w_out):
    b, seq, d = x.shape
    params = (norm1_g, w_in, rw_mu, rw_w0, rw_w_up, rw_a0, rw_a_up, rw_g_up, rw_k_k, rw_k_a, rw_r_k,
              rw_gn_w, rw_gn_b, fx_b_f, fx_q_g, fx_k_g, w_o, norm2_g, ffn_w_in, ffn_conv_w,
              ffn_conv_b, ffn_w_out)
    l = N_META + seq
    lp = -(-l // SB) * SB
    meta_b = jnp.broadcast_to(meta[None].astype(x.dtype), (b, N_META, d))
    h = jnp.concatenate([meta_b, x, jnp.zeros((b, lp - l, d), x.dtype)], axis=1)
    for layer in range(norm1_g.shape[0]):
        h = _layer(h, tuple(p[layer] for p in params))
    return h[:, N_META:l]
```

```python
import functools

import jax
import jax.numpy as jnp
from jax import lax
from jax.experimental import pallas as pl
from jax.experimental.pallas import tpu as pltpu

F32 = jnp.float32
BF16 = jnp.bfloat16

N_META = 16
HEAD_DIM = 64
N_HEADS = 8
WIDTH = N_HEADS * HEAD_DIM
DECAY_LORA = 64
AAA_LORA = 64
GATE_LORA = 160
GATE_PAD = 256
RW_PCOLS = 3 * WIDTH + 128 + GATE_PAD
FX_PCOLS = 4 * WIDTH + 128
CONV_W = 3
NORM_EPS = 1e-6
GN_EPS = HEAD_DIM * 1e-5
LANES = 128
PAIRS = WIDTH // LANES
CHUNK = 16
SB = 128
ATTN_BLOCK = 256
BIAS_LANES = 3
LOG2E = 1.4426950408889634
MASK_VALUE = -1e30
VMEM_LIMIT = 56 * 1024 * 1024


def _pick(n, cands):
    for c in cands:
        if n % c == 0:
            return c
    raise ValueError(f"no tile for {n} in {cands}")


def _cparams(sem):
    return pltpu.CompilerParams(dimension_semantics=sem, vmem_limit_bytes=VMEM_LIMIT)


def _sigmoid(x):
    return 1.0 / (1.0 + jnp.exp(-x))


def _softplus(x):
    return jnp.maximum(x, 0.0) + jnp.log(1.0 + jnp.exp(-jnp.abs(x)))


def _split2(x):
    hi = x.astype(BF16)
    lo = (x - hi.astype(F32)).astype(BF16)
    return hi, lo


def _split3(x):
    p1 = x.astype(BF16)
    rem = x - p1.astype(F32)
    p2 = rem.astype(BF16)
    p3 = (rem - p2.astype(F32)).astype(BF16)
    return p1, p2, p3


def _dot_ones(x, ones_bf16):
    hi, lo = _split2(x)
    return (jnp.dot(hi, ones_bf16, preferred_element_type=F32)
            + jnp.dot(lo, ones_bf16, preferred_element_type=F32))


def _mm3(a_hi, a_lo, b_hi, b_lo, dims=((1,), (0,))):
    dn = (dims, ((), ()))
    m_axis = 1 - dims[0][0]
    m = a_hi.shape[m_axis]
    d = lax.dot_general(jnp.concatenate([a_hi, a_lo], axis=m_axis), b_hi, dn,
                        preferred_element_type=F32)
    return d[:m] + d[m:] + lax.dot_general(a_hi, b_lo, dn, preferred_element_type=F32)


def _dot_f32(a, b):
    return _mm3(*_split2(a), *_split2(b))


def _group_ones(n, group):
    i = jnp.arange(n) // group
    return (i[:, None] == i[None, :]).astype(BF16)


def _norm_matmul_body(x_ref, g_ref, w_ref, o_ref, *, n_chunk):
    x = x_ref[...]
    ms = jnp.mean(x * x, axis=-1, keepdims=True)
    xn = (x * lax.rsqrt(ms + NORM_EPS) * g_ref[...]).astype(BF16)
    n = o_ref.shape[-1]
    for c in range(0, n, n_chunk):
        e = min(c + n_chunk, n)
        o_ref[:, c:e] = jnp.dot(xn, w_ref[:, c:e], preferred_element_type=F32)


def _norm_matmul(x2, g, w_bf16, tm):
    m, d = x2.shape
    n = w_bf16.shape[1]
    return pl.pallas_call(
        functools.partial(_norm_matmul_body, n_chunk=512),
        out_shape=jax.ShapeDtypeStruct((m, n), F32),
        grid=(m // tm,),
        in_specs=[pl.BlockSpec((tm, d), lambda i: (i, 0)),
                  pl.BlockSpec((1, d), lambda i: (0, 0)),
                  pl.BlockSpec((d, n), lambda i: (0, 0))],
        out_specs=pl.BlockSpec((tm, n), lambda i: (i, 0)),
        compiler_params=_cparams(("arbitrary",)),
        name="norm_matmul",
    )(x2, g.reshape(1, d), w_bf16)


def _rwkv_prep_body(p_ref, mu_ref, wcomb_ref, w0a0_ref, gup_ref, kk_ref, ka_ref, rk_ref, ones_ref,
                    r_o, lw_o, k_o, v_o, a_o, b_o, g_o, bonus_o, carry_ref):
    t2 = p_ref.shape[0]

    @pl.when(pl.program_id(1) == 0)
    def _():
        carry_ref[...] = jnp.zeros_like(carry_ref)

    p = p_ref[...]
    rows = lax.broadcasted_iota(jnp.int32, p.shape, 0)
    prev = jnp.where(rows == 0, carry_ref[...], pltpu.roll(p, 1, axis=0))
    carry_ref[...] = p[t2 - 1:t2, :]
    x = p + mu_ref[...] * (prev - p)

    r = x[:, 0:WIDTH]
    k = x[:, WIDTH:2 * WIDTH]
    v = x[:, 2 * WIDTH:3 * WIDTH]
    wa = x[:, 3 * WIDTH:3 * WIDTH + 128]
    gd = x[:, 3 * WIDTH + 128:]

    lane = lax.broadcasted_iota(jnp.int32, wa.shape, 1)
    xa = jnp.where(lane < DECAY_LORA, jnp.tanh(wa), wa)
    pre = _dot_f32(xa, wcomb_ref[...]) + w0a0_ref[...]
    w_log = -_softplus(-pre[:, :WIDTH]) - 0.5
    log_decay = -jnp.exp(w_log)
    a = _sigmoid(pre[:, WIDTH:])
    g = _dot_f32(_sigmoid(gd), gup_ref[...])

    ones = ones_ref[...]
    kk = k * kk_ref[...]
    ss = _dot_ones(kk * kk, ones)
    kk = kk / jnp.maximum(jnp.sqrt(ss), 1e-12)
    k2 = k * (1.0 + (a - 1.0) * ka_ref[...])
    bonus = _dot_ones(r * k2 * rk_ref[...], ones) * v

    for j in range(PAIRS):
        ls = slice(j * LANES, (j + 1) * LANES)
        r_o[j] = r[:, ls]
        lw_o[j] = log_decay[:, ls]
        k_o[j] = k2[:, ls]
        v_o[j] = v[:, ls]
        a_o[j] = -kk[:, ls]
        b_o[j] = (kk * a)[:, ls]
    g_o[...] = g
    bonus_o[...] = bonus


def _rwkv_prep(p_rw, mu_p, wcomb, w0a0, gup_p, k_k, k_a, r_k, ones512, t2):
    b, l, pc = p_rw.shape
    full = lambda shape: pl.BlockSpec(shape, lambda bi, ti: (0,) * len(shape))
    pair_sd = jax.ShapeDtypeStruct((b, PAIRS, l, LANES), F32)
    pair_spec = pl.BlockSpec((None, PAIRS, t2, LANES), lambda bi, ti: (bi, 0, ti, 0))
    row_sd = jax.ShapeDtypeStruct((b, l, WIDTH), F32)
    row_spec = pl.BlockSpec((None, t2, WIDTH), lambda bi, ti: (bi, ti, 0))
    return pl.pallas_call(
        _rwkv_prep_body,
        out_shape=[pair_sd] * 6 + [row_sd] * 2,
        grid=(b, l // t2),
        in_specs=[pl.BlockSpec((None, t2, pc), lambda bi, ti: (bi, ti, 0)),
                  full((1, pc)), full(wcomb.shape), full((1, 2 * WIDTH)), full(gup_p.shape),
                  full((1, WIDTH)), full((1, WIDTH)), full((1, WIDTH)), full((WIDTH, WIDTH))],
        out_specs=[pair_spec] * 6 + [row_spec] * 2,
        scratch_shapes=[pltpu.VMEM((1, pc), F32)],
        compiler_params=_cparams(("arbitrary", "arbitrary")),
        name="rwkv_prep",
    )(p_rw, mu_p, wcomb, w0a0, gup_p, k_k, k_a, r_k, ones512)


def _rwkv_chunk_body(r_ref, lw_ref, k_ref, v_ref, a_ref, b_ref, y_ref, h_ref):
    npg = r_ref.shape[0]
    n_chunks = SB // CHUNK

    @pl.when(pl.program_id(1) == 0)
    def _():
        h_ref[...] = jnp.zeros_like(h_ref)

    ti = lax.broadcasted_iota(jnp.int32, (SB, SB), 0)
    si = lax.broadcasted_iota(jnp.int32, (SB, SB), 1)
    same = (ti // CHUNK) == (si // CHUNK)
    incl = same & (si <= ti)
    strict = same & (si < ti)
    eye = ti == si
    same_head = (ti < HEAD_DIM) == (si < HEAD_DIM)
    head0 = lax.broadcasted_iota(jnp.int32, (SB, LANES), 1) < HEAD_DIM
    tri_ones = jnp.concatenate([incl.astype(BF16), same.astype(BF16)], axis=0)
    eye_f = eye.astype(F32)
    bf = lambda z: z.astype(BF16)
    dot = functools.partial(jnp.dot, preferred_element_type=F32)

    def keep(mask, x):
        return jnp.where(mask, x, jnp.zeros_like(x))

    pairs = range(npg)
    units = [(p, h) for p in pairs for h in range(2)]
    cat0 = lambda *xs: jnp.concatenate(xs, axis=0)
    mm = lambda a, b, **kw: _mm3(*a, *b, **kw)

    lw = [lw_ref[p] for p in pairs]
    cc = [sum(jnp.dot(tri_ones, piece, preferred_element_type=F32) for piece in _split3(lw[p]))
          for p in pairs]
    cs = [cc[p][:SB] for p in pairs]
    ce = [cc[p][SB:] for p in pairs]
    g = [jnp.exp(cs[p]) for p in pairs]
    ginv = [jnp.exp(-cs[p]) for p in pairs]
    gend = [jnp.exp(ce[p] - cs[p]) for p in pairs]
    g_c = [jnp.exp(ce[p]) for p in pairs]
    at = [a_ref[p] * jnp.exp(cs[p] - lw[p]) for p in pairs]
    rt = [r_ref[p] * g[p] for p in pairs]
    v_b = [bf(v_ref[p]) for p in pairs]
    at_b = [bf(at[p]) for p in pairs]
    rt_b = [bf(rt[p]) for p in pairs]
    bk_t = [cat0(bf(b_ref[p] * ginv[p]), bf(k_ref[p] * ginv[p])) for p in pairs]
    bh_b = [bf(b_ref[p] * gend[p]) for p in pairs]
    kh_b = [bf(k_ref[p] * gend[p]) for p in pairs]

    gram = [lax.dot_general(cat0(keep(head0, at_b[p]), keep(head0, rt_b[p]),
                                 keep(~head0, at_b[p]), keep(~head0, rt_b[p])), bk_t[p],
                            (((1,), (1,)), ((), ())), preferred_element_type=F32)
            for p in pairs]
    gb = {(p, h): gram[p][2 * SB * h:2 * SB * (h + 1)] for p, h in units}
    mab = {u: keep(strict, gb[u][:SB, :SB]) for u in units}
    mkk_b = {u: bf(cat0(keep(strict, gb[u][:SB, SB:]), keep(incl, gb[u][SB:, SB:]))) for u in units}
    mrb_b = {u: bf(keep(incl, gb[u][SB:, :SB])) for u in units}

    n1 = {u: _split2(mab[u]) for u in units}
    n2 = {u: _split2(mm(n1[u], n1[u])) for u in units}
    n4 = {u: _split2(mm(n2[u], n2[u])) for u in units}
    n8 = {u: _split2(mm(n4[u], n4[u])) for u in units}
    t = {u: eye_f + mab[u] for u in units}
    for npow in (n2, n4, n8):
        t = {u: t[u] + mm(npow[u], _split2(t[u])) for u in units}
    wk = {u: dot(mkk_b[u], v_b[u[0]]) for u in units}
    x = {u: dot(bf(t[u]), jnp.concatenate([at_b[u[0]], bf(wk[u][:SB])], axis=1)) for u in units}
    ry = {u: dot(mrb_b[u], bf(x[u])) for u in units}
    both = lambda f: [jnp.where(head0, f((p, 0)), f((p, 1))) for p in pairs]
    abar_b = [bf(z) for z in both(lambda u: x[u][:, :LANES])]
    vbar = both(lambda u: x[u][:, LANES:])
    rbar_b = [bf(z) for z in both(lambda u: rt[u[0]] + ry[u][:, :LANES])]
    ybar = both(lambda u: wk[u][SB:] + ry[u][:, LANES:])

    g_ct = [g_c[p].T for p in pairs]

    hbd = [h_ref[p] for p in pairs]
    for c in range(n_chunks):
        sl = slice(c * CHUNK, (c + 1) * CHUNK)
        yu = [dot(cat0(rbar_b[p][sl], abar_b[p][sl]), bf(hbd[p])) for p in pairs]
        for p in pairs:
            y_ref[p, sl, :] = yu[p][:CHUNK] + ybar[p][sl]
        upd = [lax.dot_general(cat0(bh_b[p][sl], kh_b[p][sl]),
                               cat0(bf(yu[p][CHUNK:] + vbar[p][sl]), v_b[p][sl]),
                               (((0,), (0,)), ((), ())), preferred_element_type=F32)
               for p in pairs]
        hbd = [jnp.broadcast_to(g_ct[p][:, c * CHUNK:c * CHUNK + 1], (LANES, LANES)) * hbd[p]
               + keep(same_head, upd[p]) for p in pairs]
    for p in pairs:
        h_ref[p] = hbd[p]


def _rwkv_chunk(r, lw, k, v, a, b, npg):
    n_pairs, l, _ = r.shape
    blk = pl.BlockSpec((npg, SB, LANES), lambda gi, ti: (gi, ti, 0))
    return pl.pallas_call(
        _rwkv_chunk_body,
        out_shape=jax.ShapeDtypeStruct((n_pairs, l, LANES), F32),
        grid=(n_pairs // npg, l // SB),
        in_specs=[blk] * 6,
        out_specs=blk,
        scratch_shapes=[pltpu.VMEM((npg, LANES, LANES), F32)],
        compiler_params=_cparams(("arbitrary", "arbitrary")),
        name="rwkv_chunk",
    )(r, lw, k, v, a, b)


def _fox_prep_body(qkv_ref, fl_ref, gain_ref, bf_ref, ones_ref, place_ref, out_ref, carry_ref):
    t2 = qkv_ref.shape[0]

    @pl.when(pl.program_id(1) == 0)
    def _():
        carry_ref[...] = jnp.zeros_like(carry_ref)

    ones = ones_ref[...]
    for h0 in (0, WIDTH):
        x = qkv_ref[:, h0:h0 + WIDTH]
        ms = _dot_ones(x * x, ones) * (1.0 / HEAD_DIM)
        out_ref[:, h0:h0 + WIDTH] = (x * lax.rsqrt(ms + NORM_EPS) * gain_ref[:, h0:h0 + WIDTH]).astype(BF16)
    out_ref[:, 2 * WIDTH:3 * WIDTH] = qkv_ref[:, 2 * WIDTH:].astype(BF16)

    logf = -_softplus(-(fl_ref[...] + bf_ref[...]))
    ri = lax.broadcasted_iota(jnp.int32, (t2, t2), 0)
    ci = lax.broadcasted_iota(jnp.int32, (t2, t2), 1)
    tri = (ci <= ri).astype(BF16)
    c = sum(jnp.dot(tri, piece, preferred_element_type=F32) for piece in _split3(logf)) + carry_ref[...]
    carry_ref[...] = c[t2 - 1:t2, :]
    pieces = jnp.concatenate(_split3(c * (-LOG2E)), axis=1)
    out_ref[:, 3 * WIDTH:] = jnp.dot(pieces, place_ref[...], preferred_element_type=F32).astype(BF16)


def _bias_placement():
    src = jnp.arange(3 * LANES)
    piece, head = src // LANES, src % LANES
    dst = (head // 2) * LANES + BIAS_LANES * (head % 2) + piece
    hit = (dst[:, None] == jnp.arange(WIDTH)[None, :]) & (head[:, None] < N_HEADS)
    return hit.astype(BF16)


def _fox_prep(p_fx, gain, bf_p, ones512, t2):
    b, l, _ = p_fx.shape
    full = lambda shape: pl.BlockSpec(shape, lambda bi, ti: (0,) * len(shape))
    return pl.pallas_call(
        _fox_prep_body,
        out_shape=jax.ShapeDtypeStruct((b, l, 4 * WIDTH), BF16),
        grid=(b, l // t2),
        in_specs=[pl.BlockSpec((None, t2, 3 * WIDTH), lambda bi, ti: (bi, ti, 0)),
                  pl.BlockSpec((None, t2, LANES), lambda bi, ti: (bi, ti, 4 * WIDTH // LANES)),
                  full((1, 2 * WIDTH)), full((1, LANES)), full((WIDTH, WIDTH)), full((3 * LANES, WIDTH))],
        out_specs=pl.BlockSpec((None, t2, 4 * WIDTH), lambda bi, ti: (bi, ti, 0)),
        scratch_shapes=[pltpu.VMEM((1, LANES), F32)],
        compiler_params=_cparams(("arbitrary", "arbitrary")),
        name="fox_prep",
    )(p_fx, p_fx, gain, bf_p, ones512, _bias_placement())


def _fox_attn_body(q_ref, k_ref, v_ref, kb_ref, o_ref, m_ref, acc_ref, *, q_row0, tk):
    tq = q_ref.shape[0]
    row_first = q_row0 + pl.program_id(1) * tq

    m_ref[...] = jnp.full_like(m_ref, MASK_VALUE)
    acc_ref[...] = jnp.zeros_like(acc_ref)

    lane = lax.broadcasted_iota(jnp.int32, (1, LANES), 1)
    head0 = lane < HEAD_DIM
    row2 = lax.broadcasted_iota(jnp.int32, (2 * tq, LANES), 0)
    lane2 = lax.broadcasted_iota(jnp.int32, (2 * tq, LANES), 1)
    first_lane = jnp.where(row2 < tq, 0, BIAS_LANES)
    bias_on = (lane2 >= first_lane) & (lane2 < first_lane + BIAS_LANES)
    bias_q = jnp.where(bias_on, 1.0, 0.0).astype(BF16)

    def block(col_first, width, masked):
        rows = pl.ds(col_first, width)
        heads = range(N_HEADS)
        lanes = [slice(j * LANES, (j + 1) * LANES) for j in range(PAIRS)]
        s2 = []
        for ls in lanes:
            q = q_ref[:, ls]
            zero = jnp.zeros_like(q)
            q2 = jnp.concatenate([jnp.where(head0, q, zero), jnp.where(head0, zero, q)], axis=0)
            s2.append(lax.dot_general(jnp.concatenate([q2, bias_q], axis=1),
                                      jnp.concatenate([k_ref[rows, ls], kb_ref[rows, ls]], axis=1),
                                      (((1,), (1,)), ((), ())), preferred_element_type=F32))
        s = [s2[h // 2][(h % 2) * tq:(h % 2 + 1) * tq] for h in heads]
        if masked:
            causal = (lax.broadcasted_iota(jnp.int32, (tq, width), 1)
                      <= lax.broadcasted_iota(jnp.int32, (tq, width), 0))
            s = [jnp.where(causal, sh, MASK_VALUE) for sh in s]
        m_prev = [m_ref[h] for h in heads]
        m_new = [jnp.maximum(m_prev[h], jnp.max(s[h], axis=1, keepdims=True)) for h in heads]
        p = [jnp.exp2(s[h] - jnp.concatenate([m_new[h]] * (width // LANES), axis=1)).astype(BF16)
             for h in heads]
        v1 = []
        for ls in lanes:
            v = v_ref[rows, ls]
            one = jnp.ones_like(v)
            v1 += [jnp.where(head0, v, one), jnp.where(head0, one, v)]
        pv = [jnp.dot(p[h], v1[h], preferred_element_type=F32) for h in heads]
        for h in heads:
            acc_ref[h] = jnp.exp2(m_prev[h] - m_new[h]) * acc_ref[h] + pv[h]
            m_ref[h] = m_new[h]

    def full_block(kj, carry):
        block(pl.multiple_of(kj * tk, tk), tk, False)
        return carry

    lax.fori_loop(0, row_first // tk, full_block, 0)
    block(pl.multiple_of(row_first, tq), tq, True)

    for j in range(PAIRS):
        a0, a1 = acc_ref[2 * j], acc_ref[2 * j + 1]
        o_ref[:, j * LANES:(j + 1) * LANES] = jnp.where(
            head0, a0 / pltpu.roll(a0, HEAD_DIM, axis=1), a1 / pltpu.roll(a1, HEAD_DIM, axis=1))


def _fox_attn(qkvb, q_row0, n_rows, tq, tk):
    b, l, _ = qkvb.shape
    assert q_row0 % tk == 0 and q_row0 % tq == 0 and (tq % tk == 0 or n_rows == tq)
    qb0 = q_row0 // tq
    kv_spec = lambda col: pl.BlockSpec((None, l, WIDTH), lambda bi, qi: (bi, 0, col))
    return pl.pallas_call(
        functools.partial(_fox_attn_body, q_row0=q_row0, tk=tk),
        out_shape=jax.ShapeDtypeStruct((b, n_rows, WIDTH), F32),
        grid=(b, n_rows // tq),
        in_specs=[pl.BlockSpec((None, tq, WIDTH), lambda bi, qi: (bi, qb0 + qi, 0)),
                  kv_spec(1), kv_spec(2), kv_spec(3)],
        out_specs=pl.BlockSpec((None, tq, WIDTH), lambda bi, qi: (bi, qi, 0)),
        scratch_shapes=[pltpu.VMEM((N_HEADS, tq, LANES), F32), pltpu.VMEM((N_HEADS, tq, LANES), F32)],
        compiler_params=_cparams(("arbitrary", "arbitrary")),
        name="fox_attn",
    )(qkvb, qkvb, qkvb, qkvb)


def _mix_out_body(h_ref, y_ref, bonus_ref, g_ref, o_ref, og_ref, gnw_ref, gnb_ref, ones_ref, wo_ref, out_ref):
    ones = ones_ref[...]
    y = jnp.concatenate([y_ref[j] for j in range(PAIRS)], axis=1)
    mean = _dot_ones(y, ones) * (1.0 / HEAD_DIM)
    d = y - mean
    var = _dot_ones(d * d, ones) * (1.0 / HEAD_DIM)
    yn = d * lax.rsqrt(var + GN_EPS) * gnw_ref[...] + gnb_ref[...]
    y_rw = ((yn + bonus_ref[...]) * g_ref[...]).astype(BF16)
    y_fx = (o_ref[...] * _sigmoid(og_ref[...])).astype(BF16)
    out_ref[...] = (h_ref[...]
                    + jnp.dot(y_rw, wo_ref[0:WIDTH, :], preferred_element_type=F32)
                    + jnp.dot(y_fx, wo_ref[WIDTH:2 * WIDTH, :], preferred_element_type=F32))


def _mix_out(h, y, bonus, g, o_fx, p_fx, gn_w, gn_b, ones512, wo_bf16, tm):
    b, l, d = h.shape
    full = lambda shape: pl.BlockSpec(shape, lambda bi, ti: (0,) * len(shape))
    rows = pl.BlockSpec((None, tm, WIDTH), lambda bi, ti: (bi, ti, 0))
    return pl.pallas_call(
        _mix_out_body,
        out_shape=jax.ShapeDtypeStruct((b, l, d), F32),
        grid=(b, l // tm),
        in_specs=[pl.BlockSpec((None, tm, d), lambda bi, ti: (bi, ti, 0)),
                  pl.BlockSpec((None, PAIRS, tm, LANES), lambda bi, ti: (bi, 0, ti, 0)),
                  rows, rows, rows,
                  pl.BlockSpec((None, tm, WIDTH), lambda bi, ti: (bi, ti, 3)),
                  full((1, WIDTH)), full((1, WIDTH)), full((WIDTH, WIDTH)), full((2 * WIDTH, d))],
        out_specs=pl.BlockSpec((None, tm, d), lambda bi, ti: (bi, ti, 0)),
        compiler_params=_cparams(("arbitrary", "arbitrary")),
        name="mix_out",
    )(h, y, bonus, g, o_fx, p_fx, gn_w, gn_b, ones512, wo_bf16)


def _conv_ffn_body(h_ref, g_ref, wu_ref, wg_ref, cwu_ref, cwg_ref, cbu_ref, cbg_ref, wout_ref,
                   out_ref, xn_ref, cu_ref, cg_ref):
    tm = h_ref.shape[0]
    ti = pl.program_id(1)
    j = pl.program_id(2)

    @pl.when(j == 0)
    def _():
        x = h_ref[...]
        ms = jnp.mean(x * x, axis=-1, keepdims=True)
        xn_ref[...] = (x * lax.rsqrt(ms + NORM_EPS) * g_ref[...]).astype(BF16)
        out_ref[...] = x

    @pl.when(ti == 0)
    def _():
        cu_ref[j] = jnp.zeros(cu_ref.shape[1:], F32)
        cg_ref[j] = jnp.zeros(cg_ref.shape[1:], F32)

    xn = xn_ref[...]

    def conv(w_ref, cw_ref, cb_ref, carry_ref):
        hcur = jnp.dot(xn, w_ref[...], preferred_element_type=F32)
        rows = lax.broadcasted_iota(jnp.int32, hcur.shape, 0)
        carry = carry_ref[j]
        h1 = jnp.where(rows == 0, carry[7:8, :], pltpu.roll(hcur, 1, axis=0))
        h2 = pltpu.roll(hcur, 2, axis=0)
        h2 = jnp.where(rows == 0, carry[6:7, :], jnp.where(rows == 1, carry[7:8, :], h2))
        carry_ref[j] = hcur[tm - 8:tm, :]
        cw = cw_ref[...]
        return cb_ref[...] + cw[0:1, :] * h2 + cw[1:2, :] * h1 + cw[2:3, :] * hcur

    u = conv(wu_ref, cwu_ref, cbu_ref, cu_ref)
    gt = conv(wg_ref, cwg_ref, cbg_ref, cg_ref)
    act = (gt * _sigmoid(gt) * u).astype(BF16)
    out_ref[...] += jnp.dot(act, wout_ref[...], preferred_element_type=F32)


def _conv_ffn(h, g, w_in_bf16, conv_w, conv_b, w_out_bf16, tm, tf):
    b, l, d = h.shape
    d_ff = w_out_bf16.shape[0]
    nj = d_ff // tf
    cw = jnp.pad(conv_w, ((0, 8 - CONV_W), (0, 0)))
    cb = conv_b.reshape(1, 2 * d_ff)
    return pl.pallas_call(
        _conv_ffn_body,
        out_shape=jax.ShapeDtypeStruct((b, l, d), F32),
        grid=(b, l // tm, nj),
        in_specs=[pl.BlockSpec((None, tm, d), lambda bi, ti, j: (bi, ti, 0)),
                  pl.BlockSpec((1, d), lambda bi, ti, j: (0, 0)),
                  pl.BlockSpec((d, tf), lambda bi, ti, j: (0, j)),
                  pl.BlockSpec((d, tf), lambda bi, ti, j: (0, nj + j)),
                  pl.BlockSpec((8, tf), lambda bi, ti, j: (0, j)),
                  pl.BlockSpec((8, tf), lambda bi, ti, j: (0, nj + j)),
                  pl.BlockSpec((1, tf), lambda bi, ti, j: (0, j)),
                  pl.BlockSpec((1, tf), lambda bi, ti, j: (0, nj + j)),
                  pl.BlockSpec((tf, d), lambda bi, ti, j: (j, 0))],
        out_specs=pl.BlockSpec((None, tm, d), lambda bi, ti, j: (bi, ti, 0)),
        scratch_shapes=[pltpu.VMEM((tm, d), BF16),
                        pltpu.VMEM((nj, 8, tf), F32), pltpu.VMEM((nj, 8, tf), F32)],
        compiler_params=_cparams(("arbitrary", "arbitrary", "arbitrary")),
        name="conv_ffn",
    )(h, g.reshape(1, d), w_in_bf16, w_in_bf16, cw, cw, cb, cb, w_out_bf16)


def _pad_cols(w, n):
    return jnp.pad(w, ((0, 0), (0, n - w.shape[1])))


def _layer(h, prm):
    (norm1_g, w_in, rw_mu, rw_w0, rw_w_up, rw_a0, rw_a_up, rw_g_up, rw_k_k, rw_k_a, rw_r_k,
     rw_gn_w, rw_gn_b, fx_b_f, fx_q_g, fx_k_g, w_o, norm2_g, ffn_w_in, ffn_conv_w, ffn_conv_b,
     ffn_w_out) = prm
    b, l, d = h.shape
    m = b * l
    rw_cols = 3 * WIDTH + DECAY_LORA + AAA_LORA + GATE_LORA

    w_rw = _pad_cols(w_in[:, :rw_cols], RW_PCOLS).astype(BF16)
    w_fx = _pad_cols(w_in[:, rw_cols:], FX_PCOLS).astype(BF16)
    mu_p = _pad_cols(rw_mu.reshape(1, rw_cols), RW_PCOLS)
    wcomb = jnp.zeros((128, 2 * WIDTH), F32)
    wcomb = wcomb.at[:DECAY_LORA, :WIDTH].set(rw_w_up).at[DECAY_LORA:, WIDTH:].set(rw_a_up)
    w0a0 = jnp.concatenate([rw_w0, rw_a0]).reshape(1, 2 * WIDTH)
    gup_p = jnp.pad(rw_g_up, ((0, GATE_PAD - GATE_LORA), (0, 0)))
    ones512 = _group_ones(WIDTH, HEAD_DIM)
    gain = jnp.concatenate([jnp.tile(fx_q_g, N_HEADS) * (HEAD_DIM ** -0.5 * LOG2E),
                            jnp.tile(fx_k_g, N_HEADS)]).reshape(1, 2 * WIDTH)
    bf_p = jnp.pad(fx_b_f, (0, LANES - N_HEADS)).reshape(1, LANES)

    tm = _pick(m, (1024, 512, 256, 128))
    t2 = _pick(l, (544, 384, 256, 128))

    h2 = h.reshape(m, d)
    p_rw = _norm_matmul(h2, norm1_g, w_rw, tm).reshape(b, l, RW_PCOLS)
    p_fx = _norm_matmul(h2, norm1_g, w_fx, tm).reshape(b, l, FX_PCOLS)

    r, lw, k, v, a, bb, g, bonus = _rwkv_prep(
        p_rw, mu_p, wcomb, w0a0, gup_p, rw_k_k.reshape(1, WIDTH), rw_k_a.reshape(1, WIDTH),
        rw_r_k.reshape(1, WIDTH), ones512, t2)
    pm = lambda z: z.reshape(b * PAIRS, l, LANES)
    y = _rwkv_chunk(pm(r), pm(lw), pm(k), pm(v), pm(a), pm(bb), npg=8)
    y = y.reshape(b, PAIRS, l, LANES)

    qkvb = _fox_prep(p_fx, gain, bf_p, ones512, t2)
    n_main = l // ATTN_BLOCK * ATTN_BLOCK
    parts = []
    if n_main:
        parts.append(_fox_attn(qkvb, 0, n_main, ATTN_BLOCK, ATTN_BLOCK))
    if l > n_main:
        parts.append(_fox_attn(qkvb, n_main, l - n_main, l - n_main, ATTN_BLOCK))
    o_fx = jnp.concatenate(parts, axis=1)

    h = _mix_out(h, y, bonus, g, o_fx, p_fx, rw_gn_w.reshape(1, WIDTH), rw_gn_b.reshape(1, WIDTH),
                 ones512, w_o.astype(BF16), t2)

    d_ff = ffn_w_out.shape[0]
    tf = _pick(d_ff, (1408, 256, 128))
    return _conv_ffn(h, norm2_g, ffn_w_in.astype(BF16), ffn_conv_w, ffn_conv_b,
                     ffn_w_out.astype(BF16), t2, tf)


def kernel(x, meta, norm1_g, w_in, rw_mu, rw_w0, rw_w_up, rw_a0, rw_a_up, rw_g_up, rw_k_k, rw_k_a,
           rw_r_k, rw_gn_w, rw_gn_b, fx_b_f, fx_q_g, fx_k_g, w_o, norm2_g, ffn_w_in, ffn_conv_w,
           ffn_conv_b, ffn_w_out):
    b, seq, d = x.shape
    params = (norm1_g, w_in, rw_mu, rw_w0, rw_w_up, rw_a0, rw_a_up, rw_g_up, rw_k_k, rw_k_a, rw_r_k,
              rw_gn_w, rw_gn_b, fx_b_f, fx_q_g, fx_k_g, w_o, norm2_g, ffn_w_in, ffn_conv_w,
              ffn_conv_b, ffn_w_out)
    l = N_META + seq
    lp = -(-l // SB) * SB
    meta_b = jnp.broadcast_to(meta[None].astype(x.dtype), (b, N_META, d))
    h = jnp.concatenate([meta_b, x, jnp.zeros((b, lp - l, d), x.dtype)], axis=1)
    for layer in range(norm1_g.shape[0]):
        h = _layer(h, tuple(p[layer] for p in params))
    return h[:, N_META:l]
```

```python
import functools

import jax
import jax.numpy as jnp
from jax import lax
from jax.experimental import pallas as pl
from jax.experimental.pallas import tpu as pltpu

F32 = jnp.float32
BF16 = jnp.bfloat16

N_META = 16
HEAD_DIM = 64
N_HEADS = 8
WIDTH = N_HEADS * HEAD_DIM
DECAY_LORA = 64
AAA_LORA = 64
GATE_LORA = 160
GATE_PAD = 256
RW_PCOLS = 3 * WIDTH + 128 + GATE_PAD
FX_PCOLS = 4 * WIDTH + 128
CONV_W = 3
NORM_EPS = 1e-6
GN_EPS = HEAD_DIM * 1e-5
LANES = 128
PAIRS = WIDTH // LANES
CHUNK = 16
SB = 128
ATTN_BLOCK = 256
BIAS_LANES = 3
LOG2E = 1.4426950408889634
MASK_VALUE = -1e30
VMEM_LIMIT = 56 * 1024 * 1024


def _pick(n, cands):
    for c in cands:
        if n % c == 0:
            return c
    raise ValueError(f"no tile for {n} in {cands}")


def _cparams(sem):
    return pltpu.CompilerParams(dimension_semantics=sem, vmem_limit_bytes=VMEM_LIMIT)


def _sigmoid(x):
    return 1.0 / (1.0 + jnp.exp(-x))


def _softplus(x):
    return jnp.maximum(x, 0.0) + jnp.log(1.0 + jnp.exp(-jnp.abs(x)))


def _split2(x):
    hi = x.astype(BF16)
    lo = (x - hi.astype(F32)).astype(BF16)
    return hi, lo


def _split3(x):
    p1 = x.astype(BF16)
    rem = x - p1.astype(F32)
    p2 = rem.astype(BF16)
    p3 = (rem - p2.astype(F32)).astype(BF16)
    return p1, p2, p3


def _dot_ones(x, ones_bf16):
    return jnp.dot(x.astype(BF16), ones_bf16, preferred_element_type=F32)


def _mm3(a_hi, a_lo, b_hi, b_lo, dims=((1,), (0,))):
    dn = (dims, ((), ()))
    m_axis = 1 - dims[0][0]
    m = a_hi.shape[m_axis]
    d = lax.dot_general(jnp.concatenate([a_hi, a_lo], axis=m_axis), b_hi, dn,
                        preferred_element_type=F32)
    return d[:m] + d[m:] + lax.dot_general(a_hi, b_lo, dn, preferred_element_type=F32)


def _dot_bf16(a, b):
    return jnp.dot(a.astype(BF16), b.astype(BF16), preferred_element_type=F32)


def _group_ones(n, group):
    i = jnp.arange(n) // group
    return (i[:, None] == i[None, :]).astype(BF16)


def _project(xn, w_ref, p_ref, n_chunk=512):
    n = p_ref.shape[-1]
    for c in range(0, n, n_chunk):
        e = min(c + n_chunk, n)
        p_ref[:, c:e] = jnp.dot(xn, w_ref[:, c:e], preferred_element_type=F32)


def _rwkv_features(p_ref, mu_ref, wcomb_ref, w0a0_ref, gup_ref, kk_ref, ka_ref, rk_ref, ones_ref,
                   r_o, lw_o, k_o, v_o, a_o, b_o, g_o, bonus_o, carry_ref):
    t2 = p_ref.shape[0]
    p = p_ref[...]
    rows = lax.broadcasted_iota(jnp.int32, p.shape, 0)
    prev = jnp.where(rows == 0, carry_ref[...], pltpu.roll(p, 1, axis=0))
    carry_ref[...] = p[t2 - 1:t2, :]
    x = p + mu_ref[...] * (prev - p)

    r = x[:, 0:WIDTH]
    k = x[:, WIDTH:2 * WIDTH]
    v = x[:, 2 * WIDTH:3 * WIDTH]
    wa = x[:, 3 * WIDTH:3 * WIDTH + 128]
    gd = x[:, 3 * WIDTH + 128:]

    lane = lax.broadcasted_iota(jnp.int32, wa.shape, 1)
    xa = jnp.where(lane < DECAY_LORA, jnp.tanh(wa), wa)
    pre = _dot_bf16(xa, wcomb_ref[...]) + w0a0_ref[...]
    w_log = -_softplus(-pre[:, :WIDTH]) - 0.5
    log_decay = -jnp.exp(w_log)
    a = _sigmoid(pre[:, WIDTH:])
    g = _dot_bf16(_sigmoid(gd), gup_ref[...])

    ones = ones_ref[...]
    kk = k * kk_ref[...]
    ss = _dot_ones(kk * kk, ones)
    kk = kk / jnp.maximum(jnp.sqrt(ss), 1e-12)
    k2 = k * (1.0 + (a - 1.0) * ka_ref[...])
    bonus = _dot_ones(r * k2 * rk_ref[...], ones) * v

    for j in range(PAIRS):
        ls = slice(j * LANES, (j + 1) * LANES)
        r_o[j] = r[:, ls]
        lw_o[j] = log_decay[:, ls]
        k_o[j] = k2[:, ls]
        v_o[j] = v[:, ls].astype(v_o.dtype)
        a_o[j] = -kk[:, ls]
        b_o[j] = (kk * a)[:, ls]
    g_o[...] = g
    bonus_o[...] = bonus


def _fox_features(p_ref, gain_ref, bf_ref, ones_ref, place_ref, og_o, qkvb_o, carry_ref):
    t2 = p_ref.shape[0]
    ones = ones_ref[...]
    for h0 in (0, WIDTH):
        x = p_ref[:, h0:h0 + WIDTH]
        ms = _dot_ones(x * x, ones) * (1.0 / HEAD_DIM)
        qkvb_o[:, h0:h0 + WIDTH] = (x * lax.rsqrt(ms + NORM_EPS) * gain_ref[:, h0:h0 + WIDTH]).astype(BF16)
    qkvb_o[:, 2 * WIDTH:3 * WIDTH] = p_ref[:, 2 * WIDTH:3 * WIDTH].astype(BF16)
    og_o[...] = p_ref[:, 3 * WIDTH:4 * WIDTH]

    logf = -_softplus(-(p_ref[:, 4 * WIDTH:] + bf_ref[...]))
    ri = lax.broadcasted_iota(jnp.int32, (t2, t2), 0)
    ci = lax.broadcasted_iota(jnp.int32, (t2, t2), 1)
    tri = (ci <= ri).astype(BF16)
    c = sum(jnp.dot(tri, piece, preferred_element_type=F32) for piece in _split3(logf)) + carry_ref[...]
    carry_ref[...] = c[t2 - 1:t2, :]
    pieces = jnp.concatenate(_split3(c * (-LOG2E)), axis=1)
    qkvb_o[:, 3 * WIDTH:] = jnp.dot(pieces, place_ref[...], preferred_element_type=F32).astype(BF16)


def _proj_prep_body(h_ref, n1_ref, wrw_ref, wfx_ref, mu_ref, wcomb_ref, w0a0_ref, gup_ref, kk_ref,
                    ka_ref, rk_ref, gain_ref, bf_ref, ones_ref, place_ref,
                    r_o, lw_o, k_o, v_o, a_o, b_o, g_o, bonus_o, og_o, qkvb_o,
                    prw_ref, pfx_ref, shift_ref, csum_ref):
    @pl.when(pl.program_id(1) == 0)
    def _():
        shift_ref[...] = jnp.zeros_like(shift_ref)
        csum_ref[...] = jnp.zeros_like(csum_ref)

    x = h_ref[...]
    ms = jnp.mean(x * x, axis=-1, keepdims=True)
    xn = (x * lax.rsqrt(ms + NORM_EPS) * n1_ref[...]).astype(BF16)
    _project(xn, wrw_ref, prw_ref)
    _project(xn, wfx_ref, pfx_ref)
    _rwkv_features(prw_ref, mu_ref, wcomb_ref, w0a0_ref, gup_ref, kk_ref, ka_ref, rk_ref, ones_ref,
                   r_o, lw_o, k_o, v_o, a_o, b_o, g_o, bonus_o, shift_ref)
    _fox_features(pfx_ref, gain_ref, bf_ref, ones_ref, place_ref, og_o, qkvb_o, csum_ref)


def _bias_placement():
    src = jnp.arange(3 * LANES)
    piece, head = src // LANES, src % LANES
    dst = (head // 2) * LANES + BIAS_LANES * (head % 2) + piece
    hit = (dst[:, None] == jnp.arange(WIDTH)[None, :]) & (head[:, None] < N_HEADS)
    return hit.astype(BF16)


def _proj_prep(h, norm1_g, w_rw, w_fx, mu_p, wcomb, w0a0, gup_p, k_k, k_a, r_k, gain, bf_p, ones512, tm):
    b, l, d = h.shape
    full = lambda shape: pl.BlockSpec(shape, lambda bi, ti: (0,) * len(shape))
    pair = lambda dt: (jax.ShapeDtypeStruct((b, PAIRS, l, LANES), dt),
                       pl.BlockSpec((None, PAIRS, tm, LANES), lambda bi, ti: (bi, 0, ti, 0)))
    rows = lambda n, dt: (jax.ShapeDtypeStruct((b, l, n), dt),
                          pl.BlockSpec((None, tm, n), lambda bi, ti: (bi, ti, 0)))
    outs = [pair(F32), pair(F32), pair(F32), pair(BF16), pair(F32), pair(F32),
            rows(WIDTH, F32), rows(WIDTH, F32), rows(WIDTH, F32), rows(4 * WIDTH, BF16)]
    consts = (norm1_g.reshape(1, d), w_rw, w_fx, mu_p, wcomb, w0a0, gup_p, k_k, k_a, r_k, gain, bf_p,
              ones512, _bias_placement())
    return pl.pallas_call(
        _proj_prep_body,
        out_shape=[o[0] for o in outs],
        grid=(b, l // tm),
        in_specs=[pl.BlockSpec((None, tm, d), lambda bi, ti: (bi, ti, 0))] + [full(c.shape) for c in consts],
        out_specs=[o[1] for o in outs],
        scratch_shapes=[pltpu.VMEM((tm, RW_PCOLS), F32), pltpu.VMEM((tm, FX_PCOLS), F32),
                        pltpu.VMEM((1, RW_PCOLS), F32), pltpu.VMEM((1, LANES), F32)],
        compiler_params=_cparams(("arbitrary", "arbitrary")),
        name="proj_prep",
    )(h, *consts)


def _rwkv_chunk_body(r_ref, lw_ref, k_ref, v_ref, a_ref, b_ref, y_ref, h_ref):
    npg = r_ref.shape[0]
    n_chunks = SB // CHUNK

    @pl.when(pl.program_id(1) == 0)
    def _():
        h_ref[...] = jnp.zeros_like(h_ref)

    ti = lax.broadcasted_iota(jnp.int32, (SB, SB), 0)
    si = lax.broadcasted_iota(jnp.int32, (SB, SB), 1)
    same = (ti // CHUNK) == (si // CHUNK)
    incl = same & (si <= ti)
    strict = same & (si < ti)
    eye = ti == si
    same_head = (ti < HEAD_DIM) == (si < HEAD_DIM)
    head0 = lax.broadcasted_iota(jnp.int32, (SB, LANES), 1) < HEAD_DIM
    tri_ones = jnp.concatenate([incl.astype(BF16), same.astype(BF16)], axis=0)
    eye_f = eye.astype(F32)
    bf = lambda z: z.astype(BF16)
    dot = functools.partial(jnp.dot, preferred_element_type=F32)

    def keep(mask, x):
        return jnp.where(mask, x, jnp.zeros_like(x))

    pairs = range(npg)
    units = [(p, h) for p in pairs for h in range(2)]
    cat0 = lambda *xs: jnp.concatenate(xs, axis=0)
    mm = lambda a, b, **kw: _mm3(*a, *b, **kw)

    lw = [lw_ref[p] for p in pairs]
    cc = [sum(jnp.dot(tri_ones, piece, preferred_element_type=F32) for piece in _split3(lw[p]))
          for p in pairs]
    cs = [cc[p][:SB] for p in pairs]
    ce = [cc[p][SB:] for p in pairs]
    g = [jnp.exp(cs[p]) for p in pairs]
    ginv = [jnp.exp(-cs[p]) for p in pairs]
    gend = [jnp.exp(ce[p] - cs[p]) for p in pairs]
    g_c = [jnp.exp(ce[p]) for p in pairs]
    at = [a_ref[p] * jnp.exp(cs[p] - lw[p]) for p in pairs]
    rt = [r_ref[p] * g[p] for p in pairs]
    v_b = [bf(v_ref[p]) for p in pairs]
    at_b = [bf(at[p]) for p in pairs]
    rt_b = [bf(rt[p]) for p in pairs]
    bk_t = [cat0(bf(b_ref[p] * ginv[p]), bf(k_ref[p] * ginv[p])) for p in pairs]
    bh_b = [bf(b_ref[p] * gend[p]) for p in pairs]
    kh_b = [bf(k_ref[p] * gend[p]) for p in pairs]

    gram = [lax.dot_general(cat0(keep(head0, at_b[p]), keep(head0, rt_b[p]),
                                 keep(~head0, at_b[p]), keep(~head0, rt_b[p])), bk_t[p],
                            (((1,), (1,)), ((), ())), preferred_element_type=F32)
            for p in pairs]
    gb = {(p, h): gram[p][2 * SB * h:2 * SB * (h + 1)] for p, h in units}
    mab = {u: keep(strict, gb[u][:SB, :SB]) for u in units}
    mkk_b = {u: bf(cat0(keep(strict, gb[u][:SB, SB:]), keep(incl, gb[u][SB:, SB:]))) for u in units}
    mrb_b = {u: bf(keep(incl, gb[u][SB:, :SB])) for u in units}

    n1 = {u: _split2(mab[u]) for u in units}
    n2 = {u: _split2(mm(n1[u], n1[u])) for u in units}
    n4 = {u: _split2(mm(n2[u], n2[u])) for u in units}
    n8 = {u: _split2(mm(n4[u], n4[u])) for u in units}
    t = {u: eye_f + mab[u] for u in units}
    for npow in (n2, n4, n8):
        t = {u: t[u] + mm(npow[u], _split2(t[u])) for u in units}
    wk = {u: dot(mkk_b[u], v_b[u[0]]) for u in units}
    x = {u: dot(bf(t[u]), jnp.concatenate([at_b[u[0]], bf(wk[u][:SB])], axis=1)) for u in units}
    ry = {u: dot(mrb_b[u], bf(x[u])) for u in units}
    both = lambda f: [jnp.where(head0, f((p, 0)), f((p, 1))) for p in pairs]
    abar_b = [bf(z) for z in both(lambda u: x[u][:, :LANES])]
    vbar = both(lambda u: x[u][:, LANES:])
    rbar_b = [bf(z) for z in both(lambda u: rt[u[0]] + ry[u][:, :LANES])]
    ybar = both(lambda u: wk[u][SB:] + ry[u][:, LANES:])

    g_ct = [g_c[p].T for p in pairs]

    hbd = [h_ref[p] for p in pairs]
    for c in range(n_chunks):
        sl = slice(c * CHUNK, (c + 1) * CHUNK)
        yu = [dot(cat0(rbar_b[p][sl], abar_b[p][sl]), bf(hbd[p])) for p in pairs]
        for p in pairs:
            y_ref[p, sl, :] = yu[p][:CHUNK] + ybar[p][sl]
        upd = [lax.dot_general(cat0(bh_b[p][sl], kh_b[p][sl]),
                               cat0(bf(yu[p][CHUNK:] + vbar[p][sl]), v_b[p][sl]),
                               (((0,), (0,)), ((), ())), preferred_element_type=F32)
               for p in pairs]
        hbd = [jnp.broadcast_to(g_ct[p][:, c * CHUNK:c * CHUNK + 1], (LANES, LANES)) * hbd[p]
               + keep(same_head, upd[p]) for p in pairs]
    for p in pairs:
        h_ref[p] = hbd[p]


def _rwkv_chunk(r, lw, k, v, a, b, npg):
    n_pairs, l, _ = r.shape
    blk = pl.BlockSpec((npg, SB, LANES), lambda gi, ti: (gi, ti, 0))
    return pl.pallas_call(
        _rwkv_chunk_body,
        out_shape=jax.ShapeDtypeStruct((n_pairs, l, LANES), F32),
        grid=(n_pairs // npg, l // SB),
        in_specs=[blk] * 6,
        out_specs=blk,
        scratch_shapes=[pltpu.VMEM((npg, LANES, LANES), F32)],
        compiler_params=_cparams(("arbitrary", "arbitrary")),
        name="rwkv_chunk",
    )(r, lw, k, v, a, b)


def _fox_attn_body(q_ref, k_ref, v_ref, kb_ref, o_ref, m_ref, acc_ref, *, q_row0, tk):
    tq = q_ref.shape[0]
    row_first = q_row0 + pl.program_id(1) * tq

    m_ref[...] = jnp.full_like(m_ref, MASK_VALUE)
    acc_ref[...] = jnp.zeros_like(acc_ref)

    lane = lax.broadcasted_iota(jnp.int32, (1, LANES), 1)
    head0 = lane < HEAD_DIM
    row2 = lax.broadcasted_iota(jnp.int32, (2 * tq, LANES), 0)
    lane2 = lax.broadcasted_iota(jnp.int32, (2 * tq, LANES), 1)
    first_lane = jnp.where(row2 < tq, 0, BIAS_LANES)
    bias_on = (lane2 >= first_lane) & (lane2 < first_lane + BIAS_LANES)
    bias_q = jnp.where(bias_on, 1.0, 0.0).astype(BF16)

    def block(col_first, width, masked):
        rows = pl.ds(col_first, width)
        heads = range(N_HEADS)
        lanes = [slice(j * LANES, (j + 1) * LANES) for j in range(PAIRS)]
        s2 = []
        for ls in lanes:
            q = q_ref[:, ls]
            zero = jnp.zeros_like(q)
            q2 = jnp.concatenate([jnp.where(head0, q, zero), jnp.where(head0, zero, q)], axis=0)
            s2.append(lax.dot_general(jnp.concatenate([q2, bias_q], axis=1),
                                      jnp.concatenate([k_ref[rows, ls], kb_ref[rows, ls]], axis=1),
                                      (((1,), (1,)), ((), ())), preferred_element_type=F32))
        s = [s2[h // 2][(h % 2) * tq:(h % 2 + 1) * tq] for h in heads]
        if masked:
            causal = (lax.broadcasted_iota(jnp.int32, (tq, width), 1)
                      <= lax.broadcasted_iota(jnp.int32, (tq, width), 0))
            s = [jnp.where(causal, sh, MASK_VALUE) for sh in s]
        m_prev = [m_ref[h] for h in heads]
        m_new = [jnp.maximum(m_prev[h], jnp.max(s[h], axis=1, keepdims=True)) for h in heads]
        p = [jnp.exp2(s[h] - jnp.concatenate([m_new[h]] * (width // LANES), axis=1)).astype(BF16)
             for h in heads]
        v1 = []
        for ls in lanes:
            v = v_ref[rows, ls]
            one = jnp.ones_like(v)
            v1 += [jnp.where(head0, v, one), jnp.where(head0, one, v)]
        pv = [jnp.dot(p[h], v1[h], preferred_element_type=F32) for h in heads]
        for h in heads:
            acc_ref[h] = jnp.exp2(m_prev[h] - m_new[h]) * acc_ref[h] + pv[h]
            m_ref[h] = m_new[h]

    def full_block(kj, carry):
        block(pl.multiple_of(kj * tk, tk), tk, False)
        return carry

    lax.fori_loop(0, row_first // tk, full_block, 0)
    block(pl.multiple_of(row_first, tq), tq, True)

    for j in range(PAIRS):
        a0, a1 = acc_ref[2 * j], acc_ref[2 * j + 1]
        o_ref[:, j * LANES:(j + 1) * LANES] = jnp.where(
            head0, a0 / pltpu.roll(a0, HEAD_DIM, axis=1), a1 / pltpu.roll(a1, HEAD_DIM, axis=1))


def _fox_attn(qkvb, q_row0, n_rows, tq, tk):
    b, l, _ = qkvb.shape
    assert q_row0 % tk == 0 and q_row0 % tq == 0 and (tq % tk == 0 or n_rows == tq)
    qb0 = q_row0 // tq
    kv_spec = lambda col: pl.BlockSpec((None, l, WIDTH), lambda bi, qi: (bi, 0, col))
    return pl.pallas_call(
        functools.partial(_fox_attn_body, q_row0=q_row0, tk=tk),
        out_shape=jax.ShapeDtypeStruct((b, n_rows, WIDTH), F32),
        grid=(b, n_rows // tq),
        in_specs=[pl.BlockSpec((None, tq, WIDTH), lambda bi, qi: (bi, qb0 + qi, 0)),
                  kv_spec(1), kv_spec(2), kv_spec(3)],
        out_specs=pl.BlockSpec((None, tq, WIDTH), lambda bi, qi: (bi, qi, 0)),
        scratch_shapes=[pltpu.VMEM((N_HEADS, tq, LANES), F32), pltpu.VMEM((N_HEADS, tq, LANES), F32)],
        compiler_params=_cparams(("arbitrary", "arbitrary")),
        name="fox_attn",
    )(qkvb, qkvb, qkvb, qkvb)


def _mix_out_body(h_ref, y_ref, bonus_ref, g_ref, o_ref, og_ref, gnw_ref, gnb_ref, ones_ref, wo_ref, out_ref):
    ones = ones_ref[...]
    y = jnp.concatenate([y_ref[j] for j in range(PAIRS)], axis=1)
    mean = _dot_ones(y, ones) * (1.0 / HEAD_DIM)
    d = y - mean
    var = _dot_ones(d * d, ones) * (1.0 / HEAD_DIM)
    yn = d * lax.rsqrt(var + GN_EPS) * gnw_ref[...] + gnb_ref[...]
    y_rw = ((yn + bonus_ref[...]) * g_ref[...]).astype(BF16)
    y_fx = (o_ref[...] * _sigmoid(og_ref[...])).astype(BF16)
    out_ref[...] = (h_ref[...]
                    + jnp.dot(y_rw, wo_ref[0:WIDTH, :], preferred_element_type=F32)
                    + jnp.dot(y_fx, wo_ref[WIDTH:2 * WIDTH, :], preferred_element_type=F32))


def _mix_out(h, y, bonus, g, o_fx, og, gn_w, gn_b, ones512, wo_bf16, tm):
    b, l, d = h.shape
    full = lambda shape: pl.BlockSpec(shape, lambda bi, ti: (0,) * len(shape))
    rows = pl.BlockSpec((None, tm, WIDTH), lambda bi, ti: (bi, ti, 0))
    return pl.pallas_call(
        _mix_out_body,
        out_shape=jax.ShapeDtypeStruct((b, l, d), F32),
        grid=(b, l // tm),
        in_specs=[pl.BlockSpec((None, tm, d), lambda bi, ti: (bi, ti, 0)),
                  pl.BlockSpec((None, PAIRS, tm, LANES), lambda bi, ti: (bi, 0, ti, 0)),
                  rows, rows, rows, rows,
                  full((1, WIDTH)), full((1, WIDTH)), full((WIDTH, WIDTH)), full((2 * WIDTH, d))],
        out_specs=pl.BlockSpec((None, tm, d), lambda bi, ti: (bi, ti, 0)),
        compiler_params=_cparams(("arbitrary", "arbitrary")),
        name="mix_out",
    )(h, y, bonus, g, o_fx, og, gn_w, gn_b, ones512, wo_bf16)


def _conv_ffn_body(h_ref, g_ref, wu_ref, wg_ref, cwu_ref, cwg_ref, cbu_ref, cbg_ref, wout_ref,
                   out_ref, xn_ref, cu_ref, cg_ref, wu_s, wg_s):
    tm = h_ref.shape[0]
    halo = wu_s.shape[0] - tm
    ti = pl.program_id(1)
    j = pl.program_id(2)

    @pl.when(j == 0)
    def _():
        x = h_ref[...]
        ms = jnp.mean(x * x, axis=-1, keepdims=True)
        xn_ref[...] = (x * lax.rsqrt(ms + NORM_EPS) * g_ref[...]).astype(BF16)
        out_ref[...] = x

    @pl.when(ti == 0)
    def _():
        cu_ref[j] = jnp.zeros(cu_ref.shape[1:], F32)
        cg_ref[j] = jnp.zeros(cg_ref.shape[1:], F32)

    xn = xn_ref[...]

    def conv(w_ref, cw_ref, cb_ref, carry_ref, work):
        work[0:halo] = carry_ref[j]
        work[halo:] = jnp.dot(xn, w_ref[...], preferred_element_type=F32)
        carry_ref[j] = work[tm:]
        cw = cw_ref[...]
        taps = [work[pl.ds(halo - (CONV_W - 1) + i, tm), :] for i in range(CONV_W)]
        return cb_ref[...] + cw[0:1, :] * taps[0] + cw[1:2, :] * taps[1] + cw[2:3, :] * taps[2]

    u = conv(wu_ref, cwu_ref, cbu_ref, cu_ref, wu_s)
    gt = conv(wg_ref, cwg_ref, cbg_ref, cg_ref, wg_s)
    act = (gt * _sigmoid(gt) * u).astype(BF16)
    out_ref[...] += jnp.dot(act, wout_ref[...], preferred_element_type=F32)


def _conv_ffn(h, g, w_in_bf16, conv_w, conv_b, w_out_bf16, tm, tf):
    b, l, d = h.shape
    d_ff = w_out_bf16.shape[0]
    nj = d_ff // tf
    cw = jnp.pad(conv_w, ((0, 8 - CONV_W), (0, 0)))
    cb = conv_b.reshape(1, 2 * d_ff)
    return pl.pallas_call(
        _conv_ffn_body,
        out_shape=jax.ShapeDtypeStruct((b, l, d), F32),
        grid=(b, l // tm, nj),
        in_specs=[pl.BlockSpec((None, tm, d), lambda bi, ti, j: (bi, ti, 0)),
                  pl.BlockSpec((1, d), lambda bi, ti, j: (0, 0)),
                  pl.BlockSpec((d, tf), lambda bi, ti, j: (0, j)),
                  pl.BlockSpec((d, tf), lambda bi, ti, j: (0, nj + j)),
                  pl.BlockSpec((8, tf), lambda bi, ti, j: (0, j)),
                  pl.BlockSpec((8, tf), lambda bi, ti, j: (0, nj + j)),
                  pl.BlockSpec((1, tf), lambda bi, ti, j: (0, j)),
                  pl.BlockSpec((1, tf), lambda bi, ti, j: (0, nj + j)),
                  pl.BlockSpec((tf, d), lambda bi, ti, j: (j, 0))],
        out_specs=pl.BlockSpec((None, tm, d), lambda bi, ti, j: (bi, ti, 0)),
        scratch_shapes=[pltpu.VMEM((tm, d), BF16),
                        pltpu.VMEM((nj, 8, tf), F32), pltpu.VMEM((nj, 8, tf), F32),
                        pltpu.VMEM((8 + tm, tf), F32), pltpu.VMEM((8 + tm, tf), F32)],
        compiler_params=_cparams(("arbitrary", "arbitrary", "arbitrary")),
        name="conv_ffn",
    )(h, g.reshape(1, d), w_in_bf16, w_in_bf16, cw, cw, cb, cb, w_out_bf16)


def _pad_cols(w, n):
    return jnp.pad(w, ((0, 0), (0, n - w.shape[1])))


def _layer(h, prm):
    (norm1_g, w_in, rw_mu, rw_w0, rw_w_up, rw_a0, rw_a_up, rw_g_up, rw_k_k, rw_k_a, rw_r_k,
     rw_gn_w, rw_gn_b, fx_b_f, fx_q_g, fx_k_g, w_o, norm2_g, ffn_w_in, ffn_conv_w, ffn_conv_b,
     ffn_w_out) = prm
    b, l, d = h.shape
    m = b * l
    rw_cols = 3 * WIDTH + DECAY_LORA + AAA_LORA + GATE_LORA

    w_rw = _pad_cols(w_in[:, :rw_cols], RW_PCOLS).astype(BF16)
    w_fx = _pad_cols(w_in[:, rw_cols:], FX_PCOLS).astype(BF16)
    mu_p = _pad_cols(rw_mu.reshape(1, rw_cols), RW_PCOLS)
    wcomb = jnp.zeros((128, 2 * WIDTH), F32)
    wcomb = wcomb.at[:DECAY_LORA, :WIDTH].set(rw_w_up).at[DECAY_LORA:, WIDTH:].set(rw_a_up)
    w0a0 = jnp.concatenate([rw_w0, rw_a0]).reshape(1, 2 * WIDTH)
    gup_p = jnp.pad(rw_g_up, ((0, GATE_PAD - GATE_LORA), (0, 0)))
    ones512 = _group_ones(WIDTH, HEAD_DIM)
    gain = jnp.concatenate([jnp.tile(fx_q_g, N_HEADS) * (HEAD_DIM ** -0.5 * LOG2E),
                            jnp.tile(fx_k_g, N_HEADS)]).reshape(1, 2 * WIDTH)
    bf_p = jnp.pad(fx_b_f, (0, LANES - N_HEADS)).reshape(1, LANES)

    t1 = _pick(l, (272, 128))
    t2 = _pick(l, (544, 384, 256, 128))

    r, lw, k, v, a, bb, g, bonus, og, qkvb = _proj_prep(
        h, norm1_g, w_rw, w_fx, mu_p, wcomb, w0a0, gup_p, rw_k_k.reshape(1, WIDTH),
        rw_k_a.reshape(1, WIDTH), rw_r_k.reshape(1, WIDTH), gain, bf_p, ones512, t1)

    pm = lambda z: z.reshape(b * PAIRS, l, LANES)
    y = _rwkv_chunk(pm(r), pm(lw), pm(k), pm(v), pm(a), pm(bb), npg=8)
    y = y.reshape(b, PAIRS, l, LANES)

    n_main = l // ATTN_BLOCK * ATTN_BLOCK
    parts = []
    if n_main:
        parts.append(_fox_attn(qkvb, 0, n_main, ATTN_BLOCK, ATTN_BLOCK))
    if l > n_main:
        parts.append(_fox_attn(qkvb, n_main, l - n_main, l - n_main, ATTN_BLOCK))
    o_fx = jnp.concatenate(parts, axis=1)

    h = _mix_out(h, y, bonus, g, o_fx, og, rw_gn_w.reshape(1, WIDTH), rw_gn_b.reshape(1, WIDTH),
                 ones512, w_o.astype(BF16), t2)

    d_ff = ffn_w_out.shape[0]
    tf = _pick(d_ff, (1408, 256, 128))
    return _conv_ffn(h, norm2_g, ffn_w_in.astype(BF16), ffn_conv_w, ffn_conv_b,
                     ffn_w_out.astype(BF16), t2, tf)


def kernel(x, meta, norm1_g, w_in, rw_mu, rw_w0, rw_w_up, rw_a0, rw_a_up, rw_g_up, rw_k_k, rw_k_a,
           rw_r_k, rw_gn_w, rw_gn_b, fx_b_f, fx_q_g, fx_k_g, w_o, norm2_g, ffn_w_in, ffn_conv_w,
           ffn_conv_b, ffn_w_out):
    b, seq, d = x.shape
    params = (norm1_g, w_in, rw_mu, rw_w0, rw_w_up, rw_a0, rw_a_up, rw_g_up, rw_k_k, rw_k_a, rw_r_k,
              rw_gn_w, rw_gn_b, fx_b_f, fx_q_g, fx_k_g, w_o, norm2_g, ffn_w_in, ffn_conv_w,
              ffn_conv_b, ffn_w_out)
    l = N_META + seq
    lp = -(-l // SB) * SB
    meta_b = jnp.broadcast_to(meta[None].astype(x.dtype), (b, N_META, d))
    h = jnp.concatenate([meta_b, x, jnp.zeros((b, lp - l, d), x.dtype)], axis=1)
    for layer in range(norm1_g.shape[0]):
        h = _layer(h, tuple(p[layer] for p in params))
    return h[:, N_META:l]
```

```python
import functools

import jax
import jax.numpy as jnp
from jax import lax
from jax.experimental import pallas as pl
from jax.experimental.pallas import tpu as pltpu

F32 = jnp.float32
BF16 = jnp.bfloat16

N_META = 16
HEAD_DIM = 64
N_HEADS = 8
WIDTH = N_HEADS * HEAD_DIM
DECAY_LORA = 64
AAA_LORA = 64
GATE_LORA = 160
GATE_PAD = 256
RW_PCOLS = 3 * WIDTH + 128 + GATE_PAD
FX_PCOLS = 4 * WIDTH + 128
CONV_W = 3
NORM_EPS = 1e-6
GN_EPS = HEAD_DIM * 1e-5
LANES = 128
PAIRS = WIDTH // LANES
CHUNK = 16
SB = 128
ATTN_BLOCK = 256
BIAS_LANES = 3
LOG2E = 1.4426950408889634
MASK_VALUE = -1e30
VMEM_LIMIT = 56 * 1024 * 1024


def _pick(n, cands):
    for c in cands:
        if n % c == 0:
            return c
    raise ValueError(f"no tile for {n} in {cands}")


def _cparams(sem):
    return pltpu.CompilerParams(dimension_semantics=sem, vmem_limit_bytes=VMEM_LIMIT)


def _sigmoid(x):
    return 1.0 / (1.0 + jnp.exp(-x))


def _softplus(x):
    return jnp.maximum(x, 0.0) + jnp.log(1.0 + jnp.exp(-jnp.abs(x)))


def _split2(x):
    hi = x.astype(BF16)
    lo = (x - hi.astype(F32)).astype(BF16)
    return hi, lo


def _split3(x):
    p1 = x.astype(BF16)
    rem = x - p1.astype(F32)
    p2 = rem.astype(BF16)
    p3 = (rem - p2.astype(F32)).astype(BF16)
    return p1, p2, p3


def _dot_ones(x, ones_bf16):
    return jnp.dot(x.astype(BF16), ones_bf16, preferred_element_type=F32)


def _mm3(a_hi, a_lo, b_hi, b_lo, dims=((1,), (0,))):
    dn = (dims, ((), ()))
    m_axis = 1 - dims[0][0]
    m = a_hi.shape[m_axis]
    d = lax.dot_general(jnp.concatenate([a_hi, a_lo], axis=m_axis), b_hi, dn,
                        preferred_element_type=F32)
    return d[:m] + d[m:] + lax.dot_general(a_hi, b_lo, dn, preferred_element_type=F32)


def _dot_bf16(a, b):
    return jnp.dot(a.astype(BF16), b.astype(BF16), preferred_element_type=F32)


def _group_ones(n, group):
    i = jnp.arange(n) // group
    return (i[:, None] == i[None, :]).astype(BF16)


def _project(xn, w_ref, p_ref, n_chunk=512):
    n = p_ref.shape[-1]
    for c in range(0, n, n_chunk):
        e = min(c + n_chunk, n)
        p_ref[:, c:e] = jnp.dot(xn, w_ref[:, c:e], preferred_element_type=F32)


def _rwkv_features(p_ref, mu_ref, wcomb_ref, w0a0_ref, gup_ref, kk_ref, ka_ref, rk_ref, ones_ref,
                   r_o, lw_o, k_o, v_o, a_o, b_o, g_o, bonus_o, carry_ref):
    t2 = p_ref.shape[0]
    p = p_ref[...]
    rows = lax.broadcasted_iota(jnp.int32, p.shape, 0)
    prev = jnp.where(rows == 0, carry_ref[...], pltpu.roll(p, 1, axis=0))
    carry_ref[...] = p[t2 - 1:t2, :]
    x = p + mu_ref[...] * (prev - p)

    r = x[:, 0:WIDTH]
    k = x[:, WIDTH:2 * WIDTH]
    v = x[:, 2 * WIDTH:3 * WIDTH]
    wa = x[:, 3 * WIDTH:3 * WIDTH + 128]
    gd = x[:, 3 * WIDTH + 128:]

    lane = lax.broadcasted_iota(jnp.int32, wa.shape, 1)
    xa = jnp.where(lane < DECAY_LORA, jnp.tanh(wa), wa)
    pre = _dot_bf16(xa, wcomb_ref[...]) + w0a0_ref[...]
    w_log = -_softplus(-pre[:, :WIDTH]) - 0.5
    log_decay = -jnp.exp(w_log)
    a = _sigmoid(pre[:, WIDTH:])
    g = _dot_bf16(_sigmoid(gd), gup_ref[...])

    ones = ones_ref[...]
    kk = k * kk_ref[...]
    ss = _dot_ones(kk * kk, ones)
    kk = kk / jnp.maximum(jnp.sqrt(ss), 1e-12)
    k2 = k * (1.0 + (a - 1.0) * ka_ref[...])
    bonus = _dot_ones(r * k2 * rk_ref[...], ones) * v

    for j in range(PAIRS):
        ls = slice(j * LANES, (j + 1) * LANES)
        r_o[j] = r[:, ls]
        lw_o[j] = log_decay[:, ls]
        k_o[j] = k2[:, ls]
        v_o[j] = v[:, ls].astype(v_o.dtype)
        a_o[j] = -kk[:, ls]
        b_o[j] = (kk * a)[:, ls]
    g_o[...] = g
    bonus_o[...] = bonus


def _fox_features(p_ref, gain_ref, bf_ref, ones_ref, place_ref, og_o, qkvb_o, carry_ref):
    t2 = p_ref.shape[0]
    ones = ones_ref[...]
    for h0 in (0, WIDTH):
        x = p_ref[:, h0:h0 + WIDTH]
        ms = _dot_ones(x * x, ones) * (1.0 / HEAD_DIM)
        qkvb_o[:, h0:h0 + WIDTH] = (x * lax.rsqrt(ms + NORM_EPS) * gain_ref[:, h0:h0 + WIDTH]).astype(BF16)
    qkvb_o[:, 2 * WIDTH:3 * WIDTH] = p_ref[:, 2 * WIDTH:3 * WIDTH].astype(BF16)
    og_o[...] = p_ref[:, 3 * WIDTH:4 * WIDTH]

    logf = -_softplus(-(p_ref[:, 4 * WIDTH:] + bf_ref[...]))
    ri = lax.broadcasted_iota(jnp.int32, (t2, t2), 0)
    ci = lax.broadcasted_iota(jnp.int32, (t2, t2), 1)
    tri = (ci <= ri).astype(BF16)
    c = sum(jnp.dot(tri, piece, preferred_element_type=F32) for piece in _split3(logf)) + carry_ref[...]
    carry_ref[...] = c[t2 - 1:t2, :]
    pieces = jnp.concatenate(_split3(c * (-LOG2E)), axis=1)
    qkvb_o[:, 3 * WIDTH:] = jnp.dot(pieces, place_ref[...], preferred_element_type=F32).astype(BF16)


def _proj_prep_body(h_ref, n1_ref, wrw_ref, wfx_ref, mu_ref, wcomb_ref, w0a0_ref, gup_ref, kk_ref,
                    ka_ref, rk_ref, gain_ref, bf_ref, ones_ref, place_ref,
                    r_o, lw_o, k_o, v_o, a_o, b_o, g_o, bonus_o, og_o, qkvb_o,
                    prw_ref, pfx_ref, shift_ref, csum_ref):
    @pl.when(pl.program_id(1) == 0)
    def _():
        shift_ref[...] = jnp.zeros_like(shift_ref)
        csum_ref[...] = jnp.zeros_like(csum_ref)

    x = h_ref[...]
    ms = jnp.mean(x * x, axis=-1, keepdims=True)
    xn = (x * lax.rsqrt(ms + NORM_EPS) * n1_ref[...]).astype(BF16)
    _project(xn, wrw_ref, prw_ref)
    _project(xn, wfx_ref, pfx_ref)
    _rwkv_features(prw_ref, mu_ref, wcomb_ref, w0a0_ref, gup_ref, kk_ref, ka_ref, rk_ref, ones_ref,
                   r_o, lw_o, k_o, v_o, a_o, b_o, g_o, bonus_o, shift_ref)
    _fox_features(pfx_ref, gain_ref, bf_ref, ones_ref, place_ref, og_o, qkvb_o, csum_ref)


def _bias_placement():
    src = jnp.arange(3 * LANES)
    piece, head = src // LANES, src % LANES
    dst = (head // 2) * LANES + BIAS_LANES * (head % 2) + piece
    hit = (dst[:, None] == jnp.arange(WIDTH)[None, :]) & (head[:, None] < N_HEADS)
    return hit.astype(BF16)


def _proj_prep(h, norm1_g, w_rw, w_fx, mu_p, wcomb, w0a0, gup_p, k_k, k_a, r_k, gain, bf_p, ones512, tm):
    b, l, d = h.shape
    full = lambda shape: pl.BlockSpec(shape, lambda bi, ti: (0,) * len(shape))
    pair = lambda dt: (jax.ShapeDtypeStruct((b, PAIRS, l, LANES), dt),
                       pl.BlockSpec((None, PAIRS, tm, LANES), lambda bi, ti: (bi, 0, ti, 0)))
    rows = lambda n, dt: (jax.ShapeDtypeStruct((b, l, n), dt),
                          pl.BlockSpec((None, tm, n), lambda bi, ti: (bi, ti, 0)))
    outs = [pair(F32), pair(F32), pair(F32), pair(BF16), pair(F32), pair(F32),
            rows(WIDTH, F32), rows(WIDTH, F32), rows(WIDTH, F32), rows(4 * WIDTH, BF16)]
    consts = (norm1_g.reshape(1, d), w_rw, w_fx, mu_p, wcomb, w0a0, gup_p, k_k, k_a, r_k, gain, bf_p,
              ones512, _bias_placement())
    return pl.pallas_call(
        _proj_prep_body,
        out_shape=[o[0] for o in outs],
        grid=(b, l // tm),
        in_specs=[pl.BlockSpec((None, tm, d), lambda bi, ti: (bi, ti, 0))] + [full(c.shape) for c in consts],
        out_specs=[o[1] for o in outs],
        scratch_shapes=[pltpu.VMEM((tm, RW_PCOLS), F32), pltpu.VMEM((tm, FX_PCOLS), F32),
                        pltpu.VMEM((1, RW_PCOLS), F32), pltpu.VMEM((1, LANES), F32)],
        compiler_params=_cparams(("arbitrary", "arbitrary")),
        name="proj_prep",
    )(h, *consts)


def _chunk_masks():
    ti = jnp.arange(SB)[:, None]
    si = jnp.arange(SB)[None, :]
    same = (ti // CHUNK) == (si // CHUNK)
    masks = [same, same & (si <= ti), same & (si < ti), ti == si, (ti < HEAD_DIM) == (si < HEAD_DIM)]
    return jnp.stack(masks).astype(F32)


def _rwkv_chunk_body(r_ref, lw_ref, k_ref, v_ref, a_ref, b_ref, masks_ref, y_ref, h_ref):
    npg = r_ref.shape[0]
    n_chunks = SB // CHUNK

    @pl.when(pl.program_id(1) == 0)
    def _():
        h_ref[...] = jnp.zeros_like(h_ref)

    same, incl, strict, _, same_head = (masks_ref[i] != 0.0 for i in range(5))
    eye_f = masks_ref[3]
    head0 = lax.broadcasted_iota(jnp.int32, (SB, LANES), 1) < HEAD_DIM
    tri_ones = jnp.concatenate([masks_ref[1], masks_ref[0]], axis=0).astype(BF16)
    bf = lambda z: z.astype(BF16)
    dot = functools.partial(jnp.dot, preferred_element_type=F32)

    def keep(mask, x):
        return jnp.where(mask, x, jnp.zeros_like(x))

    pairs = range(npg)
    units = [(p, h) for p in pairs for h in range(2)]
    cat0 = lambda *xs: jnp.concatenate(xs, axis=0)
    mm = lambda a, b, **kw: _mm3(*a, *b, **kw)

    lw = [lw_ref[p] for p in pairs]
    cc = [sum(jnp.dot(tri_ones, piece, preferred_element_type=F32) for piece in _split3(lw[p]))
          for p in pairs]
    cs = [cc[p][:SB] for p in pairs]
    ce = [cc[p][SB:] for p in pairs]
    g = [jnp.exp(cs[p]) for p in pairs]
    ginv = [jnp.exp(-cs[p]) for p in pairs]
    gend = [jnp.exp(ce[p] - cs[p]) for p in pairs]
    g_c = [jnp.exp(ce[p]) for p in pairs]
    at = [a_ref[p] * jnp.exp(cs[p] - lw[p]) for p in pairs]
    rt = [r_ref[p] * g[p] for p in pairs]
    v_b = [bf(v_ref[p]) for p in pairs]
    at_b = [bf(at[p]) for p in pairs]
    rt_b = [bf(rt[p]) for p in pairs]
    bk_t = [cat0(bf(b_ref[p] * ginv[p]), bf(k_ref[p] * ginv[p])) for p in pairs]
    bh_b = [bf(b_ref[p] * gend[p]) for p in pairs]
    kh_b = [bf(k_ref[p] * gend[p]) for p in pairs]

    gram = [lax.dot_general(cat0(keep(head0, at_b[p]), keep(head0, rt_b[p]),
                                 keep(~head0, at_b[p]), keep(~head0, rt_b[p])), bk_t[p],
                            (((1,), (1,)), ((), ())), preferred_element_type=F32)
            for p in pairs]
    gb = {(p, h): gram[p][2 * SB * h:2 * SB * (h + 1)] for p, h in units}
    mab = {u: keep(strict, gb[u][:SB, :SB]) for u in units}
    mkk_b = {u: bf(cat0(keep(strict, gb[u][:SB, SB:]), keep(incl, gb[u][SB:, SB:]))) for u in units}
    mrb_b = {u: bf(keep(incl, gb[u][SB:, :SB])) for u in units}

    def fold(z):
        return sum(z[c * CHUNK:(c + 1) * CHUNK] for c in range(n_chunks))

    def unfold(zf):
        return keep(same, jnp.concatenate([zf] * n_chunks, axis=0))

    nf = {u: fold(mab[u]) for u in units}
    tf = {u: fold(eye_f) + nf[u] for u in units}
    full = {u: _split2(mab[u]) for u in units}
    nf = {u: mm(_split2(nf[u]), full[u]) for u in units}
    for k in (2, 4, 8):
        full = {u: _split2(unfold(nf[u])) for u in units}
        if k < 8:
            both_f = {u: mm(_split2(cat0(nf[u], tf[u])), full[u]) for u in units}
            nf = {u: both_f[u][:CHUNK] for u in units}
            tf = {u: tf[u] + both_f[u][CHUNK:] for u in units}
        else:
            tf = {u: tf[u] + mm(_split2(tf[u]), full[u]) for u in units}
    t = {u: unfold(tf[u]) for u in units}
    wk = {u: dot(mkk_b[u], v_b[u[0]]) for u in units}
    x = {u: dot(bf(t[u]), jnp.concatenate([at_b[u[0]], bf(wk[u][:SB])], axis=1)) for u in units}
    ry = {u: dot(mrb_b[u], bf(x[u])) for u in units}
    both = lambda f: [jnp.where(head0, f((p, 0)), f((p, 1))) for p in pairs]
    abar_b = [bf(z) for z in both(lambda u: x[u][:, :LANES])]
    vbar = both(lambda u: x[u][:, LANES:])
    rbar_b = [bf(z) for z in both(lambda u: rt[u[0]] + ry[u][:, :LANES])]
    ybar = both(lambda u: wk[u][SB:] + ry[u][:, LANES:])

    g_ct = [g_c[p].T for p in pairs]

    hbd = [h_ref[p] for p in pairs]
    for c in range(n_chunks):
        sl = slice(c * CHUNK, (c + 1) * CHUNK)
        yu = [dot(cat0(rbar_b[p][sl], abar_b[p][sl]), bf(hbd[p])) for p in pairs]
        for p in pairs:
            y_ref[p, sl, :] = yu[p][:CHUNK] + ybar[p][sl]
        upd = [lax.dot_general(cat0(bh_b[p][sl], kh_b[p][sl]),
                               cat0(bf(yu[p][CHUNK:] + vbar[p][sl]), v_b[p][sl]),
                               (((0,), (0,)), ((), ())), preferred_element_type=F32)
               for p in pairs]
        hbd = [jnp.broadcast_to(g_ct[p][:, c * CHUNK:c * CHUNK + 1], (LANES, LANES)) * hbd[p]
               + keep(same_head, upd[p]) for p in pairs]
    for p in pairs:
        h_ref[p] = hbd[p]


def _rwkv_chunk(r, lw, k, v, a, b, npg):
    n_pairs, l, _ = r.shape
    blk = pl.BlockSpec((npg, SB, LANES), lambda gi, ti: (gi, ti, 0))
    return pl.pallas_call(
        _rwkv_chunk_body,
        out_shape=jax.ShapeDtypeStruct((n_pairs, l, LANES), F32),
        grid=(n_pairs // npg, l // SB),
        in_specs=[blk] * 6 + [pl.BlockSpec((5, SB, SB), lambda gi, ti: (0, 0, 0))],
        out_specs=blk,
        scratch_shapes=[pltpu.VMEM((npg, LANES, LANES), F32)],
        compiler_params=_cparams(("arbitrary", "arbitrary")),
        name="rwkv_chunk",
    )(r, lw, k, v, a, b, _chunk_masks())


def _fox_attn_body(q_ref, k_ref, v_ref, kb_ref, o_ref, m_ref, acc_ref, *, q_row0, tk):
    tq = q_ref.shape[0]
    row_first = q_row0 + pl.program_id(1) * tq

    m_ref[...] = jnp.full_like(m_ref, MASK_VALUE)
    acc_ref[...] = jnp.zeros_like(acc_ref)

    lane = lax.broadcasted_iota(jnp.int32, (1, LANES), 1)
    head0 = lane < HEAD_DIM
    row2 = lax.broadcasted_iota(jnp.int32, (2 * tq, LANES), 0)
    lane2 = lax.broadcasted_iota(jnp.int32, (2 * tq, LANES), 1)
    first_lane = jnp.where(row2 < tq, 0, BIAS_LANES)
    bias_on = (lane2 >= first_lane) & (lane2 < first_lane + BIAS_LANES)
    bias_q = jnp.where(bias_on, 1.0, 0.0).astype(BF16)

    def block(col_first, width, masked):
        rows = pl.ds(col_first, width)
        heads = range(N_HEADS)
        lanes = [slice(j * LANES, (j + 1) * LANES) for j in range(PAIRS)]
        s2 = []
        for ls in lanes:
            q = q_ref[:, ls]
            zero = jnp.zeros_like(q)
            q2 = jnp.concatenate([jnp.where(head0, q, zero), jnp.where(head0, zero, q)], axis=0)
            s2.append(lax.dot_general(jnp.concatenate([q2, bias_q], axis=1),
                                      jnp.concatenate([k_ref[rows, ls], kb_ref[rows, ls]], axis=1),
                                      (((1,), (1,)), ((), ())), preferred_element_type=F32))
        s = [s2[h // 2][(h % 2) * tq:(h % 2 + 1) * tq] for h in heads]
        if masked:
            causal = (lax.broadcasted_iota(jnp.int32, (tq, width), 1)
                      <= lax.broadcasted_iota(jnp.int32, (tq, width), 0))
            s = [jnp.where(causal, sh, MASK_VALUE) for sh in s]
        m_prev = [m_ref[h] for h in heads]
        m_new = [jnp.maximum(m_prev[h], jnp.max(s[h], axis=1, keepdims=True)) for h in heads]
        p = [jnp.exp2(s[h] - jnp.concatenate([m_new[h]] * (width // LANES), axis=1)).astype(BF16)
             for h in heads]
        v1 = []
        for ls in lanes:
            v = v_ref[rows, ls]
            one = jnp.ones_like(v)
            v1 += [jnp.where(head0, v, one), jnp.where(head0, one, v)]
        pv = [jnp.dot(p[h], v1[h], preferred_element_type=F32) for h in heads]
        for h in heads:
            acc_ref[h] = jnp.exp2(m_prev[h] - m_new[h]) * acc_ref[h] + pv[h]
            m_ref[h] = m_new[h]

    def full_block(kj, carry):
        block(pl.multiple_of(kj * tk, tk), tk, False)
        return carry

    lax.fori_loop(0, row_first // tk, full_block, 0)
    block(pl.multiple_of(row_first, tq), tq, True)

    for j in range(PAIRS):
        a0, a1 = acc_ref[2 * j], acc_ref[2 * j + 1]
        o_ref[:, j * LANES:(j + 1) * LANES] = jnp.where(
            head0, a0 / pltpu.roll(a0, HEAD_DIM, axis=1), a1 / pltpu.roll(a1, HEAD_DIM, axis=1))


def _fox_attn(qkvb, q_row0, n_rows, tq, tk):
    b, l, _ = qkvb.shape
    assert q_row0 % tk == 0 and q_row0 % tq == 0 and (tq % tk == 0 or n_rows == tq)
    qb0 = q_row0 // tq
    kv_spec = lambda col: pl.BlockSpec((None, l, WIDTH), lambda bi, qi: (bi, 0, col))
    return pl.pallas_call(
        functools.partial(_fox_attn_body, q_row0=q_row0, tk=tk),
        out_shape=jax.ShapeDtypeStruct((b, n_rows, WIDTH), F32),
        grid=(b, n_rows // tq),
        in_specs=[pl.BlockSpec((None, tq, WIDTH), lambda bi, qi: (bi, qb0 + qi, 0)),
                  kv_spec(1), kv_spec(2), kv_spec(3)],
        out_specs=pl.BlockSpec((None, tq, WIDTH), lambda bi, qi: (bi, qi, 0)),
        scratch_shapes=[pltpu.VMEM((N_HEADS, tq, LANES), F32), pltpu.VMEM((N_HEADS, tq, LANES), F32)],
        compiler_params=_cparams(("arbitrary", "arbitrary")),
        name="fox_attn",
    )(qkvb, qkvb, qkvb, qkvb)


def _mix_out_body(h_ref, y_ref, bonus_ref, g_ref, o_ref, og_ref, gnw_ref, gnb_ref, ones_ref, wo_ref, out_ref):
    ones = ones_ref[...]
    y = jnp.concatenate([y_ref[j] for j in range(PAIRS)], axis=1)
    mean = _dot_ones(y, ones) * (1.0 / HEAD_DIM)
    d = y - mean
    var = _dot_ones(d * d, ones) * (1.0 / HEAD_DIM)
    yn = d * lax.rsqrt(var + GN_EPS) * gnw_ref[...] + gnb_ref[...]
    y_rw = ((yn + bonus_ref[...]) * g_ref[...]).astype(BF16)
    y_fx = (o_ref[...] * _sigmoid(og_ref[...])).astype(BF16)
    out_ref[...] = (h_ref[...]
                    + jnp.dot(y_rw, wo_ref[0:WIDTH, :], preferred_element_type=F32)
                    + jnp.dot(y_fx, wo_ref[WIDTH:2 * WIDTH, :], preferred_element_type=F32))


def _mix_out(h, y, bonus, g, o_fx, og, gn_w, gn_b, ones512, wo_bf16, tm):
    b, l, d = h.shape
    full = lambda shape: pl.BlockSpec(shape, lambda bi, ti: (0,) * len(shape))
    rows = pl.BlockSpec((None, tm, WIDTH), lambda bi, ti: (bi, ti, 0))
    return pl.pallas_call(
        _mix_out_body,
        out_shape=jax.ShapeDtypeStruct((b, l, d), F32),
        grid=(b, l // tm),
        in_specs=[pl.BlockSpec((None, tm, d), lambda bi, ti: (bi, ti, 0)),
                  pl.BlockSpec((None, PAIRS, tm, LANES), lambda bi, ti: (bi, 0, ti, 0)),
                  rows, rows, rows, rows,
                  full((1, WIDTH)), full((1, WIDTH)), full((WIDTH, WIDTH)), full((2 * WIDTH, d))],
        out_specs=pl.BlockSpec((None, tm, d), lambda bi, ti: (bi, ti, 0)),
        compiler_params=_cparams(("arbitrary", "arbitrary")),
        name="mix_out",
    )(h, y, bonus, g, o_fx, og, gn_w, gn_b, ones512, wo_bf16)


def _conv_ffn_body(h_ref, g_ref, wu_ref, wg_ref, cwu_ref, cwg_ref, cbu_ref, cbg_ref, wout_ref,
                   out_ref, xn_ref, cu_ref, cg_ref, wu_s, wg_s):
    tm = h_ref.shape[0]
    halo = wu_s.shape[0] - tm
    ti = pl.program_id(1)
    j = pl.program_id(2)

    @pl.when(j == 0)
    def _():
        x = h_ref[...]
        ms = jnp.mean(x * x, axis=-1, keepdims=True)
        xn_ref[...] = (x * lax.rsqrt(ms + NORM_EPS) * g_ref[...]).astype(BF16)
        out_ref[...] = x

    @pl.when(ti == 0)
    def _():
        cu_ref[j] = jnp.zeros(cu_ref.shape[1:], F32)
        cg_ref[j] = jnp.zeros(cg_ref.shape[1:], F32)

    xn = xn_ref[...]

    def conv(w_ref, cw_ref, cb_ref, carry_ref, work):
        work[0:halo] = carry_ref[j]
        work[halo:] = jnp.dot(xn, w_ref[...], preferred_element_type=F32)
        carry_ref[j] = work[tm:]
        cw = cw_ref[...]
        taps = [work[pl.ds(halo - (CONV_W - 1) + i, tm), :] for i in range(CONV_W)]
        return cb_ref[...] + cw[0:1, :] * taps[0] + cw[1:2, :] * taps[1] + cw[2:3, :] * taps[2]

    u = conv(wu_ref, cwu_ref, cbu_ref, cu_ref, wu_s)
    gt = conv(wg_ref, cwg_ref, cbg_ref, cg_ref, wg_s)
    act = (gt * _sigmoid(gt) * u).astype(BF16)
    out_ref[...] += jnp.dot(act, wout_ref[...], preferred_element_type=F32)


def _conv_ffn(h, g, w_in_bf16, conv_w, conv_b, w_out_bf16, tm, tf):
    b, l, d = h.shape
    d_ff = w_out_bf16.shape[0]
    nj = d_ff // tf
    cw = jnp.pad(conv_w, ((0, 8 - CONV_W), (0, 0)))
    cb = conv_b.reshape(1, 2 * d_ff)
    return pl.pallas_call(
        _conv_ffn_body,
        out_shape=jax.ShapeDtypeStruct((b, l, d), F32),
        grid=(b, l // tm, nj),
        in_specs=[pl.BlockSpec((None, tm, d), lambda bi, ti, j: (bi, ti, 0)),
                  pl.BlockSpec((1, d), lambda bi, ti, j: (0, 0)),
                  pl.BlockSpec((d, tf), lambda bi, ti, j: (0, j)),
                  pl.BlockSpec((d, tf), lambda bi, ti, j: (0, nj + j)),
                  pl.BlockSpec((8, tf), lambda bi, ti, j: (0, j)),
                  pl.BlockSpec((8, tf), lambda bi, ti, j: (0, nj + j)),
                  pl.BlockSpec((1, tf), lambda bi, ti, j: (0, j)),
                  pl.BlockSpec((1, tf), lambda bi, ti, j: (0, nj + j)),
                  pl.BlockSpec((tf, d), lambda bi, ti, j: (j, 0))],
        out_specs=pl.BlockSpec((None, tm, d), lambda bi, ti, j: (bi, ti, 0)),
        scratch_shapes=[pltpu.VMEM((tm, d), BF16),
                        pltpu.VMEM((nj, 8, tf), F32), pltpu.VMEM((nj, 8, tf), F32),
                        pltpu.VMEM((8 + tm, tf), F32), pltpu.VMEM((8 + tm, tf), F32)],
        compiler_params=_cparams(("arbitrary", "arbitrary", "arbitrary")),
        name="conv_ffn",
    )(h, g.reshape(1, d), w_in_bf16, w_in_bf16, cw, cw, cb, cb, w_out_bf16)


def _pad_cols(w, n):
    return jnp.pad(w, ((0, 0), (0, n - w.shape[1])))


def _layer(h, prm):
    (norm1_g, w_in, rw_mu, rw_w0, rw_w_up, rw_a0, rw_a_up, rw_g_up, rw_k_k, rw_k_a, rw_r_k,
     rw_gn_w, rw_gn_b, fx_b_f, fx_q_g, fx_k_g, w_o, norm2_g, ffn_w_in, ffn_conv_w, ffn_conv_b,
     ffn_w_out) = prm
    b, l, d = h.shape
    m = b * l
    rw_cols = 3 * WIDTH + DECAY_LORA + AAA_LORA + GATE_LORA

    w_rw = _pad_cols(w_in[:, :rw_cols], RW_PCOLS).astype(BF16)
    w_fx = _pad_cols(w_in[:, rw_cols:], FX_PCOLS).astype(BF16)
    mu_p = _pad_cols(rw_mu.reshape(1, rw_cols), RW_PCOLS)
    wcomb = jnp.zeros((128, 2 * WIDTH), F32)
    wcomb = wcomb.at[:DECAY_LORA, :WIDTH].set(rw_w_up).at[DECAY_LORA:, WIDTH:].set(rw_a_up)
    w0a0 = jnp.concatenate([rw_w0, rw_a0]).reshape(1, 2 * WIDTH)
    gup_p = jnp.pad(rw_g_up, ((0, GATE_PAD - GATE_LORA), (0, 0)))
    ones512 = _group_ones(WIDTH, HEAD_DIM)
    gain = jnp.concatenate([jnp.tile(fx_q_g, N_HEADS) * (HEAD_DIM ** -0.5 * LOG2E),
                            jnp.tile(fx_k_g, N_HEADS)]).reshape(1, 2 * WIDTH)
    bf_p = jnp.pad(fx_b_f, (0, LANES - N_HEADS)).reshape(1, LANES)

    t1 = _pick(l, (272, 128))
    t2 = _pick(l, (544, 384, 256, 128))

    r, lw, k, v, a, bb, g, bonus, og, qkvb = _proj_prep(
        h, norm1_g, w_rw, w_fx, mu_p, wcomb, w0a0, gup_p, rw_k_k.reshape(1, WIDTH),
        rw_k_a.reshape(1, WIDTH), rw_r_k.reshape(1, WIDTH), gain, bf_p, ones512, t1)

    pm = lambda z: z.reshape(b * PAIRS, l, LANES)
    y = _rwkv_chunk(pm(r), pm(lw), pm(k), pm(v), pm(a), pm(bb), npg=8)
    y = y.reshape(b, PAIRS, l, LANES)

    n_main = l // ATTN_BLOCK * ATTN_BLOCK
    parts = []
    if n_main:
        parts.append(_fox_attn(qkvb, 0, n_main, ATTN_BLOCK, ATTN_BLOCK))
    if l > n_main:
        parts.append(_fox_attn(qkvb, n_main, l - n_main, l - n_main, ATTN_BLOCK))
    o_fx = jnp.concatenate(parts, axis=1)

    h = _mix_out(h, y, bonus, g, o_fx, og, rw_gn_w.reshape(1, WIDTH), rw_gn_b.reshape(1, WIDTH),
                 ones512, w_o.astype(BF16), t2)

    d_ff = ffn_w_out.shape[0]
    tf = _pick(d_ff, (1408, 256, 128))
    return _conv_ffn(h, norm2_g, ffn_w_in.astype(BF16), ffn_conv_w, ffn_conv_b,
                     ffn_w_out.astype(BF16), t2, tf)


def kernel(x, meta, norm1_g, w_in, rw_mu, rw_w0, rw_w_up, rw_a0, rw_a_up, rw_g_up, rw_k_k, rw_k_a,
           rw_r_k, rw_gn_w, rw_gn_b, fx_b_f, fx_q_g, fx_k_g, w_o, norm2_g, ffn_w_in, ffn_conv_w,
           ffn_conv_b, ffn_w_out):
    b, seq, d = x.shape
    params = (norm1_g, w_in, rw_mu, rw_w0, rw_w_up, rw_a0, rw_a_up, rw_g_up, rw_k_k, rw_k_a, rw_r_k,
              rw_gn_w, rw_gn_b, fx_b_f, fx_q_g, fx_k_g, w_o, norm2_g, ffn_w_in, ffn_conv_w,
              ffn_conv_b, ffn_w_out)
    l = N_META + seq
    lp = -(-l // SB) * SB
    meta_b = jnp.broadcast_to(meta[None].astype(x.dtype), (b, N_META, d))
    h = jnp.concatenate([meta_b, x, jnp.zeros((b, lp - l, d), x.dtype)], axis=1)
    for layer in range(norm1_g.shape[0]):
        h = _layer(h, tuple(p[layer] for p in params))
    return h[:, N_META:l]
```

```python
import functools

import jax
import jax.numpy as jnp
from jax import lax
from jax.experimental import pallas as pl
from jax.experimental.pallas import tpu as pltpu

F32 = jnp.float32
BF16 = jnp.bfloat16

N_META = 16
HEAD_DIM = 64
N_HEADS = 8
WIDTH = N_HEADS * HEAD_DIM
DECAY_LORA = 64
AAA_LORA = 64
GATE_LORA = 160
GATE_PAD = 256
RW_PCOLS = 3 * WIDTH + 128 + GATE_PAD
FX_PCOLS = 4 * WIDTH + 128
CONV_W = 3
NORM_EPS = 1e-6
GN_EPS = HEAD_DIM * 1e-5
LANES = 128
MXU_TILE = 256
PAIRS = WIDTH // LANES
CHUNK = 16
SB = 128
ATTN_BLOCK = 256
BIAS_LANES = 3
LOG2E = 1.4426950408889634
MASK_VALUE = -1e30
VMEM_LIMIT = 56 * 1024 * 1024


def _pick(n, cands):
    for c in cands:
        if n % c == 0:
            return c
    raise ValueError(f"no tile for {n} in {cands}")


def _cparams(sem):
    return pltpu.CompilerParams(dimension_semantics=sem, vmem_limit_bytes=VMEM_LIMIT)


def _sigmoid(x):
    return 1.0 / (1.0 + jnp.exp(-x))


def _softplus(x):
    return jnp.maximum(x, 0.0) + jnp.log(1.0 + jnp.exp(-jnp.abs(x)))


def _split2(x):
    hi = x.astype(BF16)
    lo = (x - hi.astype(F32)).astype(BF16)
    return hi, lo


def _split3(x):
    p1 = x.astype(BF16)
    rem = x - p1.astype(F32)
    p2 = rem.astype(BF16)
    p3 = (rem - p2.astype(F32)).astype(BF16)
    return p1, p2, p3


def _dot_ones(x, ones_bf16):
    xb = x.astype(BF16)
    t = ones_bf16.shape[0]
    return jnp.concatenate([jnp.dot(xb[:, c:c + t], ones_bf16, preferred_element_type=F32)
                            for c in range(0, x.shape[1], t)], axis=1)


def _mm3(a_hi, a_lo, b_hi, b_lo, dims=((1,), (0,))):
    dn = (dims, ((), ()))
    m_axis = 1 - dims[0][0]
    m = a_hi.shape[m_axis]
    d = lax.dot_general(jnp.concatenate([a_hi, a_lo], axis=m_axis), b_hi, dn,
                        preferred_element_type=F32)
    return d[:m] + d[m:] + lax.dot_general(a_hi, b_lo, dn, preferred_element_type=F32)


def _dot_bf16(a, b):
    return jnp.dot(a.astype(BF16), b.astype(BF16), preferred_element_type=F32)


def _group_ones(n, group):
    i = jnp.arange(n) // group
    return (i[:, None] == i[None, :]).astype(BF16)


def _project(xn, w_ref, p_ref, n_chunk=512):
    n = p_ref.shape[-1]
    for c in range(0, n, n_chunk):
        e = min(c + n_chunk, n)
        p_ref[:, c:e] = jnp.dot(xn, w_ref[:, c:e], preferred_element_type=F32)


def _rwkv_features(p_ref, mu_ref, wcomb_ref, w0a0_ref, gup_ref, kk_ref, ka_ref, rk_ref, ones_ref,
                   r_o, lw_o, k_o, v_o, a_o, b_o, g_o, bonus_o, carry_ref):
    t2 = p_ref.shape[0]
    p = p_ref[...]
    rows = lax.broadcasted_iota(jnp.int32, p.shape, 0)
    prev = jnp.where(rows == 0, carry_ref[...], pltpu.roll(p, 1, axis=0))
    carry_ref[...] = p[t2 - 1:t2, :]
    x = p + mu_ref[...] * (prev - p)

    r = x[:, 0:WIDTH]
    k = x[:, WIDTH:2 * WIDTH]
    v = x[:, 2 * WIDTH:3 * WIDTH]
    wa = x[:, 3 * WIDTH:3 * WIDTH + 128]
    gd = x[:, 3 * WIDTH + 128:]

    lane = lax.broadcasted_iota(jnp.int32, wa.shape, 1)
    xa = jnp.where(lane < DECAY_LORA, jnp.tanh(wa), wa)
    pre = _dot_bf16(xa, wcomb_ref[...]) + w0a0_ref[...]
    w_log = -_softplus(-pre[:, :WIDTH]) - 0.5
    log_decay = -jnp.exp(w_log)
    a = _sigmoid(pre[:, WIDTH:])
    g = _dot_bf16(_sigmoid(gd), gup_ref[...])

    ones = ones_ref[...]
    kk = k * kk_ref[...]
    ss = _dot_ones(kk * kk, ones)
    kk = kk / jnp.maximum(jnp.sqrt(ss), 1e-12)
    k2 = k * (1.0 + (a - 1.0) * ka_ref[...])
    bonus = _dot_ones(r * k2 * rk_ref[...], ones) * v

    for j in range(PAIRS):
        ls = slice(j * LANES, (j + 1) * LANES)
        r_o[j] = r[:, ls]
        lw_o[j] = log_decay[:, ls]
        k_o[j] = k2[:, ls]
        v_o[j] = v[:, ls].astype(v_o.dtype)
        a_o[j] = -kk[:, ls]
        b_o[j] = (kk * a)[:, ls]
    g_o[...] = g
    bonus_o[...] = bonus


def _fox_features(p_ref, gain_ref, bf_ref, ones_ref, place_ref, og_o, qkvb_o, carry_ref):
    t2 = p_ref.shape[0]
    ones = ones_ref[...]
    for h0 in (0, WIDTH):
        x = p_ref[:, h0:h0 + WIDTH]
        ms = _dot_ones(x * x, ones) * (1.0 / HEAD_DIM)
        qkvb_o[:, h0:h0 + WIDTH] = (x * lax.rsqrt(ms + NORM_EPS) * gain_ref[:, h0:h0 + WIDTH]).astype(BF16)
    qkvb_o[:, 2 * WIDTH:3 * WIDTH] = p_ref[:, 2 * WIDTH:3 * WIDTH].astype(BF16)
    og_o[...] = p_ref[:, 3 * WIDTH:4 * WIDTH]

    logf = -_softplus(-(p_ref[:, 4 * WIDTH:] + bf_ref[...]))
    ri = lax.broadcasted_iota(jnp.int32, (t2, t2), 0)
    ci = lax.broadcasted_iota(jnp.int32, (t2, t2), 1)
    tri = (ci <= ri).astype(BF16)
    c = sum(jnp.dot(tri, piece, preferred_element_type=F32) for piece in _split3(logf)) + carry_ref[...]
    carry_ref[...] = c[t2 - 1:t2, :]
    pieces = jnp.concatenate(_split3(c * (-LOG2E)), axis=1)
    qkvb_o[:, 3 * WIDTH:] = jnp.dot(pieces, place_ref[...], preferred_element_type=F32).astype(BF16)


def _proj_prep_body(h_ref, n1_ref, wrw_ref, wfx_ref, mu_ref, wcomb_ref, w0a0_ref, gup_ref, kk_ref,
                    ka_ref, rk_ref, gain_ref, bf_ref, ones_ref, place_ref,
                    r_o, lw_o, k_o, v_o, a_o, b_o, g_o, bonus_o, og_o, qkvb_o,
                    prw_ref, pfx_ref, shift_ref, csum_ref):
    @pl.when(pl.program_id(1) == 0)
    def _():
        shift_ref[...] = jnp.zeros_like(shift_ref)
        csum_ref[...] = jnp.zeros_like(csum_ref)

    x = h_ref[...]
    ms = jnp.mean(x * x, axis=-1, keepdims=True)
    xn = (x * lax.rsqrt(ms + NORM_EPS) * n1_ref[...]).astype(BF16)
    _project(xn, wrw_ref, prw_ref)
    _project(xn, wfx_ref, pfx_ref)
    _rwkv_features(prw_ref, mu_ref, wcomb_ref, w0a0_ref, gup_ref, kk_ref, ka_ref, rk_ref, ones_ref,
                   r_o, lw_o, k_o, v_o, a_o, b_o, g_o, bonus_o, shift_ref)
    _fox_features(pfx_ref, gain_ref, bf_ref, ones_ref, place_ref, og_o, qkvb_o, csum_ref)


def _bias_placement():
    src = jnp.arange(3 * LANES)
    piece, head = src // LANES, src % LANES
    dst = (head // 2) * LANES + BIAS_LANES * (head % 2) + piece
    hit = (dst[:, None] == jnp.arange(WIDTH)[None, :]) & (head[:, None] < N_HEADS)
    return hit.astype(BF16)


def _proj_prep(h, norm1_g, w_rw, w_fx, mu_p, wcomb, w0a0, gup_p, k_k, k_a, r_k, gain, bf_p, ones512, tm):
    b, l, d = h.shape
    full = lambda shape: pl.BlockSpec(shape, lambda bi, ti: (0,) * len(shape))
    pair = lambda dt: (jax.ShapeDtypeStruct((b, PAIRS, l, LANES), dt),
                       pl.BlockSpec((None, PAIRS, tm, LANES), lambda bi, ti: (bi, 0, ti, 0)))
    rows = lambda n, dt: (jax.ShapeDtypeStruct((b, l, n), dt),
                          pl.BlockSpec((None, tm, n), lambda bi, ti: (bi, ti, 0)))
    outs = [pair(F32), pair(F32), pair(F32), pair(BF16), pair(F32), pair(F32),
            rows(WIDTH, F32), rows(WIDTH, F32), rows(WIDTH, F32), rows(4 * WIDTH, BF16)]
    consts = (norm1_g.reshape(1, d), w_rw, w_fx, mu_p, wcomb, w0a0, gup_p, k_k, k_a, r_k, gain, bf_p,
              ones512, _bias_placement())
    return pl.pallas_call(
        _proj_prep_body,
        out_shape=[o[0] for o in outs],
        grid=(b, l // tm),
        in_specs=[pl.BlockSpec((None, tm, d), lambda bi, ti: (bi, ti, 0))] + [full(c.shape) for c in consts],
        out_specs=[o[1] for o in outs],
        scratch_shapes=[pltpu.VMEM((tm, RW_PCOLS), F32), pltpu.VMEM((tm, FX_PCOLS), F32),
                        pltpu.VMEM((1, RW_PCOLS), F32), pltpu.VMEM((1, LANES), F32)],
        compiler_params=_cparams(("arbitrary", "arbitrary")),
        name="proj_prep",
    )(h, *consts)


def _chunk_masks():
    ti = jnp.arange(SB)[:, None]
    si = jnp.arange(SB)[None, :]
    same = (ti // CHUNK) == (si // CHUNK)
    masks = [same, same & (si <= ti), same & (si < ti), ti == si, (ti < HEAD_DIM) == (si < HEAD_DIM)]
    return jnp.stack(masks).astype(F32)


def _rwkv_chunk_body(r_ref, lw_ref, k_ref, v_ref, a_ref, b_ref, masks_ref, y_ref, h_ref):
    npg = r_ref.shape[0]
    n_chunks = SB // CHUNK

    @pl.when(pl.program_id(1) == 0)
    def _():
        h_ref[...] = jnp.zeros_like(h_ref)

    same, incl, strict, _, same_head = (masks_ref[i] != 0.0 for i in range(5))
    eye_f = masks_ref[3]
    head0 = lax.broadcasted_iota(jnp.int32, (SB, LANES), 1) < HEAD_DIM
    tri = masks_ref[1].astype(BF16)
    bf = lambda z: z.astype(BF16)
    dot = functools.partial(jnp.dot, preferred_element_type=F32)

    def keep(mask, x):
        return jnp.where(mask, x, jnp.zeros_like(x))

    pairs = range(npg)
    units = [(p, h) for p in pairs for h in range(2)]
    cat0 = lambda *xs: jnp.concatenate(xs, axis=0)
    mm = lambda a, b, **kw: _mm3(*a, *b, **kw)

    lw = [lw_ref[p] for p in pairs]
    cs = [sum(dot(tri, piece) for piece in _split2(lw[p])) for p in pairs]
    ce = [cat0(*[jnp.broadcast_to(cs[p][(c + 1) * CHUNK - 1:(c + 1) * CHUNK, :], (CHUNK, LANES))
                 for c in range(n_chunks)]) for p in pairs]
    g = [jnp.exp(cs[p]) for p in pairs]
    ginv = [jnp.exp(-cs[p]) for p in pairs]
    gend = [jnp.exp(ce[p] - cs[p]) for p in pairs]
    g_c = [jnp.exp(ce[p]) for p in pairs]
    at = [a_ref[p] * jnp.exp(cs[p] - lw[p]) for p in pairs]
    rt = [r_ref[p] * g[p] for p in pairs]
    v_b = [bf(v_ref[p]) for p in pairs]
    at_b = [bf(at[p]) for p in pairs]
    rt_b = [bf(rt[p]) for p in pairs]
    bk_t = [cat0(bf(b_ref[p] * ginv[p]), bf(k_ref[p] * ginv[p])) for p in pairs]
    bh_b = [bf(b_ref[p] * gend[p]) for p in pairs]
    kh_b = [bf(k_ref[p] * gend[p]) for p in pairs]

    gram = [lax.dot_general(cat0(keep(head0, at_b[p]), keep(head0, rt_b[p]),
                                 keep(~head0, at_b[p]), keep(~head0, rt_b[p])), bk_t[p],
                            (((1,), (1,)), ((), ())), preferred_element_type=F32)
            for p in pairs]
    gb = {(p, h): gram[p][2 * SB * h:2 * SB * (h + 1)] for p, h in units}
    mab = {u: keep(strict, gb[u][:SB, :SB]) for u in units}
    mkk_b = {u: bf(cat0(keep(strict, gb[u][:SB, SB:]), keep(incl, gb[u][SB:, SB:]))) for u in units}
    mrb_b = {u: bf(keep(incl, gb[u][SB:, :SB])) for u in units}

    def fold(z):
        return sum(z[c * CHUNK:(c + 1) * CHUNK] for c in range(n_chunks))

    def unfold(zf):
        return keep(same, jnp.concatenate([zf] * n_chunks, axis=0))

    nf = {u: fold(mab[u]) for u in units}
    tf = {u: fold(eye_f) + nf[u] for u in units}
    full = {u: _split2(mab[u]) for u in units}
    nf = {u: mm(_split2(nf[u]), full[u]) for u in units}
    for k in (2, 4, 8):
        full = {u: _split2(unfold(nf[u])) for u in units}
        if k < 8:
            both_f = {u: mm(_split2(cat0(nf[u], tf[u])), full[u]) for u in units}
            nf = {u: both_f[u][:CHUNK] for u in units}
            tf = {u: tf[u] + both_f[u][CHUNK:] for u in units}
        else:
            tf = {u: tf[u] + mm(_split2(tf[u]), full[u]) for u in units}
    t = {u: unfold(tf[u]) for u in units}
    wk = {u: dot(mkk_b[u], v_b[u[0]]) for u in units}
    x = {u: dot(bf(t[u]), jnp.concatenate([at_b[u[0]], bf(wk[u][:SB])], axis=1)) for u in units}
    ry = {u: dot(mrb_b[u], bf(x[u])) for u in units}
    both = lambda f: [jnp.where(head0, f((p, 0)), f((p, 1))) for p in pairs]
    abar_b = [bf(z) for z in both(lambda u: x[u][:, :LANES])]
    vbar = both(lambda u: x[u][:, LANES:])
    rbar_b = [bf(z) for z in both(lambda u: rt[u[0]] + ry[u][:, :LANES])]
    ybar = both(lambda u: wk[u][SB:] + ry[u][:, LANES:])

    g_ct = [g_c[p].T for p in pairs]

    hbd = [h_ref[p] for p in pairs]
    for c in range(n_chunks):
        sl = slice(c * CHUNK, (c + 1) * CHUNK)
        yu = [dot(cat0(rbar_b[p][sl], abar_b[p][sl]), bf(hbd[p])) for p in pairs]
        for p in pairs:
            y_ref[p, sl, :] = yu[p][:CHUNK] + ybar[p][sl]
        upd = [lax.dot_general(cat0(bh_b[p][sl], kh_b[p][sl]),
                               cat0(bf(yu[p][CHUNK:] + vbar[p][sl]), v_b[p][sl]),
                               (((0,), (0,)), ((), ())), preferred_element_type=F32)
               for p in pairs]
        hbd = [jnp.broadcast_to(g_ct[p][:, c * CHUNK:c * CHUNK + 1], (LANES, LANES)) * hbd[p]
               + keep(same_head, upd[p]) for p in pairs]
    for p in pairs:
        h_ref[p] = hbd[p]


def _rwkv_chunk(r, lw, k, v, a, b, npg):
    n_pairs, l, _ = r.shape
    blk = pl.BlockSpec((npg, SB, LANES), lambda gi, ti: (gi, ti, 0))
    return pl.pallas_call(
        _rwkv_chunk_body,
        out_shape=jax.ShapeDtypeStruct((n_pairs, l, LANES), F32),
        grid=(n_pairs // npg, l // SB),
        in_specs=[blk] * 6 + [pl.BlockSpec((5, SB, SB), lambda gi, ti: (0, 0, 0))],
        out_specs=blk,
        scratch_shapes=[pltpu.VMEM((npg, LANES, LANES), F32)],
        compiler_params=_cparams(("arbitrary", "arbitrary")),
        name="rwkv_chunk",
    )(r, lw, k, v, a, b, _chunk_masks())


def _fox_attn_body(q_ref, k_ref, v_ref, kb_ref, o_ref, m_ref, acc_ref, *, q_row0, tk):
    tq = q_ref.shape[0]
    row_first = q_row0 + pl.program_id(1) * tq

    m_ref[...] = jnp.full_like(m_ref, MASK_VALUE)
    acc_ref[...] = jnp.zeros_like(acc_ref)

    lane = lax.broadcasted_iota(jnp.int32, (1, LANES), 1)
    head0 = lane < HEAD_DIM
    row2 = lax.broadcasted_iota(jnp.int32, (2 * tq, LANES), 0)
    lane2 = lax.broadcasted_iota(jnp.int32, (2 * tq, LANES), 1)
    first_lane = jnp.where(row2 < tq, 0, BIAS_LANES)
    bias_on = (lane2 >= first_lane) & (lane2 < first_lane + BIAS_LANES)
    bias_q = jnp.where(bias_on, 1.0, 0.0).astype(BF16)

    def block(col_first, width, masked):
        rows = pl.ds(col_first, width)
        heads = range(N_HEADS)
        lanes = [slice(j * LANES, (j + 1) * LANES) for j in range(PAIRS)]
        s2 = []
        for ls in lanes:
            q = q_ref[:, ls]
            zero = jnp.zeros_like(q)
            q2 = jnp.concatenate([jnp.where(head0, q, zero), jnp.where(head0, zero, q)], axis=0)
            s2.append(lax.dot_general(jnp.concatenate([q2, bias_q], axis=1),
                                      jnp.concatenate([k_ref[rows, ls], kb_ref[rows, ls]], axis=1),
                                      (((1,), (1,)), ((), ())), preferred_element_type=F32))
        s = [s2[h // 2][(h % 2) * tq:(h % 2 + 1) * tq] for h in heads]
        if masked:
            causal = (lax.broadcasted_iota(jnp.int32, (tq, width), 1)
                      <= lax.broadcasted_iota(jnp.int32, (tq, width), 0))
            s = [jnp.where(causal, sh, MASK_VALUE) for sh in s]
        m_prev = [m_ref[h] for h in heads]
        m_new = [jnp.maximum(m_prev[h], jnp.max(s[h], axis=1, keepdims=True)) for h in heads]
        p = [jnp.exp2(s[h] - jnp.concatenate([m_new[h]] * (width // LANES), axis=1)).astype(BF16)
             for h in heads]
        v1 = []
        for ls in lanes:
            v = v_ref[rows, ls]
            one = jnp.ones_like(v)
            v1 += [jnp.where(head0, v, one), jnp.where(head0, one, v)]
        pv = [jnp.dot(p[h], v1[h], preferred_element_type=F32) for h in heads]
        for h in heads:
            acc_ref[h] = jnp.exp2(m_prev[h] - m_new[h]) * acc_ref[h] + pv[h]
            m_ref[h] = m_new[h]

    def full_block(kj, carry):
        block(pl.multiple_of(kj * tk, tk), tk, False)
        return carry

    lax.fori_loop(0, row_first // tk, full_block, 0)
    diag_w = max(tq, LANES)
    block(pl.multiple_of(row_first, tq), diag_w, True)

    if o_ref.shape[0] > tq:
        o_ref[...] = jnp.zeros_like(o_ref)
    for j in range(PAIRS):
        a0, a1 = acc_ref[2 * j], acc_ref[2 * j + 1]
        o_ref[0:tq, j * LANES:(j + 1) * LANES] = jnp.where(
            head0, a0 / pltpu.roll(a0, HEAD_DIM, axis=1), a1 / pltpu.roll(a1, HEAD_DIM, axis=1))


def _fox_attn(qkvb, q_row0, n_rows, tq, tk, out_rows=None):
    b, l, _ = qkvb.shape
    out_rows = out_rows or tq
    assert q_row0 % tk == 0 and q_row0 % tq == 0 and (tq % tk == 0 or n_rows == tq)
    assert q_row0 + n_rows - tq + max(tq, LANES) <= l
    qb0 = q_row0 // tq
    kv_spec = lambda col: pl.BlockSpec((None, l, WIDTH), lambda bi, qi: (bi, 0, col))
    return pl.pallas_call(
        functools.partial(_fox_attn_body, q_row0=q_row0, tk=tk),
        out_shape=jax.ShapeDtypeStruct((b, n_rows // tq * out_rows, WIDTH), F32),
        grid=(b, n_rows // tq),
        in_specs=[pl.BlockSpec((None, tq, WIDTH), lambda bi, qi: (bi, qb0 + qi, 0)),
                  kv_spec(1), kv_spec(2), kv_spec(3)],
        out_specs=pl.BlockSpec((None, out_rows, WIDTH), lambda bi, qi: (bi, qi, 0)),
        scratch_shapes=[pltpu.VMEM((N_HEADS, tq, LANES), F32), pltpu.VMEM((N_HEADS, tq, LANES), F32)],
        compiler_params=_cparams(("arbitrary", "arbitrary")),
        name="fox_attn",
    )(qkvb, qkvb, qkvb, qkvb)


def _mix_out_body(h_ref, y_ref, bonus_ref, g_ref, o_ref, og_ref, gnw_ref, gnb_ref, ones_ref, wo_ref, out_ref):
    ones = ones_ref[...]
    y = jnp.concatenate([y_ref[j] for j in range(PAIRS)], axis=1)
    mean = _dot_ones(y, ones) * (1.0 / HEAD_DIM)
    d = y - mean
    var = _dot_ones(d * d, ones) * (1.0 / HEAD_DIM)
    yn = d * lax.rsqrt(var + GN_EPS) * gnw_ref[...] + gnb_ref[...]
    y_rw = ((yn + bonus_ref[...]) * g_ref[...]).astype(BF16)
    y_fx = (o_ref[...] * _sigmoid(og_ref[...])).astype(BF16)
    out_ref[...] = (h_ref[...]
                    + jnp.dot(y_rw, wo_ref[0:WIDTH, :], preferred_element_type=F32)
                    + jnp.dot(y_fx, wo_ref[WIDTH:2 * WIDTH, :], preferred_element_type=F32))


def _mix_out(h, y, bonus, g, o_fx, og, gn_w, gn_b, ones512, wo_bf16, tm):
    b, l, d = h.shape
    full = lambda shape: pl.BlockSpec(shape, lambda bi, ti: (0,) * len(shape))
    rows = pl.BlockSpec((None, tm, WIDTH), lambda bi, ti: (bi, ti, 0))
    return pl.pallas_call(
        _mix_out_body,
        out_shape=jax.ShapeDtypeStruct((b, l, d), F32),
        grid=(b, l // tm),
        in_specs=[pl.BlockSpec((None, tm, d), lambda bi, ti: (bi, ti, 0)),
                  pl.BlockSpec((None, PAIRS, tm, LANES), lambda bi, ti: (bi, 0, ti, 0)),
                  rows, rows, rows, rows,
                  full((1, WIDTH)), full((1, WIDTH)), full(ones512.shape), full((2 * WIDTH, d))],
        out_specs=pl.BlockSpec((None, tm, d), lambda bi, ti: (bi, ti, 0)),
        compiler_params=_cparams(("arbitrary", "arbitrary")),
        name="mix_out",
    )(h, y, bonus, g, o_fx, og, gn_w, gn_b, ones512, wo_bf16)


def _conv_ffn_body(h_ref, g_ref, wu_ref, wg_ref, cwu_ref, cwg_ref, cbu_ref, cbg_ref, wout_ref,
                   out_ref, xn_ref, cu_ref, cg_ref, wu_s, wg_s):
    tm = h_ref.shape[0]
    halo = wu_s.shape[0] - tm
    ti = pl.program_id(1)
    j = pl.program_id(2)

    @pl.when(j == 0)
    def _():
        x = h_ref[...]
        ms = jnp.mean(x * x, axis=-1, keepdims=True)
        xn_ref[...] = (x * lax.rsqrt(ms + NORM_EPS) * g_ref[...]).astype(BF16)
        out_ref[...] = x

    @pl.when(ti == 0)
    def _():
        cu_ref[j] = jnp.zeros(cu_ref.shape[1:], F32)
        cg_ref[j] = jnp.zeros(cg_ref.shape[1:], F32)

    xn = xn_ref[...]

    def conv(w_ref, cw_ref, cb_ref, carry_ref, work):
        work[0:halo] = carry_ref[j]
        work[halo:] = jnp.dot(xn, w_ref[...], preferred_element_type=F32)
        carry_ref[j] = work[tm:]
        cw = cw_ref[...]
        taps = [work[pl.ds(halo - (CONV_W - 1) + i, tm), :] for i in range(CONV_W)]
        return cb_ref[...] + cw[0:1, :] * taps[0] + cw[1:2, :] * taps[1] + cw[2:3, :] * taps[2]

    u = conv(wu_ref, cwu_ref, cbu_ref, cu_ref, wu_s)
    gt = conv(wg_ref, cwg_ref, cbg_ref, cg_ref, wg_s)
    act = (gt * _sigmoid(gt) * u).astype(BF16)
    out_ref[...] += jnp.dot(act, wout_ref[...], preferred_element_type=F32)


def _conv_ffn(h, g, w_in_bf16, conv_w, conv_b, w_out_bf16, tm, tf):
    b, l, d = h.shape
    d_ff = w_out_bf16.shape[0]
    nj = d_ff // tf
    cw = jnp.pad(conv_w, ((0, 8 - CONV_W), (0, 0)))
    cb = conv_b.reshape(1, 2 * d_ff)
    return pl.pallas_call(
        _conv_ffn_body,
        out_shape=jax.ShapeDtypeStruct((b, l, d), F32),
        grid=(b, l // tm, nj),
        in_specs=[pl.BlockSpec((None, tm, d), lambda bi, ti, j: (bi, ti, 0)),
                  pl.BlockSpec((1, d), lambda bi, ti, j: (0, 0)),
                  pl.BlockSpec((d, tf), lambda bi, ti, j: (0, j)),
                  pl.BlockSpec((d, tf), lambda bi, ti, j: (0, nj + j)),
                  pl.BlockSpec((8, tf), lambda bi, ti, j: (0, j)),
                  pl.BlockSpec((8, tf), lambda bi, ti, j: (0, nj + j)),
                  pl.BlockSpec((1, tf), lambda bi, ti, j: (0, j)),
                  pl.BlockSpec((1, tf), lambda bi, ti, j: (0, nj + j)),
                  pl.BlockSpec((tf, d), lambda bi, ti, j: (j, 0))],
        out_specs=pl.BlockSpec((None, tm, d), lambda bi, ti, j: (bi, ti, 0)),
        scratch_shapes=[pltpu.VMEM((tm, d), BF16),
                        pltpu.VMEM((nj, 8, tf), F32), pltpu.VMEM((nj, 8, tf), F32),
                        pltpu.VMEM((8 + tm, tf), F32), pltpu.VMEM((8 + tm, tf), F32)],
        compiler_params=_cparams(("arbitrary", "arbitrary", "arbitrary")),
        name="conv_ffn",
    )(h, g.reshape(1, d), w_in_bf16, w_in_bf16, cw, cw, cb, cb, w_out_bf16)


def _pad_cols(w, n):
    return jnp.pad(w, ((0, 0), (0, n - w.shape[1])))


def _layer(h, prm, l_real):
    (norm1_g, w_in, rw_mu, rw_w0, rw_w_up, rw_a0, rw_a_up, rw_g_up, rw_k_k, rw_k_a, rw_r_k,
     rw_gn_w, rw_gn_b, fx_b_f, fx_q_g, fx_k_g, w_o, norm2_g, ffn_w_in, ffn_conv_w, ffn_conv_b,
     ffn_w_out) = prm
    b, l, d = h.shape
    m = b * l
    rw_cols = 3 * WIDTH + DECAY_LORA + AAA_LORA + GATE_LORA

    w_rw = _pad_cols(w_in[:, :rw_cols], RW_PCOLS).astype(BF16)
    w_fx = _pad_cols(w_in[:, rw_cols:], FX_PCOLS).astype(BF16)
    mu_p = _pad_cols(rw_mu.reshape(1, rw_cols), RW_PCOLS)
    wcomb = jnp.zeros((128, 2 * WIDTH), F32)
    wcomb = wcomb.at[:DECAY_LORA, :WIDTH].set(rw_w_up).at[DECAY_LORA:, WIDTH:].set(rw_a_up)
    w0a0 = jnp.concatenate([rw_w0, rw_a0]).reshape(1, 2 * WIDTH)
    gup_p = jnp.pad(rw_g_up, ((0, GATE_PAD - GATE_LORA), (0, 0)))
    ones512 = _group_ones(MXU_TILE, HEAD_DIM)
    gain = jnp.concatenate([jnp.tile(fx_q_g, N_HEADS) * (HEAD_DIM ** -0.5 * LOG2E),
                            jnp.tile(fx_k_g, N_HEADS)]).reshape(1, 2 * WIDTH)
    bf_p = jnp.pad(fx_b_f, (0, LANES - N_HEADS)).reshape(1, LANES)

    t1 = _pick(l, (272, 128))
    t2 = _pick(l, (544, 384, 256, 128))

    r, lw, k, v, a, bb, g, bonus, og, qkvb = _proj_prep(
        h, norm1_g, w_rw, w_fx, mu_p, wcomb, w0a0, gup_p, rw_k_k.reshape(1, WIDTH),
        rw_k_a.reshape(1, WIDTH), rw_r_k.reshape(1, WIDTH), gain, bf_p, ones512, t1)

    pm = lambda z: z.reshape(b * PAIRS, l, LANES)
    y = _rwkv_chunk(pm(r), pm(lw), pm(k), pm(v), pm(a), pm(bb), npg=8)
    y = y.reshape(b, PAIRS, l, LANES)

    n_main = l // ATTN_BLOCK * ATTN_BLOCK
    parts = []
    if n_main:
        parts.append(_fox_attn(qkvb, 0, n_main, ATTN_BLOCK, ATTN_BLOCK))
    if l > n_main:
        tq_tail = min(max(l_real - n_main, 1) + 15 & ~15, l - n_main)
        parts.append(_fox_attn(qkvb, n_main, tq_tail, tq_tail, ATTN_BLOCK, out_rows=l - n_main))
    o_fx = jnp.concatenate(parts, axis=1)

    h = _mix_out(h, y, bonus, g, o_fx, og, rw_gn_w.reshape(1, WIDTH), rw_gn_b.reshape(1, WIDTH),
                 ones512, w_o.astype(BF16), t2)

    d_ff = ffn_w_out.shape[0]
    tf = _pick(d_ff, (1408, 256, 128))
    return _conv_ffn(h, norm2_g, ffn_w_in.astype(BF16), ffn_conv_w, ffn_conv_b,
                     ffn_w_out.astype(BF16), t2, tf)


def kernel(x, meta, norm1_g, w_in, rw_mu, rw_w0, rw_w_up, rw_a0, rw_a_up, rw_g_up, rw_k_k, rw_k_a,
           rw_r_k, rw_gn_w, rw_gn_b, fx_b_f, fx_q_g, fx_k_g, w_o, norm2_g, ffn_w_in, ffn_conv_w,
           ffn_conv_b, ffn_w_out):
    b, seq, d = x.shape
    params = (norm1_g, w_in, rw_mu, rw_w0, rw_w_up, rw_a0, rw_a_up, rw_g_up, rw_k_k, rw_k_a, rw_r_k,
              rw_gn_w, rw_gn_b, fx_b_f, fx_q_g, fx_k_g, w_o, norm2_g, ffn_w_in, ffn_conv_w,
              ffn_conv_b, ffn_w_out)
    l = N_META + seq
    lp = -(-l // SB) * SB
    meta_b = jnp.broadcast_to(meta[None].astype(x.dtype), (b, N_META, d))
    h = jnp.concatenate([meta_b, x, jnp.zeros((b, lp - l, d), x.dtype)], axis=1)
    for layer in range(norm1_g.shape[0]):
        h = _layer(h, tuple(p[layer] for p in params), l)
    return h[:, N_META:l]
```

```python
import functools

import jax
import jax.numpy as jnp
from jax import lax
from jax.experimental import pallas as pl
from jax.experimental.pallas import tpu as pltpu

F32 = jnp.float32
BF16 = jnp.bfloat16

N_META = 16
HEAD_DIM = 64
N_HEADS = 8
WIDTH = N_HEADS * HEAD_DIM
DECAY_LORA = 64
AAA_LORA = 64
GATE_LORA = 160
GATE_PAD = 256
RW_PCOLS = 3 * WIDTH + 128 + GATE_PAD
FX_PCOLS = 4 * WIDTH + 128
CONV_W = 3
NORM_EPS = 1e-6
GN_EPS = HEAD_DIM * 1e-5
LANES = 128
MXU_TILE = 256
PAIRS = WIDTH // LANES
CHUNK = 16
SB = 128
ATTN_BLOCK = 256
ATTN_KEYS = 512
BIAS_LANES = 3
LOG2E = 1.4426950408889634
MASK_VALUE = -1e30
VMEM_LIMIT = 56 * 1024 * 1024


def _pick(n, cands):
    for c in cands:
        if n % c == 0:
            return c
    raise ValueError(f"no tile for {n} in {cands}")


def _cparams(sem):
    return pltpu.CompilerParams(dimension_semantics=sem, vmem_limit_bytes=VMEM_LIMIT)


def _sigmoid(x):
    return 1.0 / (1.0 + jnp.exp(-x))


def _softplus(x):
    return jnp.maximum(x, 0.0) + jnp.log(1.0 + jnp.exp(-jnp.abs(x)))


def _split2(x):
    hi = x.astype(BF16)
    lo = (x - hi.astype(F32)).astype(BF16)
    return hi, lo


def _split3(x):
    p1 = x.astype(BF16)
    rem = x - p1.astype(F32)
    p2 = rem.astype(BF16)
    p3 = (rem - p2.astype(F32)).astype(BF16)
    return p1, p2, p3


def _dot_ones(x, ones_bf16):
    xb = x.astype(BF16)
    t = ones_bf16.shape[0]
    return jnp.concatenate([jnp.dot(xb[:, c:c + t], ones_bf16, preferred_element_type=F32)
                            for c in range(0, x.shape[1], t)], axis=1)


def _mm3(a_hi, a_lo, b_hi, b_lo, dims=((1,), (0,))):
    dn = (dims, ((), ()))
    m_axis = 1 - dims[0][0]
    m = a_hi.shape[m_axis]
    d = lax.dot_general(jnp.concatenate([a_hi, a_lo], axis=m_axis), b_hi, dn,
                        preferred_element_type=F32)
    return d[:m] + d[m:] + lax.dot_general(a_hi, b_lo, dn, preferred_element_type=F32)


def _dot_bf16(a, b):
    return jnp.dot(a.astype(BF16), b.astype(BF16), preferred_element_type=F32)


def _group_ones(n, group):
    i = jnp.arange(n) // group
    return (i[:, None] == i[None, :]).astype(BF16)


def _project(xn, w_ref, p_ref, n_chunk=512):
    n = p_ref.shape[-1]
    for c in range(0, n, n_chunk):
        e = min(c + n_chunk, n)
        p_ref[:, c:e] = jnp.dot(xn, w_ref[:, c:e], preferred_element_type=F32)


def _rwkv_features(p_ref, mu_ref, wcomb_ref, w0a0_ref, gup_ref, kk_ref, ka_ref, rk_ref, ones_ref,
                   r_o, lw_o, k_o, v_o, a_o, b_o, g_o, bonus_o, carry_ref):
    t2 = p_ref.shape[0]
    p = p_ref[...]
    rows = lax.broadcasted_iota(jnp.int32, p.shape, 0)
    prev = jnp.where(rows == 0, carry_ref[...], pltpu.roll(p, 1, axis=0))
    carry_ref[...] = p[t2 - 1:t2, :]
    x = p + mu_ref[...] * (prev - p)

    r = x[:, 0:WIDTH]
    k = x[:, WIDTH:2 * WIDTH]
    v = x[:, 2 * WIDTH:3 * WIDTH]
    wa = x[:, 3 * WIDTH:3 * WIDTH + 128]
    gd = x[:, 3 * WIDTH + 128:]

    lane = lax.broadcasted_iota(jnp.int32, wa.shape, 1)
    xa = jnp.where(lane < DECAY_LORA, jnp.tanh(wa), wa)
    pre = _dot_bf16(xa, wcomb_ref[...]) + w0a0_ref[...]
    w_log = -_softplus(-pre[:, :WIDTH]) - 0.5
    log_decay = -jnp.exp(w_log)
    a = _sigmoid(pre[:, WIDTH:])
    g = _dot_bf16(_sigmoid(gd), gup_ref[...])

    ones = ones_ref[...]
    kk = k * kk_ref[...]
    ss = _dot_ones(kk * kk, ones)
    kk = kk / jnp.maximum(jnp.sqrt(ss), 1e-12)
    k2 = k * (1.0 + (a - 1.0) * ka_ref[...])
    bonus = _dot_ones(r * k2 * rk_ref[...], ones) * v

    for j in range(PAIRS):
        ls = slice(j * LANES, (j + 1) * LANES)
        r_o[j] = r[:, ls]
        lw_o[j] = log_decay[:, ls]
        k_o[j] = k2[:, ls]
        v_o[j] = v[:, ls].astype(v_o.dtype)
        a_o[j] = -kk[:, ls]
        b_o[j] = (kk * a)[:, ls]
    g_o[...] = g
    bonus_o[...] = bonus


def _fox_features(p_ref, gain_ref, bf_ref, ones_ref, place_ref, og_o, qkvb_o, carry_ref):
    t2 = p_ref.shape[0]
    ones = ones_ref[...]
    for h0 in (0, WIDTH):
        x = p_ref[:, h0:h0 + WIDTH]
        ms = _dot_ones(x * x, ones) * (1.0 / HEAD_DIM)
        qkvb_o[:, h0:h0 + WIDTH] = (x * lax.rsqrt(ms + NORM_EPS) * gain_ref[:, h0:h0 + WIDTH]).astype(BF16)
    qkvb_o[:, 2 * WIDTH:3 * WIDTH] = p_ref[:, 2 * WIDTH:3 * WIDTH].astype(BF16)
    og_o[...] = p_ref[:, 3 * WIDTH:4 * WIDTH]

    logf = -_softplus(-(p_ref[:, 4 * WIDTH:] + bf_ref[...]))
    ri = lax.broadcasted_iota(jnp.int32, (t2, t2), 0)
    ci = lax.broadcasted_iota(jnp.int32, (t2, t2), 1)
    tri = (ci <= ri).astype(BF16)
    c = sum(jnp.dot(tri, piece, preferred_element_type=F32) for piece in _split3(logf)) + carry_ref[...]
    carry_ref[...] = c[t2 - 1:t2, :]
    pieces = jnp.concatenate(_split3(c * (-LOG2E)), axis=1)
    qkvb_o[:, 3 * WIDTH:] = jnp.dot(pieces, place_ref[...], preferred_element_type=F32).astype(BF16)


def _proj_prep_body(h_ref, n1_ref, wrw_ref, wfx_ref, mu_ref, wcomb_ref, w0a0_ref, gup_ref, kk_ref,
                    ka_ref, rk_ref, gain_ref, bf_ref, ones_ref, place_ref,
                    r_o, lw_o, k_o, v_o, a_o, b_o, g_o, bonus_o, og_o, qkvb_o,
                    prw_ref, pfx_ref, shift_ref, csum_ref):
    @pl.when(pl.program_id(1) == 0)
    def _():
        shift_ref[...] = jnp.zeros_like(shift_ref)
        csum_ref[...] = jnp.zeros_like(csum_ref)

    x = h_ref[...]
    ms = jnp.mean(x * x, axis=-1, keepdims=True)
    xn = (x * lax.rsqrt(ms + NORM_EPS) * n1_ref[...]).astype(BF16)
    _project(xn, wrw_ref, prw_ref)
    _project(xn, wfx_ref, pfx_ref)
    _rwkv_features(prw_ref, mu_ref, wcomb_ref, w0a0_ref, gup_ref, kk_ref, ka_ref, rk_ref, ones_ref,
                   r_o, lw_o, k_o, v_o, a_o, b_o, g_o, bonus_o, shift_ref)
    _fox_features(pfx_ref, gain_ref, bf_ref, ones_ref, place_ref, og_o, qkvb_o, csum_ref)


def _bias_placement():
    src = jnp.arange(3 * LANES)
    piece, head = src // LANES, src % LANES
    dst = (head // 2) * LANES + BIAS_LANES * (head % 2) + piece
    hit = (dst[:, None] == jnp.arange(WIDTH)[None, :]) & (head[:, None] < N_HEADS)
    return hit.astype(BF16)


def _proj_prep(h, norm1_g, w_rw, w_fx, mu_p, wcomb, w0a0, gup_p, k_k, k_a, r_k, gain, bf_p, ones512, tm):
    b, l, d = h.shape
    full = lambda shape: pl.BlockSpec(shape, lambda bi, ti: (0,) * len(shape))
    pair = lambda dt: (jax.ShapeDtypeStruct((b, PAIRS, l, LANES), dt),
                       pl.BlockSpec((None, PAIRS, tm, LANES), lambda bi, ti: (bi, 0, ti, 0)))
    rows = lambda n, dt: (jax.ShapeDtypeStruct((b, l, n), dt),
                          pl.BlockSpec((None, tm, n), lambda bi, ti: (bi, ti, 0)))
    outs = [pair(F32), pair(F32), pair(F32), pair(BF16), pair(F32), pair(F32),
            rows(WIDTH, F32), rows(WIDTH, F32), rows(WIDTH, F32), rows(4 * WIDTH, BF16)]
    consts = (norm1_g.reshape(1, d), w_rw, w_fx, mu_p, wcomb, w0a0, gup_p, k_k, k_a, r_k, gain, bf_p,
              ones512, _bias_placement())
    return pl.pallas_call(
        _proj_prep_body,
        out_shape=[o[0] for o in outs],
        grid=(b, l // tm),
        in_specs=[pl.BlockSpec((None, tm, d), lambda bi, ti: (bi, ti, 0))] + [full(c.shape) for c in consts],
        out_specs=[o[1] for o in outs],
        scratch_shapes=[pltpu.VMEM((tm, RW_PCOLS), F32), pltpu.VMEM((tm, FX_PCOLS), F32),
                        pltpu.VMEM((1, RW_PCOLS), F32), pltpu.VMEM((1, LANES), F32)],
        compiler_params=_cparams(("arbitrary", "arbitrary")),
        name="proj_prep",
    )(h, *consts)


def _chunk_masks():
    ti = jnp.arange(SB)[:, None]
    si = jnp.arange(SB)[None, :]
    same = (ti // CHUNK) == (si // CHUNK)
    masks = [same, same & (si <= ti), same & (si < ti), ti == si, (ti < HEAD_DIM) == (si < HEAD_DIM)]
    return jnp.stack(masks).astype(F32)


def _rwkv_chunk_body(r_ref, lw_ref, k_ref, v_ref, a_ref, b_ref, masks_ref, y_ref, h_ref):
    npg = r_ref.shape[0]
    n_chunks = SB // CHUNK

    @pl.when(pl.program_id(1) == 0)
    def _():
        h_ref[...] = jnp.zeros_like(h_ref)

    same, incl, strict, _, same_head = (masks_ref[i] != 0.0 for i in range(5))
    eye_f = masks_ref[3]
    head0 = lax.broadcasted_iota(jnp.int32, (SB, LANES), 1) < HEAD_DIM
    tri = masks_ref[1].astype(BF16)
    bf = lambda z: z.astype(BF16)
    dot = functools.partial(jnp.dot, preferred_element_type=F32)

    def keep(mask, x):
        return jnp.where(mask, x, jnp.zeros_like(x))

    pairs = range(npg)
    units = [(p, h) for p in pairs for h in range(2)]
    cat0 = lambda *xs: jnp.concatenate(xs, axis=0)
    mm = lambda a, b, **kw: _mm3(*a, *b, **kw)

    lw = [lw_ref[p] for p in pairs]
    cs = [sum(dot(tri, piece) for piece in _split2(lw[p])) for p in pairs]
    ce = [cat0(*[jnp.broadcast_to(cs[p][(c + 1) * CHUNK - 1:(c + 1) * CHUNK, :], (CHUNK, LANES))
                 for c in range(n_chunks)]) for p in pairs]
    g = [jnp.exp(cs[p]) for p in pairs]
    ginv = [jnp.exp(-cs[p]) for p in pairs]
    gend = [jnp.exp(ce[p] - cs[p]) for p in pairs]
    g_c = [jnp.exp(ce[p]) for p in pairs]
    at = [a_ref[p] * jnp.exp(cs[p] - lw[p]) for p in pairs]
    rt = [r_ref[p] * g[p] for p in pairs]
    v_b = [bf(v_ref[p]) for p in pairs]
    at_b = [bf(at[p]) for p in pairs]
    rt_b = [bf(rt[p]) for p in pairs]
    bk_t = [cat0(bf(b_ref[p] * ginv[p]), bf(k_ref[p] * ginv[p])) for p in pairs]
    bh_b = [bf(b_ref[p] * gend[p]) for p in pairs]
    kh_b = [bf(k_ref[p] * gend[p]) for p in pairs]

    gram = [lax.dot_general(cat0(keep(head0, at_b[p]), keep(head0, rt_b[p]),
                                 keep(~head0, at_b[p]), keep(~head0, rt_b[p])), bk_t[p],
                            (((1,), (1,)), ((), ())), preferred_element_type=F32)
            for p in pairs]
    gb = {(p, h): gram[p][2 * SB * h:2 * SB * (h + 1)] for p, h in units}
    mab = {u: keep(strict, gb[u][:SB, :SB]) for u in units}
    mkk_b = {u: bf(cat0(keep(strict, gb[u][:SB, SB:]), keep(incl, gb[u][SB:, SB:]))) for u in units}
    mrb_b = {u: bf(keep(incl, gb[u][SB:, :SB])) for u in units}

    def fold(z):
        return sum(z[c * CHUNK:(c + 1) * CHUNK] for c in range(n_chunks))

    def unfold(zf):
        return keep(same, jnp.concatenate([zf] * n_chunks, axis=0))

    nf = {u: fold(mab[u]) for u in units}
    tf = {u: fold(eye_f) + nf[u] for u in units}
    full = {u: _split2(mab[u]) for u in units}
    nf = {u: mm(_split2(nf[u]), full[u]) for u in units}
    for k in (2, 4, 8):
        full = {u: _split2(unfold(nf[u])) for u in units}
        if k < 8:
            both_f = {u: mm(_split2(cat0(nf[u], tf[u])), full[u]) for u in units}
            nf = {u: both_f[u][:CHUNK] for u in units}
            tf = {u: tf[u] + both_f[u][CHUNK:] for u in units}
        else:
            tf = {u: tf[u] + mm(_split2(tf[u]), full[u]) for u in units}
    t = {u: unfold(tf[u]) for u in units}
    wk = {u: dot(mkk_b[u], v_b[u[0]]) for u in units}
    x = {u: dot(bf(t[u]), jnp.concatenate([at_b[u[0]], bf(wk[u][:SB])], axis=1)) for u in units}
    ry = {u: dot(mrb_b[u], bf(x[u])) for u in units}
    both = lambda f: [jnp.where(head0, f((p, 0)), f((p, 1))) for p in pairs]
    abar_b = [bf(z) for z in both(lambda u: x[u][:, :LANES])]
    vbar = both(lambda u: x[u][:, LANES:])
    rbar_b = [bf(z) for z in both(lambda u: rt[u[0]] + ry[u][:, :LANES])]
    ybar = both(lambda u: wk[u][SB:] + ry[u][:, LANES:])

    g_ct = [g_c[p].T for p in pairs]

    hbd = [h_ref[p] for p in pairs]
    for c in range(n_chunks):
        sl = slice(c * CHUNK, (c + 1) * CHUNK)
        yu = [dot(cat0(rbar_b[p][sl], abar_b[p][sl]), bf(hbd[p])) for p in pairs]
        for p in pairs:
            y_ref[p, sl, :] = yu[p][:CHUNK] + ybar[p][sl]
        upd = [lax.dot_general(cat0(bh_b[p][sl], kh_b[p][sl]),
                               cat0(bf(yu[p][CHUNK:] + vbar[p][sl]), v_b[p][sl]),
                               (((0,), (0,)), ((), ())), preferred_element_type=F32)
               for p in pairs]
        hbd = [jnp.broadcast_to(g_ct[p][:, c * CHUNK:c * CHUNK + 1], (LANES, LANES)) * hbd[p]
               + keep(same_head, upd[p]) for p in pairs]
    for p in pairs:
        h_ref[p] = hbd[p]


def _rwkv_chunk(r, lw, k, v, a, b, npg):
    n_pairs, l, _ = r.shape
    blk = pl.BlockSpec((npg, SB, LANES), lambda gi, ti: (gi, ti, 0))
    return pl.pallas_call(
        _rwkv_chunk_body,
        out_shape=jax.ShapeDtypeStruct((n_pairs, l, LANES), F32),
        grid=(n_pairs // npg, l // SB),
        in_specs=[blk] * 6 + [pl.BlockSpec((5, SB, SB), lambda gi, ti: (0, 0, 0))],
        out_specs=blk,
        scratch_shapes=[pltpu.VMEM((npg, LANES, LANES), F32)],
        compiler_params=_cparams(("arbitrary", "arbitrary")),
        name="rwkv_chunk",
    )(r, lw, k, v, a, b, _chunk_masks())


def _fox_attn_body(q_ref, k_ref, v_ref, kb_ref, o_ref, m_ref, acc_ref, *, q_row0, tk):
    tq = q_ref.shape[0]
    row_first = q_row0 + pl.program_id(1) * tq

    m_ref[...] = jnp.full_like(m_ref, MASK_VALUE)
    acc_ref[...] = jnp.zeros_like(acc_ref)

    lane = lax.broadcasted_iota(jnp.int32, (1, LANES), 1)
    head0 = lane < HEAD_DIM
    row2 = lax.broadcasted_iota(jnp.int32, (2 * tq, LANES), 0)
    lane2 = lax.broadcasted_iota(jnp.int32, (2 * tq, LANES), 1)
    first_lane = jnp.where(row2 < tq, 0, BIAS_LANES)
    bias_on = (lane2 >= first_lane) & (lane2 < first_lane + BIAS_LANES)
    bias_q = jnp.where(bias_on, 1.0, 0.0).astype(BF16)

    def block(col_first, width, masked):
        rows = pl.ds(col_first, width)
        heads = range(N_HEADS)
        lanes = [slice(j * LANES, (j + 1) * LANES) for j in range(PAIRS)]
        s2 = []
        for ls in lanes:
            q = q_ref[:, ls]
            zero = jnp.zeros_like(q)
            q2 = jnp.concatenate([jnp.where(head0, q, zero), jnp.where(head0, zero, q)], axis=0)
            s2.append(lax.dot_general(jnp.concatenate([q2, bias_q], axis=1),
                                      jnp.concatenate([k_ref[rows, ls], kb_ref[rows, ls]], axis=1),
                                      (((1,), (1,)), ((), ())), preferred_element_type=F32))
        s = [s2[h // 2][(h % 2) * tq:(h % 2 + 1) * tq] for h in heads]
        if masked:
            causal = (lax.broadcasted_iota(jnp.int32, (tq, width), 1)
                      <= lax.broadcasted_iota(jnp.int32, (tq, width), 0))
            s = [jnp.where(causal, sh, MASK_VALUE) for sh in s]
        m_prev = [m_ref[h] for h in heads]
        def row_max(sh):
            tile_max = functools.reduce(jnp.maximum, [sh[:, c:c + LANES] for c in range(0, width, LANES)])
            return jnp.max(tile_max, axis=1, keepdims=True)
        m_new = [jnp.maximum(m_prev[h], row_max(s[h])) for h in heads]
        p = [jnp.exp2(s[h] - jnp.concatenate([m_new[h]] * (width // LANES), axis=1)).astype(BF16)
             for h in heads]
        v1 = []
        for ls in lanes:
            v = v_ref[rows, ls]
            one = jnp.ones_like(v)
            v1 += [jnp.where(head0, v, one), jnp.where(head0, one, v)]
        pv = [jnp.dot(p[h], v1[h], preferred_element_type=F32) for h in heads]
        for h in heads:
            acc_ref[h] = jnp.exp2(m_prev[h] - m_new[h]) * acc_ref[h] + pv[h]
            m_ref[h] = m_new[h]

    def full_block(kj, carry):
        block(pl.multiple_of(kj * tk, tk), tk, False)
        return carry

    n_full = row_first // tk
    lax.fori_loop(0, n_full, full_block, 0)
    if tk > tq >= LANES:
        def rest_block(i, carry):
            block(pl.multiple_of(n_full * tk + i * tq, tq), tq, False)
            return carry
        lax.fori_loop(0, (row_first - n_full * tk) // tq, rest_block, 0)
    diag_w = max(tq, LANES)
    block(pl.multiple_of(row_first, tq), diag_w, True)

    if o_ref.shape[0] > tq:
        o_ref[...] = jnp.zeros_like(o_ref)
    for j in range(PAIRS):
        a0, a1 = acc_ref[2 * j], acc_ref[2 * j + 1]
        o_ref[0:tq, j * LANES:(j + 1) * LANES] = jnp.where(
            head0, a0 / pltpu.roll(a0, HEAD_DIM, axis=1), a1 / pltpu.roll(a1, HEAD_DIM, axis=1))


def _fox_attn(qkvb, q_row0, n_rows, tq, tk, out_rows=None):
    b, l, _ = qkvb.shape
    out_rows = out_rows or tq
    assert q_row0 % tq == 0
    assert tk % tq == 0 if tq >= LANES else (q_row0 % tk == 0 and n_rows == tq)
    assert q_row0 + n_rows - tq + max(tq, LANES) <= l
    qb0 = q_row0 // tq
    kv_spec = lambda col: pl.BlockSpec((None, l, WIDTH), lambda bi, qi: (bi, 0, col))
    return pl.pallas_call(
        functools.partial(_fox_attn_body, q_row0=q_row0, tk=tk),
        out_shape=jax.ShapeDtypeStruct((b, n_rows // tq * out_rows, WIDTH), F32),
        grid=(b, n_rows // tq),
        in_specs=[pl.BlockSpec((None, tq, WIDTH), lambda bi, qi: (bi, qb0 + qi, 0)),
                  kv_spec(1), kv_spec(2), kv_spec(3)],
        out_specs=pl.BlockSpec((None, out_rows, WIDTH), lambda bi, qi: (bi, qi, 0)),
        scratch_shapes=[pltpu.VMEM((N_HEADS, tq, LANES), F32), pltpu.VMEM((N_HEADS, tq, LANES), F32)],
        compiler_params=_cparams(("arbitrary", "arbitrary")),
        name="fox_attn",
    )(qkvb, qkvb, qkvb, qkvb)


def _mix_out_body(h_ref, y_ref, bonus_ref, g_ref, o_ref, og_ref, gnw_ref, gnb_ref, ones_ref, wo_ref, out_ref):
    ones = ones_ref[...]
    y = jnp.concatenate([y_ref[j] for j in range(PAIRS)], axis=1)
    mean = _dot_ones(y, ones) * (1.0 / HEAD_DIM)
    d = y - mean
    var = _dot_ones(d * d, ones) * (1.0 / HEAD_DIM)
    yn = d * lax.rsqrt(var + GN_EPS) * gnw_ref[...] + gnb_ref[...]
    y_rw = ((yn + bonus_ref[...]) * g_ref[...]).astype(BF16)
    y_fx = (o_ref[...] * _sigmoid(og_ref[...])).astype(BF16)
    out_ref[...] = (h_ref[...]
                    + jnp.dot(y_rw, wo_ref[0:WIDTH, :], preferred_element_type=F32)
                    + jnp.dot(y_fx, wo_ref[WIDTH:2 * WIDTH, :], preferred_element_type=F32))


def _mix_out(h, y, bonus, g, o_fx, og, gn_w, gn_b, ones512, wo_bf16, tm):
    b, l, d = h.shape
    full = lambda shape: pl.BlockSpec(shape, lambda bi, ti: (0,) * len(shape))
    rows = pl.BlockSpec((None, tm, WIDTH), lambda bi, ti: (bi, ti, 0))
    return pl.pallas_call(
        _mix_out_body,
        out_shape=jax.ShapeDtypeStruct((b, l, d), F32),
        grid=(b, l // tm),
        in_specs=[pl.BlockSpec((None, tm, d), lambda bi, ti: (bi, ti, 0)),
                  pl.BlockSpec((None, PAIRS, tm, LANES), lambda bi, ti: (bi, 0, ti, 0)),
                  rows, rows, rows, rows,
                  full((1, WIDTH)), full((1, WIDTH)), full(ones512.shape), full((2 * WIDTH, d))],
        out_specs=pl.BlockSpec((None, tm, d), lambda bi, ti: (bi, ti, 0)),
        compiler_params=_cparams(("arbitrary", "arbitrary")),
        name="mix_out",
    )(h, y, bonus, g, o_fx, og, gn_w, gn_b, ones512, wo_bf16)


def _conv_ffn_body(h_ref, g_ref, wu_ref, wg_ref, cwu_ref, cwg_ref, cbu_ref, cbg_ref, wout_ref,
                   out_ref, xn_ref, cu_ref, cg_ref, wu_s, wg_s):
    tm = h_ref.shape[0]
    halo = wu_s.shape[0] - tm
    ti = pl.program_id(1)
    j = pl.program_id(2)

    @pl.when(j == 0)
    def _():
        x = h_ref[...]
        ms = jnp.mean(x * x, axis=-1, keepdims=True)
        xn_ref[...] = (x * lax.rsqrt(ms + NORM_EPS) * g_ref[...]).astype(BF16)
        out_ref[...] = x

    @pl.when(ti == 0)
    def _():
        cu_ref[j] = jnp.zeros(cu_ref.shape[1:], F32)
        cg_ref[j] = jnp.zeros(cg_ref.shape[1:], F32)

    xn = xn_ref[...]

    def conv(w_ref, cw_ref, cb_ref, carry_ref, work):
        work[0:halo] = carry_ref[j]
        work[halo:] = jnp.dot(xn, w_ref[...], preferred_element_type=F32)
        carry_ref[j] = work[tm:]
        cw = cw_ref[...]
        taps = [work[pl.ds(halo - (CONV_W - 1) + i, tm), :] for i in range(CONV_W)]
        return cb_ref[...] + cw[0:1, :] * taps[0] + cw[1:2, :] * taps[1] + cw[2:3, :] * taps[2]

    u = conv(wu_ref, cwu_ref, cbu_ref, cu_ref, wu_s)
    gt = conv(wg_ref, cwg_ref, cbg_ref, cg_ref, wg_s)
    act = (gt * _sigmoid(gt) * u).astype(BF16)
    out_ref[...] += jnp.dot(act, wout_ref[...], preferred_element_type=F32)


def _conv_ffn(h, g, w_in_bf16, conv_w, conv_b, w_out_bf16, tm, tf):
    b, l, d = h.shape
    d_ff = w_out_bf16.shape[0]
    nj = d_ff // tf
    cw = jnp.pad(conv_w, ((0, 8 - CONV_W), (0, 0)))
    cb = conv_b.reshape(1, 2 * d_ff)
    return pl.pallas_call(
        _conv_ffn_body,
        out_shape=jax.ShapeDtypeStruct((b, l, d), F32),
        grid=(b, l // tm, nj),
        in_specs=[pl.BlockSpec((None, tm, d), lambda bi, ti, j: (bi, ti, 0)),
                  pl.BlockSpec((1, d), lambda bi, ti, j: (0, 0)),
                  pl.BlockSpec((d, tf), lambda bi, ti, j: (0, j)),
                  pl.BlockSpec((d, tf), lambda bi, ti, j: (0, nj + j)),
                  pl.BlockSpec((8, tf), lambda bi, ti, j: (0, j)),
                  pl.BlockSpec((8, tf), lambda bi, ti, j: (0, nj + j)),
                  pl.BlockSpec((1, tf), lambda bi, ti, j: (0, j)),
                  pl.BlockSpec((1, tf), lambda bi, ti, j: (0, nj + j)),
                  pl.BlockSpec((tf, d), lambda bi, ti, j: (j, 0))],
        out_specs=pl.BlockSpec((None, tm, d), lambda bi, ti, j: (bi, ti, 0)),
        scratch_shapes=[pltpu.VMEM((tm, d), BF16),
                        pltpu.VMEM((nj, 8, tf), F32), pltpu.VMEM((nj, 8, tf), F32),
                        pltpu.VMEM((8 + tm, tf), F32), pltpu.VMEM((8 + tm, tf), F32)],
        compiler_params=_cparams(("arbitrary", "arbitrary", "arbitrary")),
        name="conv_ffn",
    )(h, g.reshape(1, d), w_in_bf16, w_in_bf16, cw, cw, cb, cb, w_out_bf16)


def _pad_cols(w, n):
    return jnp.pad(w, ((0, 0), (0, n - w.shape[1])))


def _layer(h, prm, l_real):
    (norm1_g, w_in, rw_mu, rw_w0, rw_w_up, rw_a0, rw_a_up, rw_g_up, rw_k_k, rw_k_a, rw_r_k,
     rw_gn_w, rw_gn_b, fx_b_f, fx_q_g, fx_k_g, w_o, norm2_g, ffn_w_in, ffn_conv_w, ffn_conv_b,
     ffn_w_out) = prm
    b, l, d = h.shape
    m = b * l
    rw_cols = 3 * WIDTH + DECAY_LORA + AAA_LORA + GATE_LORA

    w_rw = _pad_cols(w_in[:, :rw_cols], RW_PCOLS).astype(BF16)
    w_fx = _pad_cols(w_in[:, rw_cols:], FX_PCOLS).astype(BF16)
    mu_p = _pad_cols(rw_mu.reshape(1, rw_cols), RW_PCOLS)
    wcomb = jnp.zeros((128, 2 * WIDTH), F32)
    wcomb = wcomb.at[:DECAY_LORA, :WIDTH].set(rw_w_up).at[DECAY_LORA:, WIDTH:].set(rw_a_up)
    w0a0 = jnp.concatenate([rw_w0, rw_a0]).reshape(1, 2 * WIDTH)
    gup_p = jnp.pad(rw_g_up, ((0, GATE_PAD - GATE_LORA), (0, 0)))
    ones512 = _group_ones(MXU_TILE, HEAD_DIM)
    gain = jnp.concatenate([jnp.tile(fx_q_g, N_HEADS) * (HEAD_DIM ** -0.5 * LOG2E),
                            jnp.tile(fx_k_g, N_HEADS)]).reshape(1, 2 * WIDTH)
    bf_p = jnp.pad(fx_b_f, (0, LANES - N_HEADS)).reshape(1, LANES)

    t1 = _pick(l, (272, 128))
    t2 = _pick(l, (544, 384, 256, 128))

    r, lw, k, v, a, bb, g, bonus, og, qkvb = _proj_prep(
        h, norm1_g, w_rw, w_fx, mu_p, wcomb, w0a0, gup_p, rw_k_k.reshape(1, WIDTH),
        rw_k_a.reshape(1, WIDTH), rw_r_k.reshape(1, WIDTH), gain, bf_p, ones512, t1)

    pm = lambda z: z.reshape(b * PAIRS, l, LANES)
    y = _rwkv_chunk(pm(r), pm(lw), pm(k), pm(v), pm(a), pm(bb), npg=8)
    y = y.reshape(b, PAIRS, l, LANES)

    n_main = l // ATTN_BLOCK * ATTN_BLOCK
    parts = []
    if n_main:
        parts.append(_fox_attn(qkvb, 0, n_main, ATTN_BLOCK, ATTN_KEYS))
    if l > n_main:
        tq_tail = min(max(l_real - n_main, 1) + 15 & ~15, l - n_main)
        tk_tail = ATTN_KEYS if n_main % ATTN_KEYS == 0 else ATTN_BLOCK
        parts.append(_fox_attn(qkvb, n_main, tq_tail, tq_tail, tk_tail, out_rows=l - n_main))
    o_fx = jnp.concatenate(parts, axis=1)

    h = _mix_out(h, y, bonus, g, o_fx, og, rw_gn_w.reshape(1, WIDTH), rw_gn_b.reshape(1, WIDTH),
                 ones512, w_o.astype(BF16), t2)

    d_ff = ffn_w_out.shape[0]
    tf = _pick(d_ff, (1408, 256, 128))
    return _conv_ffn(h, norm2_g, ffn_w_in.astype(BF16), ffn_conv_w, ffn_conv_b,
                     ffn_w_out.astype(BF16), t2, tf)


def kernel(x, meta, norm1_g, w_in, rw_mu, rw_w0, rw_w_up, rw_a0, rw_a_up, rw_g_up, rw_k_k, rw_k_a,
           rw_r_k, rw_gn_w, rw_gn_b, fx_b_f, fx_q_g, fx_k_g, w_o, norm2_g, ffn_w_in, ffn_conv_w,
           ffn_conv_b, ffn_w_out):
    b, seq, d = x.shape
    params = (norm1_g, w_in, rw_mu, rw_w0, rw_w_up, rw_a0, rw_a_up, rw_g_up, rw_k_k, rw_k_a, rw_r_k,
              rw_gn_w, rw_gn_b, fx_b_f, fx_q_g, fx_k_g, w_o, norm2_g, ffn_w_in, ffn_conv_w,
              ffn_conv_b, ffn_w_out)
    l = N_META + seq
    lp = -(-l // SB) * SB
    meta_b = jnp.broadcast_to(meta[None].astype(x.dtype), (b, N_META, d))
    h = jnp.concatenate([meta_b, x, jnp.zeros((b, lp - l, d), x.dtype)], axis=1)
    for layer in range(norm1_g.shape[0]):
        h = _layer(h, tuple(p[layer] for p in params), l)
    return h[:, N_META:l]
```

```python
import functools

import jax
import jax.numpy as jnp
from jax import lax
from jax.experimental import pallas as pl
from jax.experimental.pallas import tpu as pltpu

F32 = jnp.float32
BF16 = jnp.bfloat16

N_META = 16
HEAD_DIM = 64
N_HEADS = 8
WIDTH = N_HEADS * HEAD_DIM
DECAY_LORA = 64
AAA_LORA = 64
GATE_LORA = 160
GATE_PAD = 256
RW_PCOLS = 3 * WIDTH + 128 + GATE_PAD
FX_PCOLS = 4 * WIDTH + 128
CONV_W = 3
NORM_EPS = 1e-6
GN_EPS = HEAD_DIM * 1e-5
LANES = 128
MXU_TILE = 256
PAIRS = WIDTH // LANES
CHUNK = 16
SB = 128
ATTN_BLOCK = 256
ATTN_KEYS = 512
BIAS_LANES = 3
LOG2E = 1.4426950408889634
MASK_VALUE = -1e30
VMEM_LIMIT = 56 * 1024 * 1024


def _pick(n, cands):
    for c in cands:
        if n % c == 0:
            return c
    raise ValueError(f"no tile for {n} in {cands}")


def _cparams(sem):
    return pltpu.CompilerParams(dimension_semantics=sem, vmem_limit_bytes=VMEM_LIMIT)


def _sigmoid(x):
    return 1.0 / (1.0 + jnp.exp(-x))


def _softplus(x):
    return jnp.maximum(x, 0.0) + jnp.log(1.0 + jnp.exp(-jnp.abs(x)))


def _split2(x):
    hi = x.astype(BF16)
    lo = (x - hi.astype(F32)).astype(BF16)
    return hi, lo


def _split3(x):
    p1 = x.astype(BF16)
    rem = x - p1.astype(F32)
    p2 = rem.astype(BF16)
    p3 = (rem - p2.astype(F32)).astype(BF16)
    return p1, p2, p3


def _dot_ones(x, ones_bf16):
    xb = x.astype(BF16)
    t = ones_bf16.shape[0]
    return jnp.concatenate([jnp.dot(xb[:, c:c + t], ones_bf16, preferred_element_type=F32)
                            for c in range(0, x.shape[1], t)], axis=1)


def _mm3(a_hi, a_lo, b_hi, b_lo, dims=((1,), (0,))):
    dn = (dims, ((), ()))
    m_axis = 1 - dims[0][0]
    m = a_hi.shape[m_axis]
    d = lax.dot_general(jnp.concatenate([a_hi, a_lo], axis=m_axis), b_hi, dn,
                        preferred_element_type=F32)
    return d[:m] + d[m:] + lax.dot_general(a_hi, b_lo, dn, preferred_element_type=F32)


def _dot_bf16(a, b):
    return jnp.dot(a.astype(BF16), b.astype(BF16), preferred_element_type=F32)


def _group_ones(n, group):
    i = jnp.arange(n) // group
    return (i[:, None] == i[None, :]).astype(BF16)


def _project(xn, w_ref, p_ref, n_chunk=512):
    n = p_ref.shape[-1]
    for c in range(0, n, n_chunk):
        e = min(c + n_chunk, n)
        p_ref[:, c:e] = jnp.dot(xn, w_ref[:, c:e], preferred_element_type=F32)


def _rwkv_features(p_ref, mu_ref, wcomb_ref, w0a0_ref, gup_ref, kk_ref, ka_ref, rk_ref, ones_ref,
                   r_o, lw_o, k_o, v_o, a_o, b_o, g_o, bonus_o, carry_ref):
    t2 = p_ref.shape[0]
    p = p_ref[...]
    rows = lax.broadcasted_iota(jnp.int32, p.shape, 0)
    prev = jnp.where(rows == 0, carry_ref[...], pltpu.roll(p, 1, axis=0))
    carry_ref[...] = p[t2 - 1:t2, :]
    x = p + mu_ref[...] * (prev - p)

    r = x[:, 0:WIDTH]
    k = x[:, WIDTH:2 * WIDTH]
    v = x[:, 2 * WIDTH:3 * WIDTH]
    wa = x[:, 3 * WIDTH:3 * WIDTH + 128]
    gd = x[:, 3 * WIDTH + 128:]

    lane = lax.broadcasted_iota(jnp.int32, wa.shape, 1)
    xa = jnp.where(lane < DECAY_LORA, jnp.tanh(wa), wa)
    pre = _dot_bf16(xa, wcomb_ref[...]) + w0a0_ref[...]
    w_log = -_softplus(-pre[:, :WIDTH]) - 0.5
    log_decay = -jnp.exp(w_log)
    a = _sigmoid(pre[:, WIDTH:])
    g = _dot_bf16(_sigmoid(gd), gup_ref[...])

    ones = ones_ref[...]
    kk = k * kk_ref[...]
    ss = _dot_ones(kk * kk, ones)
    kk = kk / jnp.maximum(jnp.sqrt(ss), 1e-12)
    k2 = k * (1.0 + (a - 1.0) * ka_ref[...])
    bonus = _dot_ones(r * k2 * rk_ref[...], ones) * v

    for j in range(PAIRS):
        ls = slice(j * LANES, (j + 1) * LANES)
        r_o[j] = r[:, ls]
        lw_o[j] = log_decay[:, ls]
        k_o[j] = k2[:, ls]
        v_o[j] = v[:, ls].astype(v_o.dtype)
        a_o[j] = -kk[:, ls]
        b_o[j] = (kk * a)[:, ls]
    g_o[...] = g
    bonus_o[...] = bonus


def _fox_features(p_ref, gain_ref, bf_ref, ones_ref, place_ref, og_o, qkvb_o, carry_ref):
    t2 = p_ref.shape[0]
    ones = ones_ref[...]
    for h0 in (0, WIDTH):
        x = p_ref[:, h0:h0 + WIDTH]
        ms = _dot_ones(x * x, ones) * (1.0 / HEAD_DIM)
        qkvb_o[:, h0:h0 + WIDTH] = (x * lax.rsqrt(ms + NORM_EPS) * gain_ref[:, h0:h0 + WIDTH]).astype(BF16)
    qkvb_o[:, 2 * WIDTH:3 * WIDTH] = p_ref[:, 2 * WIDTH:3 * WIDTH].astype(BF16)
    og_o[...] = p_ref[:, 3 * WIDTH:4 * WIDTH]

    logf = -_softplus(-(p_ref[:, 4 * WIDTH:] + bf_ref[...]))
    ri = lax.broadcasted_iota(jnp.int32, (t2, t2), 0)
    ci = lax.broadcasted_iota(jnp.int32, (t2, t2), 1)
    tri = (ci <= ri).astype(BF16)
    c = sum(jnp.dot(tri, piece, preferred_element_type=F32) for piece in _split3(logf)) + carry_ref[...]
    carry_ref[...] = c[t2 - 1:t2, :]
    pieces = jnp.concatenate(_split3(c * (-LOG2E)), axis=1)
    qkvb_o[:, 3 * WIDTH:] = jnp.dot(pieces, place_ref[...], preferred_element_type=F32).astype(BF16)


def _proj_prep_body(h_ref, n1_ref, wrw_ref, wfx_ref, mu_ref, wcomb_ref, w0a0_ref, gup_ref, kk_ref,
                    ka_ref, rk_ref, gain_ref, bf_ref, ones_ref, place_ref,
                    r_o, lw_o, k_o, v_o, a_o, b_o, g_o, bonus_o, og_o, qkvb_o,
                    prw_ref, pfx_ref, shift_ref, csum_ref):
    @pl.when(pl.program_id(1) == 0)
    def _():
        shift_ref[...] = jnp.zeros_like(shift_ref)
        csum_ref[...] = jnp.zeros_like(csum_ref)

    x = h_ref[...]
    ms = jnp.mean(x * x, axis=-1, keepdims=True)
    xn = (x * lax.rsqrt(ms + NORM_EPS) * n1_ref[...]).astype(BF16)
    _project(xn, wrw_ref, prw_ref)
    _project(xn, wfx_ref, pfx_ref)
    _rwkv_features(prw_ref, mu_ref, wcomb_ref, w0a0_ref, gup_ref, kk_ref, ka_ref, rk_ref, ones_ref,
                   r_o, lw_o, k_o, v_o, a_o, b_o, g_o, bonus_o, shift_ref)
    _fox_features(pfx_ref, gain_ref, bf_ref, ones_ref, place_ref, og_o, qkvb_o, csum_ref)


def _bias_placement():
    src = jnp.arange(3 * LANES)
    piece, head = src // LANES, src % LANES
    dst = (head // 2) * LANES + BIAS_LANES * (head % 2) + piece
    hit = (dst[:, None] == jnp.arange(WIDTH)[None, :]) & (head[:, None] < N_HEADS)
    return hit.astype(BF16)


def _proj_prep(h, norm1_g, w_rw, w_fx, mu_p, wcomb, w0a0, gup_p, k_k, k_a, r_k, gain, bf_p, ones512, tm):
    b, l, d = h.shape
    full = lambda shape: pl.BlockSpec(shape, lambda bi, ti: (0,) * len(shape))
    pair = lambda dt: (jax.ShapeDtypeStruct((b, PAIRS, l, LANES), dt),
                       pl.BlockSpec((None, PAIRS, tm, LANES), lambda bi, ti: (bi, 0, ti, 0)))
    rows = lambda n, dt: (jax.ShapeDtypeStruct((b, l, n), dt),
                          pl.BlockSpec((None, tm, n), lambda bi, ti: (bi, ti, 0)))
    outs = [pair(F32), pair(F32), pair(F32), pair(BF16), pair(F32), pair(F32),
            rows(WIDTH, F32), rows(WIDTH, F32), rows(WIDTH, F32), rows(4 * WIDTH, BF16)]
    consts = (norm1_g.reshape(1, d), w_rw, w_fx, mu_p, wcomb, w0a0, gup_p, k_k, k_a, r_k, gain, bf_p,
              ones512, _bias_placement())
    return pl.pallas_call(
        _proj_prep_body,
        out_shape=[o[0] for o in outs],
        grid=(b, l // tm),
        in_specs=[pl.BlockSpec((None, tm, d), lambda bi, ti: (bi, ti, 0))] + [full(c.shape) for c in consts],
        out_specs=[o[1] for o in outs],
        scratch_shapes=[pltpu.VMEM((tm, RW_PCOLS), F32), pltpu.VMEM((tm, FX_PCOLS), F32),
                        pltpu.VMEM((1, RW_PCOLS), F32), pltpu.VMEM((1, LANES), F32)],
        compiler_params=_cparams(("arbitrary", "arbitrary")),
        name="proj_prep",
    )(h, *consts)


def _chunk_masks():
    ti = jnp.arange(SB)[:, None]
    si = jnp.arange(SB)[None, :]
    same = (ti // CHUNK) == (si // CHUNK)
    masks = [same, same & (si <= ti), same & (si < ti), ti == si, (ti < HEAD_DIM) == (si < HEAD_DIM)]
    return jnp.stack(masks).astype(F32)


def _rwkv_chunk_body(r_ref, lw_ref, k_ref, v_ref, a_ref, b_ref, masks_ref, y_ref, h_ref):
    npg = r_ref.shape[0]
    n_chunks = SB // CHUNK

    @pl.when(pl.program_id(1) == 0)
    def _():
        h_ref[...] = jnp.zeros_like(h_ref)

    same, incl, strict, _, same_head = (masks_ref[i] != 0.0 for i in range(5))
    eye_f = masks_ref[3]
    head0 = lax.broadcasted_iota(jnp.int32, (SB, LANES), 1) < HEAD_DIM
    tri = masks_ref[1].astype(BF16)
    bf = lambda z: z.astype(BF16)
    dot = functools.partial(jnp.dot, preferred_element_type=F32)

    def keep(mask, x):
        return jnp.where(mask, x, jnp.zeros_like(x))

    pairs = range(npg)
    units = [(p, h) for p in pairs for h in range(2)]
    cat0 = lambda *xs: jnp.concatenate(xs, axis=0)
    mm = lambda a, b, **kw: _mm3(*a, *b, **kw)

    lw = [lw_ref[p] for p in pairs]
    cs = [sum(dot(tri, piece) for piece in _split2(lw[p])) for p in pairs]
    ce = [cat0(*[jnp.broadcast_to(cs[p][(c + 1) * CHUNK - 1:(c + 1) * CHUNK, :], (CHUNK, LANES))
                 for c in range(n_chunks)]) for p in pairs]
    g = [jnp.exp(cs[p]) for p in pairs]
    ginv = [jnp.exp(-cs[p]) for p in pairs]
    gend = [jnp.exp(ce[p] - cs[p]) for p in pairs]
    g_c = [jnp.exp(ce[p]) for p in pairs]
    at = [a_ref[p] * jnp.exp(cs[p] - lw[p]) for p in pairs]
    rt = [r_ref[p] * g[p] for p in pairs]
    v_b = [bf(v_ref[p]) for p in pairs]
    at_b = [bf(at[p]) for p in pairs]
    rt_b = [bf(rt[p]) for p in pairs]
    bk_t = [cat0(bf(b_ref[p] * ginv[p]), bf(k_ref[p] * ginv[p])) for p in pairs]
    bh_b = [bf(b_ref[p] * gend[p]) for p in pairs]
    kh_b = [bf(k_ref[p] * gend[p]) for p in pairs]

    gram = [lax.dot_general(cat0(keep(head0, at_b[p]), keep(head0, rt_b[p]),
                                 keep(~head0, at_b[p]), keep(~head0, rt_b[p])), bk_t[p],
                            (((1,), (1,)), ((), ())), preferred_element_type=F32)
            for p in pairs]
    gb = {(p, h): gram[p][2 * SB * h:2 * SB * (h + 1)] for p, h in units}
    mab = {u: keep(strict, gb[u][:SB, :SB]) for u in units}
    mkk_b = {u: bf(cat0(keep(strict, gb[u][:SB, SB:]), keep(incl, gb[u][SB:, SB:]))) for u in units}
    mrb_b = {u: bf(keep(incl, gb[u][SB:, :SB])) for u in units}

    def fold(z):
        return sum(z[c * CHUNK:(c + 1) * CHUNK] for c in range(n_chunks))

    def unfold(zf):
        return keep(same, jnp.concatenate([zf] * n_chunks, axis=0))

    nf = {u: fold(mab[u]) for u in units}
    tf = {u: fold(eye_f) + nf[u] for u in units}
    full = {u: _split2(mab[u]) for u in units}
    nf = {u: mm(_split2(nf[u]), full[u]) for u in units}
    for k in (2, 4, 8):
        full = {u: _split2(unfold(nf[u])) for u in units}
        if k < 8:
            both_f = {u: mm(_split2(cat0(nf[u], tf[u])), full[u]) for u in units}
            nf = {u: both_f[u][:CHUNK] for u in units}
            tf = {u: tf[u] + both_f[u][CHUNK:] for u in units}
        else:
            tf = {u: tf[u] + mm(_split2(tf[u]), full[u]) for u in units}
    t = {u: unfold(tf[u]) for u in units}
    wk = {u: dot(mkk_b[u], v_b[u[0]]) for u in units}
    x = {u: dot(bf(t[u]), jnp.concatenate([at_b[u[0]], bf(wk[u][:SB])], axis=1)) for u in units}
    ry = {u: dot(mrb_b[u], bf(x[u])) for u in units}
    both = lambda f: [jnp.where(head0, f((p, 0)), f((p, 1))) for p in pairs]
    abar_b = [bf(z) for z in both(lambda u: x[u][:, :LANES])]
    vbar = both(lambda u: x[u][:, LANES:])
    rbar_b = [bf(z) for z in both(lambda u: rt[u[0]] + ry[u][:, :LANES])]
    ybar = both(lambda u: wk[u][SB:] + ry[u][:, LANES:])

    g_ct = [g_c[p].T for p in pairs]

    hbd = [h_ref[p] for p in pairs]
    for c in range(n_chunks):
        sl = slice(c * CHUNK, (c + 1) * CHUNK)
        yu = [dot(cat0(rbar_b[p][sl], abar_b[p][sl]), bf(hbd[p])) for p in pairs]
        for p in pairs:
            y_ref[p, sl, :] = yu[p][:CHUNK] + ybar[p][sl]
        upd = [lax.dot_general(cat0(bh_b[p][sl], kh_b[p][sl]),
                               cat0(bf(yu[p][CHUNK:] + vbar[p][sl]), v_b[p][sl]),
                               (((0,), (0,)), ((), ())), preferred_element_type=F32)
               for p in pairs]
        hbd = [jnp.broadcast_to(g_ct[p][:, c * CHUNK:c * CHUNK + 1], (LANES, LANES)) * hbd[p]
               + keep(same_head, upd[p]) for p in pairs]
    for p in pairs:
        h_ref[p] = hbd[p]


def _rwkv_chunk(r, lw, k, v, a, b, npg):
    n_pairs, l, _ = r.shape
    blk = pl.BlockSpec((npg, SB, LANES), lambda gi, ti: (gi, ti, 0))
    return pl.pallas_call(
        _rwkv_chunk_body,
        out_shape=jax.ShapeDtypeStruct((n_pairs, l, LANES), F32),
        grid=(n_pairs // npg, l // SB),
        in_specs=[blk] * 6 + [pl.BlockSpec((5, SB, SB), lambda gi, ti: (0, 0, 0))],
        out_specs=blk,
        scratch_shapes=[pltpu.VMEM((npg, LANES, LANES), F32)],
        compiler_params=_cparams(("arbitrary", "arbitrary")),
        name="rwkv_chunk",
    )(r, lw, k, v, a, b, _chunk_masks())


def _fox_attn_body(q_ref, k_ref, v_ref, kb_ref, o_ref, m_ref, acc_ref, *, q_row0, tk):
    tq = q_ref.shape[0]
    row_first = q_row0 + pl.program_id(1) * tq

    m_ref[...] = jnp.full_like(m_ref, MASK_VALUE)
    acc_ref[...] = jnp.zeros_like(acc_ref)

    lane = lax.broadcasted_iota(jnp.int32, (1, LANES), 1)
    head0 = lane < HEAD_DIM
    row2 = lax.broadcasted_iota(jnp.int32, (2 * tq, LANES), 0)
    lane2 = lax.broadcasted_iota(jnp.int32, (2 * tq, LANES), 1)
    first_lane = jnp.where(row2 < tq, 0, BIAS_LANES)
    bias_on = (lane2 >= first_lane) & (lane2 < first_lane + BIAS_LANES)
    bias_q = jnp.where(bias_on, 1.0, 0.0).astype(BF16)

    def block(col_first, width, masked):
        rows = pl.ds(col_first, width)
        heads = range(N_HEADS)
        lanes = [slice(j * LANES, (j + 1) * LANES) for j in range(PAIRS)]
        s2 = []
        for ls in lanes:
            q = q_ref[:, ls]
            zero = jnp.zeros_like(q)
            q2 = jnp.concatenate([jnp.where(head0, q, zero), jnp.where(head0, zero, q)], axis=0)
            s2.append(lax.dot_general(jnp.concatenate([q2, bias_q], axis=1),
                                      jnp.concatenate([k_ref[rows, ls], kb_ref[rows, ls]], axis=1),
                                      (((1,), (1,)), ((), ())), preferred_element_type=F32))
        s = [s2[h // 2][(h % 2) * tq:(h % 2 + 1) * tq] for h in heads]
        if masked:
            causal = (lax.broadcasted_iota(jnp.int32, (tq, width), 1)
                      <= lax.broadcasted_iota(jnp.int32, (tq, width), 0))
            s = [jnp.where(causal, sh, MASK_VALUE) for sh in s]
        m_prev = [m_ref[h] for h in heads]
        def row_max(sh):
            tile_max = functools.reduce(jnp.maximum, [sh[:, c:c + LANES] for c in range(0, width, LANES)])
            return jnp.max(tile_max, axis=1, keepdims=True)
        m_new = [jnp.maximum(m_prev[h], row_max(s[h])) for h in heads]
        p = [jnp.exp2(s[h] - jnp.concatenate([m_new[h]] * (width // LANES), axis=1)).astype(BF16)
             for h in heads]
        v1 = []
        for ls in lanes:
            v = v_ref[rows, ls]
            one = jnp.ones_like(v)
            v1 += [jnp.where(head0, v, one), jnp.where(head0, one, v)]
        pv = [jnp.dot(p[h], v1[h], preferred_element_type=F32) for h in heads]
        for h in heads:
            acc_ref[h] = jnp.exp2(m_prev[h] - m_new[h]) * acc_ref[h] + pv[h]
            m_ref[h] = m_new[h]

    def full_block(kj, carry):
        block(pl.multiple_of(kj * tk, tk), tk, False)
        return carry

    n_full = row_first // tk
    lax.fori_loop(0, n_full, full_block, 0)
    if tk > tq >= LANES:
        def rest_block(i, carry):
            block(pl.multiple_of(n_full * tk + i * tq, tq), tq, False)
            return carry
        lax.fori_loop(0, (row_first - n_full * tk) // tq, rest_block, 0)
    diag_w = max(tq, LANES)
    block(pl.multiple_of(row_first, tq), diag_w, True)

    if o_ref.shape[0] > tq:
        o_ref[...] = jnp.zeros_like(o_ref)
    for j in range(PAIRS):
        a0, a1 = acc_ref[2 * j], acc_ref[2 * j + 1]
        o_ref[0:tq, j * LANES:(j + 1) * LANES] = jnp.where(
            head0, a0 / pltpu.roll(a0, HEAD_DIM, axis=1), a1 / pltpu.roll(a1, HEAD_DIM, axis=1))


def _fox_attn(qkvb, q_row0, n_rows, tq, tk, out_rows=None):
    b, l, _ = qkvb.shape
    out_rows = out_rows or tq
    assert q_row0 % tq == 0
    assert tk % tq == 0 if tq >= LANES else (q_row0 % tk == 0 and n_rows == tq)
    assert q_row0 + n_rows - tq + max(tq, LANES) <= l
    qb0 = q_row0 // tq
    kv_spec = lambda col: pl.BlockSpec((None, l, WIDTH), lambda bi, qi: (bi, 0, col))
    return pl.pallas_call(
        functools.partial(_fox_attn_body, q_row0=q_row0, tk=tk),
        out_shape=jax.ShapeDtypeStruct((b, n_rows // tq * out_rows, WIDTH), F32),
        grid=(b, n_rows // tq),
        in_specs=[pl.BlockSpec((None, tq, WIDTH), lambda bi, qi: (bi, qb0 + qi, 0)),
                  kv_spec(1), kv_spec(2), kv_spec(3)],
        out_specs=pl.BlockSpec((None, out_rows, WIDTH), lambda bi, qi: (bi, qi, 0)),
        scratch_shapes=[pltpu.VMEM((N_HEADS, tq, LANES), F32), pltpu.VMEM((N_HEADS, tq, LANES), F32)],
        compiler_params=_cparams(("arbitrary", "arbitrary")),
        name="fox_attn",
    )(qkvb, qkvb, qkvb, qkvb)


def _mixed_residual(h_ref, y_ref, bonus_ref, g_ref, o_ref, og_ref, gnw_ref, gnb_ref, ones_ref, wo_ref):
    ones = ones_ref[...]
    y = jnp.concatenate([y_ref[j] for j in range(PAIRS)], axis=1)
    mean = _dot_ones(y, ones) * (1.0 / HEAD_DIM)
    d = y - mean
    var = _dot_ones(d * d, ones) * (1.0 / HEAD_DIM)
    yn = d * lax.rsqrt(var + GN_EPS) * gnw_ref[...] + gnb_ref[...]
    y_rw = ((yn + bonus_ref[...]) * g_ref[...]).astype(BF16)
    y_fx = (o_ref[...] * _sigmoid(og_ref[...])).astype(BF16)
    return (h_ref[...]
            + jnp.dot(y_rw, wo_ref[0:WIDTH, :], preferred_element_type=F32)
            + jnp.dot(y_fx, wo_ref[WIDTH:2 * WIDTH, :], preferred_element_type=F32))


def _mix_ffn_body(h_ref, y_ref, bonus_ref, gate_ref, o_ref, og_ref, gnw_ref, gnb_ref, ones_ref, wo_ref,
                  g_ref, wu_ref, wg_ref, cwu_ref, cwg_ref, cbu_ref, cbg_ref, wout_ref,
                  out_ref, xn_ref, cu_ref, cg_ref, wu_s, wg_s):
    tm = h_ref.shape[0]
    halo = wu_s.shape[0] - tm
    ti = pl.program_id(1)
    j = pl.program_id(2)

    @pl.when(j == 0)
    def _():
        x = _mixed_residual(h_ref, y_ref, bonus_ref, gate_ref, o_ref, og_ref, gnw_ref, gnb_ref,
                            ones_ref, wo_ref)
        ms = jnp.mean(x * x, axis=-1, keepdims=True)
        xn_ref[...] = (x * lax.rsqrt(ms + NORM_EPS) * g_ref[...]).astype(BF16)
        out_ref[...] = x

    @pl.when(ti == 0)
    def _():
        cu_ref[j] = jnp.zeros(cu_ref.shape[1:], F32)
        cg_ref[j] = jnp.zeros(cg_ref.shape[1:], F32)

    xn = xn_ref[...]

    def conv(w_ref, cw_ref, cb_ref, carry_ref, work):
        work[0:halo] = carry_ref[j]
        work[halo:] = jnp.dot(xn, w_ref[...], preferred_element_type=F32)
        carry_ref[j] = work[tm:]
        cw = cw_ref[...]
        taps = [work[pl.ds(halo - (CONV_W - 1) + i, tm), :] for i in range(CONV_W)]
        return cb_ref[...] + cw[0:1, :] * taps[0] + cw[1:2, :] * taps[1] + cw[2:3, :] * taps[2]

    u = conv(wu_ref, cwu_ref, cbu_ref, cu_ref, wu_s)
    gt = conv(wg_ref, cwg_ref, cbg_ref, cg_ref, wg_s)
    act = (gt * _sigmoid(gt) * u).astype(BF16)
    out_ref[...] += jnp.dot(act, wout_ref[...], preferred_element_type=F32)


def _mix_ffn(h, y, bonus, gate, o_fx, og, gn_w, gn_b, ones512, wo_bf16,
             g, w_in_bf16, conv_w, conv_b, w_out_bf16, tm, tf):
    b, l, d = h.shape
    d_ff = w_out_bf16.shape[0]
    nj = d_ff // tf
    cw = jnp.pad(conv_w, ((0, 8 - CONV_W), (0, 0)))
    cb = conv_b.reshape(1, 2 * d_ff)
    full = lambda shape: pl.BlockSpec(shape, lambda bi, ti, j: (0,) * len(shape))
    rows = pl.BlockSpec((None, tm, WIDTH), lambda bi, ti, j: (bi, ti, 0))
    return pl.pallas_call(
        _mix_ffn_body,
        out_shape=jax.ShapeDtypeStruct((b, l, d), F32),
        grid=(b, l // tm, nj),
        in_specs=[pl.BlockSpec((None, tm, d), lambda bi, ti, j: (bi, ti, 0)),
                  pl.BlockSpec((None, PAIRS, tm, LANES), lambda bi, ti, j: (bi, 0, ti, 0)),
                  rows, rows, rows, rows,
                  full((1, WIDTH)), full((1, WIDTH)), full(ones512.shape), full((2 * WIDTH, d)),
                  pl.BlockSpec((1, d), lambda bi, ti, j: (0, 0)),
                  pl.BlockSpec((d, tf), lambda bi, ti, j: (0, j)),
                  pl.BlockSpec((d, tf), lambda bi, ti, j: (0, nj + j)),
                  pl.BlockSpec((8, tf), lambda bi, ti, j: (0, j)),
                  pl.BlockSpec((8, tf), lambda bi, ti, j: (0, nj + j)),
                  pl.BlockSpec((1, tf), lambda bi, ti, j: (0, j)),
                  pl.BlockSpec((1, tf), lambda bi, ti, j: (0, nj + j)),
                  pl.BlockSpec((tf, d), lambda bi, ti, j: (j, 0))],
        out_specs=pl.BlockSpec((None, tm, d), lambda bi, ti, j: (bi, ti, 0)),
        scratch_shapes=[pltpu.VMEM((tm, d), BF16),
                        pltpu.VMEM((nj, 8, tf), F32), pltpu.VMEM((nj, 8, tf), F32),
                        pltpu.VMEM((8 + tm, tf), F32), pltpu.VMEM((8 + tm, tf), F32)],
        compiler_params=_cparams(("arbitrary", "arbitrary", "arbitrary")),
        name="mix_ffn",
    )(h, y, bonus, gate, o_fx, og, gn_w, gn_b, ones512, wo_bf16,
      g.reshape(1, d), w_in_bf16, w_in_bf16, cw, cw, cb, cb, w_out_bf16)


def _pad_cols(w, n):
    return jnp.pad(w, ((0, 0), (0, n - w.shape[1])))


def _layer(h, prm, l_real):
    (norm1_g, w_in, rw_mu, rw_w0, rw_w_up, rw_a0, rw_a_up, rw_g_up, rw_k_k, rw_k_a, rw_r_k,
     rw_gn_w, rw_gn_b, fx_b_f, fx_q_g, fx_k_g, w_o, norm2_g, ffn_w_in, ffn_conv_w, ffn_conv_b,
     ffn_w_out) = prm
    b, l, d = h.shape
    m = b * l
    rw_cols = 3 * WIDTH + DECAY_LORA + AAA_LORA + GATE_LORA

    w_rw = _pad_cols(w_in[:, :rw_cols], RW_PCOLS).astype(BF16)
    w_fx = _pad_cols(w_in[:, rw_cols:], FX_PCOLS).astype(BF16)
    mu_p = _pad_cols(rw_mu.reshape(1, rw_cols), RW_PCOLS)
    wcomb = jnp.zeros((128, 2 * WIDTH), F32)
    wcomb = wcomb.at[:DECAY_LORA, :WIDTH].set(rw_w_up).at[DECAY_LORA:, WIDTH:].set(rw_a_up)
    w0a0 = jnp.concatenate([rw_w0, rw_a0]).reshape(1, 2 * WIDTH)
    gup_p = jnp.pad(rw_g_up, ((0, GATE_PAD - GATE_LORA), (0, 0)))
    ones512 = _group_ones(MXU_TILE, HEAD_DIM)
    gain = jnp.concatenate([jnp.tile(fx_q_g, N_HEADS) * (HEAD_DIM ** -0.5 * LOG2E),
                            jnp.tile(fx_k_g, N_HEADS)]).reshape(1, 2 * WIDTH)
    bf_p = jnp.pad(fx_b_f, (0, LANES - N_HEADS)).reshape(1, LANES)

    t1 = _pick(l, (272, 128))
    t2 = _pick(l, (544, 384, 256, 128))

    r, lw, k, v, a, bb, g, bonus, og, qkvb = _proj_prep(
        h, norm1_g, w_rw, w_fx, mu_p, wcomb, w0a0, gup_p, rw_k_k.reshape(1, WIDTH),
        rw_k_a.reshape(1, WIDTH), rw_r_k.reshape(1, WIDTH), gain, bf_p, ones512, t1)

    pm = lambda z: z.reshape(b * PAIRS, l, LANES)
    y = _rwkv_chunk(pm(r), pm(lw), pm(k), pm(v), pm(a), pm(bb), npg=8)
    y = y.reshape(b, PAIRS, l, LANES)

    n_main = l // ATTN_BLOCK * ATTN_BLOCK
    parts = []
    if n_main:
        parts.append(_fox_attn(qkvb, 0, n_main, ATTN_BLOCK, ATTN_KEYS))
    if l > n_main:
        tq_tail = min(max(l_real - n_main, 1) + 15 & ~15, l - n_main)
        tk_tail = ATTN_KEYS if n_main % ATTN_KEYS == 0 else ATTN_BLOCK
        parts.append(_fox_attn(qkvb, n_main, tq_tail, tq_tail, tk_tail, out_rows=l - n_main))
    o_fx = jnp.concatenate(parts, axis=1)

    d_ff = ffn_w_out.shape[0]
    tf = _pick(d_ff, (1408, 256, 128))
    return _mix_ffn(h, y, bonus, g, o_fx, og, rw_gn_w.reshape(1, WIDTH), rw_gn_b.reshape(1, WIDTH),
                    ones512, w_o.astype(BF16), norm2_g, ffn_w_in.astype(BF16), ffn_conv_w, ffn_conv_b,
                    ffn_w_out.astype(BF16), t2, tf)


def kernel(x, meta, norm1_g, w_in, rw_mu, rw_w0, rw_w_up, rw_a0, rw_a_up, rw_g_up, rw_k_k, rw_k_a,
           rw_r_k, rw_gn_w, rw_gn_b, fx_b_f, fx_q_g, fx_k_g, w_o, norm2_g, ffn_w_in, ffn_conv_w,
           ffn_conv_b, ffn_w_out):
    b, seq, d = x.shape
    params = (norm1_g, w_in, rw_mu, rw_w0, rw_w_up, rw_a0, rw_a_up, rw_g_up, rw_k_k, rw_k_a, rw_r_k,
              rw_gn_w, rw_gn_b, fx_b_f, fx_q_g, fx_k_g, w_o, norm2_g, ffn_w_in, ffn_conv_w,
              ffn_conv_b, ffn_w_out)
    l = N_META + seq
    lp = -(-l // SB) * SB
    meta_b = jnp.broadcast_to(meta[None].astype(x.dtype), (b, N_META, d))
    h = jnp.concatenate([meta_b, x, jnp.zeros((b, lp - l, d), x.dtype)], axis=1)
    for layer in range(norm1_g.shape[0]):
        h = _layer(h, tuple(p[layer] for p in params), l)
    return h[:, N_META:l]
```

```python
import functools

import jax
import jax.numpy as jnp
from jax import lax
from jax.experimental import pallas as pl
from jax.experimental.pallas import tpu as pltpu

F32 = jnp.float32
BF16 = jnp.bfloat16

N_META = 16
HEAD_DIM = 64
N_HEADS = 8
WIDTH = N_HEADS * HEAD_DIM
DECAY_LORA = 64
AAA_LORA = 64
GATE_LORA = 160
GATE_PAD = 256
RW_PCOLS = 3 * WIDTH + 128 + GATE_PAD
FX_PCOLS = 4 * WIDTH + 128
CONV_W = 3
NORM_EPS = 1e-6
GN_EPS = HEAD_DIM * 1e-5
LANES = 128
MXU_TILE = 256
PAIRS = WIDTH // LANES
CHUNK = 16
SB = 128
ATTN_BLOCK = 256
ATTN_KEYS = 512
BIAS_LANES = 3
LOG2E = 1.4426950408889634
MASK_VALUE = -1e30
VMEM_LIMIT = 56 * 1024 * 1024


def _pick(n, cands):
    for c in cands:
        if n % c == 0:
            return c
    raise ValueError(f"no tile for {n} in {cands}")


def _cparams(sem):
    return pltpu.CompilerParams(dimension_semantics=sem, vmem_limit_bytes=VMEM_LIMIT)


def _sigmoid(x):
    return 1.0 / (1.0 + jnp.exp(-x))


def _softplus(x):
    return jnp.maximum(x, 0.0) + jnp.log(1.0 + jnp.exp(-jnp.abs(x)))


def _split2(x):
    hi = x.astype(BF16)
    lo = (x - hi.astype(F32)).astype(BF16)
    return hi, lo


def _split3(x):
    p1 = x.astype(BF16)
    rem = x - p1.astype(F32)
    p2 = rem.astype(BF16)
    p3 = (rem - p2.astype(F32)).astype(BF16)
    return p1, p2, p3


def _dot_ones(x, ones_bf16):
    xb = x.astype(BF16)
    t = ones_bf16.shape[0]
    return jnp.concatenate([jnp.dot(xb[:, c:c + t], ones_bf16, preferred_element_type=F32)
                            for c in range(0, x.shape[1], t)], axis=1)


def _mm3(a_hi, a_lo, b_hi, b_lo, dims=((1,), (0,))):
    dn = (dims, ((), ()))
    m_axis = 1 - dims[0][0]
    m = a_hi.shape[m_axis]
    d = lax.dot_general(jnp.concatenate([a_hi, a_lo], axis=m_axis), b_hi, dn,
                        preferred_element_type=F32)
    return d[:m] + d[m:] + lax.dot_general(a_hi, b_lo, dn, preferred_element_type=F32)


def _dot_bf16(a, b):
    return jnp.dot(a.astype(BF16), b.astype(BF16), preferred_element_type=F32)


def _group_ones(n, group):
    i = jnp.arange(n) // group
    return (i[:, None] == i[None, :]).astype(BF16)


def _project(xn, w_ref, p_ref, n_chunk=512):
    n = p_ref.shape[-1]
    for c in range(0, n, n_chunk):
        e = min(c + n_chunk, n)
        p_ref[:, c:e] = jnp.dot(xn, w_ref[:, c:e], preferred_element_type=F32)


def _rwkv_features(p_ref, mu_ref, wcomb_ref, w0a0_ref, gup_ref, kk_ref, ka_ref, rk_ref, ones_ref,
                   r_o, lw_o, k_o, v_o, a_o, b_o, g_o, bonus_o, carry_ref):
    t2 = p_ref.shape[0]
    p = p_ref[...]
    rows = lax.broadcasted_iota(jnp.int32, p.shape, 0)
    prev = jnp.where(rows == 0, carry_ref[...], pltpu.roll(p, 1, axis=0))
    carry_ref[...] = p[t2 - 1:t2, :]
    x = p + mu_ref[...] * (prev - p)

    r = x[:, 0:WIDTH]
    k = x[:, WIDTH:2 * WIDTH]
    v = x[:, 2 * WIDTH:3 * WIDTH]
    wa = x[:, 3 * WIDTH:3 * WIDTH + 128]
    gd = x[:, 3 * WIDTH + 128:]

    lane = lax.broadcasted_iota(jnp.int32, wa.shape, 1)
    xa = jnp.where(lane < DECAY_LORA, jnp.tanh(wa), wa)
    pre = _dot_bf16(xa, wcomb_ref[...]) + w0a0_ref[...]
    w_log = -_softplus(-pre[:, :WIDTH]) - 0.5
    log_decay = -jnp.exp(w_log)
    a = _sigmoid(pre[:, WIDTH:])
    g = _dot_bf16(_sigmoid(gd), gup_ref[...])

    ones = ones_ref[...]
    kk = k * kk_ref[...]
    ss = _dot_ones(kk * kk, ones)
    kk = kk / jnp.maximum(jnp.sqrt(ss), 1e-12)
    k2 = k * (1.0 + (a - 1.0) * ka_ref[...])
    bonus = _dot_ones(r * k2 * rk_ref[...], ones) * v

    for j in range(PAIRS):
        ls = slice(j * LANES, (j + 1) * LANES)
        r_o[j] = r[:, ls]
        lw_o[j] = log_decay[:, ls]
        k_o[j] = k2[:, ls]
        v_o[j] = v[:, ls].astype(v_o.dtype)
        a_o[j] = -kk[:, ls]
        b_o[j] = (kk * a)[:, ls]
    g_o[...] = g
    bonus_o[...] = bonus


def _fox_features(p_ref, gain_ref, bf_ref, ones_ref, place_ref, og_o, qkvb_o, carry_ref):
    t2 = p_ref.shape[0]
    ones = ones_ref[...]
    for h0 in (0, WIDTH):
        x = p_ref[:, h0:h0 + WIDTH]
        ms = _dot_ones(x * x, ones) * (1.0 / HEAD_DIM)
        qkvb_o[:, h0:h0 + WIDTH] = (x * lax.rsqrt(ms + NORM_EPS) * gain_ref[:, h0:h0 + WIDTH]).astype(BF16)
    qkvb_o[:, 2 * WIDTH:3 * WIDTH] = p_ref[:, 2 * WIDTH:3 * WIDTH].astype(BF16)
    og_o[...] = p_ref[:, 3 * WIDTH:4 * WIDTH]

    logf = -_softplus(-(p_ref[:, 4 * WIDTH:] + bf_ref[...]))
    ri = lax.broadcasted_iota(jnp.int32, (t2, t2), 0)
    ci = lax.broadcasted_iota(jnp.int32, (t2, t2), 1)
    tri = (ci <= ri).astype(BF16)
    c = sum(jnp.dot(tri, piece, preferred_element_type=F32) for piece in _split3(logf)) + carry_ref[...]
    carry_ref[...] = c[t2 - 1:t2, :]
    pieces = jnp.concatenate(_split3(c * (-LOG2E)), axis=1)
    qkvb_o[:, 3 * WIDTH:] = jnp.dot(pieces, place_ref[...], preferred_element_type=F32).astype(BF16)


def _proj_prep_body(h_ref, n1_ref, wrw_ref, wfx_ref, mu_ref, wcomb_ref, w0a0_ref, gup_ref, kk_ref,
                    ka_ref, rk_ref, gain_ref, bf_ref, ones_ref, place_ref,
                    r_o, lw_o, k_o, v_o, a_o, b_o, g_o, bonus_o, og_o, qkvb_o,
                    prw_ref, pfx_ref, shift_ref, csum_ref):
    @pl.when(pl.program_id(1) == 0)
    def _():
        shift_ref[...] = jnp.zeros_like(shift_ref)
        csum_ref[...] = jnp.zeros_like(csum_ref)

    x = h_ref[...]
    ms = jnp.mean(x * x, axis=-1, keepdims=True)
    xn = (x * lax.rsqrt(ms + NORM_EPS) * n1_ref[...]).astype(BF16)
    _project(xn, wrw_ref, prw_ref)
    _project(xn, wfx_ref, pfx_ref)
    _rwkv_features(prw_ref, mu_ref, wcomb_ref, w0a0_ref, gup_ref, kk_ref, ka_ref, rk_ref, ones_ref,
                   r_o, lw_o, k_o, v_o, a_o, b_o, g_o, bonus_o, shift_ref)
    _fox_features(pfx_ref, gain_ref, bf_ref, ones_ref, place_ref, og_o, qkvb_o, csum_ref)


def _bias_placement():
    src = jnp.arange(3 * LANES)
    piece, head = src // LANES, src % LANES
    dst = (head // 2) * LANES + BIAS_LANES * (head % 2) + piece
    hit = (dst[:, None] == jnp.arange(WIDTH)[None, :]) & (head[:, None] < N_HEADS)
    return hit.astype(BF16)


def _proj_prep(h, norm1_g, w_rw, w_fx, mu_p, wcomb, w0a0, gup_p, k_k, k_a, r_k, gain, bf_p, ones512, tm):
    b, l, d = h.shape
    full = lambda shape: pl.BlockSpec(shape, lambda bi, ti: (0,) * len(shape))
    pair = lambda dt: (jax.ShapeDtypeStruct((b, PAIRS, l, LANES), dt),
                       pl.BlockSpec((None, PAIRS, tm, LANES), lambda bi, ti: (bi, 0, ti, 0)))
    rows = lambda n, dt: (jax.ShapeDtypeStruct((b, l, n), dt),
                          pl.BlockSpec((None, tm, n), lambda bi, ti: (bi, ti, 0)))
    outs = [pair(F32), pair(F32), pair(F32), pair(BF16), pair(F32), pair(F32),
            rows(WIDTH, F32), rows(WIDTH, F32), rows(WIDTH, F32), rows(4 * WIDTH, BF16)]
    consts = (norm1_g.reshape(1, d), w_rw, w_fx, mu_p, wcomb, w0a0, gup_p, k_k, k_a, r_k, gain, bf_p,
              ones512, _bias_placement())
    return pl.pallas_call(
        _proj_prep_body,
        out_shape=[o[0] for o in outs],
        grid=(b, l // tm),
        in_specs=[pl.BlockSpec((None, tm, d), lambda bi, ti: (bi, ti, 0))] + [full(c.shape) for c in consts],
        out_specs=[o[1] for o in outs],
        scratch_shapes=[pltpu.VMEM((tm, RW_PCOLS), F32), pltpu.VMEM((tm, FX_PCOLS), F32),
                        pltpu.VMEM((1, RW_PCOLS), F32), pltpu.VMEM((1, LANES), F32)],
        compiler_params=_cparams(("arbitrary", "arbitrary")),
        name="proj_prep",
    )(h, *consts)


def _chunk_masks():
    ti = jnp.arange(SB)[:, None]
    si = jnp.arange(SB)[None, :]
    same = (ti // CHUNK) == (si // CHUNK)
    masks = [same, same & (si <= ti), same & (si < ti), ti == si, (ti < HEAD_DIM) == (si < HEAD_DIM)]
    return jnp.stack(masks).astype(F32)


def _rwkv_chunk_body(r_ref, lw_ref, k_ref, v_ref, a_ref, b_ref, masks_ref, y_ref, h_ref):
    npg = r_ref.shape[0]
    n_chunks = SB // CHUNK

    @pl.when(pl.program_id(1) == 0)
    def _():
        h_ref[...] = jnp.zeros_like(h_ref)

    same, incl, strict, _, same_head = (masks_ref[i] != 0.0 for i in range(5))
    eye_f = masks_ref[3]
    head0 = lax.broadcasted_iota(jnp.int32, (SB, LANES), 1) < HEAD_DIM
    tri = masks_ref[1].astype(BF16)
    bf = lambda z: z.astype(BF16)
    dot = functools.partial(jnp.dot, preferred_element_type=F32)

    def keep(mask, x):
        return jnp.where(mask, x, jnp.zeros_like(x))

    pairs = range(npg)
    units = [(p, h) for p in pairs for h in range(2)]
    cat0 = lambda *xs: jnp.concatenate(xs, axis=0)
    mm = lambda a, b, **kw: _mm3(*a, *b, **kw)

    lw = [lw_ref[p] for p in pairs]
    cs = [sum(dot(tri, piece) for piece in _split2(lw[p])) for p in pairs]
    ce = [cat0(*[jnp.broadcast_to(cs[p][(c + 1) * CHUNK - 1:(c + 1) * CHUNK, :], (CHUNK, LANES))
                 for c in range(n_chunks)]) for p in pairs]
    g = [jnp.exp(cs[p]) for p in pairs]
    ginv = [jnp.exp(-cs[p]) for p in pairs]
    gend = [jnp.exp(ce[p] - cs[p]) for p in pairs]
    g_c = [jnp.exp(ce[p]) for p in pairs]
    at = [a_ref[p] * jnp.exp(cs[p] - lw[p]) for p in pairs]
    rt = [r_ref[p] * g[p] for p in pairs]
    v_b = [bf(v_ref[p]) for p in pairs]
    at_b = [bf(at[p]) for p in pairs]
    rt_b = [bf(rt[p]) for p in pairs]
    bk_t = [cat0(bf(b_ref[p] * ginv[p]), bf(k_ref[p] * ginv[p])) for p in pairs]
    bh_b = [bf(b_ref[p] * gend[p]) for p in pairs]
    kh_b = [bf(k_ref[p] * gend[p]) for p in pairs]

    gram = [lax.dot_general(cat0(keep(head0, at_b[p]), keep(head0, rt_b[p]),
                                 keep(~head0, at_b[p]), keep(~head0, rt_b[p])), bk_t[p],
                            (((1,), (1,)), ((), ())), preferred_element_type=F32)
            for p in pairs]
    gb = {(p, h): gram[p][2 * SB * h:2 * SB * (h + 1)] for p, h in units}
    mab = {u: keep(strict, gb[u][:SB, :SB]) for u in units}
    mkk_b = {u: bf(cat0(keep(strict, gb[u][:SB, SB:]), keep(incl, gb[u][SB:, SB:]))) for u in units}
    mrb_b = {u: bf(keep(incl, gb[u][SB:, :SB])) for u in units}

    def fold(z):
        return sum(z[c * CHUNK:(c + 1) * CHUNK] for c in range(n_chunks))

    def unfold(zf):
        return keep(same, jnp.concatenate([zf] * n_chunks, axis=0))

    nf = {u: fold(mab[u]) for u in units}
    tf = {u: fold(eye_f) + nf[u] for u in units}
    full = {u: _split2(mab[u]) for u in units}
    nf = {u: mm(_split2(nf[u]), full[u]) for u in units}
    for k in (2, 4, 8):
        full = {u: _split2(unfold(nf[u])) for u in units}
        if k < 8:
            both_f = {u: mm(_split2(cat0(nf[u], tf[u])), full[u]) for u in units}
            nf = {u: both_f[u][:CHUNK] for u in units}
            tf = {u: tf[u] + both_f[u][CHUNK:] for u in units}
        else:
            tf = {u: tf[u] + mm(_split2(tf[u]), full[u]) for u in units}
    t = {u: unfold(tf[u]) for u in units}
    wk = {u: dot(mkk_b[u], v_b[u[0]]) for u in units}
    x = {u: dot(bf(t[u]), jnp.concatenate([at_b[u[0]], bf(wk[u][:SB])], axis=1)) for u in units}
    ry = {u: dot(mrb_b[u], bf(x[u])) for u in units}
    both = lambda f: [jnp.where(head0, f((p, 0)), f((p, 1))) for p in pairs]
    abar_b = [bf(z) for z in both(lambda u: x[u][:, :LANES])]
    vbar = both(lambda u: x[u][:, LANES:])
    rbar_b = [bf(z) for z in both(lambda u: rt[u[0]] + ry[u][:, :LANES])]
    ybar = both(lambda u: wk[u][SB:] + ry[u][:, LANES:])

    g_ct = [g_c[p].T for p in pairs]

    hbd = [h_ref[p] for p in pairs]
    for c in range(n_chunks):
        sl = slice(c * CHUNK, (c + 1) * CHUNK)
        yu = [dot(cat0(rbar_b[p][sl], abar_b[p][sl]), bf(hbd[p])) for p in pairs]
        for p in pairs:
            y_ref[p, sl, :] = yu[p][:CHUNK] + ybar[p][sl]
        upd = [lax.dot_general(cat0(bh_b[p][sl], kh_b[p][sl]),
                               cat0(bf(yu[p][CHUNK:] + vbar[p][sl]), v_b[p][sl]),
                               (((0,), (0,)), ((), ())), preferred_element_type=F32)
               for p in pairs]
        hbd = [jnp.broadcast_to(g_ct[p][:, c * CHUNK:c * CHUNK + 1], (LANES, LANES)) * hbd[p]
               + keep(same_head, upd[p]) for p in pairs]
    for p in pairs:
        h_ref[p] = hbd[p]


def _rwkv_chunk(r, lw, k, v, a, b, npg):
    n_pairs, l, _ = r.shape
    blk = pl.BlockSpec((npg, SB, LANES), lambda gi, ti: (gi, ti, 0))
    return pl.pallas_call(
        _rwkv_chunk_body,
        out_shape=jax.ShapeDtypeStruct((n_pairs, l, LANES), F32),
        grid=(n_pairs // npg, l // SB),
        in_specs=[blk] * 6 + [pl.BlockSpec((5, SB, SB), lambda gi, ti: (0, 0, 0))],
        out_specs=blk,
        scratch_shapes=[pltpu.VMEM((npg, LANES, LANES), F32)],
        compiler_params=_cparams(("arbitrary", "arbitrary")),
        name="rwkv_chunk",
    )(r, lw, k, v, a, b, _chunk_masks())


def _fox_attn_body(q_ref, k_ref, v_ref, kb_ref, *rest, q_row0, tk):
    o_ref, m_ref, acc_ref = rest[-3:]
    tq = q_ref.shape[0]
    row_first = q_row0 + pl.program_id(1) * tq

    m_ref[...] = jnp.full_like(m_ref, MASK_VALUE)
    acc_ref[...] = jnp.zeros_like(acc_ref)

    lane = lax.broadcasted_iota(jnp.int32, (1, LANES), 1)
    head0 = lane < HEAD_DIM
    row2 = lax.broadcasted_iota(jnp.int32, (2 * tq, LANES), 0)
    lane2 = lax.broadcasted_iota(jnp.int32, (2 * tq, LANES), 1)
    first_lane = jnp.where(row2 < tq, 0, BIAS_LANES)
    bias_on = (lane2 >= first_lane) & (lane2 < first_lane + BIAS_LANES)
    bias_q = jnp.where(bias_on, 1.0, 0.0).astype(BF16)

    def block(col_first, width, masked):
        rows = pl.ds(col_first, width)
        heads = range(N_HEADS)
        lanes = [slice(j * LANES, (j + 1) * LANES) for j in range(PAIRS)]
        s2 = []
        for ls in lanes:
            q = q_ref[:, ls]
            zero = jnp.zeros_like(q)
            q2 = jnp.concatenate([jnp.where(head0, q, zero), jnp.where(head0, zero, q)], axis=0)
            s2.append(lax.dot_general(jnp.concatenate([q2, bias_q], axis=1),
                                      jnp.concatenate([k_ref[rows, ls], kb_ref[rows, ls]], axis=1),
                                      (((1,), (1,)), ((), ())), preferred_element_type=F32))
        s = [s2[h // 2][(h % 2) * tq:(h % 2 + 1) * tq] for h in heads]
        if masked:
            causal = (lax.broadcasted_iota(jnp.int32, (tq, width), 1)
                      <= lax.broadcasted_iota(jnp.int32, (tq, width), 0))
            s = [jnp.where(causal, sh, MASK_VALUE) for sh in s]
        m_prev = [m_ref[h] for h in heads]
        def row_max(sh):
            tile_max = functools.reduce(jnp.maximum, [sh[:, c:c + LANES] for c in range(0, width, LANES)])
            return jnp.max(tile_max, axis=1, keepdims=True)
        m_new = [jnp.maximum(m_prev[h], row_max(s[h])) for h in heads]
        p = [jnp.exp2(s[h] - jnp.concatenate([m_new[h]] * (width // LANES), axis=1)).astype(BF16)
             for h in heads]
        v1 = []
        for ls in lanes:
            v = v_ref[rows, ls]
            one = jnp.ones_like(v)
            v1 += [jnp.where(head0, v, one), jnp.where(head0, one, v)]
        pv = [jnp.dot(p[h], v1[h], preferred_element_type=F32) for h in heads]
        for h in heads:
            acc_ref[h] = jnp.exp2(m_prev[h] - m_new[h]) * acc_ref[h] + pv[h]
            m_ref[h] = m_new[h]

    def full_block(kj, carry):
        block(pl.multiple_of(kj * tk, tk), tk, False)
        return carry

    n_full = row_first // tk
    lax.fori_loop(0, n_full, full_block, 0)
    if tk > tq >= LANES:
        def rest_block(i, carry):
            block(pl.multiple_of(n_full * tk + i * tq, tq), tq, False)
            return carry
        lax.fori_loop(0, (row_first - n_full * tk) // tq, rest_block, 0)
    diag_w = max(tq, LANES)
    block(pl.multiple_of(row_first, tq), diag_w, True)

    if o_ref.shape[0] > tq:
        o_ref[...] = jnp.zeros_like(o_ref)
    for j in range(PAIRS):
        a0, a1 = acc_ref[2 * j], acc_ref[2 * j + 1]
        o_ref[0:tq, j * LANES:(j + 1) * LANES] = jnp.where(
            head0, a0 / pltpu.roll(a0, HEAD_DIM, axis=1), a1 / pltpu.roll(a1, HEAD_DIM, axis=1))


def _fox_attn(qkvb, q_row0, n_rows, tq, tk, out_rows=None, into=None):
    b, l, _ = qkvb.shape
    out_rows = out_rows or tq
    assert q_row0 % tq == 0 and q_row0 % out_rows == 0
    assert tk % tq == 0 if tq >= LANES else (q_row0 % tk == 0 and n_rows == tq)
    assert q_row0 + n_rows - tq + max(tq, LANES) <= l
    qb0, ob0 = q_row0 // tq, q_row0 // out_rows
    kv_spec = lambda col: pl.BlockSpec((None, l, WIDTH), lambda bi, qi: (bi, 0, col))
    extra_in, extra_spec, alias = (), [], {}
    if into is not None:
        extra_in, extra_spec, alias = (into,), [pl.BlockSpec(memory_space=pl.ANY)], {4: 0}
    return pl.pallas_call(
        functools.partial(_fox_attn_body, q_row0=q_row0, tk=tk),
        out_shape=jax.ShapeDtypeStruct((b, l, WIDTH), F32),
        grid=(b, n_rows // tq),
        in_specs=[pl.BlockSpec((None, tq, WIDTH), lambda bi, qi: (bi, qb0 + qi, 0)),
                  kv_spec(1), kv_spec(2), kv_spec(3)] + extra_spec,
        out_specs=pl.BlockSpec((None, out_rows, WIDTH), lambda bi, qi: (bi, ob0 + qi, 0)),
        scratch_shapes=[pltpu.VMEM((N_HEADS, tq, LANES), F32), pltpu.VMEM((N_HEADS, tq, LANES), F32)],
        input_output_aliases=alias,
        compiler_params=_cparams(("arbitrary", "arbitrary")),
        name="fox_attn",
    )(qkvb, qkvb, qkvb, qkvb, *extra_in)


def _mixed_residual(h_ref, y_ref, bonus_ref, g_ref, o_ref, og_ref, gnw_ref, gnb_ref, ones_ref, wo_ref):
    ones = ones_ref[...]
    y = jnp.concatenate([y_ref[j] for j in range(PAIRS)], axis=1)
    mean = _dot_ones(y, ones) * (1.0 / HEAD_DIM)
    d = y - mean
    var = _dot_ones(d * d, ones) * (1.0 / HEAD_DIM)
    yn = d * lax.rsqrt(var + GN_EPS) * gnw_ref[...] + gnb_ref[...]
    y_rw = ((yn + bonus_ref[...]) * g_ref[...]).astype(BF16)
    y_fx = (o_ref[...] * _sigmoid(og_ref[...])).astype(BF16)
    return (h_ref[...]
            + jnp.dot(y_rw, wo_ref[0:WIDTH, :], preferred_element_type=F32)
            + jnp.dot(y_fx, wo_ref[WIDTH:2 * WIDTH, :], preferred_element_type=F32))


def _mix_ffn_body(h_ref, y_ref, bonus_ref, gate_ref, o_ref, og_ref, gnw_ref, gnb_ref, ones_ref, wo_ref,
                  g_ref, wu_ref, wg_ref, cwu_ref, cwg_ref, cbu_ref, cbg_ref, wout_ref,
                  out_ref, xn_ref, cu_ref, cg_ref, wu_s, wg_s):
    tm = h_ref.shape[0]
    halo = wu_s.shape[0] - tm
    ti = pl.program_id(1)
    j = pl.program_id(2)

    @pl.when(j == 0)
    def _():
        x = _mixed_residual(h_ref, y_ref, bonus_ref, gate_ref, o_ref, og_ref, gnw_ref, gnb_ref,
                            ones_ref, wo_ref)
        ms = jnp.mean(x * x, axis=-1, keepdims=True)
        xn_ref[...] = (x * lax.rsqrt(ms + NORM_EPS) * g_ref[...]).astype(BF16)
        out_ref[...] = x

    @pl.when(ti == 0)
    def _():
        cu_ref[j] = jnp.zeros(cu_ref.shape[1:], F32)
        cg_ref[j] = jnp.zeros(cg_ref.shape[1:], F32)

    xn = xn_ref[...]

    def conv(w_ref, cw_ref, cb_ref, carry_ref, work):
        work[0:halo] = carry_ref[j]
        work[halo:] = jnp.dot(xn, w_ref[...], preferred_element_type=F32)
        carry_ref[j] = work[tm:]
        cw = cw_ref[...]
        taps = [work[pl.ds(halo - (CONV_W - 1) + i, tm), :] for i in range(CONV_W)]
        return cb_ref[...] + cw[0:1, :] * taps[0] + cw[1:2, :] * taps[1] + cw[2:3, :] * taps[2]

    u = conv(wu_ref, cwu_ref, cbu_ref, cu_ref, wu_s)
    gt = conv(wg_ref, cwg_ref, cbg_ref, cg_ref, wg_s)
    act = (gt * _sigmoid(gt) * u).astype(BF16)
    out_ref[...] += jnp.dot(act, wout_ref[...], preferred_element_type=F32)


def _mix_ffn(h, y, bonus, gate, o_fx, og, gn_w, gn_b, ones512, wo_bf16,
             g, w_in_bf16, conv_w, conv_b, w_out_bf16, tm, tf):
    b, l, d = h.shape
    d_ff = w_out_bf16.shape[0]
    nj = d_ff // tf
    cw = jnp.pad(conv_w, ((0, 8 - CONV_W), (0, 0)))
    cb = conv_b.reshape(1, 2 * d_ff)
    full = lambda shape: pl.BlockSpec(shape, lambda bi, ti, j: (0,) * len(shape))
    rows = pl.BlockSpec((None, tm, WIDTH), lambda bi, ti, j: (bi, ti, 0))
    return pl.pallas_call(
        _mix_ffn_body,
        out_shape=jax.ShapeDtypeStruct((b, l, d), F32),
        grid=(b, l // tm, nj),
        in_specs=[pl.BlockSpec((None, tm, d), lambda bi, ti, j: (bi, ti, 0)),
                  pl.BlockSpec((None, PAIRS, tm, LANES), lambda bi, ti, j: (bi, 0, ti, 0)),
                  rows, rows, rows, rows,
                  full((1, WIDTH)), full((1, WIDTH)), full(ones512.shape), full((2 * WIDTH, d)),
                  pl.BlockSpec((1, d), lambda bi, ti, j: (0, 0)),
                  pl.BlockSpec((d, tf), lambda bi, ti, j: (0, j)),
                  pl.BlockSpec((d, tf), lambda bi, ti, j: (0, nj + j)),
                  pl.BlockSpec((8, tf), lambda bi, ti, j: (0, j)),
                  pl.BlockSpec((8, tf), lambda bi, ti, j: (0, nj + j)),
                  pl.BlockSpec((1, tf), lambda bi, ti, j: (0, j)),
                  pl.BlockSpec((1, tf), lambda bi, ti, j: (0, nj + j)),
                  pl.BlockSpec((tf, d), lambda bi, ti, j: (j, 0))],
        out_specs=pl.BlockSpec((None, tm, d), lambda bi, ti, j: (bi, ti, 0)),
        scratch_shapes=[pltpu.VMEM((tm, d), BF16),
                        pltpu.VMEM((nj, 8, tf), F32), pltpu.VMEM((nj, 8, tf), F32),
                        pltpu.VMEM((8 + tm, tf), F32), pltpu.VMEM((8 + tm, tf), F32)],
        compiler_params=_cparams(("arbitrary", "arbitrary", "arbitrary")),
        name="mix_ffn",
    )(h, y, bonus, gate, o_fx, og, gn_w, gn_b, ones512, wo_bf16,
      g.reshape(1, d), w_in_bf16, w_in_bf16, cw, cw, cb, cb, w_out_bf16)


def _pad_cols(w, n):
    return jnp.pad(w, ((0, 0), (0, n - w.shape[1])))


def _layer(h, prm, l_real):
    (norm1_g, w_in, rw_mu, rw_w0, rw_w_up, rw_a0, rw_a_up, rw_g_up, rw_k_k, rw_k_a, rw_r_k,
     rw_gn_w, rw_gn_b, fx_b_f, fx_q_g, fx_k_g, w_o, norm2_g, ffn_w_in, ffn_conv_w, ffn_conv_b,
     ffn_w_out) = prm
    b, l, d = h.shape
    rw_cols = 3 * WIDTH + DECAY_LORA + AAA_LORA + GATE_LORA

    w_rw = _pad_cols(w_in[:, :rw_cols], RW_PCOLS)
    w_fx = _pad_cols(w_in[:, rw_cols:], FX_PCOLS)
    mu_p = _pad_cols(rw_mu.reshape(1, rw_cols), RW_PCOLS)
    wcomb = jnp.zeros((128, 2 * WIDTH), F32)
    wcomb = wcomb.at[:DECAY_LORA, :WIDTH].set(rw_w_up).at[DECAY_LORA:, WIDTH:].set(rw_a_up)
    w0a0 = jnp.concatenate([rw_w0, rw_a0]).reshape(1, 2 * WIDTH)
    gup_p = jnp.pad(rw_g_up, ((0, GATE_PAD - GATE_LORA), (0, 0)))
    ones512 = _group_ones(MXU_TILE, HEAD_DIM)
    gain = jnp.concatenate([jnp.tile(fx_q_g, N_HEADS) * (HEAD_DIM ** -0.5 * LOG2E),
                            jnp.tile(fx_k_g, N_HEADS)]).reshape(1, 2 * WIDTH)
    bf_p = jnp.pad(fx_b_f, (0, LANES - N_HEADS)).reshape(1, LANES)

    t1 = _pick(l, (272, 128))
    t2 = _pick(l, (544, 384, 256, 128))

    r, lw, k, v, a, bb, g, bonus, og, qkvb = _proj_prep(
        h, norm1_g, w_rw, w_fx, mu_p, wcomb, w0a0, gup_p, rw_k_k.reshape(1, WIDTH),
        rw_k_a.reshape(1, WIDTH), rw_r_k.reshape(1, WIDTH), gain, bf_p, ones512, t1)

    pm = lambda z: z.reshape(b * PAIRS, l, LANES)
    y = _rwkv_chunk(pm(r), pm(lw), pm(k), pm(v), pm(a), pm(bb), npg=8)
    y = y.reshape(b, PAIRS, l, LANES)

    n_main = l // ATTN_BLOCK * ATTN_BLOCK
    o_fx = _fox_attn(qkvb, 0, n_main, ATTN_BLOCK, ATTN_KEYS) if n_main else None
    if l > n_main:
        tq_tail = min(max(l_real - n_main, 1) + 15 & ~15, l - n_main)
        tk_tail = ATTN_KEYS if n_main % ATTN_KEYS == 0 else ATTN_BLOCK
        o_fx = _fox_attn(qkvb, n_main, tq_tail, tq_tail, tk_tail, out_rows=l - n_main, into=o_fx)

    d_ff = ffn_w_out.shape[0]
    tf = _pick(d_ff, (1408, 256, 128))
    return _mix_ffn(h, y, bonus, g, o_fx, og, rw_gn_w.reshape(1, WIDTH), rw_gn_b.reshape(1, WIDTH),
                    ones512, w_o, norm2_g, ffn_w_in, ffn_conv_w, ffn_conv_b, ffn_w_out, t2, tf)


def kernel(x, meta, norm1_g, w_in, rw_mu, rw_w0, rw_w_up, rw_a0, rw_a_up, rw_g_up, rw_k_k, rw_k_a,
           rw_r_k, rw_gn_w, rw_gn_b, fx_b_f, fx_q_g, fx_k_g, w_o, norm2_g, ffn_w_in, ffn_conv_w,
           ffn_conv_b, ffn_w_out):
    b, seq, d = x.shape
    w_in, w_o, ffn_w_in, ffn_w_out = (w.astype(BF16) for w in (w_in, w_o, ffn_w_in, ffn_w_out))
    params = (norm1_g, w_in, rw_mu, rw_w0, rw_w_up, rw_a0, rw_a_up, rw_g_up, rw_k_k, rw_k_a, rw_r_k,
              rw_gn_w, rw_gn_b, fx_b_f, fx_q_g, fx_k_g, w_o, norm2_g, ffn_w_in, ffn_conv_w,
              ffn_conv_b, ffn_w_out)
    l = N_META + seq
    lp = -(-l // SB) * SB
    meta_b = jnp.broadcast_to(meta[None].astype(x.dtype), (b, N_META, d))
    h = jnp.concatenate([meta_b, x, jnp.zeros((b, lp - l, d), x.dtype)], axis=1)
    for layer in range(norm1_g.shape[0]):
        h = _layer(h, tuple(p[layer] for p in params), l)
    return h[:, N_META:l]
```

```python
import functools

import jax
import jax.numpy as jnp
from jax import lax
from jax.experimental import pallas as pl
from jax.experimental.pallas import tpu as pltpu

F32 = jnp.float32
BF16 = jnp.bfloat16

N_META = 16
HEAD_DIM = 64
N_HEADS = 8
WIDTH = N_HEADS * HEAD_DIM
DECAY_LORA = 64
AAA_LORA = 64
GATE_LORA = 160
GATE_PAD = 256
RW_PCOLS = 3 * WIDTH + 128 + GATE_PAD
FX_PCOLS = 4 * WIDTH + 128
CONV_W = 3
NORM_EPS = 1e-6
GN_EPS = HEAD_DIM * 1e-5
LANES = 128
MXU_TILE = 256
PAIRS = WIDTH // LANES
CHUNK = 16
SB = 128
ATTN_BLOCK = 256
ATTN_KEYS = 512
BIAS_LANES = 3
LOG2E = 1.4426950408889634
MASK_VALUE = -1e30
VMEM_LIMIT = 56 * 1024 * 1024


def _pick(n, cands):
    for c in cands:
        if n % c == 0:
            return c
    raise ValueError(f"no tile for {n} in {cands}")


def _cparams(sem):
    return pltpu.CompilerParams(dimension_semantics=sem, vmem_limit_bytes=VMEM_LIMIT)


def _sigmoid(x):
    return 1.0 / (1.0 + jnp.exp(-x))


def _softplus(x):
    return jnp.maximum(x, 0.0) + jnp.log(1.0 + jnp.exp(-jnp.abs(x)))


def _split2(x):
    hi = x.astype(BF16)
    lo = (x - hi.astype(F32)).astype(BF16)
    return hi, lo


def _split3(x):
    p1 = x.astype(BF16)
    rem = x - p1.astype(F32)
    p2 = rem.astype(BF16)
    p3 = (rem - p2.astype(F32)).astype(BF16)
    return p1, p2, p3


def _dot_ones(x, ones_bf16):
    xb = x.astype(BF16)
    t = ones_bf16.shape[0]
    return jnp.concatenate([jnp.dot(xb[:, c:c + t], ones_bf16, preferred_element_type=F32)
                            for c in range(0, x.shape[1], t)], axis=1)


def _mm3(a_hi, a_lo, b_hi, b_lo, dims=((1,), (0,))):
    dn = (dims, ((), ()))
    m_axis = 1 - dims[0][0]
    m = a_hi.shape[m_axis]
    d = lax.dot_general(jnp.concatenate([a_hi, a_lo], axis=m_axis), b_hi, dn,
                        preferred_element_type=F32)
    return d[:m] + d[m:] + lax.dot_general(a_hi, b_lo, dn, preferred_element_type=F32)


def _dot_bf16(a, b):
    return jnp.dot(a.astype(BF16), b.astype(BF16), preferred_element_type=F32)


def _group_ones(n, group):
    i = jnp.arange(n) // group
    return (i[:, None] == i[None, :]).astype(BF16)


def _project(xn, w_ref, p_ref, n_chunk=512):
    n = p_ref.shape[-1]
    for c in range(0, n, n_chunk):
        e = min(c + n_chunk, n)
        p_ref[:, c:e] = jnp.dot(xn, w_ref[:, c:e], preferred_element_type=F32)


def _rwkv_features(p_ref, mu_ref, wcomb_ref, w0a0_ref, gup_ref, kk_ref, ka_ref, rk_ref, ones_ref,
                   r_o, lw_o, k_o, v_o, a_o, b_o, g_o, bonus_o, carry_ref):
    t2 = p_ref.shape[0]
    p = p_ref[...]
    rows = lax.broadcasted_iota(jnp.int32, p.shape, 0)
    prev = jnp.where(rows == 0, carry_ref[...], pltpu.roll(p, 1, axis=0))
    carry_ref[...] = p[t2 - 1:t2, :]
    x = p + mu_ref[...] * (prev - p)

    r = x[:, 0:WIDTH]
    k = x[:, WIDTH:2 * WIDTH]
    v = x[:, 2 * WIDTH:3 * WIDTH]
    wa = x[:, 3 * WIDTH:3 * WIDTH + 128]
    gd = x[:, 3 * WIDTH + 128:]

    lane = lax.broadcasted_iota(jnp.int32, wa.shape, 1)
    xa = jnp.where(lane < DECAY_LORA, jnp.tanh(wa), wa)
    pre = _dot_bf16(xa, wcomb_ref[...]) + w0a0_ref[...]
    w_log = -_softplus(-pre[:, :WIDTH]) - 0.5
    log_decay = -jnp.exp(w_log)
    a = _sigmoid(pre[:, WIDTH:])
    g = _dot_bf16(_sigmoid(gd), gup_ref[...])

    ones = ones_ref[...]
    kk = k * kk_ref[...]
    ss = _dot_ones(kk * kk, ones)
    kk = kk / jnp.maximum(jnp.sqrt(ss), 1e-12)
    k2 = k * (1.0 + (a - 1.0) * ka_ref[...])
    bonus = _dot_ones(r * k2 * rk_ref[...], ones) * v

    for j in range(PAIRS):
        ls = slice(j * LANES, (j + 1) * LANES)
        r_o[j] = r[:, ls]
        lw_o[j] = log_decay[:, ls]
        k_o[j] = k2[:, ls]
        v_o[j] = v[:, ls].astype(v_o.dtype)
        a_o[j] = -kk[:, ls]
        b_o[j] = (kk * a)[:, ls]
    g_o[...] = g
    bonus_o[...] = bonus


def _fox_features(p_ref, gain_ref, bf_ref, ones_ref, place_ref, og_o, qkvb_o, carry_ref):
    t2 = p_ref.shape[0]
    ones = ones_ref[...]
    for h0 in (0, WIDTH):
        x = p_ref[:, h0:h0 + WIDTH]
        ms = _dot_ones(x * x, ones) * (1.0 / HEAD_DIM)
        qkvb_o[:, h0:h0 + WIDTH] = (x * lax.rsqrt(ms + NORM_EPS) * gain_ref[:, h0:h0 + WIDTH]).astype(BF16)
    qkvb_o[:, 2 * WIDTH:3 * WIDTH] = p_ref[:, 2 * WIDTH:3 * WIDTH].astype(BF16)
    og_o[...] = p_ref[:, 3 * WIDTH:4 * WIDTH]

    logf = -_softplus(-(p_ref[:, 4 * WIDTH:] + bf_ref[...]))
    ri = lax.broadcasted_iota(jnp.int32, (t2, t2), 0)
    ci = lax.broadcasted_iota(jnp.int32, (t2, t2), 1)
    tri = (ci <= ri).astype(BF16)
    c = sum(jnp.dot(tri, piece, preferred_element_type=F32) for piece in _split3(logf)) + carry_ref[...]
    carry_ref[...] = c[t2 - 1:t2, :]
    pieces = jnp.concatenate(_split3(c * (-LOG2E)), axis=1)
    qkvb_o[:, 3 * WIDTH:] = jnp.dot(pieces, place_ref[...], preferred_element_type=F32).astype(BF16)


def _proj_prep_body(h_ref, n1_ref, wrw_ref, wfx_ref, mu_ref, wcomb_ref, w0a0_ref, gup_ref, kk_ref,
                    ka_ref, rk_ref, gain_ref, bf_ref, ones_ref, place_ref,
                    r_o, lw_o, k_o, v_o, a_o, b_o, g_o, bonus_o, og_o, qkvb_o,
                    prw_ref, pfx_ref, shift_ref, csum_ref):
    @pl.when(pl.program_id(1) == 0)
    def _():
        shift_ref[...] = jnp.zeros_like(shift_ref)
        csum_ref[...] = jnp.zeros_like(csum_ref)

    x = h_ref[...]
    ms = jnp.mean(x * x, axis=-1, keepdims=True)
    xn = (x * lax.rsqrt(ms + NORM_EPS) * n1_ref[...]).astype(BF16)
    _project(xn, wrw_ref, prw_ref)
    _project(xn, wfx_ref, pfx_ref)
    _rwkv_features(prw_ref, mu_ref, wcomb_ref, w0a0_ref, gup_ref, kk_ref, ka_ref, rk_ref, ones_ref,
                   r_o, lw_o, k_o, v_o, a_o, b_o, g_o, bonus_o, shift_ref)
    _fox_features(pfx_ref, gain_ref, bf_ref, ones_ref, place_ref, og_o, qkvb_o, csum_ref)


def _bias_placement():
    src = jnp.arange(3 * LANES)
    piece, head = src // LANES, src % LANES
    dst = (head // 2) * LANES + BIAS_LANES * (head % 2) + piece
    hit = (dst[:, None] == jnp.arange(WIDTH)[None, :]) & (head[:, None] < N_HEADS)
    return hit.astype(BF16)


def _proj_prep(h, norm1_g, w_rw, w_fx, mu_p, wcomb, w0a0, gup_p, k_k, k_a, r_k, gain, bf_p, ones512, tm):
    b, l, d = h.shape
    full = lambda shape: pl.BlockSpec(shape, lambda bi, ti: (0,) * len(shape), pipeline_mode=pl.Buffered(1))
    pair = lambda dt: (jax.ShapeDtypeStruct((b, PAIRS, l, LANES), dt),
                       pl.BlockSpec((None, PAIRS, tm, LANES), lambda bi, ti: (bi, 0, ti, 0)))
    rows = lambda n, dt: (jax.ShapeDtypeStruct((b, l, n), dt),
                          pl.BlockSpec((None, tm, n), lambda bi, ti: (bi, ti, 0)))
    outs = [pair(F32), pair(F32), pair(F32), pair(BF16), pair(F32), pair(F32),
            rows(WIDTH, F32), rows(WIDTH, F32), rows(WIDTH, F32), rows(4 * WIDTH, BF16)]
    consts = (norm1_g.reshape(1, d), w_rw, w_fx, mu_p, wcomb, w0a0, gup_p, k_k, k_a, r_k, gain, bf_p,
              ones512, _bias_placement())
    return pl.pallas_call(
        _proj_prep_body,
        out_shape=[o[0] for o in outs],
        grid=(b, l // tm),
        in_specs=[pl.BlockSpec((None, tm, d), lambda bi, ti: (bi, ti, 0))] + [full(c.shape) for c in consts],
        out_specs=[o[1] for o in outs],
        scratch_shapes=[pltpu.VMEM((tm, RW_PCOLS), F32), pltpu.VMEM((tm, FX_PCOLS), F32),
                        pltpu.VMEM((1, RW_PCOLS), F32), pltpu.VMEM((1, LANES), F32)],
        compiler_params=_cparams(("arbitrary", "arbitrary")),
        name="proj_prep",
    )(h, *consts)


def _chunk_masks():
    ti = jnp.arange(SB)[:, None]
    si = jnp.arange(SB)[None, :]
    same = (ti // CHUNK) == (si // CHUNK)
    masks = [same, same & (si <= ti), same & (si < ti), ti == si, (ti < HEAD_DIM) == (si < HEAD_DIM)]
    return jnp.stack(masks).astype(F32)


def _rwkv_chunk_body(r_ref, lw_ref, k_ref, v_ref, a_ref, b_ref, masks_ref, y_ref, h_ref):
    npg = r_ref.shape[0]
    n_chunks = SB // CHUNK

    @pl.when(pl.program_id(1) == 0)
    def _():
        h_ref[...] = jnp.zeros_like(h_ref)

    same, incl, strict, _, same_head = (masks_ref[i] != 0.0 for i in range(5))
    eye_f = masks_ref[3]
    head0 = lax.broadcasted_iota(jnp.int32, (SB, LANES), 1) < HEAD_DIM
    tri = masks_ref[1].astype(BF16)
    bf = lambda z: z.astype(BF16)
    dot = functools.partial(jnp.dot, preferred_element_type=F32)

    def keep(mask, x):
        return jnp.where(mask, x, jnp.zeros_like(x))

    pairs = range(npg)
    units = [(p, h) for p in pairs for h in range(2)]
    cat0 = lambda *xs: jnp.concatenate(xs, axis=0)
    mm = lambda a, b, **kw: _mm3(*a, *b, **kw)

    lw = [lw_ref[p] for p in pairs]
    cs = [sum(dot(tri, piece) for piece in _split2(lw[p])) for p in pairs]
    ce = [cat0(*[jnp.broadcast_to(cs[p][(c + 1) * CHUNK - 1:(c + 1) * CHUNK, :], (CHUNK, LANES))
                 for c in range(n_chunks)]) for p in pairs]
    g = [jnp.exp(cs[p]) for p in pairs]
    ginv = [jnp.exp(-cs[p]) for p in pairs]
    gend = [jnp.exp(ce[p] - cs[p]) for p in pairs]
    g_c = [jnp.exp(ce[p]) for p in pairs]
    at = [a_ref[p] * jnp.exp(cs[p] - lw[p]) for p in pairs]
    rt = [r_ref[p] * g[p] for p in pairs]
    v_b = [bf(v_ref[p]) for p in pairs]
    at_b = [bf(at[p]) for p in pairs]
    rt_b = [bf(rt[p]) for p in pairs]
    bk_t = [cat0(bf(b_ref[p] * ginv[p]), bf(k_ref[p] * ginv[p])) for p in pairs]
    bh_b = [bf(b_ref[p] * gend[p]) for p in pairs]
    kh_b = [bf(k_ref[p] * gend[p]) for p in pairs]

    gram = [lax.dot_general(cat0(keep(head0, at_b[p]), keep(head0, rt_b[p]),
                                 keep(~head0, at_b[p]), keep(~head0, rt_b[p])), bk_t[p],
                            (((1,), (1,)), ((), ())), preferred_element_type=F32)
            for p in pairs]
    gb = {(p, h): gram[p][2 * SB * h:2 * SB * (h + 1)] for p, h in units}
    mab = {u: keep(strict, gb[u][:SB, :SB]) for u in units}
    mkk_b = {u: bf(cat0(keep(strict, gb[u][:SB, SB:]), keep(incl, gb[u][SB:, SB:]))) for u in units}
    mrb_b = {u: bf(keep(incl, gb[u][SB:, :SB])) for u in units}

    def fold(z):
        return sum(z[c * CHUNK:(c + 1) * CHUNK] for c in range(n_chunks))

    def unfold(zf):
        return keep(same, jnp.concatenate([zf] * n_chunks, axis=0))

    nf = {u: fold(mab[u]) for u in units}
    tf = {u: fold(eye_f) + nf[u] for u in units}
    full = {u: _split2(mab[u]) for u in units}
    nf = {u: mm(_split2(nf[u]), full[u]) for u in units}
    for k in (2, 4, 8):
        full = {u: _split2(unfold(nf[u])) for u in units}
        if k < 8:
            both_f = {u: mm(_split2(cat0(nf[u], tf[u])), full[u]) for u in units}
            nf = {u: both_f[u][:CHUNK] for u in units}
            tf = {u: tf[u] + both_f[u][CHUNK:] for u in units}
        else:
            tf = {u: tf[u] + mm(_split2(tf[u]), full[u]) for u in units}
    t = {u: unfold(tf[u]) for u in units}
    wk = {u: dot(mkk_b[u], v_b[u[0]]) for u in units}
    x = {u: dot(bf(t[u]), jnp.concatenate([at_b[u[0]], bf(wk[u][:SB])], axis=1)) for u in units}
    ry = {u: dot(mrb_b[u], bf(x[u])) for u in units}
    both = lambda f: [jnp.where(head0, f((p, 0)), f((p, 1))) for p in pairs]
    abar_b = [bf(z) for z in both(lambda u: x[u][:, :LANES])]
    vbar = both(lambda u: x[u][:, LANES:])
    rbar_b = [bf(z) for z in both(lambda u: rt[u[0]] + ry[u][:, :LANES])]
    ybar = both(lambda u: wk[u][SB:] + ry[u][:, LANES:])

    g_ct = [g_c[p].T for p in pairs]

    hbd = [h_ref[p] for p in pairs]
    for c in range(n_chunks):
        sl = slice(c * CHUNK, (c + 1) * CHUNK)
        yu = [dot(cat0(rbar_b[p][sl], abar_b[p][sl]), bf(hbd[p])) for p in pairs]
        for p in pairs:
            y_ref[p, sl, :] = yu[p][:CHUNK] + ybar[p][sl]
        upd = [lax.dot_general(cat0(bh_b[p][sl], kh_b[p][sl]),
                               cat0(bf(yu[p][CHUNK:] + vbar[p][sl]), v_b[p][sl]),
                               (((0,), (0,)), ((), ())), preferred_element_type=F32)
               for p in pairs]
        hbd = [jnp.broadcast_to(g_ct[p][:, c * CHUNK:c * CHUNK + 1], (LANES, LANES)) * hbd[p]
               + keep(same_head, upd[p]) for p in pairs]
    for p in pairs:
        h_ref[p] = hbd[p]


def _rwkv_chunk(r, lw, k, v, a, b, npg):
    n_pairs, l, _ = r.shape
    blk = pl.BlockSpec((npg, SB, LANES), lambda gi, ti: (gi, ti, 0))
    return pl.pallas_call(
        _rwkv_chunk_body,
        out_shape=jax.ShapeDtypeStruct((n_pairs, l, LANES), F32),
        grid=(n_pairs // npg, l // SB),
        in_specs=[blk] * 6 + [pl.BlockSpec((5, SB, SB), lambda gi, ti: (0, 0, 0))],
        out_specs=blk,
        scratch_shapes=[pltpu.VMEM((npg, LANES, LANES), F32)],
        compiler_params=_cparams(("arbitrary", "arbitrary")),
        name="rwkv_chunk",
    )(r, lw, k, v, a, b, _chunk_masks())


def _fox_attn_body(q_ref, k_ref, v_ref, kb_ref, *rest, q_row0, tk):
    o_ref, m_ref, acc_ref = rest[-3:]
    tq = q_ref.shape[0]
    row_first = q_row0 + pl.program_id(1) * tq

    m_ref[...] = jnp.full_like(m_ref, MASK_VALUE)
    acc_ref[...] = jnp.zeros_like(acc_ref)

    lane = lax.broadcasted_iota(jnp.int32, (1, LANES), 1)
    head0 = lane < HEAD_DIM
    row2 = lax.broadcasted_iota(jnp.int32, (2 * tq, LANES), 0)
    lane2 = lax.broadcasted_iota(jnp.int32, (2 * tq, LANES), 1)
    first_lane = jnp.where(row2 < tq, 0, BIAS_LANES)
    bias_on = (lane2 >= first_lane) & (lane2 < first_lane + BIAS_LANES)
    bias_q = jnp.where(bias_on, 1.0, 0.0).astype(BF16)

    def block(col_first, width, masked):
        rows = pl.ds(col_first, width)
        heads = range(N_HEADS)
        lanes = [slice(j * LANES, (j + 1) * LANES) for j in range(PAIRS)]
        s2 = []
        for ls in lanes:
            q = q_ref[:, ls]
            zero = jnp.zeros_like(q)
            q2 = jnp.concatenate([jnp.where(head0, q, zero), jnp.where(head0, zero, q)], axis=0)
            s2.append(lax.dot_general(jnp.concatenate([q2, bias_q], axis=1),
                                      jnp.concatenate([k_ref[rows, ls], kb_ref[rows, ls]], axis=1),
                                      (((1,), (1,)), ((), ())), preferred_element_type=F32))
        s = [s2[h // 2][(h % 2) * tq:(h % 2 + 1) * tq] for h in heads]
        if masked:
            causal = (lax.broadcasted_iota(jnp.int32, (tq, width), 1)
                      <= lax.broadcasted_iota(jnp.int32, (tq, width), 0))
            s = [jnp.where(causal, sh, MASK_VALUE) for sh in s]
        m_prev = [m_ref[h] for h in heads]
        def row_max(sh):
            tile_max = functools.reduce(jnp.maximum, [sh[:, c:c + LANES] for c in range(0, width, LANES)])
            return jnp.max(tile_max, axis=1, keepdims=True)
        m_new = [jnp.maximum(m_prev[h], row_max(s[h])) for h in heads]
        p = [jnp.exp2(s[h] - jnp.concatenate([m_new[h]] * (width // LANES), axis=1)).astype(BF16)
             for h in heads]
        v1 = []
        for ls in lanes:
            v = v_ref[rows, ls]
            one = jnp.ones_like(v)
            v1 += [jnp.where(head0, v, one), jnp.where(head0, one, v)]
        pv = [jnp.dot(p[h], v1[h], preferred_element_type=F32) for h in heads]
        for h in heads:
            acc_ref[h] = jnp.exp2(m_prev[h] - m_new[h]) * acc_ref[h] + pv[h]
            m_ref[h] = m_new[h]

    def full_block(kj, carry):
        block(pl.multiple_of(kj * tk, tk), tk, False)
        return carry

    n_full = row_first // tk
    lax.fori_loop(0, n_full, full_block, 0)
    if tk > tq >= LANES:
        def rest_block(i, carry):
            block(pl.multiple_of(n_full * tk + i * tq, tq), tq, False)
            return carry
        lax.fori_loop(0, (row_first - n_full * tk) // tq, rest_block, 0)
    diag_w = max(tq, LANES)
    block(pl.multiple_of(row_first, tq), diag_w, True)

    if o_ref.shape[0] > tq:
        o_ref[...] = jnp.zeros_like(o_ref)
    for j in range(PAIRS):
        a0, a1 = acc_ref[2 * j], acc_ref[2 * j + 1]
        o_ref[0:tq, j * LANES:(j + 1) * LANES] = jnp.where(
            head0, a0 / pltpu.roll(a0, HEAD_DIM, axis=1), a1 / pltpu.roll(a1, HEAD_DIM, axis=1))


def _fox_attn(qkvb, q_row0, n_rows, tq, tk, out_rows=None, into=None):
    b, l, _ = qkvb.shape
    out_rows = out_rows or tq
    assert q_row0 % tq == 0 and q_row0 % out_rows == 0
    assert tk % tq == 0 if tq >= LANES else (q_row0 % tk == 0 and n_rows == tq)
    assert q_row0 + n_rows - tq + max(tq, LANES) <= l
    qb0, ob0 = q_row0 // tq, q_row0 // out_rows
    kv_spec = lambda col: pl.BlockSpec((None, l, WIDTH), lambda bi, qi: (bi, 0, col))
    extra_in, extra_spec, alias = (), [], {}
    if into is not None:
        extra_in, extra_spec, alias = (into,), [pl.BlockSpec(memory_space=pl.ANY)], {4: 0}
    return pl.pallas_call(
        functools.partial(_fox_attn_body, q_row0=q_row0, tk=tk),
        out_shape=jax.ShapeDtypeStruct((b, l, WIDTH), F32),
        grid=(b, n_rows // tq),
        in_specs=[pl.BlockSpec((None, tq, WIDTH), lambda bi, qi: (bi, qb0 + qi, 0)),
                  kv_spec(1), kv_spec(2), kv_spec(3)] + extra_spec,
        out_specs=pl.BlockSpec((None, out_rows, WIDTH), lambda bi, qi: (bi, ob0 + qi, 0)),
        scratch_shapes=[pltpu.VMEM((N_HEADS, tq, LANES), F32), pltpu.VMEM((N_HEADS, tq, LANES), F32)],
        input_output_aliases=alias,
        compiler_params=_cparams(("arbitrary", "arbitrary")),
        name="fox_attn",
    )(qkvb, qkvb, qkvb, qkvb, *extra_in)


def _mixed_residual(h_ref, y_ref, bonus_ref, g_ref, o_ref, og_ref, gnw_ref, gnb_ref, ones_ref, wo_ref):
    ones = ones_ref[...]
    y = jnp.concatenate([y_ref[j] for j in range(PAIRS)], axis=1)
    mean = _dot_ones(y, ones) * (1.0 / HEAD_DIM)
    d = y - mean
    var = _dot_ones(d * d, ones) * (1.0 / HEAD_DIM)
    yn = d * lax.rsqrt(var + GN_EPS) * gnw_ref[...] + gnb_ref[...]
    y_rw = ((yn + bonus_ref[...]) * g_ref[...]).astype(BF16)
    y_fx = (o_ref[...] * _sigmoid(og_ref[...])).astype(BF16)
    return (h_ref[...]
            + jnp.dot(y_rw, wo_ref[0:WIDTH, :], preferred_element_type=F32)
            + jnp.dot(y_fx, wo_ref[WIDTH:2 * WIDTH, :], preferred_element_type=F32))


def _mix_ffn_body(h_ref, y_ref, bonus_ref, gate_ref, o_ref, og_ref, gnw_ref, gnb_ref, ones_ref, wo_ref,
                  g_ref, wu_ref, wg_ref, cwu_ref, cwg_ref, cbu_ref, cbg_ref, wout_ref,
                  out_ref, xn_ref, cu_ref, cg_ref, wu_s, wg_s):
    tm = h_ref.shape[0]
    halo = wu_s.shape[0] - tm
    ti = pl.program_id(1)
    j = pl.program_id(2)

    @pl.when(j == 0)
    def _():
        x = _mixed_residual(h_ref, y_ref, bonus_ref, gate_ref, o_ref, og_ref, gnw_ref, gnb_ref,
                            ones_ref, wo_ref)
        ms = jnp.mean(x * x, axis=-1, keepdims=True)
        xn_ref[...] = (x * lax.rsqrt(ms + NORM_EPS) * g_ref[...]).astype(BF16)
        out_ref[...] = x

    @pl.when(ti == 0)
    def _():
        cu_ref[j] = jnp.zeros(cu_ref.shape[1:], F32)
        cg_ref[j] = jnp.zeros(cg_ref.shape[1:], F32)

    xn = xn_ref[...]

    def conv(w_ref, cw_ref, cb_ref, carry_ref, work):
        work[0:halo] = carry_ref[j]
        work[halo:] = jnp.dot(xn, w_ref[...], preferred_element_type=F32)
        carry_ref[j] = work[tm:]
        cw = cw_ref[...]
        taps = [work[pl.ds(halo - (CONV_W - 1) + i, tm), :] for i in range(CONV_W)]
        return cb_ref[...] + cw[0:1, :] * taps[0] + cw[1:2, :] * taps[1] + cw[2:3, :] * taps[2]

    u = conv(wu_ref, cwu_ref, cbu_ref, cu_ref, wu_s)
    gt = conv(wg_ref, cwg_ref, cbg_ref, cg_ref, wg_s)
    act = (gt * _sigmoid(gt) * u).astype(BF16)
    out_ref[...] += jnp.dot(act, wout_ref[...], preferred_element_type=F32)


def _mix_ffn(h, y, bonus, gate, o_fx, og, gn_w, gn_b, ones512, wo_bf16,
             g, w_in_bf16, conv_w, conv_b, w_out_bf16, tm, tf):
    b, l, d = h.shape
    d_ff = w_out_bf16.shape[0]
    nj = d_ff // tf
    cw = jnp.pad(conv_w, ((0, 8 - CONV_W), (0, 0)))
    cb = conv_b.reshape(1, 2 * d_ff)
    full = lambda shape: pl.BlockSpec(shape, lambda bi, ti, j: (0,) * len(shape))
    rows = pl.BlockSpec((None, tm, WIDTH), lambda bi, ti, j: (bi, ti, 0))
    return pl.pallas_call(
        _mix_ffn_body,
        out_shape=jax.ShapeDtypeStruct((b, l, d), F32),
        grid=(b, l // tm, nj),
        in_specs=[pl.BlockSpec((None, tm, d), lambda bi, ti, j: (bi, ti, 0)),
                  pl.BlockSpec((None, PAIRS, tm, LANES), lambda bi, ti, j: (bi, 0, ti, 0)),
                  rows, rows, rows, rows,
                  full((1, WIDTH)), full((1, WIDTH)), full(ones512.shape), full((2 * WIDTH, d)),
                  pl.BlockSpec((1, d), lambda bi, ti, j: (0, 0)),
                  pl.BlockSpec((d, tf), lambda bi, ti, j: (0, j)),
                  pl.BlockSpec((d, tf), lambda bi, ti, j: (0, nj + j)),
                  pl.BlockSpec((8, tf), lambda bi, ti, j: (0, j)),
                  pl.BlockSpec((8, tf), lambda bi, ti, j: (0, nj + j)),
                  pl.BlockSpec((1, tf), lambda bi, ti, j: (0, j)),
                  pl.BlockSpec((1, tf), lambda bi, ti, j: (0, nj + j)),
                  pl.BlockSpec((tf, d), lambda bi, ti, j: (j, 0))],
        out_specs=pl.BlockSpec((None, tm, d), lambda bi, ti, j: (bi, ti, 0)),
        scratch_shapes=[pltpu.VMEM((tm, d), BF16),
                        pltpu.VMEM((nj, 8, tf), F32), pltpu.VMEM((nj, 8, tf), F32),
                        pltpu.VMEM((8 + tm, tf), F32), pltpu.VMEM((8 + tm, tf), F32)],
        compiler_params=_cparams(("arbitrary", "arbitrary", "arbitrary")),
        name="mix_ffn",
    )(h, y, bonus, gate, o_fx, og, gn_w, gn_b, ones512, wo_bf16,
      g.reshape(1, d), w_in_bf16, w_in_bf16, cw, cw, cb, cb, w_out_bf16)


def _pad_cols(w, n):
    return jnp.pad(w, ((0, 0), (0, n - w.shape[1])))


def _layer(h, prm, l_real):
    (norm1_g, w_in, rw_mu, rw_w0, rw_w_up, rw_a0, rw_a_up, rw_g_up, rw_k_k, rw_k_a, rw_r_k,
     rw_gn_w, rw_gn_b, fx_b_f, fx_q_g, fx_k_g, w_o, norm2_g, ffn_w_in, ffn_conv_w, ffn_conv_b,
     ffn_w_out) = prm
    b, l, d = h.shape
    rw_cols = 3 * WIDTH + DECAY_LORA + AAA_LORA + GATE_LORA

    w_rw = _pad_cols(w_in[:, :rw_cols], RW_PCOLS)
    w_fx = _pad_cols(w_in[:, rw_cols:], FX_PCOLS)
    mu_p = _pad_cols(rw_mu.reshape(1, rw_cols), RW_PCOLS)
    wcomb = jnp.zeros((128, 2 * WIDTH), F32)
    wcomb = wcomb.at[:DECAY_LORA, :WIDTH].set(rw_w_up).at[DECAY_LORA:, WIDTH:].set(rw_a_up)
    w0a0 = jnp.concatenate([rw_w0, rw_a0]).reshape(1, 2 * WIDTH)
    gup_p = jnp.pad(rw_g_up, ((0, GATE_PAD - GATE_LORA), (0, 0)))
    ones512 = _group_ones(MXU_TILE, HEAD_DIM)
    gain = jnp.concatenate([jnp.tile(fx_q_g, N_HEADS) * (HEAD_DIM ** -0.5 * LOG2E),
                            jnp.tile(fx_k_g, N_HEADS)]).reshape(1, 2 * WIDTH)
    bf_p = jnp.pad(fx_b_f, (0, LANES - N_HEADS)).reshape(1, LANES)

    t1 = _pick(l, (544, 384, 128))
    t2 = _pick(l, (544, 384, 256, 128))

    r, lw, k, v, a, bb, g, bonus, og, qkvb = _proj_prep(
        h, norm1_g, w_rw, w_fx, mu_p, wcomb, w0a0, gup_p, rw_k_k.reshape(1, WIDTH),
        rw_k_a.reshape(1, WIDTH), rw_r_k.reshape(1, WIDTH), gain, bf_p, ones512, t1)

    pm = lambda z: z.reshape(b * PAIRS, l, LANES)
    y = _rwkv_chunk(pm(r), pm(lw), pm(k), pm(v), pm(a), pm(bb), npg=8)
    y = y.reshape(b, PAIRS, l, LANES)

    n_main = l // ATTN_BLOCK * ATTN_BLOCK
    o_fx = _fox_attn(qkvb, 0, n_main, ATTN_BLOCK, ATTN_KEYS) if n_main else None
    if l > n_main:
        tq_tail = min(max(l_real - n_main, 1) + 15 & ~15, l - n_main)
        tk_tail = ATTN_KEYS if n_main % ATTN_KEYS == 0 else ATTN_BLOCK
        o_fx = _fox_attn(qkvb, n_main, tq_tail, tq_tail, tk_tail, out_rows=l - n_main, into=o_fx)

    d_ff = ffn_w_out.shape[0]
    tf = _pick(d_ff, (1408, 256, 128))
    return _mix_ffn(h, y, bonus, g, o_fx, og, rw_gn_w.reshape(1, WIDTH), rw_gn_b.reshape(1, WIDTH),
                    ones512, w_o, norm2_g, ffn_w_in, ffn_conv_w, ffn_conv_b, ffn_w_out, t2, tf)


def kernel(x, meta, norm1_g, w_in, rw_mu, rw_w0, rw_w_up, rw_a0, rw_a_up, rw_g_up, rw_k_k, rw_k_a,
           rw_r_k, rw_gn_w, rw_gn_b, fx_b_f, fx_q_g, fx_k_g, w_o, norm2_g, ffn_w_in, ffn_conv_w,
           ffn_conv_b, ffn_w_out):
    b, seq, d = x.shape
    w_in, w_o, ffn_w_in, ffn_w_out = (w.astype(BF16) for w in (w_in, w_o, ffn_w_in, ffn_w_out))
    params = (norm1_g, w_in, rw_mu, rw_w0, rw_w_up, rw_a0, rw_a_up, rw_g_up, rw_k_k, rw_k_a, rw_r_k,
              rw_gn_w, rw_gn_b, fx_b_f, fx_q_g, fx_k_g, w_o, norm2_g, ffn_w_in, ffn_conv_w,
              ffn_conv_b, ffn_w_out)
    l = N_META + seq
    lp = -(-l // SB) * SB
    meta_b = jnp.broadcast_to(meta[None].astype(x.dtype), (b, N_META, d))
    h = jnp.concatenate([meta_b, x, jnp.zeros((b, lp - l, d), x.dtype)], axis=1)
    for layer in range(norm1_g.shape[0]):
        h = _layer(h, tuple(p[layer] for p in params), l)
    return h[:, N_META:l]
```

```python
import functools

import jax
import jax.numpy as jnp
from jax import lax
from jax.experimental import pallas as pl
from jax.experimental.pallas import tpu as pltpu

F32 = jnp.float32
BF16 = jnp.bfloat16

LANES = 128
SUBLANES = 8
BF16_ROWS = 16
MXU_TILE = 256

N_META = 16
HEAD_DIM = 64
N_HEADS = 8
WIDTH = N_HEADS * HEAD_DIM
DECAY_LORA = 64
AAA_LORA = 64
LORA_WA = DECAY_LORA + AAA_LORA
GATE_LORA = 160
GATE_PAD = 256
RW_PCOLS = 3 * WIDTH + LORA_WA + GATE_PAD
FX_PCOLS = 4 * WIDTH + LANES
CONV_W = 3
NORM_EPS = 1e-6
GN_EPS = HEAD_DIM * 1e-5
PAIRS = WIDTH // LANES
CHUNK = 16
SB = 128
ATTN_BLOCK = 256
ATTN_KEYS = 512
BIAS_LANES = 3
LOG2E = 1.4426950408889634
MASK_VALUE = -1e30
VMEM_LIMIT = 58 * 1024 * 1024


def _pick(n, cands):
    for c in cands:
        if n % c == 0:
            return c
    raise ValueError(f"no tile for {n} in {cands}")


def _cparams(sem):
    return pltpu.CompilerParams(dimension_semantics=sem, vmem_limit_bytes=VMEM_LIMIT)


def _sigmoid(x):
    return 1.0 / (1.0 + jnp.exp(-x))


def _softplus(x):
    return jnp.maximum(x, 0.0) + jnp.log(1.0 + jnp.exp(-jnp.abs(x)))


def _split2(x):
    hi = x.astype(BF16)
    lo = (x - hi.astype(F32)).astype(BF16)
    return hi, lo


def _split3(x):
    p1 = x.astype(BF16)
    rem = x - p1.astype(F32)
    p2 = rem.astype(BF16)
    p3 = (rem - p2.astype(F32)).astype(BF16)
    return p1, p2, p3


def _dot_ones(x, ones_bf16):
    xb = x.astype(BF16)
    t = ones_bf16.shape[0]
    return jnp.concatenate([jnp.dot(xb[:, c:c + t], ones_bf16, preferred_element_type=F32)
                            for c in range(0, x.shape[1], t)], axis=1)


def _mm3(a_hi, a_lo, b_hi, b_lo, dims=((1,), (0,))):
    dn = (dims, ((), ()))
    m_axis = 1 - dims[0][0]
    m = a_hi.shape[m_axis]
    d = lax.dot_general(jnp.concatenate([a_hi, a_lo], axis=m_axis), b_hi, dn,
                        preferred_element_type=F32)
    return d[:m] + d[m:] + lax.dot_general(a_hi, b_lo, dn, preferred_element_type=F32)


def _dot_bf16(a, b):
    return jnp.dot(a.astype(BF16), b.astype(BF16), preferred_element_type=F32)


def _group_ones(n, group):
    i = jnp.arange(n) // group
    return (i[:, None] == i[None, :]).astype(BF16)


def _project(xn, w_ref, p_ref, n_chunk=512):
    n = p_ref.shape[-1]
    for c in range(0, n, n_chunk):
        e = min(c + n_chunk, n)
        p_ref[:, c:e] = jnp.dot(xn, w_ref[:, c:e], preferred_element_type=F32)


def _rwkv_features(p_ref, mu_ref, wcomb_ref, w0a0_ref, gup_ref, kk_ref, ka_ref, rk_ref, ones_ref,
                   r_o, lw_o, k_o, v_o, a_o, b_o, g_o, bonus_o, carry_ref):
    t2 = p_ref.shape[0]
    p = p_ref[...]
    rows = lax.broadcasted_iota(jnp.int32, p.shape, 0)
    prev = jnp.where(rows == 0, carry_ref[...], pltpu.roll(p, 1, axis=0))
    carry_ref[...] = p[t2 - 1:t2, :]
    x = p + mu_ref[...] * (prev - p)

    r = x[:, 0:WIDTH]
    k = x[:, WIDTH:2 * WIDTH]
    v = x[:, 2 * WIDTH:3 * WIDTH]
    wa = x[:, 3 * WIDTH:3 * WIDTH + LORA_WA]
    gd = x[:, 3 * WIDTH + LORA_WA:]

    lane = lax.broadcasted_iota(jnp.int32, wa.shape, 1)
    xa = jnp.where(lane < DECAY_LORA, jnp.tanh(wa), wa)
    pre = _dot_bf16(xa, wcomb_ref[...]) + w0a0_ref[...]
    w_log = -_softplus(-pre[:, :WIDTH]) - 0.5
    log_decay = -jnp.exp(w_log)
    a = _sigmoid(pre[:, WIDTH:])
    g = _dot_bf16(_sigmoid(gd), gup_ref[...])

    ones = ones_ref[...]
    kk = k * kk_ref[...]
    ss = _dot_ones(kk * kk, ones)
    kk = kk / jnp.maximum(jnp.sqrt(ss), 1e-12)
    k2 = k * (1.0 + (a - 1.0) * ka_ref[...])
    bonus = _dot_ones(r * k2 * rk_ref[...], ones) * v

    for j in range(PAIRS):
        ls = slice(j * LANES, (j + 1) * LANES)
        r_o[j] = r[:, ls]
        lw_o[j] = log_decay[:, ls]
        k_o[j] = k2[:, ls]
        v_o[j] = v[:, ls].astype(v_o.dtype)
        a_o[j] = -kk[:, ls]
        b_o[j] = (kk * a)[:, ls]
    g_o[...] = g
    bonus_o[...] = bonus


def _fox_features(p_ref, gain_ref, bf_ref, ones_ref, place_ref, og_o, qkvb_o, carry_ref):
    t2 = p_ref.shape[0]
    ones = ones_ref[...]
    for h0 in (0, WIDTH):
        x = p_ref[:, h0:h0 + WIDTH]
        ms = _dot_ones(x * x, ones) * (1.0 / HEAD_DIM)
        qkvb_o[:, h0:h0 + WIDTH] = (x * lax.rsqrt(ms + NORM_EPS) * gain_ref[:, h0:h0 + WIDTH]).astype(BF16)
    qkvb_o[:, 2 * WIDTH:3 * WIDTH] = p_ref[:, 2 * WIDTH:3 * WIDTH].astype(BF16)
    og_o[...] = p_ref[:, 3 * WIDTH:4 * WIDTH]

    logf = -_softplus(-(p_ref[:, 4 * WIDTH:] + bf_ref[...]))
    ri = lax.broadcasted_iota(jnp.int32, (t2, t2), 0)
    ci = lax.broadcasted_iota(jnp.int32, (t2, t2), 1)
    tri = (ci <= ri).astype(BF16)
    c = sum(jnp.dot(tri, piece, preferred_element_type=F32) for piece in _split3(logf)) + carry_ref[...]
    carry_ref[...] = c[t2 - 1:t2, :]
    pieces = jnp.concatenate(_split3(c * (-LOG2E)), axis=1)
    qkvb_o[:, 3 * WIDTH:] = jnp.dot(pieces, place_ref[...], preferred_element_type=F32).astype(BF16)


def _proj_prep_body(h_ref, n1_ref, w_ref, mu_ref, wcomb_ref, w0a0_ref, gup_ref, kk_ref,
                    ka_ref, rk_ref, gain_ref, bf_ref, ones_ref, place_ref,
                    r_o, lw_o, k_o, v_o, a_o, b_o, g_o, bonus_o, og_o, qkvb_o,
                    p_ref, shift_ref, csum_ref):
    prw_ref = p_ref.at[:, 0:RW_PCOLS]
    pfx_ref = p_ref.at[:, RW_PCOLS:RW_PCOLS + FX_PCOLS]
    @pl.when(pl.program_id(1) == 0)
    def _():
        shift_ref[...] = jnp.zeros_like(shift_ref)
        csum_ref[...] = jnp.zeros_like(csum_ref)

    x = h_ref[...]
    ms = jnp.mean(x * x, axis=-1, keepdims=True)
    xn = (x * lax.rsqrt(ms + NORM_EPS) * n1_ref[...]).astype(BF16)
    _project(xn, w_ref, p_ref)
    _rwkv_features(prw_ref, mu_ref, wcomb_ref, w0a0_ref, gup_ref, kk_ref, ka_ref, rk_ref, ones_ref,
                   r_o, lw_o, k_o, v_o, a_o, b_o, g_o, bonus_o, shift_ref)
    _fox_features(pfx_ref, gain_ref, bf_ref, ones_ref, place_ref, og_o, qkvb_o, csum_ref)


def _bias_placement():
    src = jnp.arange(3 * LANES)
    piece, head = src // LANES, src % LANES
    dst = (head // 2) * LANES + BIAS_LANES * (head % 2) + piece
    hit = (dst[:, None] == jnp.arange(WIDTH)[None, :]) & (head[:, None] < N_HEADS)
    return hit.astype(BF16)


def _proj_prep(h, norm1_g, w_all, mu_p, wcomb, w0a0, gup_p, k_k, k_a, r_k, gain, bf_p, ones512, tm):
    b, l, d = h.shape
    full = lambda shape: pl.BlockSpec(shape, lambda bi, ti: (0,) * len(shape), pipeline_mode=pl.Buffered(1))
    pair = lambda dt: (jax.ShapeDtypeStruct((b, PAIRS, l, LANES), dt),
                       pl.BlockSpec((None, PAIRS, tm, LANES), lambda bi, ti: (bi, 0, ti, 0)))
    rows = lambda n, dt: (jax.ShapeDtypeStruct((b, l, n), dt),
                          pl.BlockSpec((None, tm, n), lambda bi, ti: (bi, ti, 0)))
    outs = [pair(F32), pair(F32), pair(F32), pair(BF16), pair(F32), pair(F32),
            rows(WIDTH, F32), rows(WIDTH, F32), rows(WIDTH, F32), rows(4 * WIDTH, BF16)]
    consts = (norm1_g.reshape(1, d), w_all, mu_p, wcomb, w0a0, gup_p, k_k, k_a, r_k, gain, bf_p,
              ones512, _bias_placement())
    return pl.pallas_call(
        _proj_prep_body,
        out_shape=[o[0] for o in outs],
        grid=(b, l // tm),
        in_specs=[pl.BlockSpec((None, tm, d), lambda bi, ti: (bi, ti, 0))] + [full(c.shape) for c in consts],
        out_specs=[o[1] for o in outs],
        scratch_shapes=[pltpu.VMEM((tm, RW_PCOLS + FX_PCOLS), F32),
                        pltpu.VMEM((1, RW_PCOLS), F32), pltpu.VMEM((1, LANES), F32)],
        compiler_params=_cparams(("arbitrary", "arbitrary")),
        name="proj_prep",
    )(h, *consts)


def _chunk_masks():
    ti = jnp.arange(SB)[:, None]
    si = jnp.arange(SB)[None, :]
    same = (ti // CHUNK) == (si // CHUNK)
    masks = [same, same & (si <= ti), same & (si < ti), ti == si, (ti < HEAD_DIM) == (si < HEAD_DIM)]
    return jnp.stack(masks).astype(F32)


def _rwkv_chunk_body(r_ref, lw_ref, k_ref, v_ref, a_ref, b_ref, masks_ref, y_ref, h_ref):
    npg = r_ref.shape[0]
    n_chunks = SB // CHUNK

    @pl.when(pl.program_id(1) == 0)
    def _():
        h_ref[...] = jnp.zeros_like(h_ref)

    same, incl, strict, _, same_head = (masks_ref[i] != 0.0 for i in range(5))
    eye_f = masks_ref[3]
    head0 = lax.broadcasted_iota(jnp.int32, (SB, LANES), 1) < HEAD_DIM
    tri = masks_ref[1].astype(BF16)
    bf = lambda z: z.astype(BF16)
    dot = functools.partial(jnp.dot, preferred_element_type=F32)

    def keep(mask, x):
        return jnp.where(mask, x, jnp.zeros_like(x))

    pairs = range(npg)
    units = [(p, h) for p in pairs for h in range(2)]
    cat0 = lambda *xs: jnp.concatenate(xs, axis=0)
    mm = lambda a, b, **kw: _mm3(*a, *b, **kw)

    lw = [lw_ref[p] for p in pairs]
    cs = [sum(dot(tri, piece) for piece in _split2(lw[p])) for p in pairs]
    ce = [cat0(*[jnp.broadcast_to(cs[p][(c + 1) * CHUNK - 1:(c + 1) * CHUNK, :], (CHUNK, LANES))
                 for c in range(n_chunks)]) for p in pairs]
    g = [jnp.exp(cs[p]) for p in pairs]
    ginv = [jnp.exp(-cs[p]) for p in pairs]
    gend = [jnp.exp(ce[p] - cs[p]) for p in pairs]
    g_c = [jnp.exp(ce[p]) for p in pairs]
    at = [a_ref[p] * jnp.exp(cs[p] - lw[p]) for p in pairs]
    rt = [r_ref[p] * g[p] for p in pairs]
    v_b = [bf(v_ref[p]) for p in pairs]
    at_b = [bf(at[p]) for p in pairs]
    rt_b = [bf(rt[p]) for p in pairs]
    bk_t = [cat0(bf(b_ref[p] * ginv[p]), bf(k_ref[p] * ginv[p])) for p in pairs]
    bh_b = [bf(b_ref[p] * gend[p]) for p in pairs]
    kh_b = [bf(k_ref[p] * gend[p]) for p in pairs]

    gram = [lax.dot_general(cat0(keep(head0, at_b[p]), keep(head0, rt_b[p]),
                                 keep(~head0, at_b[p]), keep(~head0, rt_b[p])), bk_t[p],
                            (((1,), (1,)), ((), ())), preferred_element_type=F32)
            for p in pairs]
    gb = {(p, h): gram[p][2 * SB * h:2 * SB * (h + 1)] for p, h in units}
    mab = {u: keep(strict, gb[u][:SB, :SB]) for u in units}
    mkk_b = {u: bf(cat0(keep(strict, gb[u][:SB, SB:]), keep(incl, gb[u][SB:, SB:]))) for u in units}
    mrb_b = {u: bf(keep(incl, gb[u][SB:, :SB])) for u in units}

    def fold(z):
        return sum(z[c * CHUNK:(c + 1) * CHUNK] for c in range(n_chunks))

    def unfold(zf):
        return keep(same, jnp.concatenate([zf] * n_chunks, axis=0))

    nf = {u: fold(mab[u]) for u in units}
    tf = {u: fold(eye_f) + nf[u] for u in units}
    full = {u: _split2(mab[u]) for u in units}
    nf = {u: mm(_split2(nf[u]), full[u]) for u in units}
    for k in (2, 4, 8):
        full = {u: _split2(unfold(nf[u])) for u in units}
        if k < 8:
            both_f = {u: mm(_split2(cat0(nf[u], tf[u])), full[u]) for u in units}
            nf = {u: both_f[u][:CHUNK] for u in units}
            tf = {u: tf[u] + both_f[u][CHUNK:] for u in units}
        else:
            tf = {u: tf[u] + mm(_split2(tf[u]), full[u]) for u in units}
    t = {u: unfold(tf[u]) for u in units}
    wk = {u: dot(mkk_b[u], v_b[u[0]]) for u in units}
    x = {u: dot(bf(t[u]), jnp.concatenate([at_b[u[0]], bf(wk[u][:SB])], axis=1)) for u in units}
    ry = {u: dot(mrb_b[u], bf(x[u])) for u in units}
    both = lambda f: [jnp.where(head0, f((p, 0)), f((p, 1))) for p in pairs]
    abar_b = [bf(z) for z in both(lambda u: x[u][:, :LANES])]
    vbar = both(lambda u: x[u][:, LANES:])
    rbar_b = [bf(z) for z in both(lambda u: rt[u[0]] + ry[u][:, :LANES])]
    ybar = both(lambda u: wk[u][SB:] + ry[u][:, LANES:])

    g_ct = [g_c[p].T for p in pairs]

    hbd = [h_ref[p] for p in pairs]
    for c in range(n_chunks):
        sl = slice(c * CHUNK, (c + 1) * CHUNK)
        yu = [dot(cat0(rbar_b[p][sl], abar_b[p][sl]), bf(hbd[p])) for p in pairs]
        for p in pairs:
            y_ref[p, sl, :] = yu[p][:CHUNK] + ybar[p][sl]
        upd = [lax.dot_general(cat0(bh_b[p][sl], kh_b[p][sl]),
                               cat0(bf(yu[p][CHUNK:] + vbar[p][sl]), v_b[p][sl]),
                               (((0,), (0,)), ((), ())), preferred_element_type=F32)
               for p in pairs]
        hbd = [jnp.broadcast_to(g_ct[p][:, c * CHUNK:c * CHUNK + 1], (LANES, LANES)) * hbd[p]
               + keep(same_head, upd[p]) for p in pairs]
    for p in pairs:
        h_ref[p] = hbd[p]


def _rwkv_chunk(r, lw, k, v, a, b, npg):
    n_pairs, l, _ = r.shape
    blk = pl.BlockSpec((npg, SB, LANES), lambda gi, ti: (gi, ti, 0))
    return pl.pallas_call(
        _rwkv_chunk_body,
        out_shape=jax.ShapeDtypeStruct((n_pairs, l, LANES), F32),
        grid=(n_pairs // npg, l // SB),
        in_specs=[blk] * 6 + [pl.BlockSpec((5, SB, SB), lambda gi, ti: (0, 0, 0))],
        out_specs=blk,
        scratch_shapes=[pltpu.VMEM((npg, LANES, LANES), F32)],
        compiler_params=_cparams(("arbitrary", "arbitrary")),
        name="rwkv_chunk",
    )(r, lw, k, v, a, b, _chunk_masks())


def _fox_attn_body(q_ref, k_ref, v_ref, kb_ref, *rest, q_row0, tk):
    o_ref, m_ref, acc_ref = rest[-3:]
    tq = q_ref.shape[0]
    row_first = q_row0 + pl.program_id(1) * tq

    m_ref[...] = jnp.full_like(m_ref, MASK_VALUE)
    acc_ref[...] = jnp.zeros_like(acc_ref)

    lane = lax.broadcasted_iota(jnp.int32, (1, LANES), 1)
    head0 = lane < HEAD_DIM
    row2 = lax.broadcasted_iota(jnp.int32, (2 * tq, LANES), 0)
    lane2 = lax.broadcasted_iota(jnp.int32, (2 * tq, LANES), 1)
    first_lane = jnp.where(row2 < tq, 0, BIAS_LANES)
    bias_on = (lane2 >= first_lane) & (lane2 < first_lane + BIAS_LANES)
    bias_q = jnp.where(bias_on, 1.0, 0.0).astype(BF16)

    def block(col_first, width, masked):
        rows = pl.ds(col_first, width)
        heads = range(N_HEADS)
        lanes = [slice(j * LANES, (j + 1) * LANES) for j in range(PAIRS)]
        s2 = []
        for ls in lanes:
            q = q_ref[:, ls]
            zero = jnp.zeros_like(q)
            q2 = jnp.concatenate([jnp.where(head0, q, zero), jnp.where(head0, zero, q)], axis=0)
            s2.append(lax.dot_general(jnp.concatenate([q2, bias_q], axis=1),
                                      jnp.concatenate([k_ref[rows, ls], kb_ref[rows, ls]], axis=1),
                                      (((1,), (1,)), ((), ())), preferred_element_type=F32))
        s = [s2[h // 2][(h % 2) * tq:(h % 2 + 1) * tq] for h in heads]
        if masked:
            causal = (lax.broadcasted_iota(jnp.int32, (tq, width), 1)
                      <= lax.broadcasted_iota(jnp.int32, (tq, width), 0))
            s = [jnp.where(causal, sh, MASK_VALUE) for sh in s]
        m_prev = [m_ref[h] for h in heads]
        def row_max(sh):
            tile_max = functools.reduce(jnp.maximum, [sh[:, c:c + LANES] for c in range(0, width, LANES)])
            return jnp.max(tile_max, axis=1, keepdims=True)
        m_new = [jnp.maximum(m_prev[h], row_max(s[h])) for h in heads]
        p = [jnp.exp2(s[h] - jnp.concatenate([m_new[h]] * (width // LANES), axis=1)).astype(BF16)
             for h in heads]
        v1 = []
        for ls in lanes:
            v = v_ref[rows, ls]
            one = jnp.ones_like(v)
            v1 += [jnp.where(head0, v, one), jnp.where(head0, one, v)]
        pv = [jnp.dot(p[h], v1[h], preferred_element_type=F32) for h in heads]
        for h in heads:
            acc_ref[h] = jnp.exp2(m_prev[h] - m_new[h]) * acc_ref[h] + pv[h]
            m_ref[h] = m_new[h]

    def full_block(kj, carry):
        block(pl.multiple_of(kj * tk, tk), tk, False)
        return carry

    n_full = row_first // tk
    lax.fori_loop(0, n_full, full_block, 0)
    if tk > tq >= LANES:
        def rest_block(i, carry):
            block(pl.multiple_of(n_full * tk + i * tq, tq), tq, False)
            return carry
        lax.fori_loop(0, (row_first - n_full * tk) // tq, rest_block, 0)
    diag_w = max(tq, LANES)
    block(pl.multiple_of(row_first, tq), diag_w, True)

    if o_ref.shape[0] > tq:
        o_ref[...] = jnp.zeros_like(o_ref)
    for j in range(PAIRS):
        a0, a1 = acc_ref[2 * j], acc_ref[2 * j + 1]
        o_ref[0:tq, j * LANES:(j + 1) * LANES] = jnp.where(
            head0, a0 / pltpu.roll(a0, HEAD_DIM, axis=1), a1 / pltpu.roll(a1, HEAD_DIM, axis=1))


def _fox_attn(qkvb, q_row0, n_rows, tq, tk, out_rows=None, into=None):
    b, l, _ = qkvb.shape
    out_rows = out_rows or tq
    assert q_row0 % tq == 0 and q_row0 % out_rows == 0
    assert tk % tq == 0 if tq >= LANES else (q_row0 % tk == 0 and n_rows == tq)
    assert q_row0 + n_rows - tq + max(tq, LANES) <= l
    qb0, ob0 = q_row0 // tq, q_row0 // out_rows
    kv_spec = lambda col: pl.BlockSpec((None, l, WIDTH), lambda bi, qi: (bi, 0, col))
    extra_in, extra_spec, alias = (), [], {}
    if into is not None:
        extra_in, extra_spec, alias = (into,), [pl.BlockSpec(memory_space=pl.ANY)], {4: 0}
    return pl.pallas_call(
        functools.partial(_fox_attn_body, q_row0=q_row0, tk=tk),
        out_shape=jax.ShapeDtypeStruct((b, l, WIDTH), F32),
        grid=(b, n_rows // tq),
        in_specs=[pl.BlockSpec((None, tq, WIDTH), lambda bi, qi: (bi, qb0 + qi, 0)),
                  kv_spec(1), kv_spec(2), kv_spec(3)] + extra_spec,
        out_specs=pl.BlockSpec((None, out_rows, WIDTH), lambda bi, qi: (bi, ob0 + qi, 0)),
        scratch_shapes=[pltpu.VMEM((N_HEADS, tq, LANES), F32), pltpu.VMEM((N_HEADS, tq, LANES), F32)],
        input_output_aliases=alias,
        compiler_params=_cparams(("arbitrary", "arbitrary")),
        name="fox_attn",
    )(qkvb, qkvb, qkvb, qkvb, *extra_in)


def _mixed_residual(h_ref, y_ref, bonus_ref, g_ref, o_ref, og_ref, gnw_ref, gnb_ref, ones_ref, wo_ref):
    ones = ones_ref[...]
    y = jnp.concatenate([y_ref[j] for j in range(PAIRS)], axis=1)
    mean = _dot_ones(y, ones) * (1.0 / HEAD_DIM)
    d = y - mean
    var = _dot_ones(d * d, ones) * (1.0 / HEAD_DIM)
    yn = d * lax.rsqrt(var + GN_EPS) * gnw_ref[...] + gnb_ref[...]
    y_rw = ((yn + bonus_ref[...]) * g_ref[...]).astype(BF16)
    y_fx = (o_ref[...] * _sigmoid(og_ref[...])).astype(BF16)
    return (h_ref[...]
            + jnp.dot(y_rw, wo_ref[0:WIDTH, :], preferred_element_type=F32)
            + jnp.dot(y_fx, wo_ref[WIDTH:2 * WIDTH, :], preferred_element_type=F32))


def _mix_ffn_body(h_ref, y_ref, bonus_ref, gate_ref, o_ref, og_ref, gnw_ref, gnb_ref, ones_ref, wo_ref,
                  g_ref, wu_ref, wg_ref, cwu_ref, cwg_ref, cbu_ref, cbg_ref, wout_ref,
                  out_ref, xn_ref, cu_ref, cg_ref, wu_s, wg_s, act_ref):
    tm = h_ref.shape[0]
    halo = wu_s.shape[0] - tm
    ti = pl.program_id(1)
    j = pl.program_id(2)

    def first_column_tile():
        x = _mixed_residual(h_ref, y_ref, bonus_ref, gate_ref, o_ref, og_ref, gnw_ref, gnb_ref,
                            ones_ref, wo_ref)
        ms = jnp.mean(x * x, axis=-1, keepdims=True)
        xn_ref[...] = (x * lax.rsqrt(ms + NORM_EPS) * g_ref[...]).astype(BF16)
        out_ref[...] = x

    if cu_ref.shape[0] == 1:
        first_column_tile()
    else:
        pl.when(j == 0)(first_column_tile)

    @pl.when(ti == 0)
    def _():
        cu_ref[j] = jnp.zeros(cu_ref.shape[1:], F32)
        cg_ref[j] = jnp.zeros(cg_ref.shape[1:], F32)

    xn = xn_ref[...]

    def project(w_ref, carry_ref, work):
        work[0:halo] = carry_ref[j]
        work[halo:] = jnp.dot(xn, w_ref[...], preferred_element_type=F32)
        carry_ref[j] = work[tm:]

    def conv(cs, cw_ref, cb_ref, work):
        cw = cw_ref[:, cs]
        taps = [work[pl.ds(halo - (CONV_W - 1) + i, tm), cs] for i in range(CONV_W)]
        return cb_ref[:, cs] + cw[0:1, :] * taps[0] + cw[1:2, :] * taps[1] + cw[2:3, :] * taps[2]

    project(wu_ref, cu_ref, wu_s)
    project(wg_ref, cg_ref, wg_s)
    tf = wu_ref.shape[1]
    for c0 in range(0, tf, 2 * MXU_TILE):
        cs = slice(c0, min(c0 + 2 * MXU_TILE, tf))
        gt = conv(cs, cwg_ref, cbg_ref, wg_s)
        act_ref[:, cs] = (gt * _sigmoid(gt) * conv(cs, cwu_ref, cbu_ref, wu_s)).astype(BF16)
    out_ref[...] += jnp.dot(act_ref[...], wout_ref[...], preferred_element_type=F32)


def _mix_ffn(h, y, bonus, gate, o_fx, og, gn_w, gn_b, ones512, wo_bf16,
             g, w_in_bf16, conv_w, conv_b, w_out_bf16, tm, tf):
    b, l, d = h.shape
    d_ff = w_out_bf16.shape[0]
    nj = d_ff // tf
    cw = jnp.pad(conv_w, ((0, SUBLANES - CONV_W), (0, 0)))
    cb = conv_b.reshape(1, 2 * d_ff)
    once = dict(pipeline_mode=pl.Buffered(1))
    full = lambda shape: pl.BlockSpec(shape, lambda bi, ti, j: (0,) * len(shape), **once)
    wmode = once if nj == 1 else {}
    rows = pl.BlockSpec((None, tm, WIDTH), lambda bi, ti, j: (bi, ti, 0))
    return pl.pallas_call(
        _mix_ffn_body,
        out_shape=jax.ShapeDtypeStruct((b, l, d), F32),
        grid=(b, l // tm, nj),
        in_specs=[pl.BlockSpec((None, tm, d), lambda bi, ti, j: (bi, ti, 0)),
                  pl.BlockSpec((None, PAIRS, tm, LANES), lambda bi, ti, j: (bi, 0, ti, 0)),
                  rows, rows, rows, rows,
                  full((1, WIDTH)), full((1, WIDTH)), full(ones512.shape), full((2 * WIDTH, d)),
                  full((1, d)),
                  pl.BlockSpec((d, tf), lambda bi, ti, j: (0, j), **wmode),
                  pl.BlockSpec((d, tf), lambda bi, ti, j: (0, nj + j), **wmode),
                  pl.BlockSpec((SUBLANES, tf), lambda bi, ti, j: (0, j), **wmode),
                  pl.BlockSpec((SUBLANES, tf), lambda bi, ti, j: (0, nj + j), **wmode),
                  pl.BlockSpec((1, tf), lambda bi, ti, j: (0, j), **wmode),
                  pl.BlockSpec((1, tf), lambda bi, ti, j: (0, nj + j), **wmode),
                  pl.BlockSpec((tf, d), lambda bi, ti, j: (j, 0), **wmode)],
        out_specs=pl.BlockSpec((None, tm, d), lambda bi, ti, j: (bi, ti, 0)),
        scratch_shapes=[pltpu.VMEM((tm, d), BF16),
                        pltpu.VMEM((nj, SUBLANES, tf), F32), pltpu.VMEM((nj, SUBLANES, tf), F32),
                        pltpu.VMEM((SUBLANES + tm, tf), F32), pltpu.VMEM((SUBLANES + tm, tf), F32),
                        pltpu.VMEM((tm, tf), BF16)],
        compiler_params=_cparams(("arbitrary", "arbitrary", "arbitrary")),
        name="mix_ffn",
    )(h, y, bonus, gate, o_fx, og, gn_w, gn_b, ones512, wo_bf16,
      g.reshape(1, d), w_in_bf16, w_in_bf16, cw, cw, cb, cb, w_out_bf16)


def _pad_cols(w, n):
    return jnp.pad(w, ((0, 0), (0, n - w.shape[1])))


def _layer(h, prm, l_real):
    (norm1_g, w_in, rw_mu, rw_w0, rw_w_up, rw_a0, rw_a_up, rw_g_up, rw_k_k, rw_k_a, rw_r_k,
     rw_gn_w, rw_gn_b, fx_b_f, fx_q_g, fx_k_g, w_o, norm2_g, ffn_w_in, ffn_conv_w, ffn_conv_b,
     ffn_w_out) = prm
    b, l, d = h.shape
    rw_cols = 3 * WIDTH + DECAY_LORA + AAA_LORA + GATE_LORA

    w_all = jnp.concatenate([_pad_cols(w_in[:, :rw_cols], RW_PCOLS),
                             _pad_cols(w_in[:, rw_cols:], FX_PCOLS)], axis=1)
    mu_p = _pad_cols(rw_mu.reshape(1, rw_cols), RW_PCOLS)
    wcomb = jnp.zeros((LORA_WA, 2 * WIDTH), F32)
    wcomb = wcomb.at[:DECAY_LORA, :WIDTH].set(rw_w_up).at[DECAY_LORA:, WIDTH:].set(rw_a_up)
    w0a0 = jnp.concatenate([rw_w0, rw_a0]).reshape(1, 2 * WIDTH)
    gup_p = jnp.pad(rw_g_up, ((0, GATE_PAD - GATE_LORA), (0, 0)))
    ones512 = _group_ones(MXU_TILE, HEAD_DIM)
    gain = jnp.concatenate([jnp.tile(fx_q_g, N_HEADS) * (HEAD_DIM ** -0.5 * LOG2E),
                            jnp.tile(fx_k_g, N_HEADS)]).reshape(1, 2 * WIDTH)
    bf_p = jnp.pad(fx_b_f, (0, LANES - N_HEADS)).reshape(1, LANES)

    t1 = _pick(l, (544, 384, 128))
    t2 = _pick(l, (544, 384, 256, 128))

    r, lw, k, v, a, bb, g, bonus, og, qkvb = _proj_prep(
        h, norm1_g, w_all, mu_p, wcomb, w0a0, gup_p, rw_k_k.reshape(1, WIDTH),
        rw_k_a.reshape(1, WIDTH), rw_r_k.reshape(1, WIDTH), gain, bf_p, ones512, t1)

    pm = lambda z: z.reshape(b * PAIRS, l, LANES)
    y = _rwkv_chunk(pm(r), pm(lw), pm(k), pm(v), pm(a), pm(bb), npg=8)
    y = y.reshape(b, PAIRS, l, LANES)

    n_main = l // ATTN_BLOCK * ATTN_BLOCK
    o_fx = _fox_attn(qkvb, 0, n_main, ATTN_BLOCK, ATTN_KEYS) if n_main else None
    if l > n_main:
        tq_tail = min(-(-max(l_real - n_main, 1) // BF16_ROWS) * BF16_ROWS, l - n_main)
        tk_tail = ATTN_KEYS if n_main % ATTN_KEYS == 0 else ATTN_BLOCK
        o_fx = _fox_attn(qkvb, n_main, tq_tail, tq_tail, tk_tail, out_rows=l - n_main, into=o_fx)

    d_ff = ffn_w_out.shape[0]
    t3 = _pick(l, (544, 384, 128))
    return _mix_ffn(h, y, bonus, g, o_fx, og, rw_gn_w.reshape(1, WIDTH), rw_gn_b.reshape(1, WIDTH),
                    ones512, w_o, norm2_g, ffn_w_in, ffn_conv_w, ffn_conv_b, ffn_w_out, t3, d_ff)


def kernel(x, meta, norm1_g, w_in, rw_mu, rw_w0, rw_w_up, rw_a0, rw_a_up, rw_g_up, rw_k_k, rw_k_a,
           rw_r_k, rw_gn_w, rw_gn_b, fx_b_f, fx_q_g, fx_k_g, w_o, norm2_g, ffn_w_in, ffn_conv_w,
           ffn_conv_b, ffn_w_out):
    b, seq, d = x.shape
    w_in, w_o, ffn_w_in, ffn_w_out = (w.astype(BF16) for w in (w_in, w_o, ffn_w_in, ffn_w_out))
    params = (norm1_g, w_in, rw_mu, rw_w0, rw_w_up, rw_a0, rw_a_up, rw_g_up, rw_k_k, rw_k_a, rw_r_k,
              rw_gn_w, rw_gn_b, fx_b_f, fx_q_g, fx_k_g, w_o, norm2_g, ffn_w_in, ffn_conv_w,
              ffn_conv_b, ffn_w_out)
    l = N_META + seq
    lp = -(-l // SB) * SB
    meta_b = jnp.broadcast_to(meta[None].astype(x.dtype), (b, N_META, d))
    h = jnp.concatenate([meta_b, x, jnp.zeros((b, lp - l, d), x.dtype)], axis=1)
    for layer in range(norm1_g.shape[0]):
        h = _layer(h, tuple(p[layer] for p in params), l)
    return h[:, N_META:l]
```

```python
import functools

import jax
import jax.numpy as jnp
from jax import lax
from jax.experimental import pallas as pl
from jax.experimental.pallas import tpu as pltpu

F32 = jnp.float32
BF16 = jnp.bfloat16

LANES = 128
SUBLANES = 8
BF16_ROWS = 16
MXU_TILE = 256

N_META = 16
HEAD_DIM = 64
N_HEADS = 8
WIDTH = N_HEADS * HEAD_DIM
DECAY_LORA = 64
AAA_LORA = 64
LORA_WA = DECAY_LORA + AAA_LORA
GATE_LORA = 160
GATE_PAD = 256
RW_PCOLS = 3 * WIDTH + LORA_WA + GATE_PAD
FX_PCOLS = 4 * WIDTH + LANES
CONV_W = 3
NORM_EPS = 1e-6
GN_EPS = HEAD_DIM * 1e-5
PAIRS = WIDTH // LANES
CHUNK = 16
SB = 128
ATTN_BLOCK = 256
ATTN_KEYS = 512
BIAS_LANES = 3
LOG2E = 1.4426950408889634
MASK_VALUE = -1e30
VMEM_LIMIT = 58 * 1024 * 1024


def _pick(n, cands):
    for c in cands:
        if n % c == 0:
            return c
    raise ValueError(f"no tile for {n} in {cands}")


def _cparams(sem):
    return pltpu.CompilerParams(dimension_semantics=sem, vmem_limit_bytes=VMEM_LIMIT)


def _sigmoid(x):
    return 1.0 / (1.0 + jnp.exp(-x))


def _softplus(x):
    return jnp.maximum(x, 0.0) + jnp.log(1.0 + jnp.exp(-jnp.abs(x)))


def _split2(x):
    hi = x.astype(BF16)
    lo = (x - hi.astype(F32)).astype(BF16)
    return hi, lo


def _split3(x):
    p1 = x.astype(BF16)
    rem = x - p1.astype(F32)
    p2 = rem.astype(BF16)
    p3 = (rem - p2.astype(F32)).astype(BF16)
    return p1, p2, p3


def _dot_ones(x, ones_bf16):
    xb = x.astype(BF16)
    t = ones_bf16.shape[0]
    return jnp.concatenate([jnp.dot(xb[:, c:c + t], ones_bf16, preferred_element_type=F32)
                            for c in range(0, x.shape[1], t)], axis=1)


def _mm3(a_hi, a_lo, b_hi, b_lo, dims=((1,), (0,))):
    dn = (dims, ((), ()))
    m_axis = 1 - dims[0][0]
    m = a_hi.shape[m_axis]
    d = lax.dot_general(jnp.concatenate([a_hi, a_lo], axis=m_axis), b_hi, dn,
                        preferred_element_type=F32)
    return d[:m] + d[m:] + lax.dot_general(a_hi, b_lo, dn, preferred_element_type=F32)


def _dot_bf16(a, b):
    return jnp.dot(a.astype(BF16), b.astype(BF16), preferred_element_type=F32)


def _group_ones(n, group):
    i = jnp.arange(n) // group
    return (i[:, None] == i[None, :]).astype(BF16)


def _project(xn, w_ref, p_ref, n_chunk=512):
    n = p_ref.shape[-1]
    for c in range(0, n, n_chunk):
        e = min(c + n_chunk, n)
        p_ref[:, c:e] = jnp.dot(xn, w_ref[:, c:e], preferred_element_type=F32)


def _rwkv_features(p_ref, mu_ref, wcomb_ref, w0a0_ref, gup_ref, kk_ref, ka_ref, rk_ref, ones_ref,
                   r_o, lw_o, k_o, v_o, a_o, b_o, g_o, bonus_o, carry_ref):
    t2 = p_ref.shape[0]
    p = p_ref[...]
    rows = lax.broadcasted_iota(jnp.int32, p.shape, 0)
    prev = jnp.where(rows == 0, carry_ref[...], pltpu.roll(p, 1, axis=0))
    carry_ref[...] = p[t2 - 1:t2, :]
    x = p + mu_ref[...] * (prev - p)

    r = x[:, 0:WIDTH]
    k = x[:, WIDTH:2 * WIDTH]
    v = x[:, 2 * WIDTH:3 * WIDTH]
    wa = x[:, 3 * WIDTH:3 * WIDTH + LORA_WA]
    gd = x[:, 3 * WIDTH + LORA_WA:]

    lane = lax.broadcasted_iota(jnp.int32, wa.shape, 1)
    xa = jnp.where(lane < DECAY_LORA, jnp.tanh(wa), wa)
    pre = _dot_bf16(xa, wcomb_ref[...]) + w0a0_ref[...]
    w_log = -_softplus(-pre[:, :WIDTH]) - 0.5
    log_decay = -jnp.exp(w_log)
    a = _sigmoid(pre[:, WIDTH:])
    g = _dot_bf16(_sigmoid(gd), gup_ref[...])

    ones = ones_ref[...]
    kk = k * kk_ref[...]
    ss = _dot_ones(kk * kk, ones)
    kk = kk / jnp.maximum(jnp.sqrt(ss), 1e-12)
    k2 = k * (1.0 + (a - 1.0) * ka_ref[...])
    bonus = _dot_ones(r * k2 * rk_ref[...], ones) * v

    for j in range(PAIRS):
        ls = slice(j * LANES, (j + 1) * LANES)
        r_o[j] = r[:, ls]
        lw_o[j] = log_decay[:, ls]
        k_o[j] = k2[:, ls]
        v_o[j] = v[:, ls].astype(v_o.dtype)
        a_o[j] = -kk[:, ls]
        b_o[j] = (kk * a)[:, ls]
    g_o[...] = g
    bonus_o[...] = bonus


def _fox_features(p_ref, gain_ref, bf_ref, ones_ref, place_ref, og_o, qkvb_o, carry_ref):
    t2 = p_ref.shape[0]
    ones = ones_ref[...]
    for h0 in (0, WIDTH):
        x = p_ref[:, h0:h0 + WIDTH]
        ms = _dot_ones(x * x, ones) * (1.0 / HEAD_DIM)
        qkvb_o[:, h0:h0 + WIDTH] = (x * lax.rsqrt(ms + NORM_EPS) * gain_ref[:, h0:h0 + WIDTH]).astype(BF16)
    qkvb_o[:, 2 * WIDTH:3 * WIDTH] = p_ref[:, 2 * WIDTH:3 * WIDTH].astype(BF16)
    og_o[...] = p_ref[:, 3 * WIDTH:4 * WIDTH]

    logf = -_softplus(-(p_ref[:, 4 * WIDTH:] + bf_ref[...]))
    ri = lax.broadcasted_iota(jnp.int32, (LANES, LANES), 0)
    ci = lax.broadcasted_iota(jnp.int32, (LANES, LANES), 1)
    tri = (ci <= ri).astype(BF16)
    carry = carry_ref[...]
    blocks = []
    for r0 in range(0, t2, LANES):
        n = min(LANES, t2 - r0)
        cb = sum(jnp.dot(tri[:n, :n], piece, preferred_element_type=F32)
                 for piece in _split3(logf[r0:r0 + n])) + carry
        carry = cb[n - 1:n, :]
        blocks.append(cb)
    c = jnp.concatenate(blocks, axis=0)
    carry_ref[...] = carry
    pieces = jnp.concatenate(_split3(c * (-LOG2E)), axis=1)
    qkvb_o[:, 3 * WIDTH:] = jnp.dot(pieces, place_ref[...], preferred_element_type=F32).astype(BF16)


def _proj_prep_body(h_ref, n1_ref, w_ref, mu_ref, wcomb_ref, w0a0_ref, gup_ref, kk_ref,
                    ka_ref, rk_ref, gain_ref, bf_ref, ones_ref, place_ref,
                    r_o, lw_o, k_o, v_o, a_o, b_o, g_o, bonus_o, og_o, qkvb_o,
                    p_ref, shift_ref, csum_ref):
    prw_ref = p_ref.at[:, 0:RW_PCOLS]
    pfx_ref = p_ref.at[:, RW_PCOLS:RW_PCOLS + FX_PCOLS]
    @pl.when(pl.program_id(1) == 0)
    def _():
        shift_ref[...] = jnp.zeros_like(shift_ref)
        csum_ref[...] = jnp.zeros_like(csum_ref)

    x = h_ref[...]
    ms = jnp.mean(x * x, axis=-1, keepdims=True)
    xn = (x * lax.rsqrt(ms + NORM_EPS) * n1_ref[...]).astype(BF16)
    _project(xn, w_ref, p_ref)
    _rwkv_features(prw_ref, mu_ref, wcomb_ref, w0a0_ref, gup_ref, kk_ref, ka_ref, rk_ref, ones_ref,
                   r_o, lw_o, k_o, v_o, a_o, b_o, g_o, bonus_o, shift_ref)
    _fox_features(pfx_ref, gain_ref, bf_ref, ones_ref, place_ref, og_o, qkvb_o, csum_ref)


def _bias_placement():
    src = jnp.arange(3 * LANES)
    piece, head = src // LANES, src % LANES
    dst = (head // 2) * LANES + BIAS_LANES * (head % 2) + piece
    hit = (dst[:, None] == jnp.arange(WIDTH)[None, :]) & (head[:, None] < N_HEADS)
    return hit.astype(BF16)


def _proj_prep(h, norm1_g, w_all, mu_p, wcomb, w0a0, gup_p, k_k, k_a, r_k, gain, bf_p, ones512, tm):
    b, l, d = h.shape
    full = lambda shape: pl.BlockSpec(shape, lambda bi, ti: (0,) * len(shape), pipeline_mode=pl.Buffered(1))
    pair = lambda dt: (jax.ShapeDtypeStruct((b, PAIRS, l, LANES), dt),
                       pl.BlockSpec((None, PAIRS, tm, LANES), lambda bi, ti: (bi, 0, ti, 0)))
    rows = lambda n, dt: (jax.ShapeDtypeStruct((b, l, n), dt),
                          pl.BlockSpec((None, tm, n), lambda bi, ti: (bi, ti, 0)))
    outs = [pair(F32), pair(F32), pair(F32), pair(BF16), pair(F32), pair(F32),
            rows(WIDTH, F32), rows(WIDTH, F32), rows(WIDTH, F32), rows(4 * WIDTH, BF16)]
    consts = (norm1_g.reshape(1, d), w_all, mu_p, wcomb, w0a0, gup_p, k_k, k_a, r_k, gain, bf_p,
              ones512, _bias_placement())
    return pl.pallas_call(
        _proj_prep_body,
        out_shape=[o[0] for o in outs],
        grid=(b, l // tm),
        in_specs=[pl.BlockSpec((None, tm, d), lambda bi, ti: (bi, ti, 0))] + [full(c.shape) for c in consts],
        out_specs=[o[1] for o in outs],
        scratch_shapes=[pltpu.VMEM((tm, RW_PCOLS + FX_PCOLS), F32),
                        pltpu.VMEM((1, RW_PCOLS), F32), pltpu.VMEM((1, LANES), F32)],
        compiler_params=_cparams(("arbitrary", "arbitrary")),
        name="proj_prep",
    )(h, *consts)


def _chunk_masks():
    ti = jnp.arange(SB)[:, None]
    si = jnp.arange(SB)[None, :]
    same = (ti // CHUNK) == (si // CHUNK)
    masks = [same, same & (si <= ti), same & (si < ti), ti == si, (ti < HEAD_DIM) == (si < HEAD_DIM)]
    return jnp.stack(masks).astype(F32)


def _rwkv_chunk_body(r_ref, lw_ref, k_ref, v_ref, a_ref, b_ref, masks_ref, y_ref, h_ref):
    npg = r_ref.shape[0]
    n_chunks = SB // CHUNK

    @pl.when(pl.program_id(1) == 0)
    def _():
        h_ref[...] = jnp.zeros_like(h_ref)

    same, incl, strict, _, same_head = (masks_ref[i] != 0.0 for i in range(5))
    eye_f = masks_ref[3]
    head0 = lax.broadcasted_iota(jnp.int32, (SB, LANES), 1) < HEAD_DIM
    tri = masks_ref[1].astype(BF16)
    bf = lambda z: z.astype(BF16)
    dot = functools.partial(jnp.dot, preferred_element_type=F32)

    def keep(mask, x):
        return jnp.where(mask, x, jnp.zeros_like(x))

    pairs = range(npg)
    units = [(p, h) for p in pairs for h in range(2)]
    cat0 = lambda *xs: jnp.concatenate(xs, axis=0)
    mm = lambda a, b, **kw: _mm3(*a, *b, **kw)

    lw = [lw_ref[p] for p in pairs]
    cs = [sum(dot(tri, piece) for piece in _split2(lw[p])) for p in pairs]
    ce = [cat0(*[jnp.broadcast_to(cs[p][(c + 1) * CHUNK - 1:(c + 1) * CHUNK, :], (CHUNK, LANES))
                 for c in range(n_chunks)]) for p in pairs]
    g = [jnp.exp(cs[p]) for p in pairs]
    ginv = [jnp.exp(-cs[p]) for p in pairs]
    gend = [jnp.exp(ce[p] - cs[p]) for p in pairs]
    g_c = [jnp.exp(ce[p]) for p in pairs]
    at = [a_ref[p] * jnp.exp(cs[p] - lw[p]) for p in pairs]
    rt = [r_ref[p] * g[p] for p in pairs]
    v_b = [bf(v_ref[p]) for p in pairs]
    at_b = [bf(at[p]) for p in pairs]
    rt_b = [bf(rt[p]) for p in pairs]
    bk_t = [cat0(bf(b_ref[p] * ginv[p]), bf(k_ref[p] * ginv[p])) for p in pairs]
    bh_b = [bf(b_ref[p] * gend[p]) for p in pairs]
    kh_b = [bf(k_ref[p] * gend[p]) for p in pairs]

    gram = [lax.dot_general(cat0(keep(head0, at_b[p]), keep(head0, rt_b[p]),
                                 keep(~head0, at_b[p]), keep(~head0, rt_b[p])), bk_t[p],
                            (((1,), (1,)), ((), ())), preferred_element_type=F32)
            for p in pairs]
    gb = {(p, h): gram[p][2 * SB * h:2 * SB * (h + 1)] for p, h in units}
    mab = {u: keep(strict, gb[u][:SB, :SB]) for u in units}
    mkk_b = {u: bf(cat0(keep(strict, gb[u][:SB, SB:]), keep(incl, gb[u][SB:, SB:]))) for u in units}
    mrb_b = {u: bf(keep(incl, gb[u][SB:, :SB])) for u in units}

    def fold(z):
        return sum(z[c * CHUNK:(c + 1) * CHUNK] for c in range(n_chunks))

    def unfold(zf):
        return keep(same, jnp.concatenate([zf] * n_chunks, axis=0))

    nf = {u: fold(mab[u]) for u in units}
    tf = {u: fold(eye_f) + nf[u] for u in units}
    full = {u: _split2(mab[u]) for u in units}
    nf = {u: mm(_split2(nf[u]), full[u]) for u in units}
    for k in (2, 4, 8):
        full = {u: _split2(unfold(nf[u])) for u in units}
        if k < 8:
            both_f = {u: mm(_split2(cat0(nf[u], tf[u])), full[u]) for u in units}
            nf = {u: both_f[u][:CHUNK] for u in units}
            tf = {u: tf[u] + both_f[u][CHUNK:] for u in units}
        else:
            tf = {u: tf[u] + mm(_split2(tf[u]), full[u]) for u in units}
    t = {u: unfold(tf[u]) for u in units}
    wk = {u: dot(mkk_b[u], v_b[u[0]]) for u in units}
    x = {u: dot(bf(t[u]), jnp.concatenate([at_b[u[0]], bf(wk[u][:SB])], axis=1)) for u in units}
    ry = {u: dot(mrb_b[u], bf(x[u])) for u in units}
    both = lambda f: [jnp.where(head0, f((p, 0)), f((p, 1))) for p in pairs]
    abar_b = [bf(z) for z in both(lambda u: x[u][:, :LANES])]
    vbar = both(lambda u: x[u][:, LANES:])
    rbar_b = [bf(z) for z in both(lambda u: rt[u[0]] + ry[u][:, :LANES])]
    ybar = both(lambda u: wk[u][SB:] + ry[u][:, LANES:])

    g_ct = [g_c[p].T for p in pairs]

    hbd = [h_ref[p] for p in pairs]
    for c in range(n_chunks):
        sl = slice(c * CHUNK, (c + 1) * CHUNK)
        yu = [dot(cat0(rbar_b[p][sl], abar_b[p][sl]), bf(hbd[p])) for p in pairs]
        for p in pairs:
            y_ref[p, sl, :] = yu[p][:CHUNK] + ybar[p][sl]
        upd = [lax.dot_general(cat0(bh_b[p][sl], kh_b[p][sl]),
                               cat0(bf(yu[p][CHUNK:] + vbar[p][sl]), v_b[p][sl]),
                               (((0,), (0,)), ((), ())), preferred_element_type=F32)
               for p in pairs]
        hbd = [jnp.broadcast_to(g_ct[p][:, c * CHUNK:c * CHUNK + 1], (LANES, LANES)) * hbd[p]
               + keep(same_head, upd[p]) for p in pairs]
    for p in pairs:
        h_ref[p] = hbd[p]


def _rwkv_chunk(r, lw, k, v, a, b, npg):
    n_pairs, l, _ = r.shape
    blk = pl.BlockSpec((npg, SB, LANES), lambda gi, ti: (gi, ti, 0))
    return pl.pallas_call(
        _rwkv_chunk_body,
        out_shape=jax.ShapeDtypeStruct((n_pairs, l, LANES), F32),
        grid=(n_pairs // npg, l // SB),
        in_specs=[blk] * 6 + [pl.BlockSpec((5, SB, SB), lambda gi, ti: (0, 0, 0))],
        out_specs=blk,
        scratch_shapes=[pltpu.VMEM((npg, LANES, LANES), F32)],
        compiler_params=_cparams(("arbitrary", "arbitrary")),
        name="rwkv_chunk",
    )(r, lw, k, v, a, b, _chunk_masks())


def _fox_attn_body(q_ref, k_ref, v_ref, kb_ref, *rest, q_row0, tk):
    o_ref, m_ref, acc_ref = rest[-3:]
    tq = q_ref.shape[0]
    row_first = q_row0 + pl.program_id(1) * tq

    m_ref[...] = jnp.full_like(m_ref, MASK_VALUE)
    acc_ref[...] = jnp.zeros_like(acc_ref)

    lane = lax.broadcasted_iota(jnp.int32, (1, LANES), 1)
    head0 = lane < HEAD_DIM
    row2 = lax.broadcasted_iota(jnp.int32, (2 * tq, LANES), 0)
    lane2 = lax.broadcasted_iota(jnp.int32, (2 * tq, LANES), 1)
    first_lane = jnp.where(row2 < tq, 0, BIAS_LANES)
    bias_on = (lane2 >= first_lane) & (lane2 < first_lane + BIAS_LANES)
    bias_q = jnp.where(bias_on, 1.0, 0.0).astype(BF16)

    def block(col_first, width, masked):
        rows = pl.ds(col_first, width)
        heads = range(N_HEADS)
        lanes = [slice(j * LANES, (j + 1) * LANES) for j in range(PAIRS)]
        s2 = []
        for ls in lanes:
            q = q_ref[:, ls]
            zero = jnp.zeros_like(q)
            q2 = jnp.concatenate([jnp.where(head0, q, zero), jnp.where(head0, zero, q)], axis=0)
            s2.append(lax.dot_general(jnp.concatenate([q2, bias_q], axis=1),
                                      jnp.concatenate([k_ref[rows, ls], kb_ref[rows, ls]], axis=1),
                                      (((1,), (1,)), ((), ())), preferred_element_type=F32))
        s = [s2[h // 2][(h % 2) * tq:(h % 2 + 1) * tq] for h in heads]
        if masked:
            causal = (lax.broadcasted_iota(jnp.int32, (tq, width), 1)
                      <= lax.broadcasted_iota(jnp.int32, (tq, width), 0))
            s = [jnp.where(causal, sh, MASK_VALUE) for sh in s]
        m_prev = [m_ref[h] for h in heads]
        def row_max(sh):
            tile_max = functools.reduce(jnp.maximum, [sh[:, c:c + LANES] for c in range(0, width, LANES)])
            return jnp.max(tile_max, axis=1, keepdims=True)
        m_new = [jnp.maximum(m_prev[h], row_max(s[h])) for h in heads]
        p = [jnp.exp2(s[h] - jnp.concatenate([m_new[h]] * (width // LANES), axis=1)).astype(BF16)
             for h in heads]
        v1 = []
        for ls in lanes:
            v = v_ref[rows, ls]
            one = jnp.ones_like(v)
            v1 += [jnp.where(head0, v, one), jnp.where(head0, one, v)]
        pv = [jnp.dot(p[h], v1[h], preferred_element_type=F32) for h in heads]
        for h in heads:
            acc_ref[h] = jnp.exp2(m_prev[h] - m_new[h]) * acc_ref[h] + pv[h]
            m_ref[h] = m_new[h]

    def full_block(kj, carry):
        block(pl.multiple_of(kj * tk, tk), tk, False)
        return carry

    n_full = row_first // tk
    lax.fori_loop(0, n_full, full_block, 0)
    if tk > tq >= LANES:
        def rest_block(i, carry):
            block(pl.multiple_of(n_full * tk + i * tq, tq), tq, False)
            return carry
        lax.fori_loop(0, (row_first - n_full * tk) // tq, rest_block, 0)
    diag_w = max(tq, LANES)
    block(pl.multiple_of(row_first, tq), diag_w, True)

    if o_ref.shape[0] > tq:
        o_ref[...] = jnp.zeros_like(o_ref)
    for j in range(PAIRS):
        a0, a1 = acc_ref[2 * j], acc_ref[2 * j + 1]
        o_ref[0:tq, j * LANES:(j + 1) * LANES] = jnp.where(
            head0, a0 / pltpu.roll(a0, HEAD_DIM, axis=1), a1 / pltpu.roll(a1, HEAD_DIM, axis=1))


def _fox_attn(qkvb, q_row0, n_rows, tq, tk, out_rows=None, into=None):
    b, l, _ = qkvb.shape
    out_rows = out_rows or tq
    assert q_row0 % tq == 0 and q_row0 % out_rows == 0
    assert tk % tq == 0 if tq >= LANES else (q_row0 % tk == 0 and n_rows == tq)
    assert q_row0 + n_rows - tq + max(tq, LANES) <= l
    qb0, ob0 = q_row0 // tq, q_row0 // out_rows
    kv_spec = lambda col: pl.BlockSpec((None, l, WIDTH), lambda bi, qi: (bi, 0, col))
    extra_in, extra_spec, alias = (), [], {}
    if into is not None:
        extra_in, extra_spec, alias = (into,), [pl.BlockSpec(memory_space=pl.ANY)], {4: 0}
    return pl.pallas_call(
        functools.partial(_fox_attn_body, q_row0=q_row0, tk=tk),
        out_shape=jax.ShapeDtypeStruct((b, l, WIDTH), F32),
        grid=(b, n_rows // tq),
        in_specs=[pl.BlockSpec((None, tq, WIDTH), lambda bi, qi: (bi, qb0 + qi, 0)),
                  kv_spec(1), kv_spec(2), kv_spec(3)] + extra_spec,
        out_specs=pl.BlockSpec((None, out_rows, WIDTH), lambda bi, qi: (bi, ob0 + qi, 0)),
        scratch_shapes=[pltpu.VMEM((N_HEADS, tq, LANES), F32), pltpu.VMEM((N_HEADS, tq, LANES), F32)],
        input_output_aliases=alias,
        compiler_params=_cparams(("arbitrary", "arbitrary")),
        name="fox_attn",
    )(qkvb, qkvb, qkvb, qkvb, *extra_in)


def _mixed_residual(h_ref, y_ref, bonus_ref, g_ref, o_ref, og_ref, gnw_ref, gnb_ref, ones_ref, wo_ref):
    ones = ones_ref[...]
    y = jnp.concatenate([y_ref[j] for j in range(PAIRS)], axis=1)
    mean = _dot_ones(y, ones) * (1.0 / HEAD_DIM)
    d = y - mean
    var = _dot_ones(d * d, ones) * (1.0 / HEAD_DIM)
    yn = d * lax.rsqrt(var + GN_EPS) * gnw_ref[...] + gnb_ref[...]
    y_rw = ((yn + bonus_ref[...]) * g_ref[...]).astype(BF16)
    y_fx = (o_ref[...] * _sigmoid(og_ref[...])).astype(BF16)
    return (h_ref[...]
            + jnp.dot(y_rw, wo_ref[0:WIDTH, :], preferred_element_type=F32)
            + jnp.dot(y_fx, wo_ref[WIDTH:2 * WIDTH, :], preferred_element_type=F32))


def _mix_ffn_body(h_ref, y_ref, bonus_ref, gate_ref, o_ref, og_ref, gnw_ref, gnb_ref, ones_ref, wo_ref,
                  g_ref, wu_ref, wg_ref, cwu_ref, cwg_ref, cbu_ref, cbg_ref, wout_ref,
                  out_ref, xn_ref, cu_ref, cg_ref, wu_s, wg_s, act_ref):
    tm = h_ref.shape[0]
    halo = wu_s.shape[0] - tm
    ti = pl.program_id(1)
    j = pl.program_id(2)

    def first_column_tile():
        x = _mixed_residual(h_ref, y_ref, bonus_ref, gate_ref, o_ref, og_ref, gnw_ref, gnb_ref,
                            ones_ref, wo_ref)
        ms = jnp.mean(x * x, axis=-1, keepdims=True)
        xn_ref[...] = (x * lax.rsqrt(ms + NORM_EPS) * g_ref[...]).astype(BF16)
        out_ref[...] = x

    if cu_ref.shape[0] == 1:
        first_column_tile()
    else:
        pl.when(j == 0)(first_column_tile)

    @pl.when(ti == 0)
    def _():
        cu_ref[j] = jnp.zeros(cu_ref.shape[1:], F32)
        cg_ref[j] = jnp.zeros(cg_ref.shape[1:], F32)

    xn = xn_ref[...]

    def project(w_ref, carry_ref, work):
        work[0:halo] = carry_ref[j]
        work[halo:] = jnp.dot(xn, w_ref[...], preferred_element_type=F32)
        carry_ref[j] = work[tm:]

    def conv(cs, cw_ref, cb_ref, work):
        cw = cw_ref[:, cs]
        taps = [work[pl.ds(halo - (CONV_W - 1) + i, tm), cs] for i in range(CONV_W)]
        return cb_ref[:, cs] + cw[0:1, :] * taps[0] + cw[1:2, :] * taps[1] + cw[2:3, :] * taps[2]

    project(wu_ref, cu_ref, wu_s)
    project(wg_ref, cg_ref, wg_s)
    tf = wu_ref.shape[1]
    for c0 in range(0, tf, 2 * MXU_TILE):
        cs = slice(c0, min(c0 + 2 * MXU_TILE, tf))
        gt = conv(cs, cwg_ref, cbg_ref, wg_s)
        act_ref[:, cs] = (gt * _sigmoid(gt) * conv(cs, cwu_ref, cbu_ref, wu_s)).astype(BF16)
    out_ref[...] += jnp.dot(act_ref[...], wout_ref[...], preferred_element_type=F32)


def _mix_ffn(h, y, bonus, gate, o_fx, og, gn_w, gn_b, ones512, wo_bf16,
             g, w_in_bf16, conv_w, conv_b, w_out_bf16, tm, tf):
    b, l, d = h.shape
    d_ff = w_out_bf16.shape[0]
    nj = d_ff // tf
    cw = jnp.pad(conv_w, ((0, SUBLANES - CONV_W), (0, 0)))
    cb = conv_b.reshape(1, 2 * d_ff)
    once = dict(pipeline_mode=pl.Buffered(1))
    full = lambda shape: pl.BlockSpec(shape, lambda bi, ti, j: (0,) * len(shape), **once)
    wmode = once if nj == 1 else {}
    rows = pl.BlockSpec((None, tm, WIDTH), lambda bi, ti, j: (bi, ti, 0))
    return pl.pallas_call(
        _mix_ffn_body,
        out_shape=jax.ShapeDtypeStruct((b, l, d), F32),
        grid=(b, l // tm, nj),
        in_specs=[pl.BlockSpec((None, tm, d), lambda bi, ti, j: (bi, ti, 0)),
                  pl.BlockSpec((None, PAIRS, tm, LANES), lambda bi, ti, j: (bi, 0, ti, 0)),
                  rows, rows, rows, rows,
                  full((1, WIDTH)), full((1, WIDTH)), full(ones512.shape), full((2 * WIDTH, d)),
                  full((1, d)),
                  pl.BlockSpec((d, tf), lambda bi, ti, j: (0, j), **wmode),
                  pl.BlockSpec((d, tf), lambda bi, ti, j: (0, nj + j), **wmode),
                  pl.BlockSpec((SUBLANES, tf), lambda bi, ti, j: (0, j), **wmode),
                  pl.BlockSpec((SUBLANES, tf), lambda bi, ti, j: (0, nj + j), **wmode),
                  pl.BlockSpec((1, tf), lambda bi, ti, j: (0, j), **wmode),
                  pl.BlockSpec((1, tf), lambda bi, ti, j: (0, nj + j), **wmode),
                  pl.BlockSpec((tf, d), lambda bi, ti, j: (j, 0), **wmode)],
        out_specs=pl.BlockSpec((None, tm, d), lambda bi, ti, j: (bi, ti, 0)),
        scratch_shapes=[pltpu.VMEM((tm, d), BF16),
                        pltpu.VMEM((nj, SUBLANES, tf), F32), pltpu.VMEM((nj, SUBLANES, tf), F32),
                        pltpu.VMEM((SUBLANES + tm, tf), F32), pltpu.VMEM((SUBLANES + tm, tf), F32),
                        pltpu.VMEM((tm, tf), BF16)],
        compiler_params=_cparams(("arbitrary", "arbitrary", "arbitrary")),
        name="mix_ffn",
    )(h, y, bonus, gate, o_fx, og, gn_w, gn_b, ones512, wo_bf16,
      g.reshape(1, d), w_in_bf16, w_in_bf16, cw, cw, cb, cb, w_out_bf16)


def _pad_cols(w, n):
    return jnp.pad(w, ((0, 0), (0, n - w.shape[1])))


def _layer(h, prm, l_real):
    (norm1_g, w_in, rw_mu, rw_w0, rw_w_up, rw_a0, rw_a_up, rw_g_up, rw_k_k, rw_k_a, rw_r_k,
     rw_gn_w, rw_gn_b, fx_b_f, fx_q_g, fx_k_g, w_o, norm2_g, ffn_w_in, ffn_conv_w, ffn_conv_b,
     ffn_w_out) = prm
    b, l, d = h.shape
    rw_cols = 3 * WIDTH + DECAY_LORA + AAA_LORA + GATE_LORA

    w_all = jnp.concatenate([_pad_cols(w_in[:, :rw_cols], RW_PCOLS),
                             _pad_cols(w_in[:, rw_cols:], FX_PCOLS)], axis=1)
    mu_p = _pad_cols(rw_mu.reshape(1, rw_cols), RW_PCOLS)
    wcomb = jnp.zeros((LORA_WA, 2 * WIDTH), F32)
    wcomb = wcomb.at[:DECAY_LORA, :WIDTH].set(rw_w_up).at[DECAY_LORA:, WIDTH:].set(rw_a_up)
    w0a0 = jnp.concatenate([rw_w0, rw_a0]).reshape(1, 2 * WIDTH)
    gup_p = jnp.pad(rw_g_up, ((0, GATE_PAD - GATE_LORA), (0, 0)))
    ones512 = _group_ones(MXU_TILE, HEAD_DIM)
    gain = jnp.concatenate([jnp.tile(fx_q_g, N_HEADS) * (HEAD_DIM ** -0.5 * LOG2E),
                            jnp.tile(fx_k_g, N_HEADS)]).reshape(1, 2 * WIDTH)
    bf_p = jnp.pad(fx_b_f, (0, LANES - N_HEADS)).reshape(1, LANES)

    t1 = _pick(l, (544, 384, 128))
    t2 = _pick(l, (544, 384, 256, 128))

    r, lw, k, v, a, bb, g, bonus, og, qkvb = _proj_prep(
        h, norm1_g, w_all, mu_p, wcomb, w0a0, gup_p, rw_k_k.reshape(1, WIDTH),
        rw_k_a.reshape(1, WIDTH), rw_r_k.reshape(1, WIDTH), gain, bf_p, ones512, t1)

    pm = lambda z: z.reshape(b * PAIRS, l, LANES)
    y = _rwkv_chunk(pm(r), pm(lw), pm(k), pm(v), pm(a), pm(bb), npg=8)
    y = y.reshape(b, PAIRS, l, LANES)

    n_main = l // ATTN_BLOCK * ATTN_BLOCK
    o_fx = _fox_attn(qkvb, 0, n_main, ATTN_BLOCK, ATTN_KEYS) if n_main else None
    if l > n_main:
        tq_tail = min(-(-max(l_real - n_main, 1) // BF16_ROWS) * BF16_ROWS, l - n_main)
        tk_tail = ATTN_KEYS if n_main % ATTN_KEYS == 0 else ATTN_BLOCK
        o_fx = _fox_attn(qkvb, n_main, tq_tail, tq_tail, tk_tail, out_rows=l - n_main, into=o_fx)

    d_ff = ffn_w_out.shape[0]
    t3 = _pick(l, (544, 384, 128))
    return _mix_ffn(h, y, bonus, g, o_fx, og, rw_gn_w.reshape(1, WIDTH), rw_gn_b.reshape(1, WIDTH),
                    ones512, w_o, norm2_g, ffn_w_in, ffn_conv_w, ffn_conv_b, ffn_w_out, t3, d_ff)


def kernel(x, meta, norm1_g, w_in, rw_mu, rw_w0, rw_w_up, rw_a0, rw_a_up, rw_g_up, rw_k_k, rw_k_a,
           rw_r_k, rw_gn_w, rw_gn_b, fx_b_f, fx_q_g, fx_k_g, w_o, norm2_g, ffn_w_in, ffn_conv_w,
           ffn_conv_b, ffn_w_out):
    b, seq, d = x.shape
    w_in, w_o, ffn_w_in, ffn_w_out = (w.astype(BF16) for w in (w_in, w_o, ffn_w_in, ffn_w_out))
    params = (norm1_g, w_in, rw_mu, rw_w0, rw_w_up, rw_a0, rw_a_up, rw_g_up, rw_k_k, rw_k_a, rw_r_k,
              rw_gn_w, rw_gn_b, fx_b_f, fx_q_g, fx_k_g, w_o, norm2_g, ffn_w_in, ffn_conv_w,
              ffn_conv_b, ffn_w_out)
    l = N_META + seq
    lp = -(-l // SB) * SB
    meta_b = jnp.broadcast_to(meta[None].astype(x.dtype), (b, N_META, d))
    h = jnp.concatenate([meta_b, x, jnp.zeros((b, lp - l, d), x.dtype)], axis=1)
    for layer in range(norm1_g.shape[0]):
        h = _layer(h, tuple(p[layer] for p in params), l)
    return h[:, N_META:l]
```

```python
import functools

import jax
import jax.numpy as jnp
from jax import lax
from jax.experimental import pallas as pl
from jax.experimental.pallas import tpu as pltpu

F32 = jnp.float32
BF16 = jnp.bfloat16

LANES = 128
SUBLANES = 8
BF16_ROWS = 16
MXU_TILE = 256

N_META = 16
HEAD_DIM = 64
N_HEADS = 8
WIDTH = N_HEADS * HEAD_DIM
DECAY_LORA = 64
AAA_LORA = 64
LORA_WA = DECAY_LORA + AAA_LORA
GATE_LORA = 160
GATE_PAD = 256
RW_PCOLS = 3 * WIDTH + LORA_WA + GATE_PAD
FX_PCOLS = 4 * WIDTH + LANES
CONV_W = 3
NORM_EPS = 1e-6
GN_EPS = HEAD_DIM * 1e-5
PAIRS = WIDTH // LANES
CHUNK = 16
SB = 128
ATTN_BLOCK = 256
ATTN_KEYS = 512
BIAS_LANES = 3
LOG2E = 1.4426950408889634
MASK_VALUE = -1e30
VMEM_LIMIT = 58 * 1024 * 1024


def _pick(n, cands):
    for c in cands:
        if n % c == 0:
            return c
    raise ValueError(f"no tile for {n} in {cands}")


def _cparams(sem):
    return pltpu.CompilerParams(dimension_semantics=sem, vmem_limit_bytes=VMEM_LIMIT)


def _sigmoid(x):
    return 1.0 / (1.0 + jnp.exp(-x))


def _softplus(x):
    return jnp.maximum(x, 0.0) + jnp.log(1.0 + jnp.exp(-jnp.abs(x)))


def _split2(x):
    hi = x.astype(BF16)
    lo = (x - hi.astype(F32)).astype(BF16)
    return hi, lo


def _split3(x):
    p1 = x.astype(BF16)
    rem = x - p1.astype(F32)
    p2 = rem.astype(BF16)
    p3 = (rem - p2.astype(F32)).astype(BF16)
    return p1, p2, p3


def _dot_ones(x, ones_bf16):
    xb = x.astype(BF16)
    t = ones_bf16.shape[0]
    return jnp.concatenate([jnp.dot(xb[:, c:c + t], ones_bf16, preferred_element_type=F32)
                            for c in range(0, x.shape[1], t)], axis=1)


def _mm3(a_hi, a_lo, b_hi, b_lo, dims=((1,), (0,))):
    dn = (dims, ((), ()))
    m_axis = 1 - dims[0][0]
    m = a_hi.shape[m_axis]
    d = lax.dot_general(jnp.concatenate([a_hi, a_lo], axis=m_axis), b_hi, dn,
                        preferred_element_type=F32)
    return d[:m] + d[m:] + lax.dot_general(a_hi, b_lo, dn, preferred_element_type=F32)


def _dot_bf16(a, b):
    return jnp.dot(a.astype(BF16), b.astype(BF16), preferred_element_type=F32)


def _group_ones(n, group):
    i = jnp.arange(n) // group
    return (i[:, None] == i[None, :]).astype(BF16)


def _project(xn, w_ref, p_ref, n_chunk=512):
    n = p_ref.shape[-1]
    for c in range(0, n, n_chunk):
        e = min(c + n_chunk, n)
        p_ref[:, c:e] = jnp.dot(xn, w_ref[:, c:e], preferred_element_type=F32)


def _rwkv_features(p_ref, mu_ref, wcomb_ref, w0a0_ref, gup_ref, kk_ref, ka_ref, rk_ref, ones_ref,
                   r_o, lw_o, k_o, v_o, a_o, b_o, g_o, bonus_o, carry_ref):
    t2 = p_ref.shape[0]
    p = p_ref[...]
    rows = lax.broadcasted_iota(jnp.int32, p.shape, 0)
    prev = jnp.where(rows == 0, carry_ref[...], pltpu.roll(p, 1, axis=0))
    carry_ref[...] = p[t2 - 1:t2, :]
    x = p + mu_ref[...] * (prev - p)

    r = x[:, 0:WIDTH]
    k = x[:, WIDTH:2 * WIDTH]
    v = x[:, 2 * WIDTH:3 * WIDTH]
    wa = x[:, 3 * WIDTH:3 * WIDTH + LORA_WA]
    gd = x[:, 3 * WIDTH + LORA_WA:]

    lane = lax.broadcasted_iota(jnp.int32, wa.shape, 1)
    xa = jnp.where(lane < DECAY_LORA, jnp.tanh(wa), wa)
    pre = _dot_bf16(xa, wcomb_ref[...]) + w0a0_ref[...]
    w_log = -_softplus(-pre[:, :WIDTH]) - 0.5
    log_decay = -jnp.exp(w_log)
    a = _sigmoid(pre[:, WIDTH:])
    g = _dot_bf16(_sigmoid(gd), gup_ref[...])

    ones = ones_ref[...]
    kk = k * kk_ref[...]
    ss = _dot_ones(kk * kk, ones)
    kk = kk / jnp.maximum(jnp.sqrt(ss), 1e-12)
    k2 = k * (1.0 + (a - 1.0) * ka_ref[...])
    bonus = _dot_ones(r * k2 * rk_ref[...], ones) * v

    for j in range(PAIRS):
        ls = slice(j * LANES, (j + 1) * LANES)
        r_o[j] = r[:, ls]
        lw_o[j] = log_decay[:, ls]
        k_o[j] = k2[:, ls]
        v_o[j] = v[:, ls].astype(v_o.dtype)
        a_o[j] = -kk[:, ls]
        b_o[j] = (kk * a)[:, ls]
    g_o[...] = g
    bonus_o[...] = bonus


def _fox_features(p_ref, gain_ref, bf_ref, ones_ref, place_ref, og_o, qkvb_o, carry_ref):
    t2 = p_ref.shape[0]
    ones = ones_ref[...]
    for h0 in (0, WIDTH):
        x = p_ref[:, h0:h0 + WIDTH]
        ms = _dot_ones(x * x, ones) * (1.0 / HEAD_DIM)
        qkvb_o[:, h0:h0 + WIDTH] = (x * lax.rsqrt(ms + NORM_EPS) * gain_ref[:, h0:h0 + WIDTH]).astype(BF16)
    qkvb_o[:, 2 * WIDTH:3 * WIDTH] = p_ref[:, 2 * WIDTH:3 * WIDTH].astype(BF16)
    og_o[...] = p_ref[:, 3 * WIDTH:4 * WIDTH]

    logf = -_softplus(-(p_ref[:, 4 * WIDTH:] + bf_ref[...]))
    ri = lax.broadcasted_iota(jnp.int32, (LANES, LANES), 0)
    ci = lax.broadcasted_iota(jnp.int32, (LANES, LANES), 1)
    tri = (ci <= ri).astype(BF16)
    carry = carry_ref[...]
    blocks = []
    for r0 in range(0, t2, LANES):
        n = min(LANES, t2 - r0)
        cb = sum(jnp.dot(tri[:n, :n], piece, preferred_element_type=F32)
                 for piece in _split3(logf[r0:r0 + n])) + carry
        carry = cb[n - 1:n, :]
        blocks.append(cb)
    c = jnp.concatenate(blocks, axis=0)
    carry_ref[...] = carry
    pieces = jnp.concatenate(_split3(c * (-LOG2E)), axis=1)
    qkvb_o[:, 3 * WIDTH:] = jnp.dot(pieces, place_ref[...], preferred_element_type=F32).astype(BF16)


def _proj_prep_body(h_ref, n1_ref, w_ref, mu_ref, wcomb_ref, w0a0_ref, gup_ref, kk_ref,
                    ka_ref, rk_ref, gain_ref, bf_ref, ones_ref, place_ref,
                    r_o, lw_o, k_o, v_o, a_o, b_o, g_o, bonus_o, og_o, qkvb_o,
                    p_ref, shift_ref, csum_ref):
    prw_ref = p_ref.at[:, 0:RW_PCOLS]
    pfx_ref = p_ref.at[:, RW_PCOLS:RW_PCOLS + FX_PCOLS]
    @pl.when(pl.program_id(1) == 0)
    def _():
        shift_ref[...] = jnp.zeros_like(shift_ref)
        csum_ref[...] = jnp.zeros_like(csum_ref)

    x = h_ref[...]
    ms = jnp.mean(x * x, axis=-1, keepdims=True)
    xn = (x * lax.rsqrt(ms + NORM_EPS) * n1_ref[...]).astype(BF16)
    _project(xn, w_ref, p_ref)
    _rwkv_features(prw_ref, mu_ref, wcomb_ref, w0a0_ref, gup_ref, kk_ref, ka_ref, rk_ref, ones_ref,
                   r_o, lw_o, k_o, v_o, a_o, b_o, g_o, bonus_o, shift_ref)
    _fox_features(pfx_ref, gain_ref, bf_ref, ones_ref, place_ref, og_o, qkvb_o, csum_ref)


def _bias_placement():
    src = jnp.arange(3 * LANES)
    piece, head = src // LANES, src % LANES
    dst = (head // 2) * LANES + BIAS_LANES * (head % 2) + piece
    hit = (dst[:, None] == jnp.arange(WIDTH)[None, :]) & (head[:, None] < N_HEADS)
    return hit.astype(BF16)


def _proj_prep(h, norm1_g, w_all, mu_p, wcomb, w0a0, gup_p, k_k, k_a, r_k, gain, bf_p, ones512, tm):
    b, l, d = h.shape
    full = lambda shape: pl.BlockSpec(shape, lambda bi, ti: (0,) * len(shape), pipeline_mode=pl.Buffered(1))
    pair = lambda dt: (jax.ShapeDtypeStruct((b, PAIRS, l, LANES), dt),
                       pl.BlockSpec((None, PAIRS, tm, LANES), lambda bi, ti: (bi, 0, ti, 0)))
    rows = lambda n, dt: (jax.ShapeDtypeStruct((b, l, n), dt),
                          pl.BlockSpec((None, tm, n), lambda bi, ti: (bi, ti, 0)))
    outs = [pair(F32), pair(F32), pair(F32), pair(BF16), pair(F32), pair(F32),
            rows(WIDTH, F32), rows(WIDTH, F32), rows(WIDTH, F32), rows(4 * WIDTH, BF16)]
    consts = (norm1_g.reshape(1, d), w_all, mu_p, wcomb, w0a0, gup_p, k_k, k_a, r_k, gain, bf_p,
              ones512, _bias_placement())
    return pl.pallas_call(
        _proj_prep_body,
        out_shape=[o[0] for o in outs],
        grid=(b, l // tm),
        in_specs=[pl.BlockSpec((None, tm, d), lambda bi, ti: (bi, ti, 0))] + [full(c.shape) for c in consts],
        out_specs=[o[1] for o in outs],
        scratch_shapes=[pltpu.VMEM((tm, RW_PCOLS + FX_PCOLS), F32),
                        pltpu.VMEM((1, RW_PCOLS), F32), pltpu.VMEM((1, LANES), F32)],
        compiler_params=_cparams(("arbitrary", "arbitrary")),
        name="proj_prep",
    )(h, *consts)


def _chunk_masks():
    ti = jnp.arange(SB)[:, None]
    si = jnp.arange(SB)[None, :]
    same = (ti // CHUNK) == (si // CHUNK)
    masks = [same, same & (si <= ti), same & (si < ti), ti == si, (ti < HEAD_DIM) == (si < HEAD_DIM)]
    return jnp.stack(masks).astype(F32)


def _rwkv_chunk_body(r_ref, lw_ref, k_ref, v_ref, a_ref, b_ref, masks_ref, y_ref, h_ref):
    npg = r_ref.shape[0]
    n_chunks = SB // CHUNK

    @pl.when(pl.program_id(1) == 0)
    def _():
        h_ref[...] = jnp.zeros_like(h_ref)

    same, incl, strict, _, same_head = (masks_ref[i] != 0.0 for i in range(5))
    eye_f = masks_ref[3]
    head0 = lax.broadcasted_iota(jnp.int32, (SB, LANES), 1) < HEAD_DIM
    tri = masks_ref[1].astype(BF16)
    bf = lambda z: z.astype(BF16)
    dot = functools.partial(jnp.dot, preferred_element_type=F32)

    def keep(mask, x):
        return jnp.where(mask, x, jnp.zeros_like(x))

    pairs = range(npg)
    units = [(p, h) for p in pairs for h in range(2)]
    cat0 = lambda *xs: jnp.concatenate(xs, axis=0)
    mm = lambda a, b, **kw: _mm3(*a, *b, **kw)

    lw = [lw_ref[p] for p in pairs]
    cs = [sum(dot(tri, piece) for piece in _split2(lw[p])) for p in pairs]
    ce = [cat0(*[jnp.broadcast_to(cs[p][(c + 1) * CHUNK - 1:(c + 1) * CHUNK, :], (CHUNK, LANES))
                 for c in range(n_chunks)]) for p in pairs]
    g = [jnp.exp(cs[p]) for p in pairs]
    ginv = [jnp.exp(-cs[p]) for p in pairs]
    gend = [jnp.exp(ce[p] - cs[p]) for p in pairs]
    g_c = [jnp.exp(ce[p]) for p in pairs]
    at = [a_ref[p] * jnp.exp(cs[p] - lw[p]) for p in pairs]
    rt = [r_ref[p] * g[p] for p in pairs]
    v_b = [bf(v_ref[p]) for p in pairs]
    at_b = [bf(at[p]) for p in pairs]
    rt_b = [bf(rt[p]) for p in pairs]
    bk_t = [cat0(bf(b_ref[p] * ginv[p]), bf(k_ref[p] * ginv[p])) for p in pairs]
    bh_b = [bf(b_ref[p] * gend[p]) for p in pairs]
    kh_b = [bf(k_ref[p] * gend[p]) for p in pairs]

    gram = [lax.dot_general(cat0(keep(head0, at_b[p]), keep(head0, rt_b[p]),
                                 keep(~head0, at_b[p]), keep(~head0, rt_b[p])), bk_t[p],
                            (((1,), (1,)), ((), ())), preferred_element_type=F32)
            for p in pairs]
    gb = {(p, h): gram[p][2 * SB * h:2 * SB * (h + 1)] for p, h in units}
    mab = {u: keep(strict, gb[u][:SB, :SB]) for u in units}
    mkk_b = {u: bf(cat0(keep(strict, gb[u][:SB, SB:]), keep(incl, gb[u][SB:, SB:]))) for u in units}
    mrb_b = {u: bf(keep(incl, gb[u][SB:, :SB])) for u in units}

    def fold(z):
        return sum(z[c * CHUNK:(c + 1) * CHUNK] for c in range(n_chunks))

    def unfold(zf):
        return keep(same, jnp.concatenate([zf] * n_chunks, axis=0))

    nf = {u: fold(mab[u]) for u in units}
    tf = {u: fold(eye_f) + nf[u] for u in units}
    full = {u: _split2(mab[u]) for u in units}
    nf = {u: mm(_split2(nf[u]), full[u]) for u in units}
    for k in (2, 4, 8):
        full = {u: _split2(unfold(nf[u])) for u in units}
        if k < 8:
            both_f = {u: mm(_split2(cat0(nf[u], tf[u])), full[u]) for u in units}
            nf = {u: both_f[u][:CHUNK] for u in units}
            tf = {u: tf[u] + both_f[u][CHUNK:] for u in units}
        else:
            tf = {u: tf[u] + mm(_split2(tf[u]), full[u]) for u in units}
    t = {u: unfold(tf[u]) for u in units}
    wk = {u: dot(mkk_b[u], v_b[u[0]]) for u in units}
    x = {u: dot(bf(t[u]), jnp.concatenate([at_b[u[0]], bf(wk[u][:SB])], axis=1)) for u in units}
    ry = {u: dot(mrb_b[u], bf(x[u])) for u in units}
    both = lambda f: [jnp.where(head0, f((p, 0)), f((p, 1))) for p in pairs]
    abar_b = [bf(z) for z in both(lambda u: x[u][:, :LANES])]
    vbar = both(lambda u: x[u][:, LANES:])
    rbar_b = [bf(z) for z in both(lambda u: rt[u[0]] + ry[u][:, :LANES])]
    ybar = both(lambda u: wk[u][SB:] + ry[u][:, LANES:])

    g_ct = [g_c[p].T for p in pairs]

    hbd = [h_ref[p] for p in pairs]
    for c in range(n_chunks):
        sl = slice(c * CHUNK, (c + 1) * CHUNK)
        yu = [dot(cat0(rbar_b[p][sl], abar_b[p][sl]), bf(hbd[p])) for p in pairs]
        for p in pairs:
            y_ref[p, sl, :] = yu[p][:CHUNK] + ybar[p][sl]
        upd = [lax.dot_general(cat0(bh_b[p][sl], kh_b[p][sl]),
                               cat0(bf(yu[p][CHUNK:] + vbar[p][sl]), v_b[p][sl]),
                               (((0,), (0,)), ((), ())), preferred_element_type=F32)
               for p in pairs]
        hbd = [jnp.broadcast_to(g_ct[p][:, c * CHUNK:c * CHUNK + 1], (LANES, LANES)) * hbd[p]
               + keep(same_head, upd[p]) for p in pairs]
    for p in pairs:
        h_ref[p] = hbd[p]


def _rwkv_chunk(r, lw, k, v, a, b, npg):
    n_pairs, l, _ = r.shape
    blk = pl.BlockSpec((npg, SB, LANES), lambda gi, ti: (gi, ti, 0))
    return pl.pallas_call(
        _rwkv_chunk_body,
        out_shape=jax.ShapeDtypeStruct((n_pairs, l, LANES), F32),
        grid=(n_pairs // npg, l // SB),
        in_specs=[blk] * 6 + [pl.BlockSpec((5, SB, SB), lambda gi, ti: (0, 0, 0))],
        out_specs=blk,
        scratch_shapes=[pltpu.VMEM((npg, LANES, LANES), F32)],
        compiler_params=_cparams(("arbitrary", "arbitrary")),
        name="rwkv_chunk",
    )(r, lw, k, v, a, b, _chunk_masks())


def _fox_attn_body(q_ref, k_ref, v_ref, kb_ref, *rest, q_row0, tk):
    o_ref, m_ref, acc_ref = rest[-3:]
    tq = q_ref.shape[0]
    row_first = q_row0 + pl.program_id(1) * tq

    m_ref[...] = jnp.full_like(m_ref, MASK_VALUE)
    acc_ref[...] = jnp.zeros_like(acc_ref)

    lane = lax.broadcasted_iota(jnp.int32, (1, LANES), 1)
    head0 = lane < HEAD_DIM
    row2 = lax.broadcasted_iota(jnp.int32, (2 * tq, LANES), 0)
    lane2 = lax.broadcasted_iota(jnp.int32, (2 * tq, LANES), 1)
    first_lane = jnp.where(row2 < tq, 0, BIAS_LANES)
    bias_on = (lane2 >= first_lane) & (lane2 < first_lane + BIAS_LANES)
    bias_q = jnp.where(bias_on, 1.0, 0.0).astype(BF16)

    def block(col_first, width, masked):
        rows = pl.ds(col_first, width)
        heads = range(N_HEADS)
        lanes = [slice(j * LANES, (j + 1) * LANES) for j in range(PAIRS)]
        s2 = []
        for ls in lanes:
            q = q_ref[:, ls]
            zero = jnp.zeros_like(q)
            q2 = jnp.concatenate([jnp.where(head0, q, zero), jnp.where(head0, zero, q)], axis=0)
            s2.append(lax.dot_general(jnp.concatenate([q2, bias_q], axis=1),
                                      jnp.concatenate([k_ref[rows, ls], kb_ref[rows, ls]], axis=1),
                                      (((1,), (1,)), ((), ())), preferred_element_type=F32))
        s = [s2[h // 2][(h % 2) * tq:(h % 2 + 1) * tq] for h in heads]
        if masked:
            causal = (lax.broadcasted_iota(jnp.int32, (tq, width), 1)
                      <= lax.broadcasted_iota(jnp.int32, (tq, width), 0))
            s = [jnp.where(causal, sh, MASK_VALUE) for sh in s]
        m_prev = [m_ref[h] for h in heads]
        def row_max(sh):
            tile_max = functools.reduce(jnp.maximum, [sh[:, c:c + LANES] for c in range(0, width, LANES)])
            return jnp.max(tile_max, axis=1, keepdims=True)
        m_new = [jnp.maximum(m_prev[h], row_max(s[h])) for h in heads]
        p = [jnp.exp2(s[h] - jnp.concatenate([m_new[h]] * (width // LANES), axis=1)).astype(BF16)
             for h in heads]
        v1 = []
        for ls in lanes:
            v = v_ref[rows, ls]
            one = jnp.ones_like(v)
            v1 += [jnp.where(head0, v, one), jnp.where(head0, one, v)]
        pv = [jnp.dot(p[h], v1[h], preferred_element_type=F32) for h in heads]
        for h in heads:
            acc_ref[h] = jnp.exp2(m_prev[h] - m_new[h]) * acc_ref[h] + pv[h]
            m_ref[h] = m_new[h]

    def full_block(kj, carry):
        block(pl.multiple_of(kj * tk, tk), tk, False)
        return carry

    n_full = row_first // tk
    lax.fori_loop(0, n_full, full_block, 0)
    if tk > tq >= LANES:
        def rest_block(i, carry):
            block(pl.multiple_of(n_full * tk + i * tq, tq), tq, False)
            return carry
        lax.fori_loop(0, (row_first - n_full * tk) // tq, rest_block, 0)
    diag_w = max(tq, LANES)
    block(pl.multiple_of(row_first, tq), diag_w, True)

    if o_ref.shape[0] > tq:
        o_ref[...] = jnp.zeros_like(o_ref)
    for j in range(PAIRS):
        a0, a1 = acc_ref[2 * j], acc_ref[2 * j + 1]
        o_ref[0:tq, j * LANES:(j + 1) * LANES] = jnp.where(
            head0, a0 / pltpu.roll(a0, HEAD_DIM, axis=1), a1 / pltpu.roll(a1, HEAD_DIM, axis=1))


def _fox_attn(qkvb, q_row0, n_rows, tq, tk, out_rows=None, into=None):
    b, l, _ = qkvb.shape
    out_rows = out_rows or tq
    assert q_row0 % tq == 0 and q_row0 % out_rows == 0
    assert tk % tq == 0 if tq >= LANES else (q_row0 % tk == 0 and n_rows == tq)
    assert q_row0 + n_rows - tq + max(tq, LANES) <= l
    qb0, ob0 = q_row0 // tq, q_row0 // out_rows
    kv_spec = lambda col: pl.BlockSpec((None, l, WIDTH), lambda bi, qi: (bi, 0, col))
    extra_in, extra_spec, alias = (), [], {}
    if into is not None:
        extra_in, extra_spec, alias = (into,), [pl.BlockSpec(memory_space=pl.ANY)], {4: 0}
    return pl.pallas_call(
        functools.partial(_fox_attn_body, q_row0=q_row0, tk=tk),
        out_shape=jax.ShapeDtypeStruct((b, l, WIDTH), F32),
        grid=(b, n_rows // tq),
        in_specs=[pl.BlockSpec((None, tq, WIDTH), lambda bi, qi: (bi, qb0 + qi, 0)),
                  kv_spec(1), kv_spec(2), kv_spec(3)] + extra_spec,
        out_specs=pl.BlockSpec((None, out_rows, WIDTH), lambda bi, qi: (bi, ob0 + qi, 0)),
        scratch_shapes=[pltpu.VMEM((N_HEADS, tq, LANES), F32), pltpu.VMEM((N_HEADS, tq, LANES), F32)],
        input_output_aliases=alias,
        compiler_params=_cparams(("arbitrary", "arbitrary")),
        name="fox_attn",
    )(qkvb, qkvb, qkvb, qkvb, *extra_in)


def _mixed_residual(h_ref, y_ref, bonus_ref, g_ref, o_ref, og_ref, gnw_ref, gnb_ref, ones_ref, wo_ref):
    ones = ones_ref[...]
    y = jnp.concatenate([y_ref[j] for j in range(PAIRS)], axis=1)
    mean = _dot_ones(y, ones) * (1.0 / HEAD_DIM)
    d = y - mean
    var = _dot_ones(d * d, ones) * (1.0 / HEAD_DIM)
    yn = d * lax.rsqrt(var + GN_EPS) * gnw_ref[...] + gnb_ref[...]
    y_rw = ((yn + bonus_ref[...]) * g_ref[...]).astype(BF16)
    y_fx = (o_ref[...] * _sigmoid(og_ref[...])).astype(BF16)
    return (h_ref[...]
            + jnp.dot(y_rw, wo_ref[0:WIDTH, :], preferred_element_type=F32)
            + jnp.dot(y_fx, wo_ref[WIDTH:2 * WIDTH, :], preferred_element_type=F32))


def _mix_ffn_body(h_ref, y_ref, bonus_ref, gate_ref, o_ref, og_ref, gnw_ref, gnb_ref, ones_ref, wo_ref,
                  g_ref, wu_ref, wg_ref, cwu_ref, cwg_ref, cbu_ref, cbg_ref, wout_ref,
                  out_ref, xn_ref, cu_ref, cg_ref, wu_s, wg_s, act_ref):
    tm = h_ref.shape[0]
    halo = wu_s.shape[0] - tm
    ti = pl.program_id(1)
    j = pl.program_id(2)

    def first_column_tile():
        x = _mixed_residual(h_ref, y_ref, bonus_ref, gate_ref, o_ref, og_ref, gnw_ref, gnb_ref,
                            ones_ref, wo_ref)
        ms = jnp.mean(x * x, axis=-1, keepdims=True)
        xn_ref[...] = (x * lax.rsqrt(ms + NORM_EPS) * g_ref[...]).astype(BF16)
        out_ref[...] = x

    if cu_ref.shape[0] == 1:
        first_column_tile()
    else:
        pl.when(j == 0)(first_column_tile)

    @pl.when(ti == 0)
    def _():
        cu_ref[j] = jnp.zeros(cu_ref.shape[1:], F32)
        cg_ref[j] = jnp.zeros(cg_ref.shape[1:], F32)

    xn = xn_ref[...]

    def project(w_ref, carry_ref, work):
        work[0:halo] = carry_ref[j]
        work[halo:] = jnp.dot(xn, w_ref[...], preferred_element_type=F32)
        carry_ref[j] = work[tm:]

    def conv(cs, cw_ref, cb_ref, work):
        cw = cw_ref[:, cs]
        taps = [work[pl.ds(halo - (CONV_W - 1) + i, tm), cs] for i in range(CONV_W)]
        return cb_ref[:, cs] + cw[0:1, :] * taps[0] + cw[1:2, :] * taps[1] + cw[2:3, :] * taps[2]

    project(wu_ref, cu_ref, wu_s)
    project(wg_ref, cg_ref, wg_s)
    tf = wu_ref.shape[1]
    for c0 in range(0, tf, 2 * MXU_TILE):
        cs = slice(c0, min(c0 + 2 * MXU_TILE, tf))
        gt = conv(cs, cwg_ref, cbg_ref, wg_s)
        act_ref[:, cs] = (gt * _sigmoid(gt) * conv(cs, cwu_ref, cbu_ref, wu_s)).astype(BF16)
    out_ref[...] += jnp.dot(act_ref[...], wout_ref[...], preferred_element_type=F32)


def _mix_ffn(h, y, bonus, gate, o_fx, og, gn_w, gn_b, ones512, wo_bf16,
             g, w_in_bf16, conv_w, conv_b, w_out_bf16, layer, tm, tf):
    b, l, d = h.shape
    d_ff = w_out_bf16.shape[1]
    nj = d_ff // tf
    cw = jnp.pad(conv_w, ((0, SUBLANES - CONV_W), (0, 0)))
    cb = conv_b.reshape(1, 2 * d_ff)
    once = dict(pipeline_mode=pl.Buffered(1))
    full = lambda shape: pl.BlockSpec(shape, lambda bi, ti, j: (0,) * len(shape), **once)
    wmode = once if nj == 1 else {}
    rows = pl.BlockSpec((None, tm, WIDTH), lambda bi, ti, j: (bi, ti, 0))
    return pl.pallas_call(
        _mix_ffn_body,
        out_shape=jax.ShapeDtypeStruct((b, l, d), F32),
        grid=(b, l // tm, nj),
        in_specs=[pl.BlockSpec((None, tm, d), lambda bi, ti, j: (bi, ti, 0)),
                  pl.BlockSpec((None, PAIRS, tm, LANES), lambda bi, ti, j: (bi, 0, ti, 0)),
                  rows, rows, rows, rows,
                  full((1, WIDTH)), full((1, WIDTH)), full(ones512.shape),
                  pl.BlockSpec((None, 2 * WIDTH, d), lambda bi, ti, j: (layer, 0, 0), **once),
                  full((1, d)),
                  pl.BlockSpec((None, d, tf), lambda bi, ti, j: (layer, 0, j), **wmode),
                  pl.BlockSpec((None, d, tf), lambda bi, ti, j: (layer, 0, nj + j), **wmode),
                  pl.BlockSpec((SUBLANES, tf), lambda bi, ti, j: (0, j), **wmode),
                  pl.BlockSpec((SUBLANES, tf), lambda bi, ti, j: (0, nj + j), **wmode),
                  pl.BlockSpec((1, tf), lambda bi, ti, j: (0, j), **wmode),
                  pl.BlockSpec((1, tf), lambda bi, ti, j: (0, nj + j), **wmode),
                  pl.BlockSpec((None, tf, d), lambda bi, ti, j: (layer, j, 0), **wmode)],
        out_specs=pl.BlockSpec((None, tm, d), lambda bi, ti, j: (bi, ti, 0)),
        scratch_shapes=[pltpu.VMEM((tm, d), BF16),
                        pltpu.VMEM((nj, SUBLANES, tf), F32), pltpu.VMEM((nj, SUBLANES, tf), F32),
                        pltpu.VMEM((SUBLANES + tm, tf), F32), pltpu.VMEM((SUBLANES + tm, tf), F32),
                        pltpu.VMEM((tm, tf), BF16)],
        compiler_params=_cparams(("arbitrary", "arbitrary", "arbitrary")),
        name="mix_ffn",
    )(h, y, bonus, gate, o_fx, og, gn_w, gn_b, ones512, wo_bf16,
      g.reshape(1, d), w_in_bf16, w_in_bf16, cw, cw, cb, cb, w_out_bf16)


def _pad_cols(w, n):
    return jnp.pad(w, ((0, 0), (0, n - w.shape[1])))


def _layer(h, prm, stacks, layer, l_real):
    (norm1_g, w_in, rw_mu, rw_w0, rw_w_up, rw_a0, rw_a_up, rw_g_up, rw_k_k, rw_k_a, rw_r_k,
     rw_gn_w, rw_gn_b, fx_b_f, fx_q_g, fx_k_g, norm2_g, ffn_conv_w, ffn_conv_b) = prm
    w_o, ffn_w_in, ffn_w_out = stacks
    b, l, d = h.shape
    rw_cols = 3 * WIDTH + DECAY_LORA + AAA_LORA + GATE_LORA

    w_all = jnp.concatenate([_pad_cols(w_in[:, :rw_cols], RW_PCOLS),
                             _pad_cols(w_in[:, rw_cols:], FX_PCOLS)], axis=1)
    mu_p = _pad_cols(rw_mu.reshape(1, rw_cols), RW_PCOLS)
    wcomb = jnp.zeros((LORA_WA, 2 * WIDTH), F32)
    wcomb = wcomb.at[:DECAY_LORA, :WIDTH].set(rw_w_up).at[DECAY_LORA:, WIDTH:].set(rw_a_up)
    w0a0 = jnp.concatenate([rw_w0, rw_a0]).reshape(1, 2 * WIDTH)
    gup_p = jnp.pad(rw_g_up, ((0, GATE_PAD - GATE_LORA), (0, 0)))
    ones512 = _group_ones(MXU_TILE, HEAD_DIM)
    gain = jnp.concatenate([jnp.tile(fx_q_g, N_HEADS) * (HEAD_DIM ** -0.5 * LOG2E),
                            jnp.tile(fx_k_g, N_HEADS)]).reshape(1, 2 * WIDTH)
    bf_p = jnp.pad(fx_b_f, (0, LANES - N_HEADS)).reshape(1, LANES)

    t1 = _pick(l, (544, 384, 128))
    t2 = _pick(l, (544, 384, 256, 128))

    r, lw, k, v, a, bb, g, bonus, og, qkvb = _proj_prep(
        h, norm1_g, w_all, mu_p, wcomb, w0a0, gup_p, rw_k_k.reshape(1, WIDTH),
        rw_k_a.reshape(1, WIDTH), rw_r_k.reshape(1, WIDTH), gain, bf_p, ones512, t1)

    pm = lambda z: z.reshape(b * PAIRS, l, LANES)
    y = _rwkv_chunk(pm(r), pm(lw), pm(k), pm(v), pm(a), pm(bb), npg=8)
    y = y.reshape(b, PAIRS, l, LANES)

    n_main = l // ATTN_BLOCK * ATTN_BLOCK
    o_fx = _fox_attn(qkvb, 0, n_main, ATTN_BLOCK, ATTN_KEYS) if n_main else None
    if l > n_main:
        tq_tail = min(-(-max(l_real - n_main, 1) // BF16_ROWS) * BF16_ROWS, l - n_main)
        tk_tail = ATTN_KEYS if n_main % ATTN_KEYS == 0 else ATTN_BLOCK
        o_fx = _fox_attn(qkvb, n_main, tq_tail, tq_tail, tk_tail, out_rows=l - n_main, into=o_fx)

    d_ff = ffn_w_out.shape[1]
    t3 = _pick(l, (544, 384, 128))
    return _mix_ffn(h, y, bonus, g, o_fx, og, rw_gn_w.reshape(1, WIDTH), rw_gn_b.reshape(1, WIDTH),
                    ones512, w_o, norm2_g, ffn_w_in, ffn_conv_w, ffn_conv_b, ffn_w_out, layer, t3, d_ff)


def kernel(x, meta, norm1_g, w_in, rw_mu, rw_w0, rw_w_up, rw_a0, rw_a_up, rw_g_up, rw_k_k, rw_k_a,
           rw_r_k, rw_gn_w, rw_gn_b, fx_b_f, fx_q_g, fx_k_g, w_o, norm2_g, ffn_w_in, ffn_conv_w,
           ffn_conv_b, ffn_w_out):
    b, seq, d = x.shape
    w_in, w_o, ffn_w_in, ffn_w_out = (w.astype(BF16) for w in (w_in, w_o, ffn_w_in, ffn_w_out))
    params = (norm1_g, w_in, rw_mu, rw_w0, rw_w_up, rw_a0, rw_a_up, rw_g_up, rw_k_k, rw_k_a, rw_r_k,
              rw_gn_w, rw_gn_b, fx_b_f, fx_q_g, fx_k_g, norm2_g, ffn_conv_w, ffn_conv_b)
    stacks = (w_o, ffn_w_in, ffn_w_out)
    l = N_META + seq
    lp = -(-l // SB) * SB
    meta_b = jnp.broadcast_to(meta[None].astype(x.dtype), (b, N_META, d))
    h = jnp.concatenate([meta_b, x, jnp.zeros((b, lp - l, d), x.dtype)], axis=1)
    for layer in range(norm1_g.shape[0]):
        h = _layer(h, tuple(p[layer] for p in params), stacks, layer, l)
    return h[:, N_META:l]
```

```python
import functools

import jax
import jax.numpy as jnp
from jax import lax
from jax.experimental import pallas as pl
from jax.experimental.pallas import tpu as pltpu

F32 = jnp.float32
BF16 = jnp.bfloat16

LANES = 128
SUBLANES = 8
MXU_TILE = 256

N_META = 16
HEAD_DIM = 64
N_HEADS = 8
WIDTH = N_HEADS * HEAD_DIM
DECAY_LORA = 64
AAA_LORA = 64
LORA_WA = DECAY_LORA + AAA_LORA
GATE_LORA = 160
GATE_PAD = 256
RW_PCOLS = 3 * WIDTH + LORA_WA + GATE_PAD
FX_PCOLS = 4 * WIDTH + LANES
CONV_W = 3
NORM_EPS = 1e-6
GN_EPS = HEAD_DIM * 1e-5
PAIRS = WIDTH // LANES
CHUNK = 16
SB = 128
ATTN_BLOCK = 256
ATTN_KEYS = 512
BIAS_LANES = 3
LOG2E = 1.4426950408889634
MASK_VALUE = -1e30
VMEM_LIMIT = 58 * 1024 * 1024


def _pick(n, cands):
    for c in cands:
        if n % c == 0:
            return c
    raise ValueError(f"no tile for {n} in {cands}")


def _cparams(sem):
    return pltpu.CompilerParams(dimension_semantics=sem, vmem_limit_bytes=VMEM_LIMIT)


def _sigmoid(x):
    return 1.0 / (1.0 + jnp.exp(-x))


def _softplus(x):
    return jnp.maximum(x, 0.0) + jnp.log(1.0 + jnp.exp(-jnp.abs(x)))


def _split2(x):
    hi = x.astype(BF16)
    lo = (x - hi.astype(F32)).astype(BF16)
    return hi, lo


def _split3(x):
    p1 = x.astype(BF16)
    rem = x - p1.astype(F32)
    p2 = rem.astype(BF16)
    p3 = (rem - p2.astype(F32)).astype(BF16)
    return p1, p2, p3


def _dot_ones(x, ones_bf16):
    xb = x.astype(BF16)
    t = ones_bf16.shape[0]
    return jnp.concatenate([jnp.dot(xb[:, c:c + t], ones_bf16, preferred_element_type=F32)
                            for c in range(0, x.shape[1], t)], axis=1)


def _mm3(a_hi, a_lo, b_hi, b_lo, dims=((1,), (0,))):
    dn = (dims, ((), ()))
    m_axis = 1 - dims[0][0]
    m = a_hi.shape[m_axis]
    d = lax.dot_general(jnp.concatenate([a_hi, a_lo], axis=m_axis), b_hi, dn,
                        preferred_element_type=F32)
    return d[:m] + d[m:] + lax.dot_general(a_hi, b_lo, dn, preferred_element_type=F32)


def _dot_bf16(a, b):
    return jnp.dot(a.astype(BF16), b.astype(BF16), preferred_element_type=F32)


def _group_ones(n, group):
    i = jnp.arange(n) // group
    return (i[:, None] == i[None, :]).astype(BF16)


def _project(xn, w_ref, p_ref, n_chunk=512):
    n = p_ref.shape[-1]
    for c in range(0, n, n_chunk):
        e = min(c + n_chunk, n)
        p_ref[:, c:e] = jnp.dot(xn, w_ref[:, c:e], preferred_element_type=F32)


def _rwkv_features(p_ref, mu_ref, wcomb_ref, w0a0_ref, gup_ref, kk_ref, ka_ref, rk_ref, ones_ref,
                   r_o, lw_o, k_o, v_o, a_o, b_o, g_o, bonus_o, carry_ref):
    t2 = p_ref.shape[0]
    p = p_ref[...]
    rows = lax.broadcasted_iota(jnp.int32, p.shape, 0)
    prev = jnp.where(rows == 0, carry_ref[...], pltpu.roll(p, 1, axis=0))
    carry_ref[...] = p[t2 - 1:t2, :]
    x = p + mu_ref[...] * (prev - p)

    r = x[:, 0:WIDTH]
    k = x[:, WIDTH:2 * WIDTH]
    v = x[:, 2 * WIDTH:3 * WIDTH]
    wa = x[:, 3 * WIDTH:3 * WIDTH + LORA_WA]
    gd = x[:, 3 * WIDTH + LORA_WA:]

    lane = lax.broadcasted_iota(jnp.int32, wa.shape, 1)
    xa = jnp.where(lane < DECAY_LORA, jnp.tanh(wa), wa)
    pre = _dot_bf16(xa, wcomb_ref[...]) + w0a0_ref[...]
    w_log = -_softplus(-pre[:, :WIDTH]) - 0.5
    log_decay = -jnp.exp(w_log)
    a = _sigmoid(pre[:, WIDTH:])
    g = _dot_bf16(_sigmoid(gd), gup_ref[...])

    ones = ones_ref[...]
    kk = k * kk_ref[...]
    ss = _dot_ones(kk * kk, ones)
    kk = kk / jnp.maximum(jnp.sqrt(ss), 1e-12)
    k2 = k * (1.0 + (a - 1.0) * ka_ref[...])
    bonus = _dot_ones(r * k2 * rk_ref[...], ones) * v

    for j in range(PAIRS):
        ls = slice(j * LANES, (j + 1) * LANES)
        r_o[j] = r[:, ls]
        lw_o[j] = log_decay[:, ls]
        k_o[j] = k2[:, ls]
        v_o[j] = v[:, ls].astype(v_o.dtype)
        a_o[j] = -kk[:, ls]
        b_o[j] = (kk * a)[:, ls]
    g_o[...] = g
    bonus_o[...] = bonus


def _fox_features(p_ref, gain_ref, bf_ref, ones_ref, place_ref, og_o, qkvb_o, carry_ref):
    t2 = p_ref.shape[0]
    ones = ones_ref[...]
    for h0 in (0, WIDTH):
        x = p_ref[:, h0:h0 + WIDTH]
        ms = _dot_ones(x * x, ones) * (1.0 / HEAD_DIM)
        qkvb_o[:, h0:h0 + WIDTH] = (x * lax.rsqrt(ms + NORM_EPS) * gain_ref[:, h0:h0 + WIDTH]).astype(BF16)
    qkvb_o[:, 2 * WIDTH:3 * WIDTH] = p_ref[:, 2 * WIDTH:3 * WIDTH].astype(BF16)
    og_o[...] = p_ref[:, 3 * WIDTH:4 * WIDTH]

    logf = -_softplus(-(p_ref[:, 4 * WIDTH:] + bf_ref[...]))
    ri = lax.broadcasted_iota(jnp.int32, (LANES, LANES), 0)
    ci = lax.broadcasted_iota(jnp.int32, (LANES, LANES), 1)
    tri = (ci <= ri).astype(BF16)
    carry = carry_ref[...]
    blocks = []
    for r0 in range(0, t2, LANES):
        n = min(LANES, t2 - r0)
        cb = sum(jnp.dot(tri[:n, :n], piece, preferred_element_type=F32)
                 for piece in _split3(logf[r0:r0 + n])) + carry
        carry = cb[n - 1:n, :]
        blocks.append(cb)
    c = jnp.concatenate(blocks, axis=0)
    carry_ref[...] = carry
    pieces = jnp.concatenate(_split3(c * (-LOG2E)), axis=1)
    qkvb_o[:, 3 * WIDTH:] = jnp.dot(pieces, place_ref[...], preferred_element_type=F32).astype(BF16)


def _proj_prep_body(h_ref, n1_ref, w_ref, mu_ref, wcomb_ref, w0a0_ref, gup_ref, kk_ref,
                    ka_ref, rk_ref, gain_ref, bf_ref, ones_ref, place_ref,
                    r_o, lw_o, k_o, v_o, a_o, b_o, g_o, bonus_o, og_o, qkvb_o,
                    p_ref, shift_ref, csum_ref):
    prw_ref = p_ref.at[:, 0:RW_PCOLS]
    pfx_ref = p_ref.at[:, RW_PCOLS:RW_PCOLS + FX_PCOLS]
    @pl.when(pl.program_id(1) == 0)
    def _():
        shift_ref[...] = jnp.zeros_like(shift_ref)
        csum_ref[...] = jnp.zeros_like(csum_ref)

    x = h_ref[...]
    ms = jnp.mean(x * x, axis=-1, keepdims=True)
    xn = (x * lax.rsqrt(ms + NORM_EPS) * n1_ref[...]).astype(BF16)
    _project(xn, w_ref, p_ref)
    _rwkv_features(prw_ref, mu_ref, wcomb_ref, w0a0_ref, gup_ref, kk_ref, ka_ref, rk_ref, ones_ref,
                   r_o, lw_o, k_o, v_o, a_o, b_o, g_o, bonus_o, shift_ref)
    _fox_features(pfx_ref, gain_ref, bf_ref, ones_ref, place_ref, og_o, qkvb_o, csum_ref)


def _bias_placement():
    src = jnp.arange(3 * LANES)
    piece, head = src // LANES, src % LANES
    dst = (head // 2) * LANES + BIAS_LANES * (head % 2) + piece
    hit = (dst[:, None] == jnp.arange(WIDTH)[None, :]) & (head[:, None] < N_HEADS)
    return hit.astype(BF16)


def _proj_prep(h, norm1_g, w_all, mu_p, wcomb, w0a0, gup_p, k_k, k_a, r_k, gain, bf_p, ones512, tm):
    b, l, d = h.shape
    full = lambda shape: pl.BlockSpec(shape, lambda bi, ti: (0,) * len(shape), pipeline_mode=pl.Buffered(1))
    pair = lambda dt: (jax.ShapeDtypeStruct((b, PAIRS, l, LANES), dt),
                       pl.BlockSpec((None, PAIRS, tm, LANES), lambda bi, ti: (bi, 0, ti, 0)))
    rows = lambda n, dt: (jax.ShapeDtypeStruct((b, l, n), dt),
                          pl.BlockSpec((None, tm, n), lambda bi, ti: (bi, ti, 0)))
    outs = [pair(F32), pair(F32), pair(F32), pair(BF16), pair(F32), pair(F32),
            rows(WIDTH, F32), rows(WIDTH, F32), rows(WIDTH, F32), rows(4 * WIDTH, BF16)]
    consts = (norm1_g.reshape(1, d), w_all, mu_p, wcomb, w0a0, gup_p, k_k, k_a, r_k, gain, bf_p,
              ones512, _bias_placement())
    return pl.pallas_call(
        _proj_prep_body,
        out_shape=[o[0] for o in outs],
        grid=(b, l // tm),
        in_specs=[pl.BlockSpec((None, tm, d), lambda bi, ti: (bi, ti, 0))] + [full(c.shape) for c in consts],
        out_specs=[o[1] for o in outs],
        scratch_shapes=[pltpu.VMEM((tm, RW_PCOLS + FX_PCOLS), F32),
                        pltpu.VMEM((1, RW_PCOLS), F32), pltpu.VMEM((1, LANES), F32)],
        compiler_params=_cparams(("arbitrary", "arbitrary")),
        name="proj_prep",
    )(h, *consts)


def _chunk_masks():
    ti = jnp.arange(SB)[:, None]
    si = jnp.arange(SB)[None, :]
    same = (ti // CHUNK) == (si // CHUNK)
    masks = [same, same & (si <= ti), same & (si < ti), ti == si, (ti < HEAD_DIM) == (si < HEAD_DIM)]
    return jnp.stack(masks).astype(F32)


def _rwkv_chunk_body(r_ref, lw_ref, k_ref, v_ref, a_ref, b_ref, masks_ref, y_ref, h_ref):
    npg = r_ref.shape[0]
    n_chunks = SB // CHUNK

    @pl.when(pl.program_id(1) == 0)
    def _():
        h_ref[...] = jnp.zeros_like(h_ref)

    same, incl, strict, _, same_head = (masks_ref[i] != 0.0 for i in range(5))
    eye_f = masks_ref[3]
    head0 = lax.broadcasted_iota(jnp.int32, (SB, LANES), 1) < HEAD_DIM
    tri = masks_ref[1].astype(BF16)
    bf = lambda z: z.astype(BF16)
    dot = functools.partial(jnp.dot, preferred_element_type=F32)

    def keep(mask, x):
        return jnp.where(mask, x, jnp.zeros_like(x))

    pairs = range(npg)
    units = [(p, h) for p in pairs for h in range(2)]
    cat0 = lambda *xs: jnp.concatenate(xs, axis=0)
    mm = lambda a, b, **kw: _mm3(*a, *b, **kw)

    lw = [lw_ref[p] for p in pairs]
    cs = [sum(dot(tri, piece) for piece in _split2(lw[p])) for p in pairs]
    ce = [cat0(*[jnp.broadcast_to(cs[p][(c + 1) * CHUNK - 1:(c + 1) * CHUNK, :], (CHUNK, LANES))
                 for c in range(n_chunks)]) for p in pairs]
    g = [jnp.exp(cs[p]) for p in pairs]
    ginv = [jnp.exp(-cs[p]) for p in pairs]
    gend = [jnp.exp(ce[p] - cs[p]) for p in pairs]
    g_c = [jnp.exp(ce[p]) for p in pairs]
    at = [a_ref[p] * jnp.exp(cs[p] - lw[p]) for p in pairs]
    rt = [r_ref[p] * g[p] for p in pairs]
    v_b = [bf(v_ref[p]) for p in pairs]
    at_b = [bf(at[p]) for p in pairs]
    rt_b = [bf(rt[p]) for p in pairs]
    bk_t = [cat0(bf(b_ref[p] * ginv[p]), bf(k_ref[p] * ginv[p])) for p in pairs]
    bh_b = [bf(b_ref[p] * gend[p]) for p in pairs]
    kh_b = [bf(k_ref[p] * gend[p]) for p in pairs]

    gram = [lax.dot_general(cat0(keep(head0, at_b[p]), keep(head0, rt_b[p]),
                                 keep(~head0, at_b[p]), keep(~head0, rt_b[p])), bk_t[p],
                            (((1,), (1,)), ((), ())), preferred_element_type=F32)
            for p in pairs]
    gb = {(p, h): gram[p][2 * SB * h:2 * SB * (h + 1)] for p, h in units}
    mab = {u: keep(strict, gb[u][:SB, :SB]) for u in units}
    mkk_b = {u: bf(cat0(keep(strict, gb[u][:SB, SB:]), keep(incl, gb[u][SB:, SB:]))) for u in units}
    mrb_b = {u: bf(keep(incl, gb[u][SB:, :SB])) for u in units}

    def fold(z):
        return sum(z[c * CHUNK:(c + 1) * CHUNK] for c in range(n_chunks))

    def unfold(zf):
        return keep(same, jnp.concatenate([zf] * n_chunks, axis=0))

    nf = {u: fold(mab[u]) for u in units}
    tf = {u: fold(eye_f) + nf[u] for u in units}
    full = {u: _split2(mab[u]) for u in units}
    nf = {u: mm(_split2(nf[u]), full[u]) for u in units}
    for k in (2, 4, 8):
        full = {u: _split2(unfold(nf[u])) for u in units}
        if k < 8:
            both_f = {u: mm(_split2(cat0(nf[u], tf[u])), full[u]) for u in units}
            nf = {u: both_f[u][:CHUNK] for u in units}
            tf = {u: tf[u] + both_f[u][CHUNK:] for u in units}
        else:
            tf = {u: tf[u] + mm(_split2(tf[u]), full[u]) for u in units}
    t = {u: unfold(tf[u]) for u in units}
    wk = {u: dot(mkk_b[u], v_b[u[0]]) for u in units}
    x = {u: dot(bf(t[u]), jnp.concatenate([at_b[u[0]], bf(wk[u][:SB])], axis=1)) for u in units}
    ry = {u: dot(mrb_b[u], bf(x[u])) for u in units}
    both = lambda f: [jnp.where(head0, f((p, 0)), f((p, 1))) for p in pairs]
    abar_b = [bf(z) for z in both(lambda u: x[u][:, :LANES])]
    vbar = both(lambda u: x[u][:, LANES:])
    rbar_b = [bf(z) for z in both(lambda u: rt[u[0]] + ry[u][:, :LANES])]
    ybar = both(lambda u: wk[u][SB:] + ry[u][:, LANES:])

    g_ct = [g_c[p].T for p in pairs]

    hbd = [h_ref[p] for p in pairs]
    for c in range(n_chunks):
        sl = slice(c * CHUNK, (c + 1) * CHUNK)
        yu = [dot(cat0(rbar_b[p][sl], abar_b[p][sl]), bf(hbd[p])) for p in pairs]
        for p in pairs:
            y_ref[p, sl, :] = yu[p][:CHUNK] + ybar[p][sl]
        upd = [lax.dot_general(cat0(bh_b[p][sl], kh_b[p][sl]),
                               cat0(bf(yu[p][CHUNK:] + vbar[p][sl]), v_b[p][sl]),
                               (((0,), (0,)), ((), ())), preferred_element_type=F32)
               for p in pairs]
        hbd = [jnp.broadcast_to(g_ct[p][:, c * CHUNK:c * CHUNK + 1], (LANES, LANES)) * hbd[p]
               + keep(same_head, upd[p]) for p in pairs]
    for p in pairs:
        h_ref[p] = hbd[p]


def _rwkv_chunk(r, lw, k, v, a, b, npg):
    n_pairs, l, _ = r.shape
    blk = pl.BlockSpec((npg, SB, LANES), lambda gi, ti: (gi, ti, 0))
    return pl.pallas_call(
        _rwkv_chunk_body,
        out_shape=jax.ShapeDtypeStruct((n_pairs, l, LANES), F32),
        grid=(n_pairs // npg, l // SB),
        in_specs=[blk] * 6 + [pl.BlockSpec((5, SB, SB), lambda gi, ti: (0, 0, 0))],
        out_specs=blk,
        scratch_shapes=[pltpu.VMEM((npg, LANES, LANES), F32)],
        compiler_params=_cparams(("arbitrary", "arbitrary")),
        name="rwkv_chunk",
    )(r, lw, k, v, a, b, _chunk_masks())


def _fox_attn_body(q_ref, k_ref, v_ref, kb_ref, *rest, q_row0, tk):
    o_ref, m_ref, acc_ref = rest[-3:]
    tq = q_ref.shape[0]
    row_first = q_row0 + pl.program_id(1) * tq

    m_ref[...] = jnp.full_like(m_ref, MASK_VALUE)
    acc_ref[...] = jnp.zeros_like(acc_ref)

    lane = lax.broadcasted_iota(jnp.int32, (1, LANES), 1)
    head0 = lane < HEAD_DIM
    row2 = lax.broadcasted_iota(jnp.int32, (2 * tq, LANES), 0)
    lane2 = lax.broadcasted_iota(jnp.int32, (2 * tq, LANES), 1)
    first_lane = jnp.where(row2 < tq, 0, BIAS_LANES)
    bias_on = (lane2 >= first_lane) & (lane2 < first_lane + BIAS_LANES)
    bias_q = jnp.where(bias_on, 1.0, 0.0).astype(BF16)

    own0 = lax.broadcasted_iota(jnp.int32, (LANES, 1), 0) < HEAD_DIM

    def block(col_first, width, masked):
        pairs = range(PAIRS)
        rows = pl.ds(col_first, width)
        lanes = {j: slice(j * LANES, (j + 1) * LANES) for j in pairs}
        st = {}
        for j in pairs:
            q = q_ref[:, lanes[j]]
            zero = jnp.zeros_like(q)
            q2 = jnp.concatenate([jnp.where(head0, q, zero), jnp.where(head0, zero, q)], axis=0)
            st[j] = lax.dot_general(jnp.concatenate([k_ref[rows, lanes[j]], kb_ref[rows, lanes[j]]], axis=1),
                                    jnp.concatenate([q2, bias_q], axis=1),
                                    (((1,), (1,)), ((), ())), preferred_element_type=F32)
        if masked:
            key = lax.broadcasted_iota(jnp.int32, (width, 2 * tq), 0)
            col = lax.broadcasted_iota(jnp.int32, (width, 2 * tq), 1)
            causal = key <= jnp.where(col < tq, col, col - tq)
            st = {j: jnp.where(causal, st[j], MASK_VALUE) for j in pairs}
        m_prev = {j: m_ref[j] for j in pairs}
        m_new = {j: jnp.maximum(m_prev[j], jnp.max(st[j], axis=0, keepdims=True)) for j in pairs}
        pt = {j: jnp.exp2(st[j] - m_new[j]).astype(BF16) for j in pairs}
        alpha = {j: jnp.exp2(m_prev[j] - m_new[j]) for j in pairs}
        for j in pairs:
            vt = v_ref[rows, lanes[j]].astype(F32).T.astype(BF16)
            one = jnp.ones_like(vt)
            for hh, vt1 in enumerate((jnp.where(own0, vt, one), jnp.where(own0, one, vt))):
                cols = slice(hh * tq, (hh + 1) * tq)
                acc_ref[2 * j + hh] = (alpha[j][:, cols] * acc_ref[2 * j + hh]
                                       + jnp.dot(vt1, pt[j][:, cols], preferred_element_type=F32))
            m_ref[j] = m_new[j]

    def full_block(kj, carry):
        block(pl.multiple_of(kj * tk, tk), tk, False)
        return carry

    n_full = row_first // tk
    lax.fori_loop(0, n_full, full_block, 0)
    if tk > tq:
        def rest_block(i, carry):
            block(pl.multiple_of(n_full * tk + i * tq, tq), tq, False)
            return carry
        lax.fori_loop(0, (row_first - n_full * tk) // tq, rest_block, 0)
    block(pl.multiple_of(row_first, tq), tq, True)

    for j in range(PAIRS):
        a0, a1 = acc_ref[2 * j], acc_ref[2 * j + 1]
        ot = jnp.concatenate([a0[:HEAD_DIM] / a0[HEAD_DIM:HEAD_DIM + 1],
                              a1[HEAD_DIM:] / a1[0:1]], axis=0)
        o_ref[:, j * LANES:(j + 1) * LANES] = ot.T


def _fox_attn(qkvb, q_row0, n_rows, tq, tk, into=None):
    b, l, _ = qkvb.shape
    assert q_row0 % tq == 0 and tk % tq == 0 and tq % LANES == 0 and q_row0 + n_rows <= l
    qb0 = q_row0 // tq
    kv_spec = lambda col: pl.BlockSpec((None, l, WIDTH), lambda bi, qi: (bi, 0, col))
    extra_in, extra_spec, alias = (), [], {}
    if into is not None:
        extra_in, extra_spec, alias = (into,), [pl.BlockSpec(memory_space=pl.ANY)], {4: 0}
    return pl.pallas_call(
        functools.partial(_fox_attn_body, q_row0=q_row0, tk=tk),
        out_shape=jax.ShapeDtypeStruct((b, l, WIDTH), F32),
        grid=(b, n_rows // tq),
        in_specs=[pl.BlockSpec((None, tq, WIDTH), lambda bi, qi: (bi, qb0 + qi, 0)),
                  kv_spec(1), kv_spec(2), kv_spec(3)] + extra_spec,
        out_specs=pl.BlockSpec((None, tq, WIDTH), lambda bi, qi: (bi, qb0 + qi, 0)),
        scratch_shapes=[pltpu.VMEM((PAIRS, 1, 2 * tq), F32), pltpu.VMEM((N_HEADS, LANES, tq), F32)],
        input_output_aliases=alias,
        compiler_params=_cparams(("arbitrary", "arbitrary")),
        name="fox_attn",
    )(qkvb, qkvb, qkvb, qkvb, *extra_in)


def _mixed_residual(h_ref, y_ref, bonus_ref, g_ref, o_ref, og_ref, gnw_ref, gnb_ref, ones_ref, wo_ref):
    ones = ones_ref[...]
    y = jnp.concatenate([y_ref[j] for j in range(PAIRS)], axis=1)
    mean = _dot_ones(y, ones) * (1.0 / HEAD_DIM)
    d = y - mean
    var = _dot_ones(d * d, ones) * (1.0 / HEAD_DIM)
    yn = d * lax.rsqrt(var + GN_EPS) * gnw_ref[...] + gnb_ref[...]
    y_rw = ((yn + bonus_ref[...]) * g_ref[...]).astype(BF16)
    y_fx = (o_ref[...] * _sigmoid(og_ref[...])).astype(BF16)
    return (h_ref[...]
            + jnp.dot(y_rw, wo_ref[0:WIDTH, :], preferred_element_type=F32)
            + jnp.dot(y_fx, wo_ref[WIDTH:2 * WIDTH, :], preferred_element_type=F32))


def _mix_ffn_body(h_ref, y_ref, bonus_ref, gate_ref, o_ref, og_ref, gnw_ref, gnb_ref, ones_ref, wo_ref,
                  g_ref, wu_ref, wg_ref, cwu_ref, cwg_ref, cbu_ref, cbg_ref, wout_ref,
                  out_ref, xn_ref, cu_ref, cg_ref, wu_s, wg_s, act_ref):
    tm = h_ref.shape[0]
    halo = wu_s.shape[0] - tm
    ti = pl.program_id(1)
    j = pl.program_id(2)

    def first_column_tile():
        x = _mixed_residual(h_ref, y_ref, bonus_ref, gate_ref, o_ref, og_ref, gnw_ref, gnb_ref,
                            ones_ref, wo_ref)
        ms = jnp.mean(x * x, axis=-1, keepdims=True)
        xn_ref[...] = (x * lax.rsqrt(ms + NORM_EPS) * g_ref[...]).astype(BF16)
        out_ref[...] = x

    if cu_ref.shape[0] == 1:
        first_column_tile()
    else:
        pl.when(j == 0)(first_column_tile)

    @pl.when(ti == 0)
    def _():
        cu_ref[j] = jnp.zeros(cu_ref.shape[1:], F32)
        cg_ref[j] = jnp.zeros(cg_ref.shape[1:], F32)

    xn = xn_ref[...]

    def project(w_ref, carry_ref, work):
        work[0:halo] = carry_ref[j]
        work[halo:] = jnp.dot(xn, w_ref[...], preferred_element_type=F32)
        carry_ref[j] = work[tm:]

    def conv(cs, cw_ref, cb_ref, work):
        cw = cw_ref[:, cs]
        taps = [work[pl.ds(halo - (CONV_W - 1) + i, tm), cs] for i in range(CONV_W)]
        return cb_ref[:, cs] + cw[0:1, :] * taps[0] + cw[1:2, :] * taps[1] + cw[2:3, :] * taps[2]

    project(wu_ref, cu_ref, wu_s)
    project(wg_ref, cg_ref, wg_s)
    tf = wu_ref.shape[1]
    for c0 in range(0, tf, 2 * MXU_TILE):
        cs = slice(c0, min(c0 + 2 * MXU_TILE, tf))
        gt = conv(cs, cwg_ref, cbg_ref, wg_s)
        act_ref[:, cs] = (gt * _sigmoid(gt) * conv(cs, cwu_ref, cbu_ref, wu_s)).astype(BF16)
    out_ref[...] += jnp.dot(act_ref[...], wout_ref[...], preferred_element_type=F32)


def _mix_ffn(h, y, bonus, gate, o_fx, og, gn_w, gn_b, ones512, wo_bf16,
             g, w_in_bf16, conv_w, conv_b, w_out_bf16, layer, tm, tf):
    b, l, d = h.shape
    d_ff = w_out_bf16.shape[1]
    nj = d_ff // tf
    cw = jnp.pad(conv_w, ((0, SUBLANES - CONV_W), (0, 0)))
    cb = conv_b.reshape(1, 2 * d_ff)
    once = dict(pipeline_mode=pl.Buffered(1))
    full = lambda shape: pl.BlockSpec(shape, lambda bi, ti, j: (0,) * len(shape), **once)
    wmode = once if nj == 1 else {}
    rows = pl.BlockSpec((None, tm, WIDTH), lambda bi, ti, j: (bi, ti, 0))
    return pl.pallas_call(
        _mix_ffn_body,
        out_shape=jax.ShapeDtypeStruct((b, l, d), F32),
        grid=(b, l // tm, nj),
        in_specs=[pl.BlockSpec((None, tm, d), lambda bi, ti, j: (bi, ti, 0)),
                  pl.BlockSpec((None, PAIRS, tm, LANES), lambda bi, ti, j: (bi, 0, ti, 0)),
                  rows, rows, rows, rows,
                  full((1, WIDTH)), full((1, WIDTH)), full(ones512.shape),
                  pl.BlockSpec((None, 2 * WIDTH, d), lambda bi, ti, j: (layer, 0, 0), **once),
                  full((1, d)),
                  pl.BlockSpec((None, d, tf), lambda bi, ti, j: (layer, 0, j), **wmode),
                  pl.BlockSpec((None, d, tf), lambda bi, ti, j: (layer, 0, nj + j), **wmode),
                  pl.BlockSpec((SUBLANES, tf), lambda bi, ti, j: (0, j), **wmode),
                  pl.BlockSpec((SUBLANES, tf), lambda bi, ti, j: (0, nj + j), **wmode),
                  pl.BlockSpec((1, tf), lambda bi, ti, j: (0, j), **wmode),
                  pl.BlockSpec((1, tf), lambda bi, ti, j: (0, nj + j), **wmode),
                  pl.BlockSpec((None, tf, d), lambda bi, ti, j: (layer, j, 0), **wmode)],
        out_specs=pl.BlockSpec((None, tm, d), lambda bi, ti, j: (bi, ti, 0)),
        scratch_shapes=[pltpu.VMEM((tm, d), BF16),
                        pltpu.VMEM((nj, SUBLANES, tf), F32), pltpu.VMEM((nj, SUBLANES, tf), F32),
                        pltpu.VMEM((SUBLANES + tm, tf), F32), pltpu.VMEM((SUBLANES + tm, tf), F32),
                        pltpu.VMEM((tm, tf), BF16)],
        compiler_params=_cparams(("arbitrary", "arbitrary", "arbitrary")),
        name="mix_ffn",
    )(h, y, bonus, gate, o_fx, og, gn_w, gn_b, ones512, wo_bf16,
      g.reshape(1, d), w_in_bf16, w_in_bf16, cw, cw, cb, cb, w_out_bf16)


def _pad_cols(w, n):
    return jnp.pad(w, ((0, 0), (0, n - w.shape[1])))


def _layer(h, prm, stacks, layer):
    (norm1_g, w_in, rw_mu, rw_w0, rw_w_up, rw_a0, rw_a_up, rw_g_up, rw_k_k, rw_k_a, rw_r_k,
     rw_gn_w, rw_gn_b, fx_b_f, fx_q_g, fx_k_g, norm2_g, ffn_conv_w, ffn_conv_b) = prm
    w_o, ffn_w_in, ffn_w_out = stacks
    b, l, d = h.shape
    rw_cols = 3 * WIDTH + DECAY_LORA + AAA_LORA + GATE_LORA

    w_all = jnp.concatenate([_pad_cols(w_in[:, :rw_cols], RW_PCOLS),
                             _pad_cols(w_in[:, rw_cols:], FX_PCOLS)], axis=1)
    mu_p = _pad_cols(rw_mu.reshape(1, rw_cols), RW_PCOLS)
    wcomb = jnp.zeros((LORA_WA, 2 * WIDTH), F32)
    wcomb = wcomb.at[:DECAY_LORA, :WIDTH].set(rw_w_up).at[DECAY_LORA:, WIDTH:].set(rw_a_up)
    w0a0 = jnp.concatenate([rw_w0, rw_a0]).reshape(1, 2 * WIDTH)
    gup_p = jnp.pad(rw_g_up, ((0, GATE_PAD - GATE_LORA), (0, 0)))
    ones512 = _group_ones(MXU_TILE, HEAD_DIM)
    gain = jnp.concatenate([jnp.tile(fx_q_g, N_HEADS) * (HEAD_DIM ** -0.5 * LOG2E),
                            jnp.tile(fx_k_g, N_HEADS)]).reshape(1, 2 * WIDTH)
    bf_p = jnp.pad(fx_b_f, (0, LANES - N_HEADS)).reshape(1, LANES)

    t1 = _pick(l, (544, 384, 128))
    t2 = _pick(l, (544, 384, 256, 128))

    r, lw, k, v, a, bb, g, bonus, og, qkvb = _proj_prep(
        h, norm1_g, w_all, mu_p, wcomb, w0a0, gup_p, rw_k_k.reshape(1, WIDTH),
        rw_k_a.reshape(1, WIDTH), rw_r_k.reshape(1, WIDTH), gain, bf_p, ones512, t1)

    pm = lambda z: z.reshape(b * PAIRS, l, LANES)
    y = _rwkv_chunk(pm(r), pm(lw), pm(k), pm(v), pm(a), pm(bb), npg=8)
    y = y.reshape(b, PAIRS, l, LANES)

    n_main = l // ATTN_BLOCK * ATTN_BLOCK
    o_fx = _fox_attn(qkvb, 0, n_main, ATTN_BLOCK, ATTN_KEYS) if n_main else None
    if l > n_main:
        o_fx = _fox_attn(qkvb, n_main, l - n_main, l - n_main, ATTN_KEYS, into=o_fx)

    d_ff = ffn_w_out.shape[1]
    t3 = _pick(l, (544, 384, 128))
    return _mix_ffn(h, y, bonus, g, o_fx, og, rw_gn_w.reshape(1, WIDTH), rw_gn_b.reshape(1, WIDTH),
                    ones512, w_o, norm2_g, ffn_w_in, ffn_conv_w, ffn_conv_b, ffn_w_out, layer, t3, d_ff)


def kernel(x, meta, norm1_g, w_in, rw_mu, rw_w0, rw_w_up, rw_a0, rw_a_up, rw_g_up, rw_k_k, rw_k_a,
           rw_r_k, rw_gn_w, rw_gn_b, fx_b_f, fx_q_g, fx_k_g, w_o, norm2_g, ffn_w_in, ffn_conv_w,
           ffn_conv_b, ffn_w_out):
    b, seq, d = x.shape
    w_in, w_o, ffn_w_in, ffn_w_out = (w.astype(BF16) for w in (w_in, w_o, ffn_w_in, ffn_w_out))
    params = (norm1_g, w_in, rw_mu, rw_w0, rw_w_up, rw_a0, rw_a_up, rw_g_up, rw_k_k, rw_k_a, rw_r_k,
              rw_gn_w, rw_gn_b, fx_b_f, fx_q_g, fx_k_g, norm2_g, ffn_conv_w, ffn_conv_b)
    stacks = (w_o, ffn_w_in, ffn_w_out)
    l = N_META + seq
    lp = -(-l // SB) * SB
    meta_b = jnp.broadcast_to(meta[None].astype(x.dtype), (b, N_META, d))
    h = jnp.concatenate([meta_b, x, jnp.zeros((b, lp - l, d), x.dtype)], axis=1)
    for layer in range(norm1_g.shape[0]):
        h = _layer(h, tuple(p[layer] for p in params), stacks, layer)
    return h[:, N_META:l]
```

```python
import functools

import jax
import jax.numpy as jnp
from jax import lax
from jax.experimental import pallas as pl
from jax.experimental.pallas import tpu as pltpu

F32 = jnp.float32
BF16 = jnp.bfloat16

LANES = 128
SUBLANES = 8
BF16_ROWS = 16
MXU_TILE = 256

N_META = 16
HEAD_DIM = 64
N_HEADS = 8
WIDTH = N_HEADS * HEAD_DIM
DECAY_LORA = 64
AAA_LORA = 64
LORA_WA = DECAY_LORA + AAA_LORA
GATE_LORA = 160
GATE_PAD = 256
RW_PCOLS = 3 * WIDTH + LORA_WA + GATE_PAD
FX_PCOLS = 4 * WIDTH + LANES
CONV_W = 3
NORM_EPS = 1e-6
GN_EPS = HEAD_DIM * 1e-5
PAIRS = WIDTH // LANES
CHUNK = 16
SB = 128
ATTN_BLOCK = 256
ATTN_KEYS = 512
BIAS_LANES = 3
LOG2E = 1.4426950408889634
MASK_VALUE = -1e30
VMEM_LIMIT = 58 * 1024 * 1024


def _pick(n, cands):
    for c in cands:
        if n % c == 0:
            return c
    raise ValueError(f"no tile for {n} in {cands}")


def _cparams(sem):
    return pltpu.CompilerParams(dimension_semantics=sem, vmem_limit_bytes=VMEM_LIMIT)


def _sigmoid(x):
    return 1.0 / (1.0 + jnp.exp(-x))


def _softplus(x):
    return jnp.maximum(x, 0.0) + jnp.log(1.0 + jnp.exp(-jnp.abs(x)))


def _split2(x):
    hi = x.astype(BF16)
    lo = (x - hi.astype(F32)).astype(BF16)
    return hi, lo


def _split3(x):
    p1 = x.astype(BF16)
    rem = x - p1.astype(F32)
    p2 = rem.astype(BF16)
    p3 = (rem - p2.astype(F32)).astype(BF16)
    return p1, p2, p3


def _dot_ones(x, ones_bf16):
    xb = x.astype(BF16)
    t = ones_bf16.shape[0]
    return jnp.concatenate([jnp.dot(xb[:, c:c + t], ones_bf16, preferred_element_type=F32)
                            for c in range(0, x.shape[1], t)], axis=1)


def _mm3(a_hi, a_lo, b_hi, b_lo, dims=((1,), (0,))):
    dn = (dims, ((), ()))
    m_axis = 1 - dims[0][0]
    m = a_hi.shape[m_axis]
    d = lax.dot_general(jnp.concatenate([a_hi, a_lo], axis=m_axis), b_hi, dn,
                        preferred_element_type=F32)
    return d[:m] + d[m:] + lax.dot_general(a_hi, b_lo, dn, preferred_element_type=F32)


def _dot_bf16(a, b):
    return jnp.dot(a.astype(BF16), b.astype(BF16), preferred_element_type=F32)


def _group_ones(n, group):
    i = jnp.arange(n) // group
    return (i[:, None] == i[None, :]).astype(BF16)


def _project(xn, w_ref, p_ref, n_chunk=512):
    n = p_ref.shape[-1]
    for c in range(0, n, n_chunk):
        e = min(c + n_chunk, n)
        p_ref[:, c:e] = jnp.dot(xn, w_ref[:, c:e], preferred_element_type=F32)


def _rwkv_features(p_ref, mu_ref, wcomb_ref, w0a0_ref, gup_ref, kk_ref, ka_ref, rk_ref, ones_ref,
                   r_o, lw_o, k_o, v_o, a_o, b_o, g_o, bonus_o, carry_ref):
    t2 = p_ref.shape[0]
    p = p_ref[...]
    rows = lax.broadcasted_iota(jnp.int32, p.shape, 0)
    prev = jnp.where(rows == 0, carry_ref[...], pltpu.roll(p, 1, axis=0))
    carry_ref[...] = p[t2 - 1:t2, :]
    x = p + mu_ref[...] * (prev - p)

    r = x[:, 0:WIDTH]
    k = x[:, WIDTH:2 * WIDTH]
    v = x[:, 2 * WIDTH:3 * WIDTH]
    wa = x[:, 3 * WIDTH:3 * WIDTH + LORA_WA]
    gd = x[:, 3 * WIDTH + LORA_WA:]

    lane = lax.broadcasted_iota(jnp.int32, wa.shape, 1)
    xa = jnp.where(lane < DECAY_LORA, jnp.tanh(wa), wa)
    pre = _dot_bf16(xa, wcomb_ref[...]) + w0a0_ref[...]
    w_log = -_softplus(-pre[:, :WIDTH]) - 0.5
    log_decay = -jnp.exp(w_log)
    a = _sigmoid(pre[:, WIDTH:])
    g = _dot_bf16(_sigmoid(gd), gup_ref[...])

    ones = ones_ref[...]
    kk = k * kk_ref[...]
    ss = _dot_ones(kk * kk, ones)
    kk = kk / jnp.maximum(jnp.sqrt(ss), 1e-12)
    k2 = k * (1.0 + (a - 1.0) * ka_ref[...])
    bonus = _dot_ones(r * k2 * rk_ref[...], ones) * v

    for j in range(PAIRS):
        ls = slice(j * LANES, (j + 1) * LANES)
        r_o[j] = r[:, ls]
        lw_o[j] = log_decay[:, ls]
        k_o[j] = k2[:, ls]
        v_o[j] = v[:, ls].astype(v_o.dtype)
        a_o[j] = -kk[:, ls]
        b_o[j] = (kk * a)[:, ls]
    g_o[...] = g
    bonus_o[...] = bonus


def _fox_features(p_ref, gain_ref, bf_ref, ones_ref, place_ref, og_o, qkvb_o, carry_ref):
    t2 = p_ref.shape[0]
    ones = ones_ref[...]
    for h0 in (0, WIDTH):
        x = p_ref[:, h0:h0 + WIDTH]
        ms = _dot_ones(x * x, ones) * (1.0 / HEAD_DIM)
        qkvb_o[:, h0:h0 + WIDTH] = (x * lax.rsqrt(ms + NORM_EPS) * gain_ref[:, h0:h0 + WIDTH]).astype(BF16)
    qkvb_o[:, 2 * WIDTH:3 * WIDTH] = p_ref[:, 2 * WIDTH:3 * WIDTH].astype(BF16)
    og_o[...] = p_ref[:, 3 * WIDTH:4 * WIDTH]

    logf = -_softplus(-(p_ref[:, 4 * WIDTH:] + bf_ref[...]))
    ri = lax.broadcasted_iota(jnp.int32, (LANES, LANES), 0)
    ci = lax.broadcasted_iota(jnp.int32, (LANES, LANES), 1)
    tri = (ci <= ri).astype(BF16)
    carry = carry_ref[...]
    blocks = []
    for r0 in range(0, t2, LANES):
        n = min(LANES, t2 - r0)
        cb = sum(jnp.dot(tri[:n, :n], piece, preferred_element_type=F32)
                 for piece in _split3(logf[r0:r0 + n])) + carry
        carry = cb[n - 1:n, :]
        blocks.append(cb)
    c = jnp.concatenate(blocks, axis=0)
    carry_ref[...] = carry
    pieces = jnp.concatenate(_split3(c * (-LOG2E)), axis=1)
    qkvb_o[:, 3 * WIDTH:] = jnp.dot(pieces, place_ref[...], preferred_element_type=F32).astype(BF16)


def _proj_prep_body(h_ref, n1_ref, w_ref, mu_ref, wcomb_ref, w0a0_ref, gup_ref, kk_ref,
                    ka_ref, rk_ref, gain_ref, bf_ref, ones_ref, place_ref,
                    r_o, lw_o, k_o, v_o, a_o, b_o, g_o, bonus_o, og_o, qkvb_o,
                    p_ref, shift_ref, csum_ref):
    prw_ref = p_ref.at[:, 0:RW_PCOLS]
    pfx_ref = p_ref.at[:, RW_PCOLS:RW_PCOLS + FX_PCOLS]
    @pl.when(pl.program_id(1) == 0)
    def _():
        shift_ref[...] = jnp.zeros_like(shift_ref)
        csum_ref[...] = jnp.zeros_like(csum_ref)

    x = h_ref[...]
    ms = jnp.mean(x * x, axis=-1, keepdims=True)
    xn = (x * lax.rsqrt(ms + NORM_EPS) * n1_ref[...]).astype(BF16)
    _project(xn, w_ref, p_ref)
    _rwkv_features(prw_ref, mu_ref, wcomb_ref, w0a0_ref, gup_ref, kk_ref, ka_ref, rk_ref, ones_ref,
                   r_o, lw_o, k_o, v_o, a_o, b_o, g_o, bonus_o, shift_ref)
    _fox_features(pfx_ref, gain_ref, bf_ref, ones_ref, place_ref, og_o, qkvb_o, csum_ref)


def _bias_placement():
    src = jnp.arange(3 * LANES)
    piece, head = src // LANES, src % LANES
    dst = (head // 2) * LANES + BIAS_LANES * (head % 2) + piece
    hit = (dst[:, None] == jnp.arange(WIDTH)[None, :]) & (head[:, None] < N_HEADS)
    return hit.astype(BF16)


def _proj_prep(h, norm1_g, w_all, mu_p, wcomb, w0a0, gup_p, k_k, k_a, r_k, gain, bf_p, ones512, tm):
    b, l, d = h.shape
    full = lambda shape: pl.BlockSpec(shape, lambda bi, ti: (0,) * len(shape), pipeline_mode=pl.Buffered(1))
    pair = lambda dt: (jax.ShapeDtypeStruct((b, PAIRS, l, LANES), dt),
                       pl.BlockSpec((None, PAIRS, tm, LANES), lambda bi, ti: (bi, 0, ti, 0)))
    rows = lambda n, dt: (jax.ShapeDtypeStruct((b, l, n), dt),
                          pl.BlockSpec((None, tm, n), lambda bi, ti: (bi, ti, 0)))
    outs = [pair(F32), pair(F32), pair(F32), pair(BF16), pair(F32), pair(F32),
            rows(WIDTH, F32), rows(WIDTH, F32), rows(WIDTH, F32), rows(4 * WIDTH, BF16)]
    consts = (norm1_g.reshape(1, d), w_all, mu_p, wcomb, w0a0, gup_p, k_k, k_a, r_k, gain, bf_p,
              ones512, _bias_placement())
    return pl.pallas_call(
        _proj_prep_body,
        out_shape=[o[0] for o in outs],
        grid=(b, l // tm),
        in_specs=[pl.BlockSpec((None, tm, d), lambda bi, ti: (bi, ti, 0))] + [full(c.shape) for c in consts],
        out_specs=[o[1] for o in outs],
        scratch_shapes=[pltpu.VMEM((tm, RW_PCOLS + FX_PCOLS), F32),
                        pltpu.VMEM((1, RW_PCOLS), F32), pltpu.VMEM((1, LANES), F32)],
        compiler_params=_cparams(("arbitrary", "arbitrary")),
        name="proj_prep",
    )(h, *consts)


def _chunk_masks():
    ti = jnp.arange(SB)[:, None]
    si = jnp.arange(SB)[None, :]
    same = (ti // CHUNK) == (si // CHUNK)
    masks = [same, same & (si <= ti), same & (si < ti), ti == si, (ti < HEAD_DIM) == (si < HEAD_DIM)]
    return jnp.stack(masks).astype(F32)


def _rwkv_chunk_body(r_ref, lw_ref, k_ref, v_ref, a_ref, b_ref, masks_ref, y_ref, h_ref):
    npg = r_ref.shape[0]
    n_chunks = SB // CHUNK

    @pl.when(pl.program_id(1) == 0)
    def _():
        h_ref[...] = jnp.zeros_like(h_ref)

    same, incl, strict, _, same_head = (masks_ref[i] != 0.0 for i in range(5))
    eye_f = masks_ref[3]
    head0 = lax.broadcasted_iota(jnp.int32, (SB, LANES), 1) < HEAD_DIM
    tri = masks_ref[1].astype(BF16)
    bf = lambda z: z.astype(BF16)
    dot = functools.partial(jnp.dot, preferred_element_type=F32)

    def keep(mask, x):
        return jnp.where(mask, x, jnp.zeros_like(x))

    pairs = range(npg)
    units = [(p, h) for p in pairs for h in range(2)]
    cat0 = lambda *xs: jnp.concatenate(xs, axis=0)
    mm = lambda a, b, **kw: _mm3(*a, *b, **kw)

    lw = [lw_ref[p] for p in pairs]
    cs = [sum(dot(tri, piece) for piece in _split2(lw[p])) for p in pairs]
    ce = [cat0(*[jnp.broadcast_to(cs[p][(c + 1) * CHUNK - 1:(c + 1) * CHUNK, :], (CHUNK, LANES))
                 for c in range(n_chunks)]) for p in pairs]
    g = [jnp.exp(cs[p]) for p in pairs]
    ginv = [jnp.exp(-cs[p]) for p in pairs]
    gend = [jnp.exp(ce[p] - cs[p]) for p in pairs]
    g_c = [jnp.exp(ce[p]) for p in pairs]
    at = [a_ref[p] * jnp.exp(cs[p] - lw[p]) for p in pairs]
    rt = [r_ref[p] * g[p] for p in pairs]
    v_b = [bf(v_ref[p]) for p in pairs]
    at_b = [bf(at[p]) for p in pairs]
    rt_b = [bf(rt[p]) for p in pairs]
    bk_t = [cat0(bf(b_ref[p] * ginv[p]), bf(k_ref[p] * ginv[p])) for p in pairs]
    bh_b = [bf(b_ref[p] * gend[p]) for p in pairs]
    kh_b = [bf(k_ref[p] * gend[p]) for p in pairs]

    gram = [lax.dot_general(cat0(keep(head0, at_b[p]), keep(head0, rt_b[p]),
                                 keep(~head0, at_b[p]), keep(~head0, rt_b[p])), bk_t[p],
                            (((1,), (1,)), ((), ())), preferred_element_type=F32)
            for p in pairs]
    gb = {(p, h): gram[p][2 * SB * h:2 * SB * (h + 1)] for p, h in units}
    mab = {u: keep(strict, gb[u][:SB, :SB]) for u in units}
    mkk_b = {u: bf(cat0(keep(strict, gb[u][:SB, SB:]), keep(incl, gb[u][SB:, SB:]))) for u in units}
    mrb_b = {u: bf(keep(incl, gb[u][SB:, :SB])) for u in units}

    def fold(z):
        return sum(z[c * CHUNK:(c + 1) * CHUNK] for c in range(n_chunks))

    def unfold(zf):
        return keep(same, jnp.concatenate([zf] * n_chunks, axis=0))

    nf = {u: fold(mab[u]) for u in units}
    tf = {u: fold(eye_f) + nf[u] for u in units}
    full = {u: _split2(mab[u]) for u in units}
    nf = {u: mm(_split2(nf[u]), full[u]) for u in units}
    for k in (2, 4, 8):
        full = {u: _split2(unfold(nf[u])) for u in units}
        if k < 8:
            both_f = {u: mm(_split2(cat0(nf[u], tf[u])), full[u]) for u in units}
            nf = {u: both_f[u][:CHUNK] for u in units}
            tf = {u: tf[u] + both_f[u][CHUNK:] for u in units}
        else:
            tf = {u: tf[u] + mm(_split2(tf[u]), full[u]) for u in units}
    t = {u: unfold(tf[u]) for u in units}
    wk = {u: dot(mkk_b[u], v_b[u[0]]) for u in units}
    x = {u: dot(bf(t[u]), jnp.concatenate([at_b[u[0]], bf(wk[u][:SB])], axis=1)) for u in units}
    ry = {u: dot(mrb_b[u], bf(x[u])) for u in units}
    both = lambda f: [jnp.where(head0, f((p, 0)), f((p, 1))) for p in pairs]
    abar_b = [bf(z) for z in both(lambda u: x[u][:, :LANES])]
    vbar = both(lambda u: x[u][:, LANES:])
    rbar_b = [bf(z) for z in both(lambda u: rt[u[0]] + ry[u][:, :LANES])]
    ybar = both(lambda u: wk[u][SB:] + ry[u][:, LANES:])

    g_ct = [g_c[p].T for p in pairs]

    hbd = [h_ref[p] for p in pairs]
    for c in range(n_chunks):
        sl = slice(c * CHUNK, (c + 1) * CHUNK)
        yu = [dot(cat0(rbar_b[p][sl], abar_b[p][sl]), bf(hbd[p])) for p in pairs]
        for p in pairs:
            y_ref[p, sl, :] = yu[p][:CHUNK] + ybar[p][sl]
        upd = [lax.dot_general(cat0(bh_b[p][sl], kh_b[p][sl]),
                               cat0(bf(yu[p][CHUNK:] + vbar[p][sl]), v_b[p][sl]),
                               (((0,), (0,)), ((), ())), preferred_element_type=F32)
               for p in pairs]
        hbd = [jnp.broadcast_to(g_ct[p][:, c * CHUNK:c * CHUNK + 1], (LANES, LANES)) * hbd[p]
               + keep(same_head, upd[p]) for p in pairs]
    for p in pairs:
        h_ref[p] = hbd[p]


def _rwkv_chunk(r, lw, k, v, a, b, npg):
    n_pairs, l, _ = r.shape
    blk = pl.BlockSpec((npg, SB, LANES), lambda gi, ti: (gi, ti, 0))
    return pl.pallas_call(
        _rwkv_chunk_body,
        out_shape=jax.ShapeDtypeStruct((n_pairs, l, LANES), F32),
        grid=(n_pairs // npg, l // SB),
        in_specs=[blk] * 6 + [pl.BlockSpec((5, SB, SB), lambda gi, ti: (0, 0, 0))],
        out_specs=blk,
        scratch_shapes=[pltpu.VMEM((npg, LANES, LANES), F32)],
        compiler_params=_cparams(("arbitrary", "arbitrary")),
        name="rwkv_chunk",
    )(r, lw, k, v, a, b, _chunk_masks())


def _fox_attn_body(q_ref, k_ref, v_ref, kb_ref, *rest, q_row0, tk):
    o_ref, m_ref, acc_ref = rest[-3:]
    tq = q_ref.shape[0]
    row_first = q_row0 + pl.program_id(1) * tq

    m_ref[...] = jnp.full_like(m_ref, MASK_VALUE)
    acc_ref[...] = jnp.zeros_like(acc_ref)

    lane = lax.broadcasted_iota(jnp.int32, (1, LANES), 1)
    head0 = lane < HEAD_DIM
    row2 = lax.broadcasted_iota(jnp.int32, (2 * tq, LANES), 0)
    lane2 = lax.broadcasted_iota(jnp.int32, (2 * tq, LANES), 1)
    first_lane = jnp.where(row2 < tq, 0, BIAS_LANES)
    bias_on = (lane2 >= first_lane) & (lane2 < first_lane + BIAS_LANES)
    bias_q = jnp.where(bias_on, 1.0, 0.0).astype(BF16)

    def block(col_first, width, masked):
        rows = pl.ds(col_first, width)
        heads = range(N_HEADS)
        lanes = [slice(j * LANES, (j + 1) * LANES) for j in range(PAIRS)]
        s2 = []
        for ls in lanes:
            q = q_ref[:, ls]
            zero = jnp.zeros_like(q)
            q2 = jnp.concatenate([jnp.where(head0, q, zero), jnp.where(head0, zero, q)], axis=0)
            s2.append(lax.dot_general(jnp.concatenate([q2, bias_q], axis=1),
                                      jnp.concatenate([k_ref[rows, ls], kb_ref[rows, ls]], axis=1),
                                      (((1,), (1,)), ((), ())), preferred_element_type=F32))
        s = [s2[h // 2][(h % 2) * tq:(h % 2 + 1) * tq] for h in heads]
        if masked:
            causal = (lax.broadcasted_iota(jnp.int32, (tq, width), 1)
                      <= lax.broadcasted_iota(jnp.int32, (tq, width), 0))
            s = [jnp.where(causal, sh, MASK_VALUE) for sh in s]
        m_prev = [m_ref[h] for h in heads]
        def row_max(sh):
            tile_max = functools.reduce(jnp.maximum, [sh[:, c:c + LANES] for c in range(0, width, LANES)])
            return jnp.max(tile_max, axis=1, keepdims=True)
        m_new = [jnp.maximum(m_prev[h], row_max(s[h])) for h in heads]
        p = [jnp.exp2(s[h] - jnp.concatenate([m_new[h]] * (width // LANES), axis=1)).astype(BF16)
             for h in heads]
        v1 = []
        for ls in lanes:
            v = v_ref[rows, ls]
            one = jnp.ones_like(v)
            v1 += [jnp.where(head0, v, one), jnp.where(head0, one, v)]
        pv = [jnp.dot(p[h], v1[h], preferred_element_type=F32) for h in heads]
        for h in heads:
            acc_ref[h] = jnp.exp2(m_prev[h] - m_new[h]) * acc_ref[h] + pv[h]
            m_ref[h] = m_new[h]

    def full_block(kj, carry):
        block(pl.multiple_of(kj * tk, tk), tk, False)
        return carry

    n_full = row_first // tk
    lax.fori_loop(0, n_full, full_block, 0)
    if tk > tq >= LANES:
        def rest_block(i, carry):
            block(pl.multiple_of(n_full * tk + i * tq, tq), tq, False)
            return carry
        lax.fori_loop(0, (row_first - n_full * tk) // tq, rest_block, 0)
    diag_w = max(tq, LANES)
    block(pl.multiple_of(row_first, tq), diag_w, True)

    if o_ref.shape[0] > tq:
        o_ref[...] = jnp.zeros_like(o_ref)
    for j in range(PAIRS):
        a0, a1 = acc_ref[2 * j], acc_ref[2 * j + 1]
        o_ref[0:tq, j * LANES:(j + 1) * LANES] = jnp.where(
            head0, a0 / pltpu.roll(a0, HEAD_DIM, axis=1), a1 / pltpu.roll(a1, HEAD_DIM, axis=1))


def _fox_attn(qkvb, q_row0, n_rows, tq, tk, out_rows=None, into=None):
    b, l, _ = qkvb.shape
    out_rows = out_rows or tq
    assert q_row0 % tq == 0 and q_row0 % out_rows == 0
    assert tk % tq == 0 if tq >= LANES else (q_row0 % tk == 0 and n_rows == tq)
    assert q_row0 + n_rows - tq + max(tq, LANES) <= l
    qb0, ob0 = q_row0 // tq, q_row0 // out_rows
    kv_spec = lambda col: pl.BlockSpec((None, l, WIDTH), lambda bi, qi: (bi, 0, col))
    extra_in, extra_spec, alias = (), [], {}
    if into is not None:
        extra_in, extra_spec, alias = (into,), [pl.BlockSpec(memory_space=pl.ANY)], {4: 0}
    return pl.pallas_call(
        functools.partial(_fox_attn_body, q_row0=q_row0, tk=tk),
        out_shape=jax.ShapeDtypeStruct((b, l, WIDTH), F32),
        grid=(b, n_rows // tq),
        in_specs=[pl.BlockSpec((None, tq, WIDTH), lambda bi, qi: (bi, qb0 + qi, 0)),
                  kv_spec(1), kv_spec(2), kv_spec(3)] + extra_spec,
        out_specs=pl.BlockSpec((None, out_rows, WIDTH), lambda bi, qi: (bi, ob0 + qi, 0)),
        scratch_shapes=[pltpu.VMEM((N_HEADS, tq, LANES), F32), pltpu.VMEM((N_HEADS, tq, LANES), F32)],
        input_output_aliases=alias,
        compiler_params=_cparams(("arbitrary", "arbitrary")),
        name="fox_attn",
    )(qkvb, qkvb, qkvb, qkvb, *extra_in)


def _mixed_residual(h_ref, y_ref, bonus_ref, g_ref, o_ref, og_ref, gnw_ref, gnb_ref, ones_ref, wo_ref):
    ones = ones_ref[...]
    y = jnp.concatenate([y_ref[j] for j in range(PAIRS)], axis=1)
    mean = _dot_ones(y, ones) * (1.0 / HEAD_DIM)
    d = y - mean
    var = _dot_ones(d * d, ones) * (1.0 / HEAD_DIM)
    yn = d * lax.rsqrt(var + GN_EPS) * gnw_ref[...] + gnb_ref[...]
    y_rw = ((yn + bonus_ref[...]) * g_ref[...]).astype(BF16)
    y_fx = (o_ref[...] * _sigmoid(og_ref[...])).astype(BF16)
    return (h_ref[...]
            + jnp.dot(y_rw, wo_ref[0:WIDTH, :], preferred_element_type=F32)
            + jnp.dot(y_fx, wo_ref[WIDTH:2 * WIDTH, :], preferred_element_type=F32))


def _mix_ffn_body(h_ref, y_ref, bonus_ref, gate_ref, o_ref, og_ref, gnw_ref, gnb_ref, ones_ref, wo_ref,
                  g_ref, wu_ref, wg_ref, cwu_ref, cwg_ref, cbu_ref, cbg_ref, wout_ref,
                  out_ref, xn_ref, cu_ref, cg_ref, wu_s, wg_s, act_ref):
    tm = h_ref.shape[0]
    halo = wu_s.shape[0] - tm
    ti = pl.program_id(1)
    j = pl.program_id(2)

    def first_column_tile():
        x = _mixed_residual(h_ref, y_ref, bonus_ref, gate_ref, o_ref, og_ref, gnw_ref, gnb_ref,
                            ones_ref, wo_ref)
        ms = jnp.mean(x * x, axis=-1, keepdims=True)
        xn_ref[...] = (x * lax.rsqrt(ms + NORM_EPS) * g_ref[...]).astype(BF16)
        out_ref[...] = x

    if cu_ref.shape[0] == 1:
        first_column_tile()
    else:
        pl.when(j == 0)(first_column_tile)

    @pl.when(ti == 0)
    def _():
        cu_ref[j] = jnp.zeros(cu_ref.shape[1:], F32)
        cg_ref[j] = jnp.zeros(cg_ref.shape[1:], F32)

    xn = xn_ref[...]

    def project(w_ref, carry_ref, work):
        work[0:halo] = carry_ref[j]
        work[halo:] = jnp.dot(xn, w_ref[...], preferred_element_type=F32)
        carry_ref[j] = work[tm:]

    def conv(cs, cw_ref, cb_ref, work):
        cw = cw_ref[:, cs]
        taps = [work[pl.ds(halo - (CONV_W - 1) + i, tm), cs] for i in range(CONV_W)]
        return cb_ref[:, cs] + cw[0:1, :] * taps[0] + cw[1:2, :] * taps[1] + cw[2:3, :] * taps[2]

    project(wu_ref, cu_ref, wu_s)
    project(wg_ref, cg_ref, wg_s)
    tf = wu_ref.shape[1]
    for c0 in range(0, tf, 2 * MXU_TILE):
        cs = slice(c0, min(c0 + 2 * MXU_TILE, tf))
        gt = conv(cs, cwg_ref, cbg_ref, wg_s)
        act_ref[:, cs] = (gt * _sigmoid(gt) * conv(cs, cwu_ref, cbu_ref, wu_s)).astype(BF16)
    out_ref[...] += jnp.dot(act_ref[...], wout_ref[...], preferred_element_type=F32)


def _mix_ffn(h, y, bonus, gate, o_fx, og, gn_w, gn_b, ones512, wo_bf16,
             g, w_in_bf16, conv_w, conv_b, w_out_bf16, layer, tm, tf):
    b, l, d = h.shape
    d_ff = w_out_bf16.shape[1]
    nj = d_ff // tf
    cw = jnp.pad(conv_w, ((0, SUBLANES - CONV_W), (0, 0)))
    cb = conv_b.reshape(1, 2 * d_ff)
    once = dict(pipeline_mode=pl.Buffered(1))
    full = lambda shape: pl.BlockSpec(shape, lambda bi, ti, j: (0,) * len(shape), **once)
    wmode = once if nj == 1 else {}
    rows = pl.BlockSpec((None, tm, WIDTH), lambda bi, ti, j: (bi, ti, 0))
    return pl.pallas_call(
        _mix_ffn_body,
        out_shape=jax.ShapeDtypeStruct((b, l, d), F32),
        grid=(b, l // tm, nj),
        in_specs=[pl.BlockSpec((None, tm, d), lambda bi, ti, j: (bi, ti, 0)),
                  pl.BlockSpec((None, PAIRS, tm, LANES), lambda bi, ti, j: (bi, 0, ti, 0)),
                  rows, rows, rows, rows,
                  full((1, WIDTH)), full((1, WIDTH)), full(ones512.shape),
                  pl.BlockSpec((None, 2 * WIDTH, d), lambda bi, ti, j: (layer, 0, 0), **once),
                  full((1, d)),
                  pl.BlockSpec((None, d, tf), lambda bi, ti, j: (layer, 0, j), **wmode),
                  pl.BlockSpec((None, d, tf), lambda bi, ti, j: (layer, 0, nj + j), **wmode),
                  pl.BlockSpec((SUBLANES, tf), lambda bi, ti, j: (0, j), **wmode),
                  pl.BlockSpec((SUBLANES, tf), lambda bi, ti, j: (0, nj + j), **wmode),
                  pl.BlockSpec((1, tf), lambda bi, ti, j: (0, j), **wmode),
                  pl.BlockSpec((1, tf), lambda bi, ti, j: (0, nj + j), **wmode),
                  pl.BlockSpec((None, tf, d), lambda bi, ti, j: (layer, j, 0), **wmode)],
        out_specs=pl.BlockSpec((None, tm, d), lambda bi, ti, j: (bi, ti, 0)),
        scratch_shapes=[pltpu.VMEM((tm, d), BF16),
                        pltpu.VMEM((nj, SUBLANES, tf), F32), pltpu.VMEM((nj, SUBLANES, tf), F32),
                        pltpu.VMEM((SUBLANES + tm, tf), F32), pltpu.VMEM((SUBLANES + tm, tf), F32),
                        pltpu.VMEM((tm, tf), BF16)],
        compiler_params=_cparams(("arbitrary", "arbitrary", "arbitrary")),
        name="mix_ffn",
    )(h, y, bonus, gate, o_fx, og, gn_w, gn_b, ones512, wo_bf16,
      g.reshape(1, d), w_in_bf16, w_in_bf16, cw, cw, cb, cb, w_out_bf16)


def _pad_cols(w, n):
    return jnp.pad(w, ((0, 0), (0, n - w.shape[1])))


def _layer(h, prm, stacks, layer, l_real):
    (norm1_g, w_in, rw_mu, rw_w0, rw_w_up, rw_a0, rw_a_up, rw_g_up, rw_k_k, rw_k_a, rw_r_k,
     rw_gn_w, rw_gn_b, fx_b_f, fx_q_g, fx_k_g, norm2_g, ffn_conv_w, ffn_conv_b) = prm
    w_o, ffn_w_in, ffn_w_out = stacks
    b, l, d = h.shape
    rw_cols = 3 * WIDTH + DECAY_LORA + AAA_LORA + GATE_LORA

    w_all = jnp.concatenate([_pad_cols(w_in[:, :rw_cols], RW_PCOLS),
                             _pad_cols(w_in[:, rw_cols:], FX_PCOLS)], axis=1)
    mu_p = _pad_cols(rw_mu.reshape(1, rw_cols), RW_PCOLS)
    wcomb = jnp.zeros((LORA_WA, 2 * WIDTH), F32)
    wcomb = wcomb.at[:DECAY_LORA, :WIDTH].set(rw_w_up).at[DECAY_LORA:, WIDTH:].set(rw_a_up)
    w0a0 = jnp.concatenate([rw_w0, rw_a0]).reshape(1, 2 * WIDTH)
    gup_p = jnp.pad(rw_g_up, ((0, GATE_PAD - GATE_LORA), (0, 0)))
    ones512 = _group_ones(MXU_TILE, HEAD_DIM)
    gain = jnp.concatenate([jnp.tile(fx_q_g, N_HEADS) * (HEAD_DIM ** -0.5 * LOG2E),
                            jnp.tile(fx_k_g, N_HEADS)]).reshape(1, 2 * WIDTH)
    bf_p = jnp.pad(fx_b_f, (0, LANES - N_HEADS)).reshape(1, LANES)

    t1 = _pick(l, (544, 384, 128))
    t2 = _pick(l, (544, 384, 256, 128))

    r, lw, k, v, a, bb, g, bonus, og, qkvb = _proj_prep(
        h, norm1_g, w_all, mu_p, wcomb, w0a0, gup_p, rw_k_k.reshape(1, WIDTH),
        rw_k_a.reshape(1, WIDTH), rw_r_k.reshape(1, WIDTH), gain, bf_p, ones512, t1)

    pm = lambda z: z.reshape(b * PAIRS, l, LANES)
    y = _rwkv_chunk(pm(r), pm(lw), pm(k), pm(v), pm(a), pm(bb), npg=_pick(b * PAIRS, (16, 8, 4)))
    y = y.reshape(b, PAIRS, l, LANES)

    n_main = l // ATTN_BLOCK * ATTN_BLOCK
    o_fx = _fox_attn(qkvb, 0, n_main, ATTN_BLOCK, ATTN_KEYS) if n_main else None
    if l > n_main:
        tq_tail = min(-(-max(l_real - n_main, 1) // BF16_ROWS) * BF16_ROWS, l - n_main)
        tk_tail = ATTN_KEYS if n_main % ATTN_KEYS == 0 else ATTN_BLOCK
        o_fx = _fox_attn(qkvb, n_main, tq_tail, tq_tail, tk_tail, out_rows=l - n_main, into=o_fx)

    d_ff = ffn_w_out.shape[1]
    t3 = _pick(l, (544, 384, 128))
    return _mix_ffn(h, y, bonus, g, o_fx, og, rw_gn_w.reshape(1, WIDTH), rw_gn_b.reshape(1, WIDTH),
                    ones512, w_o, norm2_g, ffn_w_in, ffn_conv_w, ffn_conv_b, ffn_w_out, layer, t3, d_ff)


def kernel(x, meta, norm1_g, w_in, rw_mu, rw_w0, rw_w_up, rw_a0, rw_a_up, rw_g_up, rw_k_k, rw_k_a,
           rw_r_k, rw_gn_w, rw_gn_b, fx_b_f, fx_q_g, fx_k_g, w_o, norm2_g, ffn_w_in, ffn_conv_w,
           ffn_conv_b, ffn_w_out):
    b, seq, d = x.shape
    w_in, w_o, ffn_w_in, ffn_w_out = (w.astype(BF16) for w in (w_in, w_o, ffn_w_in, ffn_w_out))
    params = (norm1_g, w_in, rw_mu, rw_w0, rw_w_up, rw_a0, rw_a_up, rw_g_up, rw_k_k, rw_k_a, rw_r_k,
              rw_gn_w, rw_gn_b, fx_b_f, fx_q_g, fx_k_g, norm2_g, ffn_conv_w, ffn_conv_b)
    stacks = (w_o, ffn_w_in, ffn_w_out)
    l = N_META + seq
    lp = -(-l // SB) * SB
    meta_b = jnp.broadcast_to(meta[None].astype(x.dtype), (b, N_META, d))
    h = jnp.concatenate([meta_b, x, jnp.zeros((b, lp - l, d), x.dtype)], axis=1)
    for layer in range(norm1_g.shape[0]):
        h = _layer(h, tuple(p[layer] for p in params), stacks, layer, l)
    return h[:, N_META:l]
```

```python
import functools

import jax
import jax.numpy as jnp
from jax import lax
from jax.experimental import pallas as pl
from jax.experimental.pallas import tpu as pltpu

F32 = jnp.float32
BF16 = jnp.bfloat16

LANES = 128
SUBLANES = 8
BF16_ROWS = 16
MXU_TILE = 256

N_META = 16
HEAD_DIM = 64
N_HEADS = 8
WIDTH = N_HEADS * HEAD_DIM
DECAY_LORA = 64
AAA_LORA = 64
LORA_WA = DECAY_LORA + AAA_LORA
GATE_LORA = 160
GATE_PAD = 256
RW_PCOLS = 3 * WIDTH + LORA_WA + GATE_PAD
FX_PCOLS = 4 * WIDTH + LANES
CONV_W = 3
NORM_EPS = 1e-6
GN_EPS = HEAD_DIM * 1e-5
PAIRS = WIDTH // LANES
CHUNK = 16
SB = 128
ATTN_BLOCK = 256
ATTN_KEYS = 512
BIAS_LANES = 3
LOG2E = 1.4426950408889634
MASK_VALUE = -1e30
VMEM_LIMIT = 58 * 1024 * 1024


def _pick(n, cands):
    for c in cands:
        if n % c == 0:
            return c
    raise ValueError(f"no tile for {n} in {cands}")


def _cparams(sem):
    return pltpu.CompilerParams(dimension_semantics=sem, vmem_limit_bytes=VMEM_LIMIT)


def _sigmoid(x):
    return 1.0 / (1.0 + jnp.exp(-x))


def _softplus(x):
    return jnp.maximum(x, 0.0) + jnp.log(1.0 + jnp.exp(-jnp.abs(x)))


def _split2(x):
    hi = x.astype(BF16)
    lo = (x - hi.astype(F32)).astype(BF16)
    return hi, lo


def _split3(x):
    p1 = x.astype(BF16)
    rem = x - p1.astype(F32)
    p2 = rem.astype(BF16)
    p3 = (rem - p2.astype(F32)).astype(BF16)
    return p1, p2, p3


def _dot_ones(x, ones_bf16):
    xb = x.astype(BF16)
    t = ones_bf16.shape[0]
    return jnp.concatenate([jnp.dot(xb[:, c:c + t], ones_bf16, preferred_element_type=F32)
                            for c in range(0, x.shape[1], t)], axis=1)


def _mm3(a_hi, a_lo, b_hi, b_lo, dims=((1,), (0,))):
    dn = (dims, ((), ()))
    m_axis = 1 - dims[0][0]
    m = a_hi.shape[m_axis]
    d = lax.dot_general(jnp.concatenate([a_hi, a_lo], axis=m_axis), b_hi, dn,
                        preferred_element_type=F32)
    return d[:m] + d[m:] + lax.dot_general(a_hi, b_lo, dn, preferred_element_type=F32)


def _dot_bf16(a, b):
    return jnp.dot(a.astype(BF16), b.astype(BF16), preferred_element_type=F32)


def _group_ones(n, group):
    i = jnp.arange(n) // group
    return (i[:, None] == i[None, :]).astype(BF16)


def _project(xn, w_ref, p_ref, n_chunk=512):
    n = p_ref.shape[-1]
    for c in range(0, n, n_chunk):
        e = min(c + n_chunk, n)
        p_ref[:, c:e] = jnp.dot(xn, w_ref[:, c:e], preferred_element_type=F32)


def _rwkv_features(p_ref, mu_ref, wcomb_ref, w0a0_ref, gup_ref, kk_ref, ka_ref, rk_ref, ones_ref,
                   r_o, lw_o, k_o, v_o, a_o, b_o, g_o, bonus_o, carry_ref):
    t2 = p_ref.shape[0]
    p = p_ref[...]
    rows = lax.broadcasted_iota(jnp.int32, p.shape, 0)
    prev = jnp.where(rows == 0, carry_ref[...], pltpu.roll(p, 1, axis=0))
    carry_ref[...] = p[t2 - 1:t2, :]
    x = p + mu_ref[...] * (prev - p)

    r = x[:, 0:WIDTH]
    k = x[:, WIDTH:2 * WIDTH]
    v = x[:, 2 * WIDTH:3 * WIDTH]
    wa = x[:, 3 * WIDTH:3 * WIDTH + LORA_WA]
    gd = x[:, 3 * WIDTH + LORA_WA:]

    lane = lax.broadcasted_iota(jnp.int32, wa.shape, 1)
    xa = jnp.where(lane < DECAY_LORA, jnp.tanh(wa), wa)
    pre = _dot_bf16(xa, wcomb_ref[...]) + w0a0_ref[...]
    w_log = -_softplus(-pre[:, :WIDTH]) - 0.5
    log_decay = -jnp.exp(w_log)
    a = _sigmoid(pre[:, WIDTH:])
    g = _dot_bf16(_sigmoid(gd), gup_ref[...])

    ones = ones_ref[...]
    kk = k * kk_ref[...]
    ss = _dot_ones(kk * kk, ones)
    kk = kk / jnp.maximum(jnp.sqrt(ss), 1e-12)
    k2 = k * (1.0 + (a - 1.0) * ka_ref[...])
    bonus = _dot_ones(r * k2 * rk_ref[...], ones) * v

    for j in range(PAIRS):
        ls = slice(j * LANES, (j + 1) * LANES)
        r_o[j] = r[:, ls]
        lw_o[j] = log_decay[:, ls]
        k_o[j] = k2[:, ls]
        v_o[j] = v[:, ls].astype(v_o.dtype)
        a_o[j] = -kk[:, ls]
        b_o[j] = (kk * a)[:, ls]
    g_o[...] = g
    bonus_o[...] = bonus


def _fox_features(p_ref, gain_ref, bf_ref, ones_ref, place_ref, og_o, qkvb_o, carry_ref):
    t2 = p_ref.shape[0]
    ones = ones_ref[...]
    for h0 in (0, WIDTH):
        x = p_ref[:, h0:h0 + WIDTH]
        ms = _dot_ones(x * x, ones) * (1.0 / HEAD_DIM)
        qkvb_o[:, h0:h0 + WIDTH] = (x * lax.rsqrt(ms + NORM_EPS) * gain_ref[:, h0:h0 + WIDTH]).astype(BF16)
    qkvb_o[:, 2 * WIDTH:3 * WIDTH] = p_ref[:, 2 * WIDTH:3 * WIDTH].astype(BF16)
    og_o[...] = p_ref[:, 3 * WIDTH:4 * WIDTH]

    logf = -_softplus(-(p_ref[:, 4 * WIDTH:] + bf_ref[...]))
    ri = lax.broadcasted_iota(jnp.int32, (LANES, LANES), 0)
    ci = lax.broadcasted_iota(jnp.int32, (LANES, LANES), 1)
    tri = (ci <= ri).astype(BF16)
    carry = carry_ref[...]
    blocks = []
    for r0 in range(0, t2, LANES):
        n = min(LANES, t2 - r0)
        cb = sum(jnp.dot(tri[:n, :n], piece, preferred_element_type=F32)
                 for piece in _split3(logf[r0:r0 + n])) + carry
        carry = cb[n - 1:n, :]
        blocks.append(cb)
    c = jnp.concatenate(blocks, axis=0)
    carry_ref[...] = carry
    pieces = jnp.concatenate(_split3(c * (-LOG2E)), axis=1)
    qkvb_o[:, 3 * WIDTH:] = jnp.dot(pieces, place_ref[...], preferred_element_type=F32).astype(BF16)


def _proj_prep_body(h_ref, n1_ref, w_ref, mu_ref, wcomb_ref, w0a0_ref, gup_ref, kk_ref,
                    ka_ref, rk_ref, gain_ref, bf_ref, ones_ref, place_ref,
                    r_o, lw_o, k_o, v_o, a_o, b_o, g_o, bonus_o, og_o, qkvb_o,
                    p_ref, shift_ref, csum_ref):
    prw_ref = p_ref.at[:, 0:RW_PCOLS]
    pfx_ref = p_ref.at[:, RW_PCOLS:RW_PCOLS + FX_PCOLS]
    @pl.when(pl.program_id(1) == 0)
    def _():
        shift_ref[...] = jnp.zeros_like(shift_ref)
        csum_ref[...] = jnp.zeros_like(csum_ref)

    x = h_ref[...]
    ms = jnp.mean(x * x, axis=-1, keepdims=True)
    xn = (x * lax.rsqrt(ms + NORM_EPS) * n1_ref[...]).astype(BF16)
    _project(xn, w_ref, p_ref)
    _rwkv_features(prw_ref, mu_ref, wcomb_ref, w0a0_ref, gup_ref, kk_ref, ka_ref, rk_ref, ones_ref,
                   r_o, lw_o, k_o, v_o, a_o, b_o, g_o, bonus_o, shift_ref)
    _fox_features(pfx_ref, gain_ref, bf_ref, ones_ref, place_ref, og_o, qkvb_o, csum_ref)


def _bias_placement():
    src = jnp.arange(3 * LANES)
    piece, head = src // LANES, src % LANES
    dst = (head // 2) * LANES + BIAS_LANES * (head % 2) + piece
    hit = (dst[:, None] == jnp.arange(WIDTH)[None, :]) & (head[:, None] < N_HEADS)
    return hit.astype(BF16)


def _proj_prep(h, norm1_g, w_all, mu_p, wcomb, w0a0, gup_p, k_k, k_a, r_k, gain, bf_p, ones512, tm):
    b, l, d = h.shape
    full = lambda shape: pl.BlockSpec(shape, lambda bi, ti: (0,) * len(shape), pipeline_mode=pl.Buffered(1))
    pair = lambda dt: (jax.ShapeDtypeStruct((b, PAIRS, l, LANES), dt),
                       pl.BlockSpec((None, PAIRS, tm, LANES), lambda bi, ti: (bi, 0, ti, 0)))
    rows = lambda n, dt: (jax.ShapeDtypeStruct((b, l, n), dt),
                          pl.BlockSpec((None, tm, n), lambda bi, ti: (bi, ti, 0)))
    outs = [pair(F32), pair(F32), pair(F32), pair(BF16), pair(F32), pair(F32),
            rows(WIDTH, F32), rows(WIDTH, F32), rows(WIDTH, F32), rows(4 * WIDTH, BF16)]
    consts = (norm1_g.reshape(1, d), w_all, mu_p, wcomb, w0a0, gup_p, k_k, k_a, r_k, gain, bf_p,
              ones512, _bias_placement())
    return pl.pallas_call(
        _proj_prep_body,
        out_shape=[o[0] for o in outs],
        grid=(b, l // tm),
        in_specs=[pl.BlockSpec((None, tm, d), lambda bi, ti: (bi, ti, 0))] + [full(c.shape) for c in consts],
        out_specs=[o[1] for o in outs],
        scratch_shapes=[pltpu.VMEM((tm, RW_PCOLS + FX_PCOLS), F32),
                        pltpu.VMEM((1, RW_PCOLS), F32), pltpu.VMEM((1, LANES), F32)],
        compiler_params=_cparams(("arbitrary", "arbitrary")),
        name="proj_prep",
    )(h, *consts)


def _chunk_masks():
    ti = jnp.arange(SB)[:, None]
    si = jnp.arange(SB)[None, :]
    same = (ti // CHUNK) == (si // CHUNK)
    masks = [same, same & (si <= ti), same & (si < ti), ti == si, (ti < HEAD_DIM) == (si < HEAD_DIM)]
    return jnp.stack(masks).astype(F32)


def _rwkv_chunk_body(r_ref, lw_ref, k_ref, v_ref, a_ref, b_ref, masks_ref, y_ref, h_ref):
    npg = r_ref.shape[0]
    n_chunks = SB // CHUNK

    @pl.when(pl.program_id(1) == 0)
    def _():
        h_ref[...] = jnp.zeros_like(h_ref)

    same, incl, strict, _, same_head = (masks_ref[i] != 0.0 for i in range(5))
    eye_f = masks_ref[3]
    head0 = lax.broadcasted_iota(jnp.int32, (SB, LANES), 1) < HEAD_DIM
    tri = masks_ref[1].astype(BF16)
    bf = lambda z: z.astype(BF16)
    dot = functools.partial(jnp.dot, preferred_element_type=F32)

    def keep(mask, x):
        return jnp.where(mask, x, jnp.zeros_like(x))

    pairs = range(npg)
    units = [(p, h) for p in pairs for h in range(2)]
    cat0 = lambda *xs: jnp.concatenate(xs, axis=0)
    mm = lambda a, b, **kw: _mm3(*a, *b, **kw)

    lw = [lw_ref[p] for p in pairs]
    cs = [sum(dot(tri, piece) for piece in _split2(lw[p])) for p in pairs]
    ce = [cat0(*[jnp.broadcast_to(cs[p][(c + 1) * CHUNK - 1:(c + 1) * CHUNK, :], (CHUNK, LANES))
                 for c in range(n_chunks)]) for p in pairs]
    g = [jnp.exp(cs[p]) for p in pairs]
    ginv = [jnp.exp(-cs[p]) for p in pairs]
    gend = [jnp.exp(ce[p] - cs[p]) for p in pairs]
    g_c = [jnp.exp(ce[p]) for p in pairs]
    at = [a_ref[p] * jnp.exp(cs[p] - lw[p]) for p in pairs]
    rt = [r_ref[p] * g[p] for p in pairs]
    v_b = [bf(v_ref[p]) for p in pairs]
    at_b = [bf(at[p]) for p in pairs]
    rt_b = [bf(rt[p]) for p in pairs]
    bk_t = [cat0(bf(b_ref[p] * ginv[p]), bf(k_ref[p] * ginv[p])) for p in pairs]
    bh_b = [bf(b_ref[p] * gend[p]) for p in pairs]
    kh_b = [bf(k_ref[p] * gend[p]) for p in pairs]

    gram = [lax.dot_general(cat0(keep(head0, at_b[p]), keep(head0, rt_b[p]),
                                 keep(~head0, at_b[p]), keep(~head0, rt_b[p])), bk_t[p],
                            (((1,), (1,)), ((), ())), preferred_element_type=F32)
            for p in pairs]
    gb = {(p, h): gram[p][2 * SB * h:2 * SB * (h + 1)] for p, h in units}
    mab = {u: keep(strict, gb[u][:SB, :SB]) for u in units}
    mkk_b = {u: bf(cat0(keep(strict, gb[u][:SB, SB:]), keep(incl, gb[u][SB:, SB:]))) for u in units}
    mrb_b = {u: bf(keep(incl, gb[u][SB:, :SB])) for u in units}

    def fold(z):
        return sum(z[c * CHUNK:(c + 1) * CHUNK] for c in range(n_chunks))

    def unfold(zf):
        return keep(same, jnp.concatenate([zf] * n_chunks, axis=0))

    nf = {u: fold(mab[u]) for u in units}
    tf = {u: fold(eye_f) + nf[u] for u in units}
    full = {u: _split2(mab[u]) for u in units}
    nf = {u: mm(_split2(nf[u]), full[u]) for u in units}
    for k in (2, 4, 8):
        full = {u: _split2(unfold(nf[u])) for u in units}
        if k < 8:
            both_f = {u: mm(_split2(cat0(nf[u], tf[u])), full[u]) for u in units}
            nf = {u: both_f[u][:CHUNK] for u in units}
            tf = {u: tf[u] + both_f[u][CHUNK:] for u in units}
        else:
            tf = {u: tf[u] + mm(_split2(tf[u]), full[u]) for u in units}
    t = {u: unfold(tf[u]) for u in units}
    wk = {u: dot(mkk_b[u], v_b[u[0]]) for u in units}
    x = {u: dot(bf(t[u]), jnp.concatenate([at_b[u[0]], bf(wk[u][:SB])], axis=1)) for u in units}
    ry = {u: dot(mrb_b[u], bf(x[u])) for u in units}
    both = lambda f: [jnp.where(head0, f((p, 0)), f((p, 1))) for p in pairs]
    abar_b = [bf(z) for z in both(lambda u: x[u][:, :LANES])]
    vbar = both(lambda u: x[u][:, LANES:])
    rbar_b = [bf(z) for z in both(lambda u: rt[u[0]] + ry[u][:, :LANES])]
    ybar = both(lambda u: wk[u][SB:] + ry[u][:, LANES:])

    g_ct = [g_c[p].T for p in pairs]

    hbd = [h_ref[p] for p in pairs]
    for c in range(n_chunks):
        sl = slice(c * CHUNK, (c + 1) * CHUNK)
        yu = [dot(cat0(rbar_b[p][sl], abar_b[p][sl]), bf(hbd[p])) for p in pairs]
        for p in pairs:
            y_ref[p, sl, :] = yu[p][:CHUNK] + ybar[p][sl]
        upd = [lax.dot_general(cat0(bh_b[p][sl], kh_b[p][sl]),
                               cat0(bf(yu[p][CHUNK:] + vbar[p][sl]), v_b[p][sl]),
                               (((0,), (0,)), ((), ())), preferred_element_type=F32)
               for p in pairs]
        hbd = [jnp.broadcast_to(g_ct[p][:, c * CHUNK:c * CHUNK + 1], (LANES, LANES)) * hbd[p]
               + keep(same_head, upd[p]) for p in pairs]
    for p in pairs:
        h_ref[p] = hbd[p]


def _rwkv_chunk(r, lw, k, v, a, b, npg):
    n_pairs, l, _ = r.shape
    blk = pl.BlockSpec((npg, SB, LANES), lambda gi, ti: (gi, ti, 0))
    return pl.pallas_call(
        _rwkv_chunk_body,
        out_shape=jax.ShapeDtypeStruct((n_pairs, l, LANES), F32),
        grid=(n_pairs // npg, l // SB),
        in_specs=[blk] * 6 + [pl.BlockSpec((5, SB, SB), lambda gi, ti: (0, 0, 0))],
        out_specs=blk,
        scratch_shapes=[pltpu.VMEM((npg, LANES, LANES), F32)],
        compiler_params=_cparams(("arbitrary", "arbitrary")),
        name="rwkv_chunk",
    )(r, lw, k, v, a, b, _chunk_masks())


def _fox_attn_body(q_ref, k_ref, v_ref, kb_ref, *rest, q_row0, tk):
    o_ref, m_ref, acc_ref = rest[-3:]
    tq = q_ref.shape[0]
    row_first = q_row0 + pl.program_id(1) * tq

    m_ref[...] = jnp.full_like(m_ref, MASK_VALUE)
    acc_ref[...] = jnp.zeros_like(acc_ref)

    lane = lax.broadcasted_iota(jnp.int32, (1, LANES), 1)
    head0 = lane < HEAD_DIM
    row2 = lax.broadcasted_iota(jnp.int32, (2 * tq, LANES), 0)
    lane2 = lax.broadcasted_iota(jnp.int32, (2 * tq, LANES), 1)
    first_lane = jnp.where(row2 < tq, 0, BIAS_LANES)
    bias_on = (lane2 >= first_lane) & (lane2 < first_lane + BIAS_LANES)
    bias_q = jnp.where(bias_on, 1.0, 0.0).astype(BF16)

    def block(col_first, width, masked):
        rows = pl.ds(col_first, width)
        heads = range(N_HEADS)
        lanes = [slice(j * LANES, (j + 1) * LANES) for j in range(PAIRS)]
        s2 = []
        for ls in lanes:
            q = q_ref[:, ls]
            zero = jnp.zeros_like(q)
            q2 = jnp.concatenate([jnp.where(head0, q, zero), jnp.where(head0, zero, q)], axis=0)
            s2.append(lax.dot_general(jnp.concatenate([q2, bias_q], axis=1),
                                      jnp.concatenate([k_ref[rows, ls], kb_ref[rows, ls]], axis=1),
                                      (((1,), (1,)), ((), ())), preferred_element_type=F32))
        s = [s2[h // 2][(h % 2) * tq:(h % 2 + 1) * tq] for h in heads]
        if masked:
            causal = (lax.broadcasted_iota(jnp.int32, (tq, width), 1)
                      <= lax.broadcasted_iota(jnp.int32, (tq, width), 0))
            s = [jnp.where(causal, sh, MASK_VALUE) for sh in s]
        m_prev = [m_ref[h] for h in heads]
        def row_max(sh):
            tile_max = functools.reduce(jnp.maximum, [sh[:, c:c + LANES] for c in range(0, width, LANES)])
            return jnp.max(tile_max, axis=1, keepdims=True)
        m_new = [jnp.maximum(m_prev[h], row_max(s[h])) for h in heads]
        p = [jnp.exp2(s[h] - jnp.concatenate([m_new[h]] * (width // LANES), axis=1)).astype(BF16)
             for h in heads]
        v1 = []
        for ls in lanes:
            v = v_ref[rows, ls]
            one = jnp.ones_like(v)
            v1 += [jnp.where(head0, v, one), jnp.where(head0, one, v)]
        pv = [jnp.dot(p[h], v1[h], preferred_element_type=F32) for h in heads]
        for h in heads:
            acc_ref[h] = jnp.exp2(m_prev[h] - m_new[h]) * acc_ref[h] + pv[h]
            m_ref[h] = m_new[h]

    def full_block(kj, carry):
        block(pl.multiple_of(kj * tk, tk), tk, False)
        return carry

    n_full = row_first // tk
    lax.fori_loop(0, n_full, full_block, 0)
    if tk > tq >= LANES:
        def rest_block(i, carry):
            block(pl.multiple_of(n_full * tk + i * tq, tq), tq, False)
            return carry
        lax.fori_loop(0, (row_first - n_full * tk) // tq, rest_block, 0)
    diag_w = max(tq, LANES)
    block(pl.multiple_of(row_first, tq), diag_w, True)

    if o_ref.shape[0] > tq:
        o_ref[...] = jnp.zeros_like(o_ref)
    for j in range(PAIRS):
        a0, a1 = acc_ref[2 * j], acc_ref[2 * j + 1]
        o_ref[0:tq, j * LANES:(j + 1) * LANES] = jnp.where(
            head0, a0 / pltpu.roll(a0, HEAD_DIM, axis=1), a1 / pltpu.roll(a1, HEAD_DIM, axis=1))


def _fox_attn(qkvb, q_row0, n_rows, tq, tk, out_rows=None, into=None):
    b, l, _ = qkvb.shape
    out_rows = out_rows or tq
    assert q_row0 % tq == 0 and q_row0 % out_rows == 0
    assert tk % tq == 0 if tq >= LANES else (q_row0 % tk == 0 and n_rows == tq)
    assert q_row0 + n_rows - tq + max(tq, LANES) <= l
    qb0, ob0 = q_row0 // tq, q_row0 // out_rows
    kv_spec = lambda col: pl.BlockSpec((None, l, WIDTH), lambda bi, qi: (bi, 0, col))
    extra_in, extra_spec, alias = (), [], {}
    if into is not None:
        extra_in, extra_spec, alias = (into,), [pl.BlockSpec(memory_space=pl.ANY)], {4: 0}
    return pl.pallas_call(
        functools.partial(_fox_attn_body, q_row0=q_row0, tk=tk),
        out_shape=jax.ShapeDtypeStruct((b, l, WIDTH), F32),
        grid=(b, n_rows // tq),
        in_specs=[pl.BlockSpec((None, tq, WIDTH), lambda bi, qi: (bi, qb0 + qi, 0)),
                  kv_spec(1), kv_spec(2), kv_spec(3)] + extra_spec,
        out_specs=pl.BlockSpec((None, out_rows, WIDTH), lambda bi, qi: (bi, ob0 + qi, 0)),
        scratch_shapes=[pltpu.VMEM((N_HEADS, tq, LANES), F32), pltpu.VMEM((N_HEADS, tq, LANES), F32)],
        input_output_aliases=alias,
        compiler_params=_cparams(("arbitrary", "arbitrary")),
        name="fox_attn",
    )(qkvb, qkvb, qkvb, qkvb, *extra_in)


def _mixed_residual(h_ref, y_ref, bonus_ref, g_ref, o_ref, og_ref, gnw_ref, gnb_ref, ones_ref, wo_ref):
    ones = ones_ref[...]
    y = jnp.concatenate([y_ref[j] for j in range(PAIRS)], axis=1)
    mean = _dot_ones(y, ones) * (1.0 / HEAD_DIM)
    d = y - mean
    var = _dot_ones(d * d, ones) * (1.0 / HEAD_DIM)
    yn = d * lax.rsqrt(var + GN_EPS) * gnw_ref[...] + gnb_ref[...]
    y_rw = ((yn + bonus_ref[...]) * g_ref[...]).astype(BF16)
    y_fx = (o_ref[...] * _sigmoid(og_ref[...])).astype(BF16)
    return (h_ref[...]
            + jnp.dot(y_rw, wo_ref[0:WIDTH, :], preferred_element_type=F32)
            + jnp.dot(y_fx, wo_ref[WIDTH:2 * WIDTH, :], preferred_element_type=F32))


def _mix_ffn_body(h_ref, y_ref, bonus_ref, gate_ref, o_ref, og_ref, gnw_ref, gnb_ref, ones_ref, wo_ref,
                  g_ref, wu_ref, wg_ref, cwu_ref, cwg_ref, cbu_ref, cbg_ref, wout_ref,
                  out_ref, xn_ref, cu_ref, cg_ref, wu_s, wg_s, act_ref):
    tm = h_ref.shape[0]
    halo = wu_s.shape[0] - tm
    ti = pl.program_id(1)
    j = pl.program_id(2)

    def first_column_tile():
        x = _mixed_residual(h_ref, y_ref, bonus_ref, gate_ref, o_ref, og_ref, gnw_ref, gnb_ref,
                            ones_ref, wo_ref)
        ms = jnp.mean(x * x, axis=-1, keepdims=True)
        xn_ref[...] = (x * lax.rsqrt(ms + NORM_EPS) * g_ref[...]).astype(BF16)
        out_ref[...] = x

    if cu_ref.shape[0] == 1:
        first_column_tile()
    else:
        pl.when(j == 0)(first_column_tile)

    @pl.when(ti == 0)
    def _():
        cu_ref[j] = jnp.zeros(cu_ref.shape[1:], F32)
        cg_ref[j] = jnp.zeros(cg_ref.shape[1:], F32)

    xn = xn_ref[...]

    def project(w_ref, carry_ref, work):
        work[0:halo] = carry_ref[j]
        work[halo:] = jnp.dot(xn, w_ref[...], preferred_element_type=F32)
        carry_ref[j] = work[tm:]

    def conv(cs, cw_ref, cb_ref, work):
        cw = cw_ref[:, cs]
        taps = [work[pl.ds(halo - (CONV_W - 1) + i, tm), cs] for i in range(CONV_W)]
        return cb_ref[:, cs] + cw[0:1, :] * taps[0] + cw[1:2, :] * taps[1] + cw[2:3, :] * taps[2]

    project(wu_ref, cu_ref, wu_s)
    project(wg_ref, cg_ref, wg_s)
    tf = wu_ref.shape[1]
    for c0 in range(0, tf, 2 * MXU_TILE):
        cs = slice(c0, min(c0 + 2 * MXU_TILE, tf))
        gt = conv(cs, cwg_ref, cbg_ref, wg_s)
        act_ref[:, cs] = (gt * _sigmoid(gt) * conv(cs, cwu_ref, cbu_ref, wu_s)).astype(BF16)
    out_ref[...] += jnp.dot(act_ref[...], wout_ref[...], preferred_element_type=F32)


def _mix_ffn(h, y, bonus, gate, o_fx, og, gn_w, gn_b, ones512, wo_bf16,
             g, w_in_bf16, conv_w, conv_b, w_out_bf16, layer, tm, tf):
    b, l, d = h.shape
    d_ff = w_out_bf16.shape[1]
    nj = d_ff // tf
    cw = jnp.pad(conv_w, ((0, SUBLANES - CONV_W), (0, 0)))
    cb = conv_b.reshape(1, 2 * d_ff)
    once = dict(pipeline_mode=pl.Buffered(1))
    full = lambda shape: pl.BlockSpec(shape, lambda bi, ti, j: (0,) * len(shape), **once)
    wmode = once if nj == 1 else {}
    rows = pl.BlockSpec((None, tm, WIDTH), lambda bi, ti, j: (bi, ti, 0))
    return pl.pallas_call(
        _mix_ffn_body,
        out_shape=jax.ShapeDtypeStruct((b, l, d), F32),
        grid=(b, l // tm, nj),
        in_specs=[pl.BlockSpec((None, tm, d), lambda bi, ti, j: (bi, ti, 0)),
                  pl.BlockSpec((None, PAIRS, tm, LANES), lambda bi, ti, j: (bi, 0, ti, 0)),
                  rows, rows, rows, rows,
                  full((1, WIDTH)), full((1, WIDTH)), full(ones512.shape),
                  pl.BlockSpec((None, 2 * WIDTH, d), lambda bi, ti, j: (layer, 0, 0), **once),
                  full((1, d)),
                  pl.BlockSpec((None, d, tf), lambda bi, ti, j: (layer, 0, j), **wmode),
                  pl.BlockSpec((None, d, tf), lambda bi, ti, j: (layer, 0, nj + j), **wmode),
                  pl.BlockSpec((SUBLANES, tf), lambda bi, ti, j: (0, j), **wmode),
                  pl.BlockSpec((SUBLANES, tf), lambda bi, ti, j: (0, nj + j), **wmode),
                  pl.BlockSpec((1, tf), lambda bi, ti, j: (0, j), **wmode),
                  pl.BlockSpec((1, tf), lambda bi, ti, j: (0, nj + j), **wmode),
                  pl.BlockSpec((None, tf, d), lambda bi, ti, j: (layer, j, 0), **wmode)],
        out_specs=pl.BlockSpec((None, tm, d), lambda bi, ti, j: (bi, ti, 0)),
        scratch_shapes=[pltpu.VMEM((tm, d), BF16),
                        pltpu.VMEM((nj, SUBLANES, tf), F32), pltpu.VMEM((nj, SUBLANES, tf), F32),
                        pltpu.VMEM((SUBLANES + tm, tf), F32), pltpu.VMEM((SUBLANES + tm, tf), F32),
                        pltpu.VMEM((tm, tf), BF16)],
        compiler_params=_cparams(("arbitrary", "arbitrary", "arbitrary")),
        name="mix_ffn",
    )(h, y, bonus, gate, o_fx, og, gn_w, gn_b, ones512, wo_bf16,
      g.reshape(1, d), w_in_bf16, w_in_bf16, cw, cw, cb, cb, w_out_bf16)


def _pad_cols(w, n):
    return jnp.pad(w, ((0, 0), (0, n - w.shape[1])))


def _layer(h, prm, stacks, layer, l_real):
    (norm1_g, w_in, rw_mu, rw_w0, rw_w_up, rw_a0, rw_a_up, rw_g_up, rw_k_k, rw_k_a, rw_r_k,
     rw_gn_w, rw_gn_b, fx_b_f, fx_q_g, fx_k_g, norm2_g, ffn_conv_w, ffn_conv_b) = prm
    w_o, ffn_w_in, ffn_w_out = stacks
    b, l, d = h.shape
    rw_cols = 3 * WIDTH + DECAY_LORA + AAA_LORA + GATE_LORA

    w_all = jnp.concatenate([_pad_cols(w_in[:, :rw_cols], RW_PCOLS),
                             _pad_cols(w_in[:, rw_cols:], FX_PCOLS)], axis=1)
    mu_p = _pad_cols(rw_mu.reshape(1, rw_cols), RW_PCOLS)
    wcomb = jnp.zeros((LORA_WA, 2 * WIDTH), F32)
    wcomb = wcomb.at[:DECAY_LORA, :WIDTH].set(rw_w_up).at[DECAY_LORA:, WIDTH:].set(rw_a_up)
    w0a0 = jnp.concatenate([rw_w0, rw_a0]).reshape(1, 2 * WIDTH)
    gup_p = jnp.pad(rw_g_up, ((0, GATE_PAD - GATE_LORA), (0, 0)))
    ones512 = _group_ones(MXU_TILE, HEAD_DIM)
    gain = jnp.concatenate([jnp.tile(fx_q_g, N_HEADS) * (HEAD_DIM ** -0.5 * LOG2E),
                            jnp.tile(fx_k_g, N_HEADS)]).reshape(1, 2 * WIDTH)
    bf_p = jnp.pad(fx_b_f, (0, LANES - N_HEADS)).reshape(1, LANES)

    t1 = _pick(l, (544, 384, 128))
    t2 = _pick(l, (544, 384, 256, 128))

    r, lw, k, v, a, bb, g, bonus, og, qkvb = _proj_prep(
        h, norm1_g, w_all, mu_p, wcomb, w0a0, gup_p, rw_k_k.reshape(1, WIDTH),
        rw_k_a.reshape(1, WIDTH), rw_r_k.reshape(1, WIDTH), gain, bf_p, ones512, t1)

    pm = lambda z: z.reshape(b * PAIRS, l, LANES)
    y = _rwkv_chunk(pm(r), pm(lw), pm(k), pm(v), pm(a), pm(bb), npg=_pick(b * PAIRS, (16, 8, 4)))
    y = y.reshape(b, PAIRS, l, LANES)

    n_main = l // ATTN_BLOCK * ATTN_BLOCK
    o_fx = None
    if n_main:
        o_fx = _fox_attn(qkvb, 0, n_main, ATTN_BLOCK, ATTN_KEYS, into=jnp.zeros((b, l, WIDTH), F32))
    if l > n_main:
        tq_tail = min(-(-max(l_real - n_main, 1) // BF16_ROWS) * BF16_ROWS, l - n_main)
        tk_tail = ATTN_KEYS if n_main % ATTN_KEYS == 0 else ATTN_BLOCK
        o_fx = _fox_attn(qkvb, n_main, tq_tail, tq_tail, tk_tail, out_rows=l - n_main, into=o_fx)

    d_ff = ffn_w_out.shape[1]
    t3 = _pick(l, (544, 384, 128))
    return _mix_ffn(h, y, bonus, g, o_fx, og, rw_gn_w.reshape(1, WIDTH), rw_gn_b.reshape(1, WIDTH),
                    ones512, w_o, norm2_g, ffn_w_in, ffn_conv_w, ffn_conv_b, ffn_w_out, layer, t3, d_ff)


def kernel(x, meta, norm1_g, w_in, rw_mu, rw_w0, rw_w_up, rw_a0, rw_a_up, rw_g_up, rw_k_k, rw_k_a,
           rw_r_k, rw_gn_w, rw_gn_b, fx_b_f, fx_q_g, fx_k_g, w_o, norm2_g, ffn_w_in, ffn_conv_w,
           ffn_conv_b, ffn_w_out):
    b, seq, d = x.shape
    w_in, w_o, ffn_w_in, ffn_w_out = (w.astype(BF16) for w in (w_in, w_o, ffn_w_in, ffn_w_out))
    params = (norm1_g, w_in, rw_mu, rw_w0, rw_w_up, rw_a0, rw_a_up, rw_g_up, rw_k_k, rw_k_a, rw_r_k,
              rw_gn_w, rw_gn_b, fx_b_f, fx_q_g, fx_k_g, norm2_g, ffn_conv_w, ffn_conv_b)
    stacks = (w_o, ffn_w_in, ffn_w_out)
    l = N_META + seq
    lp = -(-l // SB) * SB
    meta_b = jnp.broadcast_to(meta[None].astype(x.dtype), (b, N_META, d))
    h = jnp.concatenate([meta_b, x, jnp.zeros((b, lp - l, d), x.dtype)], axis=1)
    for layer in range(norm1_g.shape[0]):
        h = _layer(h, tuple(p[layer] for p in params), stacks, layer, l)
    return h[:, N_META:l]
```

```python
import functools

import jax
import jax.numpy as jnp
from jax import lax
from jax.experimental import pallas as pl
from jax.experimental.pallas import tpu as pltpu

F32 = jnp.float32
BF16 = jnp.bfloat16

LANES = 128
SUBLANES = 8
BF16_ROWS = 16
MXU_TILE = 256

N_META = 16
HEAD_DIM = 64
N_HEADS = 8
WIDTH = N_HEADS * HEAD_DIM
DECAY_LORA = 64
AAA_LORA = 64
LORA_WA = DECAY_LORA + AAA_LORA
GATE_LORA = 160
GATE_PAD = 256
RW_PCOLS = 3 * WIDTH + LORA_WA + GATE_PAD
FX_PCOLS = 4 * WIDTH + LANES
CONV_W = 3
NORM_EPS = 1e-6
GN_EPS = HEAD_DIM * 1e-5
PAIRS = WIDTH // LANES
CHUNK = 16
SB = 128
ATTN_BLOCK = 256
ATTN_KEYS = 512
BIAS_LANES = 3
LOG2E = 1.4426950408889634
MASK_VALUE = -1e30
VMEM_LIMIT = 58 * 1024 * 1024


def _pick(n, cands):
    for c in cands:
        if n % c == 0:
            return c
    raise ValueError(f"no tile for {n} in {cands}")


def _cparams(sem):
    return pltpu.CompilerParams(dimension_semantics=sem, vmem_limit_bytes=VMEM_LIMIT)


def _sigmoid(x):
    return 1.0 / (1.0 + jnp.exp(-x))


def _softplus(x):
    return jnp.maximum(x, 0.0) + jnp.log(1.0 + jnp.exp(-jnp.abs(x)))


def _split2(x):
    hi = x.astype(BF16)
    lo = (x - hi.astype(F32)).astype(BF16)
    return hi, lo


def _split3(x):
    p1 = x.astype(BF16)
    rem = x - p1.astype(F32)
    p2 = rem.astype(BF16)
    p3 = (rem - p2.astype(F32)).astype(BF16)
    return p1, p2, p3


def _dot_ones(x, ones_bf16):
    xb = x.astype(BF16)
    t = ones_bf16.shape[0]
    return jnp.concatenate([jnp.dot(xb[:, c:c + t], ones_bf16, preferred_element_type=F32)
                            for c in range(0, x.shape[1], t)], axis=1)


def _mm3(a_hi, a_lo, b_hi, b_lo, dims=((1,), (0,))):
    dn = (dims, ((), ()))
    m_axis = 1 - dims[0][0]
    m = a_hi.shape[m_axis]
    d = lax.dot_general(jnp.concatenate([a_hi, a_lo], axis=m_axis), b_hi, dn,
                        preferred_element_type=F32)
    return d[:m] + d[m:] + lax.dot_general(a_hi, b_lo, dn, preferred_element_type=F32)


def _dot_bf16(a, b):
    return jnp.dot(a.astype(BF16), b.astype(BF16), preferred_element_type=F32)


def _group_ones(n, group):
    i = jnp.arange(n) // group
    return (i[:, None] == i[None, :]).astype(BF16)


def _project(xn, w_ref, p_ref, n_chunk=512):
    n = p_ref.shape[-1]
    for c in range(0, n, n_chunk):
        e = min(c + n_chunk, n)
        p_ref[:, c:e] = jnp.dot(xn, w_ref[:, c:e], preferred_element_type=F32)


def _rwkv_features(p_ref, mu_ref, wcomb_ref, w0a0_ref, gup_ref, kk_ref, ka_ref, rk_ref, ones_ref,
                   r_o, lw_o, k_o, v_o, a_o, b_o, g_o, bonus_o, carry_ref):
    t2 = p_ref.shape[0]
    p = p_ref[...]
    rows = lax.broadcasted_iota(jnp.int32, p.shape, 0)
    prev = jnp.where(rows == 0, carry_ref[...], pltpu.roll(p, 1, axis=0))
    carry_ref[...] = p[t2 - 1:t2, :]
    x = p + mu_ref[...] * (prev - p)

    r = x[:, 0:WIDTH]
    k = x[:, WIDTH:2 * WIDTH]
    v = x[:, 2 * WIDTH:3 * WIDTH]
    wa = x[:, 3 * WIDTH:3 * WIDTH + LORA_WA]
    gd = x[:, 3 * WIDTH + LORA_WA:]

    lane = lax.broadcasted_iota(jnp.int32, wa.shape, 1)
    xa = jnp.where(lane < DECAY_LORA, jnp.tanh(wa), wa)
    pre = _dot_bf16(xa, wcomb_ref[...]) + w0a0_ref[...]
    w_log = -_softplus(-pre[:, :WIDTH]) - 0.5
    log_decay = -jnp.exp(w_log)
    a = _sigmoid(pre[:, WIDTH:])
    g = _dot_bf16(_sigmoid(gd), gup_ref[...])

    ones = ones_ref[...]
    kk = k * kk_ref[...]
    ss = _dot_ones(kk * kk, ones)
    kk = kk / jnp.maximum(jnp.sqrt(ss), 1e-12)
    k2 = k * (1.0 + (a - 1.0) * ka_ref[...])
    bonus = _dot_ones(r * k2 * rk_ref[...], ones) * v

    for j in range(PAIRS):
        ls = slice(j * LANES, (j + 1) * LANES)
        r_o[j] = r[:, ls]
        lw_o[j] = log_decay[:, ls]
        k_o[j] = k2[:, ls]
        v_o[j] = v[:, ls].astype(v_o.dtype)
        a_o[j] = -kk[:, ls]
        b_o[j] = (kk * a)[:, ls]
    g_o[...] = g
    bonus_o[...] = bonus


def _fox_features(p_ref, gain_ref, bf_ref, ones_ref, place_ref, og_o, qkvb_o, carry_ref):
    t2 = p_ref.shape[0]
    ones = ones_ref[...]
    for h0 in (0, WIDTH):
        x = p_ref[:, h0:h0 + WIDTH]
        ms = _dot_ones(x * x, ones) * (1.0 / HEAD_DIM)
        qkvb_o[:, h0:h0 + WIDTH] = (x * lax.rsqrt(ms + NORM_EPS) * gain_ref[:, h0:h0 + WIDTH]).astype(BF16)
    qkvb_o[:, 2 * WIDTH:3 * WIDTH] = p_ref[:, 2 * WIDTH:3 * WIDTH].astype(BF16)
    og_o[...] = p_ref[:, 3 * WIDTH:4 * WIDTH]

    logf = -_softplus(-(p_ref[:, 4 * WIDTH:] + bf_ref[...]))
    ri = lax.broadcasted_iota(jnp.int32, (LANES, LANES), 0)
    ci = lax.broadcasted_iota(jnp.int32, (LANES, LANES), 1)
    tri = (ci <= ri).astype(BF16)
    carry = carry_ref[...]
    blocks = []
    for r0 in range(0, t2, LANES):
        n = min(LANES, t2 - r0)
        cb = sum(jnp.dot(tri[:n, :n], piece, preferred_element_type=F32)
                 for piece in _split3(logf[r0:r0 + n])) + carry
        carry = cb[n - 1:n, :]
        blocks.append(cb)
    c = jnp.concatenate(blocks, axis=0)
    carry_ref[...] = carry
    pieces = jnp.concatenate(_split3(c * (-LOG2E)), axis=1)
    qkvb_o[:, 3 * WIDTH:] = jnp.dot(pieces, place_ref[...], preferred_element_type=F32).astype(BF16)


def _proj_prep_body(h_ref, n1_ref, w_ref, mu_ref, wcomb_ref, w0a0_ref, gup_ref, kk_ref,
                    ka_ref, rk_ref, gain_ref, bf_ref, ones_ref, place_ref,
                    r_o, lw_o, k_o, v_o, a_o, b_o, g_o, bonus_o, og_o, qkvb_o,
                    p_ref, shift_ref, csum_ref):
    prw_ref = p_ref.at[:, 0:RW_PCOLS]
    pfx_ref = p_ref.at[:, RW_PCOLS:RW_PCOLS + FX_PCOLS]
    @pl.when(pl.program_id(1) == 0)
    def _():
        shift_ref[...] = jnp.zeros_like(shift_ref)
        csum_ref[...] = jnp.zeros_like(csum_ref)

    x = h_ref[...]
    ms = jnp.mean(x * x, axis=-1, keepdims=True)
    xn = (x * lax.rsqrt(ms + NORM_EPS) * n1_ref[...]).astype(BF16)
    _project(xn, w_ref, p_ref)
    _rwkv_features(prw_ref, mu_ref, wcomb_ref, w0a0_ref, gup_ref, kk_ref, ka_ref, rk_ref, ones_ref,
                   r_o, lw_o, k_o, v_o, a_o, b_o, g_o, bonus_o, shift_ref)
    _fox_features(pfx_ref, gain_ref, bf_ref, ones_ref, place_ref, og_o, qkvb_o, csum_ref)


def _bias_placement():
    src = jnp.arange(3 * LANES)
    piece, head = src // LANES, src % LANES
    dst = (head // 2) * LANES + BIAS_LANES * (head % 2) + piece
    hit = (dst[:, None] == jnp.arange(WIDTH)[None, :]) & (head[:, None] < N_HEADS)
    return hit.astype(BF16)


def _proj_prep(h, norm1_g, w_all, mu_p, wcomb, w0a0, gup_p, k_k, k_a, r_k, gain, bf_p, ones512, tm):
    b, l, d = h.shape
    full = lambda shape: pl.BlockSpec(shape, lambda bi, ti: (0,) * len(shape), pipeline_mode=pl.Buffered(1))
    pair = lambda dt: (jax.ShapeDtypeStruct((b, PAIRS, l, LANES), dt),
                       pl.BlockSpec((None, PAIRS, tm, LANES), lambda bi, ti: (bi, 0, ti, 0)))
    rows = lambda n, dt: (jax.ShapeDtypeStruct((b, l, n), dt),
                          pl.BlockSpec((None, tm, n), lambda bi, ti: (bi, ti, 0)))
    outs = [pair(F32), pair(F32), pair(F32), pair(BF16), pair(F32), pair(F32),
            rows(WIDTH, F32), rows(WIDTH, F32), rows(WIDTH, F32), rows(4 * WIDTH, BF16)]
    consts = (norm1_g.reshape(1, d), w_all, mu_p, wcomb, w0a0, gup_p, k_k, k_a, r_k, gain, bf_p,
              ones512, _bias_placement())
    return pl.pallas_call(
        _proj_prep_body,
        out_shape=[o[0] for o in outs],
        grid=(b, l // tm),
        in_specs=[pl.BlockSpec((None, tm, d), lambda bi, ti: (bi, ti, 0))] + [full(c.shape) for c in consts],
        out_specs=[o[1] for o in outs],
        scratch_shapes=[pltpu.VMEM((tm, RW_PCOLS + FX_PCOLS), F32),
                        pltpu.VMEM((1, RW_PCOLS), F32), pltpu.VMEM((1, LANES), F32)],
        compiler_params=_cparams(("arbitrary", "arbitrary")),
        name="proj_prep",
    )(h, *consts)


def _chunk_masks():
    ti = jnp.arange(SB)[:, None]
    si = jnp.arange(SB)[None, :]
    same = (ti // CHUNK) == (si // CHUNK)
    masks = [same, same & (si <= ti), same & (si < ti), ti == si, (ti < HEAD_DIM) == (si < HEAD_DIM)]
    return jnp.stack(masks).astype(F32)


def _rwkv_chunk_body(r_ref, lw_ref, k_ref, v_ref, a_ref, b_ref, masks_ref, y_ref, h_ref):
    npg = r_ref.shape[0]
    n_chunks = SB // CHUNK

    @pl.when(pl.program_id(1) == 0)
    def _():
        h_ref[...] = jnp.zeros_like(h_ref)

    same, incl, strict, _, same_head = (masks_ref[i] != 0.0 for i in range(5))
    eye_f = masks_ref[3]
    head0 = lax.broadcasted_iota(jnp.int32, (SB, LANES), 1) < HEAD_DIM
    tri = masks_ref[1].astype(BF16)
    bf = lambda z: z.astype(BF16)
    dot = functools.partial(jnp.dot, preferred_element_type=F32)

    def keep(mask, x):
        return jnp.where(mask, x, jnp.zeros_like(x))

    pairs = range(npg)
    units = [(p, h) for p in pairs for h in range(2)]
    cat0 = lambda *xs: jnp.concatenate(xs, axis=0)
    mm = lambda a, b, **kw: _mm3(*a, *b, **kw)

    lw = [lw_ref[p] for p in pairs]
    cs = [sum(dot(tri, piece) for piece in _split2(lw[p])) for p in pairs]
    ce = [cat0(*[jnp.broadcast_to(cs[p][(c + 1) * CHUNK - 1:(c + 1) * CHUNK, :], (CHUNK, LANES))
                 for c in range(n_chunks)]) for p in pairs]
    g = [jnp.exp(cs[p]) for p in pairs]
    ginv = [jnp.exp(-cs[p]) for p in pairs]
    gend = [jnp.exp(ce[p] - cs[p]) for p in pairs]
    g_c = [jnp.exp(ce[p]) for p in pairs]
    at = [a_ref[p] * jnp.exp(cs[p] - lw[p]) for p in pairs]
    rt = [r_ref[p] * g[p] for p in pairs]
    v_b = [bf(v_ref[p]) for p in pairs]
    at_b = [bf(at[p]) for p in pairs]
    rt_b = [bf(rt[p]) for p in pairs]
    bk_t = [cat0(bf(b_ref[p] * ginv[p]), bf(k_ref[p] * ginv[p])) for p in pairs]
    bh_b = [bf(b_ref[p] * gend[p]) for p in pairs]
    kh_b = [bf(k_ref[p] * gend[p]) for p in pairs]

    gram = [lax.dot_general(cat0(keep(head0, at_b[p]), keep(head0, rt_b[p]),
                                 keep(~head0, at_b[p]), keep(~head0, rt_b[p])), bk_t[p],
                            (((1,), (1,)), ((), ())), preferred_element_type=F32)
            for p in pairs]
    gb = {(p, h): gram[p][2 * SB * h:2 * SB * (h + 1)] for p, h in units}
    mab = {u: keep(strict, gb[u][:SB, :SB]) for u in units}
    mkk_b = {u: bf(cat0(keep(strict, gb[u][:SB, SB:]), keep(incl, gb[u][SB:, SB:]))) for u in units}
    mrb_b = {u: bf(keep(incl, gb[u][SB:, :SB])) for u in units}

    def fold(z):
        return sum(z[c * CHUNK:(c + 1) * CHUNK] for c in range(n_chunks))

    def unfold(zf):
        return keep(same, jnp.concatenate([zf] * n_chunks, axis=0))

    nf = {u: fold(mab[u]) for u in units}
    tf = {u: fold(eye_f) + nf[u] for u in units}
    full = {u: _split2(mab[u]) for u in units}
    nf = {u: mm(_split2(nf[u]), full[u]) for u in units}
    for k in (2, 4, 8):
        full = {u: _split2(unfold(nf[u])) for u in units}
        if k < 8:
            both_f = {u: mm(_split2(cat0(nf[u], tf[u])), full[u]) for u in units}
            nf = {u: both_f[u][:CHUNK] for u in units}
            tf = {u: tf[u] + both_f[u][CHUNK:] for u in units}
        else:
            tf = {u: tf[u] + mm(_split2(tf[u]), full[u]) for u in units}
    t = {u: unfold(tf[u]) for u in units}
    wk = {u: dot(mkk_b[u], v_b[u[0]]) for u in units}
    x = {u: dot(bf(t[u]), jnp.concatenate([at_b[u[0]], bf(wk[u][:SB])], axis=1)) for u in units}
    ry = {u: dot(mrb_b[u], bf(x[u])) for u in units}
    both = lambda f: [jnp.where(head0, f((p, 0)), f((p, 1))) for p in pairs]
    abar_b = [bf(z) for z in both(lambda u: x[u][:, :LANES])]
    vbar = both(lambda u: x[u][:, LANES:])
    rbar_b = [bf(z) for z in both(lambda u: rt[u[0]] + ry[u][:, :LANES])]
    ybar = both(lambda u: wk[u][SB:] + ry[u][:, LANES:])

    g_ct = [g_c[p].T for p in pairs]

    hbd = [h_ref[p] for p in pairs]
    for c in range(n_chunks):
        sl = slice(c * CHUNK, (c + 1) * CHUNK)
        yu = [dot(cat0(rbar_b[p][sl], abar_b[p][sl]), bf(hbd[p])) for p in pairs]
        for p in pairs:
            y_ref[p, sl, :] = yu[p][:CHUNK] + ybar[p][sl]
        upd = [lax.dot_general(cat0(bh_b[p][sl], kh_b[p][sl]),
                               cat0(bf(yu[p][CHUNK:] + vbar[p][sl]), v_b[p][sl]),
                               (((0,), (0,)), ((), ())), preferred_element_type=F32)
               for p in pairs]
        hbd = [jnp.broadcast_to(g_ct[p][:, c * CHUNK:c * CHUNK + 1], (LANES, LANES)) * hbd[p]
               + keep(same_head, upd[p]) for p in pairs]
    for p in pairs:
        h_ref[p] = hbd[p]


def _rwkv_chunk(r, lw, k, v, a, b, npg):
    n_pairs, l, _ = r.shape
    blk = pl.BlockSpec((npg, SB, LANES), lambda gi, ti: (gi, ti, 0))
    return pl.pallas_call(
        _rwkv_chunk_body,
        out_shape=jax.ShapeDtypeStruct((n_pairs, l, LANES), F32),
        grid=(n_pairs // npg, l // SB),
        in_specs=[blk] * 6 + [pl.BlockSpec((5, SB, SB), lambda gi, ti: (0, 0, 0))],
        out_specs=blk,
        scratch_shapes=[pltpu.VMEM((npg, LANES, LANES), F32)],
        compiler_params=_cparams(("arbitrary", "arbitrary")),
        name="rwkv_chunk",
    )(r, lw, k, v, a, b, _chunk_masks())


def _fox_attn_body(q_ref, k_ref, v_ref, kb_ref, *rest, q_row0, tk):
    o_ref, m_ref, acc_ref = rest[-3:]
    tq = q_ref.shape[0]
    row_first = q_row0 + pl.program_id(1) * tq

    m_ref[...] = jnp.full_like(m_ref, MASK_VALUE)
    acc_ref[...] = jnp.zeros_like(acc_ref)

    lane = lax.broadcasted_iota(jnp.int32, (1, LANES), 1)
    head0 = lane < HEAD_DIM
    row2 = lax.broadcasted_iota(jnp.int32, (2 * tq, LANES), 0)
    lane2 = lax.broadcasted_iota(jnp.int32, (2 * tq, LANES), 1)
    first_lane = jnp.where(row2 < tq, 0, BIAS_LANES)
    bias_on = (lane2 >= first_lane) & (lane2 < first_lane + BIAS_LANES)
    bias_q = jnp.where(bias_on, 1.0, 0.0).astype(BF16)

    def block(col_first, width, masked):
        rows = pl.ds(col_first, width)
        heads = range(N_HEADS)
        lanes = [slice(j * LANES, (j + 1) * LANES) for j in range(PAIRS)]
        s2 = []
        for ls in lanes:
            q = q_ref[:, ls]
            zero = jnp.zeros_like(q)
            q2 = jnp.concatenate([jnp.where(head0, q, zero), jnp.where(head0, zero, q)], axis=0)
            s2.append(lax.dot_general(jnp.concatenate([q2, bias_q], axis=1),
                                      jnp.concatenate([k_ref[rows, ls], kb_ref[rows, ls]], axis=1),
                                      (((1,), (1,)), ((), ())), preferred_element_type=F32))
        s = [s2[h // 2][(h % 2) * tq:(h % 2 + 1) * tq] for h in heads]
        if masked:
            causal = (lax.broadcasted_iota(jnp.int32, (tq, width), 1)
                      <= lax.broadcasted_iota(jnp.int32, (tq, width), 0))
            s = [jnp.where(causal, sh, MASK_VALUE) for sh in s]
        m_prev = [m_ref[h] for h in heads]
        def row_max(sh):
            tile_max = functools.reduce(jnp.maximum, [sh[:, c:c + LANES] for c in range(0, width, LANES)])
            return jnp.max(tile_max, axis=1, keepdims=True)
        m_new = [jnp.maximum(m_prev[h], row_max(s[h])) for h in heads]
        p = [jnp.exp2(s[h] - jnp.concatenate([m_new[h]] * (width // LANES), axis=1)).astype(BF16)
             for h in heads]
        v1 = []
        for ls in lanes:
            v = v_ref[rows, ls]
            one = jnp.ones_like(v)
            v1 += [jnp.where(head0, v, one), jnp.where(head0, one, v)]
        pv = [jnp.dot(p[h], v1[h], preferred_element_type=F32) for h in heads]
        for h in heads:
            acc_ref[h] = jnp.exp2(m_prev[h] - m_new[h]) * acc_ref[h] + pv[h]
            m_ref[h] = m_new[h]

    def full_block(kj, carry):
        block(pl.multiple_of(kj * tk, tk), tk, False)
        return carry

    n_full = row_first // tk
    lax.fori_loop(0, n_full, full_block, 0)
    if tk > tq >= LANES:
        def rest_block(i, carry):
            block(pl.multiple_of(n_full * tk + i * tq, tq), tq, False)
            return carry
        lax.fori_loop(0, (row_first - n_full * tk) // tq, rest_block, 0)
    diag_w = max(tq, LANES)
    block(pl.multiple_of(row_first, tq), diag_w, True)

    if o_ref.shape[0] > tq:
        o_ref[...] = jnp.zeros_like(o_ref)
    for j in range(PAIRS):
        a0, a1 = acc_ref[2 * j], acc_ref[2 * j + 1]
        o_ref[0:tq, j * LANES:(j + 1) * LANES] = jnp.where(
            head0, a0 / pltpu.roll(a0, HEAD_DIM, axis=1), a1 / pltpu.roll(a1, HEAD_DIM, axis=1))


def _fox_attn(qkvb, q_row0, n_rows, tq, tk, out_rows=None, into=None):
    b, l, _ = qkvb.shape
    out_rows = out_rows or tq
    assert q_row0 % tq == 0 and q_row0 % out_rows == 0
    assert tk % tq == 0 if tq >= LANES else (q_row0 % tk == 0 and n_rows == tq)
    assert q_row0 + n_rows - tq + max(tq, LANES) <= l
    qb0, ob0 = q_row0 // tq, q_row0 // out_rows
    kv_spec = lambda col: pl.BlockSpec((None, l, WIDTH), lambda bi, qi: (bi, 0, col))
    extra_in, extra_spec, alias = (), [], {}
    if into is not None:
        extra_in, extra_spec, alias = (into,), [pl.BlockSpec(memory_space=pl.ANY)], {4: 0}
    return pl.pallas_call(
        functools.partial(_fox_attn_body, q_row0=q_row0, tk=tk),
        out_shape=jax.ShapeDtypeStruct((b, l, WIDTH), F32),
        grid=(b, n_rows // tq),
        in_specs=[pl.BlockSpec((None, tq, WIDTH), lambda bi, qi: (bi, qb0 + qi, 0)),
                  kv_spec(1), kv_spec(2), kv_spec(3)] + extra_spec,
        out_specs=pl.BlockSpec((None, out_rows, WIDTH), lambda bi, qi: (bi, ob0 + qi, 0)),
        scratch_shapes=[pltpu.VMEM((N_HEADS, tq, LANES), F32), pltpu.VMEM((N_HEADS, tq, LANES), F32)],
        input_output_aliases=alias,
        compiler_params=_cparams(("arbitrary", "arbitrary")),
        name="fox_attn",
    )(qkvb, qkvb, qkvb, qkvb, *extra_in)


def _mixed_residual(h_ref, y_ref, bonus_ref, g_ref, o_ref, og_ref, gnw_ref, gnb_ref, ones_ref, wo_ref):
    ones = ones_ref[...]
    y = jnp.concatenate([y_ref[j] for j in range(PAIRS)], axis=1)
    mean = _dot_ones(y, ones) * (1.0 / HEAD_DIM)
    d = y - mean
    var = _dot_ones(d * d, ones) * (1.0 / HEAD_DIM)
    yn = d * lax.rsqrt(var + GN_EPS) * gnw_ref[...] + gnb_ref[...]
    y_rw = ((yn + bonus_ref[...]) * g_ref[...]).astype(BF16)
    y_fx = (o_ref[...] * _sigmoid(og_ref[...])).astype(BF16)
    return (h_ref[...]
            + jnp.dot(y_rw, wo_ref[0:WIDTH, :], preferred_element_type=F32)
            + jnp.dot(y_fx, wo_ref[WIDTH:2 * WIDTH, :], preferred_element_type=F32))


def _mix_ffn_body(h_ref, y_ref, bonus_ref, gate_ref, o_ref, og_ref, gnw_ref, gnb_ref, ones_ref, wo_ref,
                  g_ref, wu_ref, wg_ref, cwu_ref, cwg_ref, cbu_ref, cbg_ref, wout_ref,
                  out_ref, xn_ref, cu_ref, cg_ref, wu_s, wg_s, act_ref):
    tm = h_ref.shape[0]
    halo = wu_s.shape[0] - tm
    ti = pl.program_id(1)
    j = pl.program_id(2)

    def first_column_tile():
        x = _mixed_residual(h_ref, y_ref, bonus_ref, gate_ref, o_ref, og_ref, gnw_ref, gnb_ref,
                            ones_ref, wo_ref)
        ms = jnp.mean(x * x, axis=-1, keepdims=True)
        xn_ref[...] = (x * lax.rsqrt(ms + NORM_EPS) * g_ref[...]).astype(BF16)
        out_ref[...] = x

    if cu_ref.shape[0] == 1:
        first_column_tile()
    else:
        pl.when(j == 0)(first_column_tile)

    @pl.when(ti == 0)
    def _():
        cu_ref[j] = jnp.zeros(cu_ref.shape[1:], F32)
        cg_ref[j] = jnp.zeros(cg_ref.shape[1:], F32)

    xn = xn_ref[...]

    def project(w_ref, carry_ref, work):
        work[0:halo] = carry_ref[j]
        work[halo:] = jnp.dot(xn, w_ref[...], preferred_element_type=F32)
        carry_ref[j] = work[tm:]

    def conv(cs, cw_ref, cb_ref, work):
        cw = cw_ref[:, cs]
        taps = [work[pl.ds(halo - (CONV_W - 1) + i, tm), cs] for i in range(CONV_W)]
        return cb_ref[:, cs] + cw[0:1, :] * taps[0] + cw[1:2, :] * taps[1] + cw[2:3, :] * taps[2]

    project(wu_ref, cu_ref, wu_s)
    project(wg_ref, cg_ref, wg_s)
    tf = wu_ref.shape[1]
    for c0 in range(0, tf, 2 * MXU_TILE):
        cs = slice(c0, min(c0 + 2 * MXU_TILE, tf))
        gt = conv(cs, cwg_ref, cbg_ref, wg_s)
        act_ref[:, cs] = (gt * _sigmoid(gt) * conv(cs, cwu_ref, cbu_ref, wu_s)).astype(BF16)
    out_ref[...] += jnp.dot(act_ref[...], wout_ref[...], preferred_element_type=F32)


def _mix_ffn(h, y, bonus, gate, o_fx, og, gn_w, gn_b, ones512, wo_bf16,
             g, w_in_bf16, conv_w, conv_b, w_out_bf16, layer, tm, tf):
    b, l, d = h.shape
    d_ff = w_out_bf16.shape[1]
    nj = d_ff // tf
    cw = jnp.pad(conv_w, ((0, SUBLANES - CONV_W), (0, 0)))
    cb = conv_b.reshape(1, 2 * d_ff)
    once = dict(pipeline_mode=pl.Buffered(1))
    full = lambda shape: pl.BlockSpec(shape, lambda bi, ti, j: (0,) * len(shape), **once)
    wmode = once if nj == 1 else {}
    rows = pl.BlockSpec((None, tm, WIDTH), lambda bi, ti, j: (bi, ti, 0))
    return pl.pallas_call(
        _mix_ffn_body,
        out_shape=jax.ShapeDtypeStruct((b, l, d), F32),
        grid=(b, l // tm, nj),
        in_specs=[pl.BlockSpec((None, tm, d), lambda bi, ti, j: (bi, ti, 0)),
                  pl.BlockSpec((None, PAIRS, tm, LANES), lambda bi, ti, j: (bi, 0, ti, 0)),
                  rows, rows, rows, rows,
                  full((1, WIDTH)), full((1, WIDTH)), full(ones512.shape),
                  pl.BlockSpec((None, 2 * WIDTH, d), lambda bi, ti, j: (layer, 0, 0), **once),
                  full((1, d)),
                  pl.BlockSpec((None, d, tf), lambda bi, ti, j: (layer, 0, j), **wmode),
                  pl.BlockSpec((None, d, tf), lambda bi, ti, j: (layer, 0, nj + j), **wmode),
                  pl.BlockSpec((SUBLANES, tf), lambda bi, ti, j: (0, j), **wmode),
                  pl.BlockSpec((SUBLANES, tf), lambda bi, ti, j: (0, nj + j), **wmode),
                  pl.BlockSpec((1, tf), lambda bi, ti, j: (0, j), **wmode),
                  pl.BlockSpec((1, tf), lambda bi, ti, j: (0, nj + j), **wmode),
                  pl.BlockSpec((None, tf, d), lambda bi, ti, j: (layer, j, 0), **wmode)],
        out_specs=pl.BlockSpec((None, tm, d), lambda bi, ti, j: (bi, ti, 0)),
        scratch_shapes=[pltpu.VMEM((tm, d), BF16),
                        pltpu.VMEM((nj, SUBLANES, tf), F32), pltpu.VMEM((nj, SUBLANES, tf), F32),
                        pltpu.VMEM((SUBLANES + tm, tf), F32), pltpu.VMEM((SUBLANES + tm, tf), F32),
                        pltpu.VMEM((tm, tf), BF16)],
        compiler_params=_cparams(("arbitrary", "arbitrary", "arbitrary")),
        name="mix_ffn",
    )(h, y, bonus, gate, o_fx, og, gn_w, gn_b, ones512, wo_bf16,
      g.reshape(1, d), w_in_bf16, w_in_bf16, cw, cw, cb, cb, w_out_bf16)


def _pad_cols(w, n):
    return jnp.pad(w, ((0, 0), (0, n - w.shape[1])))


def _layer(h, prm, stacks, layer, l_real, o_buf):
    (norm1_g, w_in, rw_mu, rw_w0, rw_w_up, rw_a0, rw_a_up, rw_g_up, rw_k_k, rw_k_a, rw_r_k,
     rw_gn_w, rw_gn_b, fx_b_f, fx_q_g, fx_k_g, norm2_g, ffn_conv_w, ffn_conv_b) = prm
    w_o, ffn_w_in, ffn_w_out = stacks
    b, l, d = h.shape
    rw_cols = 3 * WIDTH + DECAY_LORA + AAA_LORA + GATE_LORA

    w_all = jnp.concatenate([_pad_cols(w_in[:, :rw_cols], RW_PCOLS),
                             _pad_cols(w_in[:, rw_cols:], FX_PCOLS)], axis=1)
    mu_p = _pad_cols(rw_mu.reshape(1, rw_cols), RW_PCOLS)
    wcomb = jnp.zeros((LORA_WA, 2 * WIDTH), F32)
    wcomb = wcomb.at[:DECAY_LORA, :WIDTH].set(rw_w_up).at[DECAY_LORA:, WIDTH:].set(rw_a_up)
    w0a0 = jnp.concatenate([rw_w0, rw_a0]).reshape(1, 2 * WIDTH)
    gup_p = jnp.pad(rw_g_up, ((0, GATE_PAD - GATE_LORA), (0, 0)))
    ones512 = _group_ones(MXU_TILE, HEAD_DIM)
    gain = jnp.concatenate([jnp.tile(fx_q_g, N_HEADS) * (HEAD_DIM ** -0.5 * LOG2E),
                            jnp.tile(fx_k_g, N_HEADS)]).reshape(1, 2 * WIDTH)
    bf_p = jnp.pad(fx_b_f, (0, LANES - N_HEADS)).reshape(1, LANES)

    t1 = _pick(l, (544, 384, 128))
    t2 = _pick(l, (544, 384, 256, 128))

    r, lw, k, v, a, bb, g, bonus, og, qkvb = _proj_prep(
        h, norm1_g, w_all, mu_p, wcomb, w0a0, gup_p, rw_k_k.reshape(1, WIDTH),
        rw_k_a.reshape(1, WIDTH), rw_r_k.reshape(1, WIDTH), gain, bf_p, ones512, t1)

    pm = lambda z: z.reshape(b * PAIRS, l, LANES)
    y = _rwkv_chunk(pm(r), pm(lw), pm(k), pm(v), pm(a), pm(bb), npg=_pick(b * PAIRS, (16, 8, 4)))
    y = y.reshape(b, PAIRS, l, LANES)

    n_main = l // ATTN_BLOCK * ATTN_BLOCK
    o_fx = o_buf
    if n_main:
        o_fx = _fox_attn(qkvb, 0, n_main, ATTN_BLOCK, ATTN_KEYS, into=o_fx)
    if l > n_main:
        tq_tail = min(-(-max(l_real - n_main, 1) // BF16_ROWS) * BF16_ROWS, l - n_main)
        tk_tail = ATTN_KEYS if n_main % ATTN_KEYS == 0 else ATTN_BLOCK
        o_fx = _fox_attn(qkvb, n_main, tq_tail, tq_tail, tk_tail, out_rows=l - n_main, into=o_fx)

    d_ff = ffn_w_out.shape[1]
    t3 = _pick(l, (544, 384, 128))
    h = _mix_ffn(h, y, bonus, g, o_fx, og, rw_gn_w.reshape(1, WIDTH), rw_gn_b.reshape(1, WIDTH),
                 ones512, w_o, norm2_g, ffn_w_in, ffn_conv_w, ffn_conv_b, ffn_w_out, layer, t3, d_ff)
    return h, o_fx


def kernel(x, meta, norm1_g, w_in, rw_mu, rw_w0, rw_w_up, rw_a0, rw_a_up, rw_g_up, rw_k_k, rw_k_a,
           rw_r_k, rw_gn_w, rw_gn_b, fx_b_f, fx_q_g, fx_k_g, w_o, norm2_g, ffn_w_in, ffn_conv_w,
           ffn_conv_b, ffn_w_out):
    b, seq, d = x.shape
    w_in, w_o, ffn_w_in, ffn_w_out = (w.astype(BF16) for w in (w_in, w_o, ffn_w_in, ffn_w_out))
    params = (norm1_g, w_in, rw_mu, rw_w0, rw_w_up, rw_a0, rw_a_up, rw_g_up, rw_k_k, rw_k_a, rw_r_k,
              rw_gn_w, rw_gn_b, fx_b_f, fx_q_g, fx_k_g, norm2_g, ffn_conv_w, ffn_conv_b)
    stacks = (w_o, ffn_w_in, ffn_w_out)
    l = N_META + seq
    lp = -(-l // SB) * SB
    meta_b = jnp.broadcast_to(meta[None].astype(x.dtype), (b, N_META, d))
    h = jnp.concatenate([meta_b, x, jnp.zeros((b, lp - l, d), x.dtype)], axis=1)
    o_buf = jnp.zeros((b, lp, WIDTH), F32)
    for layer in range(norm1_g.shape[0]):
        h, o_buf = _layer(h, tuple(p[layer] for p in params), stacks, layer, l, o_buf)
    return h[:, N_META:l]
```

```python
import functools

import jax
import jax.numpy as jnp
from jax import lax
from jax.experimental import pallas as pl
from jax.experimental.pallas import tpu as pltpu

F32 = jnp.float32
BF16 = jnp.bfloat16

LANES = 128
SUBLANES = 8
BF16_ROWS = 16
MXU_TILE = 256

N_META = 16
HEAD_DIM = 64
N_HEADS = 8
WIDTH = N_HEADS * HEAD_DIM
DECAY_LORA = 64
AAA_LORA = 64
LORA_WA = DECAY_LORA + AAA_LORA
GATE_LORA = 160
GATE_PAD = 256
RW_COLS = 3 * WIDTH + LORA_WA + GATE_LORA
RW_PCOLS = 3 * WIDTH + LORA_WA + GATE_PAD
FX_PCOLS = 4 * WIDTH + LANES
CONV_W = 3
NORM_EPS = 1e-6
GN_EPS = HEAD_DIM * 1e-5
PAIRS = WIDTH // LANES
CHUNK = 16
SB = 128
ATTN_BLOCK = 256
ATTN_KEYS = 512
BIAS_LANES = 3
LOG2E = 1.4426950408889634
MASK_VALUE = -1e30
VMEM_LIMIT = 58 * 1024 * 1024


def _pick(n, cands):
    for c in cands:
        if n % c == 0:
            return c
    raise ValueError(f"no tile for {n} in {cands}")


def _cparams(sem):
    return pltpu.CompilerParams(dimension_semantics=sem, vmem_limit_bytes=VMEM_LIMIT)


def _sigmoid(x):
    return 1.0 / (1.0 + jnp.exp(-x))


def _softplus(x):
    return jnp.maximum(x, 0.0) + jnp.log(1.0 + jnp.exp(-jnp.abs(x)))


def _split2(x):
    hi = x.astype(BF16)
    lo = (x - hi.astype(F32)).astype(BF16)
    return hi, lo


def _split3(x):
    p1 = x.astype(BF16)
    rem = x - p1.astype(F32)
    p2 = rem.astype(BF16)
    p3 = (rem - p2.astype(F32)).astype(BF16)
    return p1, p2, p3


def _dot_ones(x, ones_bf16):
    xb = x.astype(BF16)
    t = ones_bf16.shape[0]
    return jnp.concatenate([jnp.dot(xb[:, c:c + t], ones_bf16, preferred_element_type=F32)
                            for c in range(0, x.shape[1], t)], axis=1)


def _mm3(a_hi, a_lo, b_hi, b_lo, dims=((1,), (0,))):
    dn = (dims, ((), ()))
    m_axis = 1 - dims[0][0]
    m = a_hi.shape[m_axis]
    d = lax.dot_general(jnp.concatenate([a_hi, a_lo], axis=m_axis), b_hi, dn,
                        preferred_element_type=F32)
    return d[:m] + d[m:] + lax.dot_general(a_hi, b_lo, dn, preferred_element_type=F32)


def _dot_bf16(a, b):
    return jnp.dot(a.astype(BF16), b.astype(BF16), preferred_element_type=F32)


def _group_ones(n, group):
    i = jnp.arange(n) // group
    return (i[:, None] == i[None, :]).astype(BF16)


def _project(xn, w_ref, p_ref, n_chunk=512):
    n = p_ref.shape[-1]
    for c in range(0, n, n_chunk):
        e = min(c + n_chunk, n)
        p_ref[:, c:e] = jnp.dot(xn, w_ref[:, c:e], preferred_element_type=F32)


def _rwkv_features(p_ref, mu_ref, wcomb_ref, w0a0_ref, gup_ref, kk_ref, ka_ref, rk_ref, ones_ref,
                   r_o, lw_o, k_o, v_o, a_o, b_o, g_o, bonus_o, carry_ref):
    t2 = p_ref.shape[0]
    p = p_ref[...]
    rows = lax.broadcasted_iota(jnp.int32, p.shape, 0)
    prev = jnp.where(rows == 0, carry_ref[...], pltpu.roll(p, 1, axis=0))
    carry_ref[...] = p[t2 - 1:t2, :]
    x = p + mu_ref[...] * (prev - p)

    r = x[:, 0:WIDTH]
    k = x[:, WIDTH:2 * WIDTH]
    v = x[:, 2 * WIDTH:3 * WIDTH]
    wa = x[:, 3 * WIDTH:3 * WIDTH + LORA_WA]
    gd = x[:, 3 * WIDTH + LORA_WA:]

    lane = lax.broadcasted_iota(jnp.int32, wa.shape, 1)
    xa = jnp.where(lane < DECAY_LORA, jnp.tanh(wa), wa)
    pre = _dot_bf16(xa, wcomb_ref[...]) + w0a0_ref[...]
    w_log = -_softplus(-pre[:, :WIDTH]) - 0.5
    log_decay = -jnp.exp(w_log)
    a = _sigmoid(pre[:, WIDTH:])
    g = _dot_bf16(_sigmoid(gd), gup_ref[...])

    ones = ones_ref[...]
    kk = k * kk_ref[...]
    ss = _dot_ones(kk * kk, ones)
    kk = kk / jnp.maximum(jnp.sqrt(ss), 1e-12)
    k2 = k * (1.0 + (a - 1.0) * ka_ref[...])
    bonus = _dot_ones(r * k2 * rk_ref[...], ones) * v

    for j in range(PAIRS):
        ls = slice(j * LANES, (j + 1) * LANES)
        r_o[j] = r[:, ls]
        lw_o[j] = log_decay[:, ls]
        k_o[j] = k2[:, ls]
        v_o[j] = v[:, ls].astype(v_o.dtype)
        a_o[j] = -kk[:, ls]
        b_o[j] = (kk * a)[:, ls]
    g_o[...] = g
    bonus_o[...] = bonus


def _fox_features(p_ref, gain_ref, bf_ref, ones_ref, place_ref, og_o, qkvb_o, carry_ref):
    t2 = p_ref.shape[0]
    ones = ones_ref[...]
    for h0 in (0, WIDTH):
        x = p_ref[:, h0:h0 + WIDTH]
        ms = _dot_ones(x * x, ones) * (1.0 / HEAD_DIM)
        qkvb_o[:, h0:h0 + WIDTH] = (x * lax.rsqrt(ms + NORM_EPS) * gain_ref[:, h0:h0 + WIDTH]).astype(BF16)
    qkvb_o[:, 2 * WIDTH:3 * WIDTH] = p_ref[:, 2 * WIDTH:3 * WIDTH].astype(BF16)
    og_o[...] = p_ref[:, 3 * WIDTH:4 * WIDTH]

    logf = -_softplus(-(p_ref[:, 4 * WIDTH:] + bf_ref[...]))
    ri = lax.broadcasted_iota(jnp.int32, (LANES, LANES), 0)
    ci = lax.broadcasted_iota(jnp.int32, (LANES, LANES), 1)
    tri = (ci <= ri).astype(BF16)
    carry = carry_ref[...]
    blocks = []
    for r0 in range(0, t2, LANES):
        n = min(LANES, t2 - r0)
        cb = sum(jnp.dot(tri[:n, :n], piece, preferred_element_type=F32)
                 for piece in _split3(logf[r0:r0 + n])) + carry
        carry = cb[n - 1:n, :]
        blocks.append(cb)
    c = jnp.concatenate(blocks, axis=0)
    carry_ref[...] = carry
    pieces = jnp.concatenate(_split3(c * (-LOG2E)), axis=1)
    qkvb_o[:, 3 * WIDTH:] = jnp.dot(pieces, place_ref[...], preferred_element_type=F32).astype(BF16)


def _proj_prep_body(h_ref, n1_ref, w_ref, mu_ref, wcomb_ref, w0a0_ref, gup_ref, kk_ref,
                    ka_ref, rk_ref, gain_ref, bf_ref, ones_ref, place_ref,
                    r_o, lw_o, k_o, v_o, a_o, b_o, g_o, bonus_o, og_o, qkvb_o,
                    p_ref, shift_ref, csum_ref):
    prw_ref = p_ref.at[:, 0:RW_PCOLS]
    pfx_ref = p_ref.at[:, RW_PCOLS:RW_PCOLS + FX_PCOLS]
    @pl.when(pl.program_id(1) == 0)
    def _():
        shift_ref[...] = jnp.zeros_like(shift_ref)
        csum_ref[...] = jnp.zeros_like(csum_ref)

    x = h_ref[...]
    ms = jnp.mean(x * x, axis=-1, keepdims=True)
    xn = (x * lax.rsqrt(ms + NORM_EPS) * n1_ref[...]).astype(BF16)
    _project(xn, w_ref, p_ref)
    _rwkv_features(prw_ref, mu_ref, wcomb_ref, w0a0_ref, gup_ref, kk_ref, ka_ref, rk_ref, ones_ref,
                   r_o, lw_o, k_o, v_o, a_o, b_o, g_o, bonus_o, shift_ref)
    _fox_features(pfx_ref, gain_ref, bf_ref, ones_ref, place_ref, og_o, qkvb_o, csum_ref)


def _bias_placement():
    src = jnp.arange(3 * LANES)
    piece, head = src // LANES, src % LANES
    dst = (head // 2) * LANES + BIAS_LANES * (head % 2) + piece
    hit = (dst[:, None] == jnp.arange(WIDTH)[None, :]) & (head[:, None] < N_HEADS)
    return hit.astype(BF16)


def _proj_prep(h, norm1_g, w_all, layer, mu_p, wcomb, w0a0, gup_p, k_k, k_a, r_k, gain, bf_p, ones512, tm):
    b, l, d = h.shape
    once = dict(pipeline_mode=pl.Buffered(1))
    full = lambda shape: pl.BlockSpec(shape, lambda bi, ti: (0,) * len(shape), **once)
    w_spec = pl.BlockSpec((None,) + w_all.shape[1:], lambda bi, ti: (layer, 0, 0), **once)
    pair = lambda dt: (jax.ShapeDtypeStruct((b, PAIRS, l, LANES), dt),
                       pl.BlockSpec((None, PAIRS, tm, LANES), lambda bi, ti: (bi, 0, ti, 0)))
    rows = lambda n, dt: (jax.ShapeDtypeStruct((b, l, n), dt),
                          pl.BlockSpec((None, tm, n), lambda bi, ti: (bi, ti, 0)))
    outs = [pair(F32), pair(F32), pair(F32), pair(BF16), pair(F32), pair(F32),
            rows(WIDTH, F32), rows(WIDTH, F32), rows(WIDTH, F32), rows(4 * WIDTH, BF16)]
    consts = (mu_p, wcomb, w0a0, gup_p, k_k, k_a, r_k, gain, bf_p, ones512, _bias_placement())
    return pl.pallas_call(
        _proj_prep_body,
        out_shape=[o[0] for o in outs],
        grid=(b, l // tm),
        in_specs=[pl.BlockSpec((None, tm, d), lambda bi, ti: (bi, ti, 0)), full((1, d)), w_spec]
                 + [full(c.shape) for c in consts],
        out_specs=[o[1] for o in outs],
        scratch_shapes=[pltpu.VMEM((tm, RW_PCOLS + FX_PCOLS), F32),
                        pltpu.VMEM((1, RW_PCOLS), F32), pltpu.VMEM((1, LANES), F32)],
        compiler_params=_cparams(("arbitrary", "arbitrary")),
        name="proj_prep",
    )(h, norm1_g.reshape(1, d), w_all, *consts)


def _chunk_masks():
    ti = jnp.arange(SB)[:, None]
    si = jnp.arange(SB)[None, :]
    same = (ti // CHUNK) == (si // CHUNK)
    masks = [same, same & (si <= ti), same & (si < ti), ti == si, (ti < HEAD_DIM) == (si < HEAD_DIM)]
    return jnp.stack(masks).astype(F32)


def _rwkv_chunk_body(r_ref, lw_ref, k_ref, v_ref, a_ref, b_ref, masks_ref, y_ref, h_ref):
    npg = r_ref.shape[0]
    n_chunks = SB // CHUNK

    @pl.when(pl.program_id(1) == 0)
    def _():
        h_ref[...] = jnp.zeros_like(h_ref)

    same, incl, strict, _, same_head = (masks_ref[i] != 0.0 for i in range(5))
    eye_f = masks_ref[3]
    head0 = lax.broadcasted_iota(jnp.int32, (SB, LANES), 1) < HEAD_DIM
    tri = masks_ref[1].astype(BF16)
    bf = lambda z: z.astype(BF16)
    dot = functools.partial(jnp.dot, preferred_element_type=F32)

    def keep(mask, x):
        return jnp.where(mask, x, jnp.zeros_like(x))

    pairs = range(npg)
    units = [(p, h) for p in pairs for h in range(2)]
    cat0 = lambda *xs: jnp.concatenate(xs, axis=0)
    mm = lambda a, b, **kw: _mm3(*a, *b, **kw)

    lw = [lw_ref[p] for p in pairs]
    cs = [sum(dot(tri, piece) for piece in _split2(lw[p])) for p in pairs]
    ce = [cat0(*[jnp.broadcast_to(cs[p][(c + 1) * CHUNK - 1:(c + 1) * CHUNK, :], (CHUNK, LANES))
                 for c in range(n_chunks)]) for p in pairs]
    g = [jnp.exp(cs[p]) for p in pairs]
    ginv = [jnp.exp(-cs[p]) for p in pairs]
    gend = [jnp.exp(ce[p] - cs[p]) for p in pairs]
    g_c = [jnp.exp(ce[p]) for p in pairs]
    at = [a_ref[p] * jnp.exp(cs[p] - lw[p]) for p in pairs]
    rt = [r_ref[p] * g[p] for p in pairs]
    v_b = [bf(v_ref[p]) for p in pairs]
    at_b = [bf(at[p]) for p in pairs]
    rt_b = [bf(rt[p]) for p in pairs]
    bk_t = [cat0(bf(b_ref[p] * ginv[p]), bf(k_ref[p] * ginv[p])) for p in pairs]
    bh_b = [bf(b_ref[p] * gend[p]) for p in pairs]
    kh_b = [bf(k_ref[p] * gend[p]) for p in pairs]

    gram = [lax.dot_general(cat0(keep(head0, at_b[p]), keep(head0, rt_b[p]),
                                 keep(~head0, at_b[p]), keep(~head0, rt_b[p])), bk_t[p],
                            (((1,), (1,)), ((), ())), preferred_element_type=F32)
            for p in pairs]
    gb = {(p, h): gram[p][2 * SB * h:2 * SB * (h + 1)] for p, h in units}
    mab = {u: keep(strict, gb[u][:SB, :SB]) for u in units}
    mkk_b = {u: bf(cat0(keep(strict, gb[u][:SB, SB:]), keep(incl, gb[u][SB:, SB:]))) for u in units}
    mrb_b = {u: bf(keep(incl, gb[u][SB:, :SB])) for u in units}

    def fold(z):
        return sum(z[c * CHUNK:(c + 1) * CHUNK] for c in range(n_chunks))

    def unfold(zf):
        return keep(same, jnp.concatenate([zf] * n_chunks, axis=0))

    nf = {u: fold(mab[u]) for u in units}
    tf = {u: fold(eye_f) + nf[u] for u in units}
    full = {u: _split2(mab[u]) for u in units}
    nf = {u: mm(_split2(nf[u]), full[u]) for u in units}
    for k in (2, 4, 8):
        full = {u: _split2(unfold(nf[u])) for u in units}
        if k < 8:
            both_f = {u: mm(_split2(cat0(nf[u], tf[u])), full[u]) for u in units}
            nf = {u: both_f[u][:CHUNK] for u in units}
            tf = {u: tf[u] + both_f[u][CHUNK:] for u in units}
        else:
            tf = {u: tf[u] + mm(_split2(tf[u]), full[u]) for u in units}
    t = {u: unfold(tf[u]) for u in units}
    wk = {u: dot(mkk_b[u], v_b[u[0]]) for u in units}
    x = {u: dot(bf(t[u]), jnp.concatenate([at_b[u[0]], bf(wk[u][:SB])], axis=1)) for u in units}
    ry = {u: dot(mrb_b[u], bf(x[u])) for u in units}
    both = lambda f: [jnp.where(head0, f((p, 0)), f((p, 1))) for p in pairs]
    abar_b = [bf(z) for z in both(lambda u: x[u][:, :LANES])]
    vbar = both(lambda u: x[u][:, LANES:])
    rbar_b = [bf(z) for z in both(lambda u: rt[u[0]] + ry[u][:, :LANES])]
    ybar = both(lambda u: wk[u][SB:] + ry[u][:, LANES:])

    g_ct = [g_c[p].T for p in pairs]

    hbd = [h_ref[p] for p in pairs]
    for c in range(n_chunks):
        sl = slice(c * CHUNK, (c + 1) * CHUNK)
        yu = [dot(cat0(rbar_b[p][sl], abar_b[p][sl]), bf(hbd[p])) for p in pairs]
        for p in pairs:
            y_ref[p, sl, :] = yu[p][:CHUNK] + ybar[p][sl]
        upd = [lax.dot_general(cat0(bh_b[p][sl], kh_b[p][sl]),
                               cat0(bf(yu[p][CHUNK:] + vbar[p][sl]), v_b[p][sl]),
                               (((0,), (0,)), ((), ())), preferred_element_type=F32)
               for p in pairs]
        hbd = [jnp.broadcast_to(g_ct[p][:, c * CHUNK:c * CHUNK + 1], (LANES, LANES)) * hbd[p]
               + keep(same_head, upd[p]) for p in pairs]
    for p in pairs:
        h_ref[p] = hbd[p]


def _rwkv_chunk(r, lw, k, v, a, b, npg):
    n_pairs, l, _ = r.shape
    blk = pl.BlockSpec((npg, SB, LANES), lambda gi, ti: (gi, ti, 0))
    return pl.pallas_call(
        _rwkv_chunk_body,
        out_shape=jax.ShapeDtypeStruct((n_pairs, l, LANES), F32),
        grid=(n_pairs // npg, l // SB),
        in_specs=[blk] * 6 + [pl.BlockSpec((5, SB, SB), lambda gi, ti: (0, 0, 0))],
        out_specs=blk,
        scratch_shapes=[pltpu.VMEM((npg, LANES, LANES), F32)],
        compiler_params=_cparams(("arbitrary", "arbitrary")),
        name="rwkv_chunk",
    )(r, lw, k, v, a, b, _chunk_masks())


def _fox_attn_body(q_ref, k_ref, v_ref, kb_ref, *rest, q_row0, tk):
    o_ref, m_ref, acc_ref = rest[-3:]
    tq = q_ref.shape[0]
    row_first = q_row0 + pl.program_id(1) * tq

    m_ref[...] = jnp.full_like(m_ref, MASK_VALUE)
    acc_ref[...] = jnp.zeros_like(acc_ref)

    lane = lax.broadcasted_iota(jnp.int32, (1, LANES), 1)
    head0 = lane < HEAD_DIM
    row2 = lax.broadcasted_iota(jnp.int32, (2 * tq, LANES), 0)
    lane2 = lax.broadcasted_iota(jnp.int32, (2 * tq, LANES), 1)
    first_lane = jnp.where(row2 < tq, 0, BIAS_LANES)
    bias_on = (lane2 >= first_lane) & (lane2 < first_lane + BIAS_LANES)
    bias_q = jnp.where(bias_on, 1.0, 0.0).astype(BF16)

    def block(col_first, width, masked):
        rows = pl.ds(col_first, width)
        heads = range(N_HEADS)
        lanes = [slice(j * LANES, (j + 1) * LANES) for j in range(PAIRS)]
        s2 = []
        for ls in lanes:
            q = q_ref[:, ls]
            zero = jnp.zeros_like(q)
            q2 = jnp.concatenate([jnp.where(head0, q, zero), jnp.where(head0, zero, q)], axis=0)
            s2.append(lax.dot_general(jnp.concatenate([q2, bias_q], axis=1),
                                      jnp.concatenate([k_ref[rows, ls], kb_ref[rows, ls]], axis=1),
                                      (((1,), (1,)), ((), ())), preferred_element_type=F32))
        s = [s2[h // 2][(h % 2) * tq:(h % 2 + 1) * tq] for h in heads]
        if masked:
            causal = (lax.broadcasted_iota(jnp.int32, (tq, width), 1)
                      <= lax.broadcasted_iota(jnp.int32, (tq, width), 0))
            s = [jnp.where(causal, sh, MASK_VALUE) for sh in s]
        m_prev = [m_ref[h] for h in heads]
        def row_max(sh):
            tile_max = functools.reduce(jnp.maximum, [sh[:, c:c + LANES] for c in range(0, width, LANES)])
            return jnp.max(tile_max, axis=1, keepdims=True)
        m_new = [jnp.maximum(m_prev[h], row_max(s[h])) for h in heads]
        p = [jnp.exp2(s[h] - jnp.concatenate([m_new[h]] * (width // LANES), axis=1)).astype(BF16)
             for h in heads]
        v1 = []
        for ls in lanes:
            v = v_ref[rows, ls]
            one = jnp.ones_like(v)
            v1 += [jnp.where(head0, v, one), jnp.where(head0, one, v)]
        pv = [jnp.dot(p[h], v1[h], preferred_element_type=F32) for h in heads]
        for h in heads:
            acc_ref[h] = jnp.exp2(m_prev[h] - m_new[h]) * acc_ref[h] + pv[h]
            m_ref[h] = m_new[h]

    def full_block(kj, carry):
        block(pl.multiple_of(kj * tk, tk), tk, False)
        return carry

    n_full = row_first // tk
    lax.fori_loop(0, n_full, full_block, 0)
    if tk > tq >= LANES:
        def rest_block(i, carry):
            block(pl.multiple_of(n_full * tk + i * tq, tq), tq, False)
            return carry
        lax.fori_loop(0, (row_first - n_full * tk) // tq, rest_block, 0)
    diag_w = max(tq, LANES)
    block(pl.multiple_of(row_first, tq), diag_w, True)

    if o_ref.shape[0] > tq:
        o_ref[...] = jnp.zeros_like(o_ref)
    for j in range(PAIRS):
        a0, a1 = acc_ref[2 * j], acc_ref[2 * j + 1]
        o_ref[0:tq, j * LANES:(j + 1) * LANES] = jnp.where(
            head0, a0 / pltpu.roll(a0, HEAD_DIM, axis=1), a1 / pltpu.roll(a1, HEAD_DIM, axis=1))


def _fox_attn(qkvb, q_row0, n_rows, tq, tk, out_rows=None, into=None):
    b, l, _ = qkvb.shape
    out_rows = out_rows or tq
    assert q_row0 % tq == 0 and q_row0 % out_rows == 0
    assert tk % tq == 0 if tq >= LANES else (q_row0 % tk == 0 and n_rows == tq)
    assert q_row0 + n_rows - tq + max(tq, LANES) <= l
    qb0, ob0 = q_row0 // tq, q_row0 // out_rows
    kv_spec = lambda col: pl.BlockSpec((None, l, WIDTH), lambda bi, qi: (bi, 0, col))
    extra_in, extra_spec, alias = (), [], {}
    if into is not None:
        extra_in, extra_spec, alias = (into,), [pl.BlockSpec(memory_space=pl.ANY)], {4: 0}
    return pl.pallas_call(
        functools.partial(_fox_attn_body, q_row0=q_row0, tk=tk),
        out_shape=jax.ShapeDtypeStruct((b, l, WIDTH), F32),
        grid=(b, n_rows // tq),
        in_specs=[pl.BlockSpec((None, tq, WIDTH), lambda bi, qi: (bi, qb0 + qi, 0)),
                  kv_spec(1), kv_spec(2), kv_spec(3)] + extra_spec,
        out_specs=pl.BlockSpec((None, out_rows, WIDTH), lambda bi, qi: (bi, ob0 + qi, 0)),
        scratch_shapes=[pltpu.VMEM((N_HEADS, tq, LANES), F32), pltpu.VMEM((N_HEADS, tq, LANES), F32)],
        input_output_aliases=alias,
        compiler_params=_cparams(("arbitrary", "arbitrary")),
        name="fox_attn",
    )(qkvb, qkvb, qkvb, qkvb, *extra_in)


def _mixed_residual(h_ref, y_ref, bonus_ref, g_ref, o_ref, og_ref, gnw_ref, gnb_ref, ones_ref, wo_ref):
    ones = ones_ref[...]
    y = jnp.concatenate([y_ref[j] for j in range(PAIRS)], axis=1)
    mean = _dot_ones(y, ones) * (1.0 / HEAD_DIM)
    d = y - mean
    var = _dot_ones(d * d, ones) * (1.0 / HEAD_DIM)
    yn = d * lax.rsqrt(var + GN_EPS) * gnw_ref[...] + gnb_ref[...]
    y_rw = ((yn + bonus_ref[...]) * g_ref[...]).astype(BF16)
    y_fx = (o_ref[...] * _sigmoid(og_ref[...])).astype(BF16)
    return (h_ref[...]
            + jnp.dot(y_rw, wo_ref[0:WIDTH, :], preferred_element_type=F32)
            + jnp.dot(y_fx, wo_ref[WIDTH:2 * WIDTH, :], preferred_element_type=F32))


def _mix_ffn_body(h_ref, y_ref, bonus_ref, gate_ref, o_ref, og_ref, gnw_ref, gnb_ref, ones_ref, wo_ref,
                  g_ref, wu_ref, wg_ref, cwu_ref, cwg_ref, cbu_ref, cbg_ref, wout_ref,
                  out_ref, xn_ref, cu_ref, cg_ref, wu_s, wg_s, act_ref):
    tm = h_ref.shape[0]
    halo = wu_s.shape[0] - tm
    ti = pl.program_id(1)
    j = pl.program_id(2)

    def first_column_tile():
        x = _mixed_residual(h_ref, y_ref, bonus_ref, gate_ref, o_ref, og_ref, gnw_ref, gnb_ref,
                            ones_ref, wo_ref)
        ms = jnp.mean(x * x, axis=-1, keepdims=True)
        xn_ref[...] = (x * lax.rsqrt(ms + NORM_EPS) * g_ref[...]).astype(BF16)
        out_ref[...] = x

    if cu_ref.shape[0] == 1:
        first_column_tile()
    else:
        pl.when(j == 0)(first_column_tile)

    @pl.when(ti == 0)
    def _():
        cu_ref[j] = jnp.zeros(cu_ref.shape[1:], F32)
        cg_ref[j] = jnp.zeros(cg_ref.shape[1:], F32)

    xn = xn_ref[...]

    def project(w_ref, carry_ref, work):
        work[0:halo] = carry_ref[j]
        work[halo:] = jnp.dot(xn, w_ref[...], preferred_element_type=F32)
        carry_ref[j] = work[tm:]

    def conv(cs, cw_ref, cb_ref, work):
        cw = cw_ref[:, cs]
        taps = [work[pl.ds(halo - (CONV_W - 1) + i, tm), cs] for i in range(CONV_W)]
        return cb_ref[:, cs] + cw[0:1, :] * taps[0] + cw[1:2, :] * taps[1] + cw[2:3, :] * taps[2]

    project(wu_ref, cu_ref, wu_s)
    project(wg_ref, cg_ref, wg_s)
    tf = wu_ref.shape[1]
    for c0 in range(0, tf, 2 * MXU_TILE):
        cs = slice(c0, min(c0 + 2 * MXU_TILE, tf))
        gt = conv(cs, cwg_ref, cbg_ref, wg_s)
        act_ref[:, cs] = (gt * _sigmoid(gt) * conv(cs, cwu_ref, cbu_ref, wu_s)).astype(BF16)
    out_ref[...] += jnp.dot(act_ref[...], wout_ref[...], preferred_element_type=F32)


def _mix_ffn(h, y, bonus, gate, o_fx, og, gn_w, gn_b, ones512, wo_bf16,
             g, w_in_bf16, conv_w, conv_b, w_out_bf16, layer, tm, tf):
    b, l, d = h.shape
    d_ff = w_out_bf16.shape[1]
    nj = d_ff // tf
    cw = jnp.pad(conv_w, ((0, SUBLANES - CONV_W), (0, 0)))
    cb = conv_b.reshape(1, 2 * d_ff)
    once = dict(pipeline_mode=pl.Buffered(1))
    full = lambda shape: pl.BlockSpec(shape, lambda bi, ti, j: (0,) * len(shape), **once)
    wmode = once if nj == 1 else {}
    rows = pl.BlockSpec((None, tm, WIDTH), lambda bi, ti, j: (bi, ti, 0))
    return pl.pallas_call(
        _mix_ffn_body,
        out_shape=jax.ShapeDtypeStruct((b, l, d), F32),
        grid=(b, l // tm, nj),
        in_specs=[pl.BlockSpec((None, tm, d), lambda bi, ti, j: (bi, ti, 0)),
                  pl.BlockSpec((None, PAIRS, tm, LANES), lambda bi, ti, j: (bi, 0, ti, 0)),
                  rows, rows, rows, rows,
                  full((1, WIDTH)), full((1, WIDTH)), full(ones512.shape),
                  pl.BlockSpec((None, 2 * WIDTH, d), lambda bi, ti, j: (layer, 0, 0), **once),
                  full((1, d)),
                  pl.BlockSpec((None, d, tf), lambda bi, ti, j: (layer, 0, j), **wmode),
                  pl.BlockSpec((None, d, tf), lambda bi, ti, j: (layer, 0, nj + j), **wmode),
                  pl.BlockSpec((SUBLANES, tf), lambda bi, ti, j: (0, j), **wmode),
                  pl.BlockSpec((SUBLANES, tf), lambda bi, ti, j: (0, nj + j), **wmode),
                  pl.BlockSpec((1, tf), lambda bi, ti, j: (0, j), **wmode),
                  pl.BlockSpec((1, tf), lambda bi, ti, j: (0, nj + j), **wmode),
                  pl.BlockSpec((None, tf, d), lambda bi, ti, j: (layer, j, 0), **wmode)],
        out_specs=pl.BlockSpec((None, tm, d), lambda bi, ti, j: (bi, ti, 0)),
        scratch_shapes=[pltpu.VMEM((tm, d), BF16),
                        pltpu.VMEM((nj, SUBLANES, tf), F32), pltpu.VMEM((nj, SUBLANES, tf), F32),
                        pltpu.VMEM((SUBLANES + tm, tf), F32), pltpu.VMEM((SUBLANES + tm, tf), F32),
                        pltpu.VMEM((tm, tf), BF16)],
        compiler_params=_cparams(("arbitrary", "arbitrary", "arbitrary")),
        name="mix_ffn",
    )(h, y, bonus, gate, o_fx, og, gn_w, gn_b, ones512, wo_bf16,
      g.reshape(1, d), w_in_bf16, w_in_bf16, cw, cw, cb, cb, w_out_bf16)


def _pad_cols(w, n):
    return jnp.pad(w, ((0, 0),) * (w.ndim - 1) + ((0, n - w.shape[-1]),))


def _layer(h, prm, stacks, layer, l_real, o_buf):
    (norm1_g, rw_mu, rw_w0, rw_w_up, rw_a0, rw_a_up, rw_g_up, rw_k_k, rw_k_a, rw_r_k,
     rw_gn_w, rw_gn_b, fx_b_f, fx_q_g, fx_k_g, norm2_g, ffn_conv_w, ffn_conv_b) = prm
    w_all, w_o, ffn_w_in, ffn_w_out = stacks
    b, l, d = h.shape

    mu_p = _pad_cols(rw_mu.reshape(1, RW_COLS), RW_PCOLS)
    wcomb = jnp.zeros((LORA_WA, 2 * WIDTH), F32)
    wcomb = wcomb.at[:DECAY_LORA, :WIDTH].set(rw_w_up).at[DECAY_LORA:, WIDTH:].set(rw_a_up)
    w0a0 = jnp.concatenate([rw_w0, rw_a0]).reshape(1, 2 * WIDTH)
    gup_p = jnp.pad(rw_g_up, ((0, GATE_PAD - GATE_LORA), (0, 0)))
    ones512 = _group_ones(MXU_TILE, HEAD_DIM)
    gain = jnp.concatenate([jnp.tile(fx_q_g, N_HEADS) * (HEAD_DIM ** -0.5 * LOG2E),
                            jnp.tile(fx_k_g, N_HEADS)]).reshape(1, 2 * WIDTH)
    bf_p = jnp.pad(fx_b_f, (0, LANES - N_HEADS)).reshape(1, LANES)

    t1 = _pick(l, (544, 384, 128))
    t2 = _pick(l, (544, 384, 256, 128))

    r, lw, k, v, a, bb, g, bonus, og, qkvb = _proj_prep(
        h, norm1_g, w_all, layer, mu_p, wcomb, w0a0, gup_p, rw_k_k.reshape(1, WIDTH),
        rw_k_a.reshape(1, WIDTH), rw_r_k.reshape(1, WIDTH), gain, bf_p, ones512, t1)

    pm = lambda z: z.reshape(b * PAIRS, l, LANES)
    y = _rwkv_chunk(pm(r), pm(lw), pm(k), pm(v), pm(a), pm(bb), npg=_pick(b * PAIRS, (16, 8, 4)))
    y = y.reshape(b, PAIRS, l, LANES)

    n_main = l // ATTN_BLOCK * ATTN_BLOCK
    o_fx = o_buf
    if n_main:
        o_fx = _fox_attn(qkvb, 0, n_main, ATTN_BLOCK, ATTN_KEYS, into=o_fx)
    if l > n_main:
        tq_tail = min(-(-max(l_real - n_main, 1) // BF16_ROWS) * BF16_ROWS, l - n_main)
        tk_tail = ATTN_KEYS if n_main % ATTN_KEYS == 0 else ATTN_BLOCK
        o_fx = _fox_attn(qkvb, n_main, tq_tail, tq_tail, tk_tail, out_rows=l - n_main, into=o_fx)

    d_ff = ffn_w_out.shape[1]
    t3 = _pick(l, (544, 384, 128))
    h = _mix_ffn(h, y, bonus, g, o_fx, og, rw_gn_w.reshape(1, WIDTH), rw_gn_b.reshape(1, WIDTH),
                 ones512, w_o, norm2_g, ffn_w_in, ffn_conv_w, ffn_conv_b, ffn_w_out, layer, t3, d_ff)
    return h, o_fx


def kernel(x, meta, norm1_g, w_in, rw_mu, rw_w0, rw_w_up, rw_a0, rw_a_up, rw_g_up, rw_k_k, rw_k_a,
           rw_r_k, rw_gn_w, rw_gn_b, fx_b_f, fx_q_g, fx_k_g, w_o, norm2_g, ffn_w_in, ffn_conv_w,
           ffn_conv_b, ffn_w_out):
    b, seq, d = x.shape
    w_all = jnp.concatenate([_pad_cols(w_in[..., :RW_COLS], RW_PCOLS),
                             _pad_cols(w_in[..., RW_COLS:], FX_PCOLS)], axis=-1).astype(BF16)
    w_o, ffn_w_in, ffn_w_out = (w.astype(BF16) for w in (w_o, ffn_w_in, ffn_w_out))
    params = (norm1_g, rw_mu, rw_w0, rw_w_up, rw_a0, rw_a_up, rw_g_up, rw_k_k, rw_k_a, rw_r_k,
              rw_gn_w, rw_gn_b, fx_b_f, fx_q_g, fx_k_g, norm2_g, ffn_conv_w, ffn_conv_b)
    stacks = (w_all, w_o, ffn_w_in, ffn_w_out)
    l = N_META + seq
    lp = -(-l // SB) * SB
    meta_b = jnp.broadcast_to(meta[None].astype(x.dtype), (b, N_META, d))
    h = jnp.concatenate([meta_b, x, jnp.zeros((b, lp - l, d), x.dtype)], axis=1)
    o_buf = jnp.zeros((b, lp, WIDTH), F32)
    for layer in range(norm1_g.shape[0]):
        h, o_buf = _layer(h, tuple(p[layer] for p in params), stacks, layer, l, o_buf)
    return h[:, N_META:l]
```

```python
import functools

import jax
import jax.numpy as jnp
from jax import lax
from jax.experimental import pallas as pl
from jax.experimental.pallas import tpu as pltpu

F32 = jnp.float32
BF16 = jnp.bfloat16

LANES = 128
SUBLANES = 8
BF16_ROWS = 16
MXU_TILE = 256

N_META = 16
HEAD_DIM = 64
N_HEADS = 8
WIDTH = N_HEADS * HEAD_DIM
DECAY_LORA = 64
AAA_LORA = 64
LORA_WA = DECAY_LORA + AAA_LORA
GATE_LORA = 160
GATE_PAD = 256
RW_COLS = 3 * WIDTH + LORA_WA + GATE_LORA
RW_PCOLS = 3 * WIDTH + LORA_WA + GATE_PAD
FX_PCOLS = 4 * WIDTH + LANES
CONV_W = 3
NORM_EPS = 1e-6
GN_EPS = HEAD_DIM * 1e-5
PAIRS = WIDTH // LANES
CHUNK = 16
SB = 128
ATTN_BLOCK = 256
ATTN_KEYS = 512
BIAS_LANES = 3
LOG2E = 1.4426950408889634
MASK_VALUE = -1e30
VMEM_LIMIT = 58 * 1024 * 1024


def _pick(n, cands):
    for c in cands:
        if n % c == 0:
            return c
    raise ValueError(f"no tile for {n} in {cands}")


def _cparams(sem):
    return pltpu.CompilerParams(dimension_semantics=sem, vmem_limit_bytes=VMEM_LIMIT)


def _sigmoid(x):
    return 1.0 / (1.0 + jnp.exp(-x))


def _softplus(x):
    return jnp.maximum(x, 0.0) + jnp.log(1.0 + jnp.exp(-jnp.abs(x)))


def _split2(x):
    hi = x.astype(BF16)
    lo = (x - hi.astype(F32)).astype(BF16)
    return hi, lo


def _split3(x):
    p1 = x.astype(BF16)
    rem = x - p1.astype(F32)
    p2 = rem.astype(BF16)
    p3 = (rem - p2.astype(F32)).astype(BF16)
    return p1, p2, p3


def _dot_ones(x, ones_bf16):
    xb = x.astype(BF16)
    t = ones_bf16.shape[0]
    return jnp.concatenate([jnp.dot(xb[:, c:c + t], ones_bf16, preferred_element_type=F32)
                            for c in range(0, x.shape[1], t)], axis=1)


def _mm3(a_hi, a_lo, b_hi, b_lo, dims=((1,), (0,))):
    dn = (dims, ((), ()))
    m_axis = 1 - dims[0][0]
    m = a_hi.shape[m_axis]
    d = lax.dot_general(jnp.concatenate([a_hi, a_lo], axis=m_axis), b_hi, dn,
                        preferred_element_type=F32)
    return d[:m] + d[m:] + lax.dot_general(a_hi, b_lo, dn, preferred_element_type=F32)


def _dot_bf16(a, b):
    return jnp.dot(a.astype(BF16), b.astype(BF16), preferred_element_type=F32)


def _group_ones(n, group):
    i = jnp.arange(n) // group
    return (i[:, None] == i[None, :]).astype(BF16)


def _project(xn, w_ref, p_ref, n_chunk=512):
    n = p_ref.shape[-1]
    for c in range(0, n, n_chunk):
        e = min(c + n_chunk, n)
        p_ref[:, c:e] = jnp.dot(xn, w_ref[:, c:e], preferred_element_type=F32)


def _rwkv_features(p_ref, mu_ref, wcomb_ref, w0a0_ref, gup_ref, kk_ref, ka_ref, rk_ref, ones_ref,
                   r_o, lw_o, k_o, v_o, a_o, b_o, g_o, bonus_o, carry_ref):
    t2 = p_ref.shape[0]

    def shifted(c0, c1):
        p = p_ref[:, c0:c1]
        rows = lax.broadcasted_iota(jnp.int32, p.shape, 0)
        prev = jnp.where(rows == 0, carry_ref[:, c0:c1], pltpu.roll(p, 1, axis=0))
        carry_ref[:, c0:c1] = p[t2 - 1:t2, :]
        return p + mu_ref[:, c0:c1] * (prev - p)

    r = shifted(0, WIDTH)
    k = shifted(WIDTH, 2 * WIDTH)
    v = shifted(2 * WIDTH, 3 * WIDTH)
    wa = shifted(3 * WIDTH, 3 * WIDTH + LORA_WA)
    gd = shifted(3 * WIDTH + LORA_WA, RW_PCOLS)

    lane = lax.broadcasted_iota(jnp.int32, wa.shape, 1)
    xa = jnp.where(lane < DECAY_LORA, jnp.tanh(wa), wa)
    pre = _dot_bf16(xa, wcomb_ref[...]) + w0a0_ref[...]
    w_log = -_softplus(-pre[:, :WIDTH]) - 0.5
    log_decay = -jnp.exp(w_log)
    a = _sigmoid(pre[:, WIDTH:])
    g = _dot_bf16(_sigmoid(gd), gup_ref[...])

    ones = ones_ref[...]
    kk = k * kk_ref[...]
    ss = _dot_ones(kk * kk, ones)
    kk = kk / jnp.maximum(jnp.sqrt(ss), 1e-12)
    k2 = k * (1.0 + (a - 1.0) * ka_ref[...])
    bonus = _dot_ones(r * k2 * rk_ref[...], ones) * v

    for j in range(PAIRS):
        ls = slice(j * LANES, (j + 1) * LANES)
        r_o[j] = r[:, ls]
        lw_o[j] = log_decay[:, ls]
        k_o[j] = k2[:, ls]
        v_o[j] = v[:, ls].astype(v_o.dtype)
        a_o[j] = -kk[:, ls]
        b_o[j] = (kk * a)[:, ls]
    g_o[...] = g
    bonus_o[...] = bonus


def _fox_features(p_ref, gain_ref, bf_ref, ones_ref, place_ref, og_o, qkvb_o, carry_ref):
    t2 = p_ref.shape[0]
    ones = ones_ref[...]
    for h0 in (0, WIDTH):
        x = p_ref[:, h0:h0 + WIDTH]
        ms = _dot_ones(x * x, ones) * (1.0 / HEAD_DIM)
        qkvb_o[:, h0:h0 + WIDTH] = (x * lax.rsqrt(ms + NORM_EPS) * gain_ref[:, h0:h0 + WIDTH]).astype(BF16)
    qkvb_o[:, 2 * WIDTH:3 * WIDTH] = p_ref[:, 2 * WIDTH:3 * WIDTH].astype(BF16)
    og_o[...] = p_ref[:, 3 * WIDTH:4 * WIDTH]

    logf = -_softplus(-(p_ref[:, 4 * WIDTH:] + bf_ref[...]))
    ri = lax.broadcasted_iota(jnp.int32, (LANES, LANES), 0)
    ci = lax.broadcasted_iota(jnp.int32, (LANES, LANES), 1)
    tri = (ci <= ri).astype(BF16)
    carry = carry_ref[...]
    blocks = []
    for r0 in range(0, t2, LANES):
        n = min(LANES, t2 - r0)
        cb = sum(jnp.dot(tri[:n, :n], piece, preferred_element_type=F32)
                 for piece in _split3(logf[r0:r0 + n])) + carry
        carry = cb[n - 1:n, :]
        blocks.append(cb)
    c = jnp.concatenate(blocks, axis=0)
    carry_ref[...] = carry
    pieces = jnp.concatenate(_split3(c * (-LOG2E)), axis=1)
    qkvb_o[:, 3 * WIDTH:] = jnp.dot(pieces, place_ref[...], preferred_element_type=F32).astype(BF16)


def _proj_prep_body(h_ref, n1_ref, w_ref, mu_ref, wcomb_ref, w0a0_ref, gup_ref, kk_ref,
                    ka_ref, rk_ref, gain_ref, bf_ref, ones_ref, place_ref,
                    r_o, lw_o, k_o, v_o, a_o, b_o, g_o, bonus_o, og_o, qkvb_o,
                    p_ref, shift_ref, csum_ref):
    prw_ref = p_ref.at[:, 0:RW_PCOLS]
    pfx_ref = p_ref.at[:, RW_PCOLS:RW_PCOLS + FX_PCOLS]
    @pl.when(pl.program_id(1) == 0)
    def _():
        shift_ref[...] = jnp.zeros_like(shift_ref)
        csum_ref[...] = jnp.zeros_like(csum_ref)

    x = h_ref[...]
    ms = jnp.mean(x * x, axis=-1, keepdims=True)
    xn = (x * lax.rsqrt(ms + NORM_EPS) * n1_ref[...]).astype(BF16)
    _project(xn, w_ref, p_ref)
    _rwkv_features(prw_ref, mu_ref, wcomb_ref, w0a0_ref, gup_ref, kk_ref, ka_ref, rk_ref, ones_ref,
                   r_o, lw_o, k_o, v_o, a_o, b_o, g_o, bonus_o, shift_ref)
    _fox_features(pfx_ref, gain_ref, bf_ref, ones_ref, place_ref, og_o, qkvb_o, csum_ref)


def _bias_placement():
    src = jnp.arange(3 * LANES)
    piece, head = src // LANES, src % LANES
    dst = (head // 2) * LANES + BIAS_LANES * (head % 2) + piece
    hit = (dst[:, None] == jnp.arange(WIDTH)[None, :]) & (head[:, None] < N_HEADS)
    return hit.astype(BF16)


def _proj_prep(h, norm1_g, w_all, layer, mu_p, wcomb, w0a0, gup_p, k_k, k_a, r_k, gain, bf_p, ones512, tm):
    b, l, d = h.shape
    once = dict(pipeline_mode=pl.Buffered(1))
    full = lambda shape: pl.BlockSpec(shape, lambda bi, ti: (0,) * len(shape), **once)
    w_spec = pl.BlockSpec((None,) + w_all.shape[1:], lambda bi, ti: (layer, 0, 0), **once)
    pair = lambda dt: (jax.ShapeDtypeStruct((b, PAIRS, l, LANES), dt),
                       pl.BlockSpec((None, PAIRS, tm, LANES), lambda bi, ti: (bi, 0, ti, 0)))
    rows = lambda n, dt: (jax.ShapeDtypeStruct((b, l, n), dt),
                          pl.BlockSpec((None, tm, n), lambda bi, ti: (bi, ti, 0)))
    outs = [pair(F32), pair(F32), pair(F32), pair(BF16), pair(F32), pair(F32),
            rows(WIDTH, F32), rows(WIDTH, F32), rows(WIDTH, F32), rows(4 * WIDTH, BF16)]
    consts = (mu_p, wcomb, w0a0, gup_p, k_k, k_a, r_k, gain, bf_p, ones512, _bias_placement())
    return pl.pallas_call(
        _proj_prep_body,
        out_shape=[o[0] for o in outs],
        grid=(b, l // tm),
        in_specs=[pl.BlockSpec((None, tm, d), lambda bi, ti: (bi, ti, 0)), full((1, d)), w_spec]
                 + [full(c.shape) for c in consts],
        out_specs=[o[1] for o in outs],
        scratch_shapes=[pltpu.VMEM((tm, RW_PCOLS + FX_PCOLS), F32),
                        pltpu.VMEM((1, RW_PCOLS), F32), pltpu.VMEM((1, LANES), F32)],
        compiler_params=_cparams(("arbitrary", "arbitrary")),
        name="proj_prep",
    )(h, norm1_g.reshape(1, d), w_all, *consts)


def _chunk_masks():
    ti = jnp.arange(SB)[:, None]
    si = jnp.arange(SB)[None, :]
    same = (ti // CHUNK) == (si // CHUNK)
    masks = [same, same & (si <= ti), same & (si < ti), ti == si, (ti < HEAD_DIM) == (si < HEAD_DIM)]
    return jnp.stack(masks).astype(F32)


def _rwkv_chunk_body(r_ref, lw_ref, k_ref, v_ref, a_ref, b_ref, masks_ref, y_ref, h_ref):
    npg = r_ref.shape[0]
    n_chunks = SB // CHUNK

    @pl.when(pl.program_id(1) == 0)
    def _():
        h_ref[...] = jnp.zeros_like(h_ref)

    same, incl, strict, _, same_head = (masks_ref[i] != 0.0 for i in range(5))
    eye_f = masks_ref[3]
    head0 = lax.broadcasted_iota(jnp.int32, (SB, LANES), 1) < HEAD_DIM
    tri = masks_ref[1].astype(BF16)
    bf = lambda z: z.astype(BF16)
    dot = functools.partial(jnp.dot, preferred_element_type=F32)

    def keep(mask, x):
        return jnp.where(mask, x, jnp.zeros_like(x))

    pairs = range(npg)
    units = [(p, h) for p in pairs for h in range(2)]
    cat0 = lambda *xs: jnp.concatenate(xs, axis=0)
    mm = lambda a, b, **kw: _mm3(*a, *b, **kw)

    lw = [lw_ref[p] for p in pairs]
    cs = [sum(dot(tri, piece) for piece in _split2(lw[p])) for p in pairs]
    ce = [cat0(*[jnp.broadcast_to(cs[p][(c + 1) * CHUNK - 1:(c + 1) * CHUNK, :], (CHUNK, LANES))
                 for c in range(n_chunks)]) for p in pairs]
    g = [jnp.exp(cs[p]) for p in pairs]
    ginv = [jnp.exp(-cs[p]) for p in pairs]
    gend = [jnp.exp(ce[p] - cs[p]) for p in pairs]
    g_c = [jnp.exp(ce[p]) for p in pairs]
    at = [a_ref[p] * jnp.exp(cs[p] - lw[p]) for p in pairs]
    rt = [r_ref[p] * g[p] for p in pairs]
    v_b = [bf(v_ref[p]) for p in pairs]
    at_b = [bf(at[p]) for p in pairs]
    rt_b = [bf(rt[p]) for p in pairs]
    bk_t = [cat0(bf(b_ref[p] * ginv[p]), bf(k_ref[p] * ginv[p])) for p in pairs]
    bh_b = [bf(b_ref[p] * gend[p]) for p in pairs]
    kh_b = [bf(k_ref[p] * gend[p]) for p in pairs]

    gram = [lax.dot_general(cat0(keep(head0, at_b[p]), keep(head0, rt_b[p]),
                                 keep(~head0, at_b[p]), keep(~head0, rt_b[p])), bk_t[p],
                            (((1,), (1,)), ((), ())), preferred_element_type=F32)
            for p in pairs]
    gb = {(p, h): gram[p][2 * SB * h:2 * SB * (h + 1)] for p, h in units}
    mab = {u: keep(strict, gb[u][:SB, :SB]) for u in units}
    mkk_b = {u: bf(cat0(keep(strict, gb[u][:SB, SB:]), keep(incl, gb[u][SB:, SB:]))) for u in units}
    mrb_b = {u: bf(keep(incl, gb[u][SB:, :SB])) for u in units}

    def fold(z):
        return sum(z[c * CHUNK:(c + 1) * CHUNK] for c in range(n_chunks))

    def unfold(zf):
        return keep(same, jnp.concatenate([zf] * n_chunks, axis=0))

    nf = {u: fold(mab[u]) for u in units}
    tf = {u: fold(eye_f) + nf[u] for u in units}
    full = {u: _split2(mab[u]) for u in units}
    nf = {u: mm(_split2(nf[u]), full[u]) for u in units}
    for k in (2, 4, 8):
        full = {u: _split2(unfold(nf[u])) for u in units}
        if k < 8:
            both_f = {u: mm(_split2(cat0(nf[u], tf[u])), full[u]) for u in units}
            nf = {u: both_f[u][:CHUNK] for u in units}
            tf = {u: tf[u] + both_f[u][CHUNK:] for u in units}
        else:
            tf = {u: tf[u] + mm(_split2(tf[u]), full[u]) for u in units}
    t = {u: unfold(tf[u]) for u in units}
    wk = {u: dot(mkk_b[u], v_b[u[0]]) for u in units}
    x = {u: dot(bf(t[u]), jnp.concatenate([at_b[u[0]], bf(wk[u][:SB])], axis=1)) for u in units}
    ry = {u: dot(mrb_b[u], bf(x[u])) for u in units}
    both = lambda f: [jnp.where(head0, f((p, 0)), f((p, 1))) for p in pairs]
    abar_b = [bf(z) for z in both(lambda u: x[u][:, :LANES])]
    vbar = both(lambda u: x[u][:, LANES:])
    rbar_b = [bf(z) for z in both(lambda u: rt[u[0]] + ry[u][:, :LANES])]
    ybar = both(lambda u: wk[u][SB:] + ry[u][:, LANES:])

    g_ct = [g_c[p].T for p in pairs]

    hbd = [h_ref[p] for p in pairs]
    for c in range(n_chunks):
        sl = slice(c * CHUNK, (c + 1) * CHUNK)
        yu = [dot(cat0(rbar_b[p][sl], abar_b[p][sl]), bf(hbd[p])) for p in pairs]
        for p in pairs:
            y_ref[p, sl, :] = yu[p][:CHUNK] + ybar[p][sl]
        upd = [lax.dot_general(cat0(bh_b[p][sl], kh_b[p][sl]),
                               cat0(bf(yu[p][CHUNK:] + vbar[p][sl]), v_b[p][sl]),
                               (((0,), (0,)), ((), ())), preferred_element_type=F32)
               for p in pairs]
        hbd = [jnp.broadcast_to(g_ct[p][:, c * CHUNK:c * CHUNK + 1], (LANES, LANES)) * hbd[p]
               + keep(same_head, upd[p]) for p in pairs]
    for p in pairs:
        h_ref[p] = hbd[p]


def _rwkv_chunk(r, lw, k, v, a, b, npg):
    n_pairs, l, _ = r.shape
    blk = pl.BlockSpec((npg, SB, LANES), lambda gi, ti: (gi, ti, 0))
    return pl.pallas_call(
        _rwkv_chunk_body,
        out_shape=jax.ShapeDtypeStruct((n_pairs, l, LANES), F32),
        grid=(n_pairs // npg, l // SB),
        in_specs=[blk] * 6 + [pl.BlockSpec((5, SB, SB), lambda gi, ti: (0, 0, 0))],
        out_specs=blk,
        scratch_shapes=[pltpu.VMEM((npg, LANES, LANES), F32)],
        compiler_params=_cparams(("arbitrary", "arbitrary")),
        name="rwkv_chunk",
    )(r, lw, k, v, a, b, _chunk_masks())


def _fox_attn_body(q_ref, k_ref, v_ref, kb_ref, *rest, q_row0, tk):
    o_ref, m_ref, acc_ref = rest[-3:]
    tq = q_ref.shape[0]
    row_first = q_row0 + pl.program_id(1) * tq

    m_ref[...] = jnp.full_like(m_ref, MASK_VALUE)
    acc_ref[...] = jnp.zeros_like(acc_ref)

    lane = lax.broadcasted_iota(jnp.int32, (1, LANES), 1)
    head0 = lane < HEAD_DIM
    row2 = lax.broadcasted_iota(jnp.int32, (2 * tq, LANES), 0)
    lane2 = lax.broadcasted_iota(jnp.int32, (2 * tq, LANES), 1)
    first_lane = jnp.where(row2 < tq, 0, BIAS_LANES)
    bias_on = (lane2 >= first_lane) & (lane2 < first_lane + BIAS_LANES)
    bias_q = jnp.where(bias_on, 1.0, 0.0).astype(BF16)

    def block(col_first, width, masked):
        rows = pl.ds(col_first, width)
        heads = range(N_HEADS)
        lanes = [slice(j * LANES, (j + 1) * LANES) for j in range(PAIRS)]
        s2 = []
        for ls in lanes:
            q = q_ref[:, ls]
            zero = jnp.zeros_like(q)
            q2 = jnp.concatenate([jnp.where(head0, q, zero), jnp.where(head0, zero, q)], axis=0)
            s2.append(lax.dot_general(jnp.concatenate([q2, bias_q], axis=1),
                                      jnp.concatenate([k_ref[rows, ls], kb_ref[rows, ls]], axis=1),
                                      (((1,), (1,)), ((), ())), preferred_element_type=F32))
        s = [s2[h // 2][(h % 2) * tq:(h % 2 + 1) * tq] for h in heads]
        if masked:
            causal = (lax.broadcasted_iota(jnp.int32, (tq, width), 1)
                      <= lax.broadcasted_iota(jnp.int32, (tq, width), 0))
            s = [jnp.where(causal, sh, MASK_VALUE) for sh in s]
        m_prev = [m_ref[h] for h in heads]
        def row_max(sh):
            tile_max = functools.reduce(jnp.maximum, [sh[:, c:c + LANES] for c in range(0, width, LANES)])
            return jnp.max(tile_max, axis=1, keepdims=True)
        m_new = [jnp.maximum(m_prev[h], row_max(s[h])) for h in heads]
        p = [jnp.exp2(s[h] - jnp.concatenate([m_new[h]] * (width // LANES), axis=1)).astype(BF16)
             for h in heads]
        v1 = []
        for ls in lanes:
            v = v_ref[rows, ls]
            one = jnp.ones_like(v)
            v1 += [jnp.where(head0, v, one), jnp.where(head0, one, v)]
        pv = [jnp.dot(p[h], v1[h], preferred_element_type=F32) for h in heads]
        for h in heads:
            acc_ref[h] = jnp.exp2(m_prev[h] - m_new[h]) * acc_ref[h] + pv[h]
            m_ref[h] = m_new[h]

    def full_block(kj, carry):
        block(pl.multiple_of(kj * tk, tk), tk, False)
        return carry

    n_full = row_first // tk
    lax.fori_loop(0, n_full, full_block, 0)
    if tk > tq >= LANES:
        def rest_block(i, carry):
            block(pl.multiple_of(n_full * tk + i * tq, tq), tq, False)
            return carry
        lax.fori_loop(0, (row_first - n_full * tk) // tq, rest_block, 0)
    diag_w = max(tq, LANES)
    block(pl.multiple_of(row_first, tq), diag_w, True)

    if o_ref.shape[0] > tq:
        o_ref[...] = jnp.zeros_like(o_ref)
    for j in range(PAIRS):
        a0, a1 = acc_ref[2 * j], acc_ref[2 * j + 1]
        o_ref[0:tq, j * LANES:(j + 1) * LANES] = jnp.where(
            head0, a0 / pltpu.roll(a0, HEAD_DIM, axis=1), a1 / pltpu.roll(a1, HEAD_DIM, axis=1))


def _fox_attn(qkvb, q_row0, n_rows, tq, tk, out_rows=None, into=None):
    b, l, _ = qkvb.shape
    out_rows = out_rows or tq
    assert q_row0 % tq == 0 and q_row0 % out_rows == 0
    assert tk % tq == 0 if tq >= LANES else (q_row0 % tk == 0 and n_rows == tq)
    assert q_row0 + n_rows - tq + max(tq, LANES) <= l
    qb0, ob0 = q_row0 // tq, q_row0 // out_rows
    kv_spec = lambda col: pl.BlockSpec((None, l, WIDTH), lambda bi, qi: (bi, 0, col))
    extra_in, extra_spec, alias = (), [], {}
    if into is not None:
        extra_in, extra_spec, alias = (into,), [pl.BlockSpec(memory_space=pl.ANY)], {4: 0}
    return pl.pallas_call(
        functools.partial(_fox_attn_body, q_row0=q_row0, tk=tk),
        out_shape=jax.ShapeDtypeStruct((b, l, WIDTH), F32),
        grid=(b, n_rows // tq),
        in_specs=[pl.BlockSpec((None, tq, WIDTH), lambda bi, qi: (bi, qb0 + qi, 0)),
                  kv_spec(1), kv_spec(2), kv_spec(3)] + extra_spec,
        out_specs=pl.BlockSpec((None, out_rows, WIDTH), lambda bi, qi: (bi, ob0 + qi, 0)),
        scratch_shapes=[pltpu.VMEM((N_HEADS, tq, LANES), F32), pltpu.VMEM((N_HEADS, tq, LANES), F32)],
        input_output_aliases=alias,
        compiler_params=_cparams(("arbitrary", "arbitrary")),
        name="fox_attn",
    )(qkvb, qkvb, qkvb, qkvb, *extra_in)


def _mixed_residual(h_ref, y_ref, bonus_ref, g_ref, o_ref, og_ref, gnw_ref, gnb_ref, ones_ref, wo_ref):
    ones = ones_ref[...]
    y = jnp.concatenate([y_ref[j] for j in range(PAIRS)], axis=1)
    mean = _dot_ones(y, ones) * (1.0 / HEAD_DIM)
    d = y - mean
    var = _dot_ones(d * d, ones) * (1.0 / HEAD_DIM)
    yn = d * lax.rsqrt(var + GN_EPS) * gnw_ref[...] + gnb_ref[...]
    y_rw = ((yn + bonus_ref[...]) * g_ref[...]).astype(BF16)
    y_fx = (o_ref[...] * _sigmoid(og_ref[...])).astype(BF16)
    return (h_ref[...]
            + jnp.dot(y_rw, wo_ref[0:WIDTH, :], preferred_element_type=F32)
            + jnp.dot(y_fx, wo_ref[WIDTH:2 * WIDTH, :], preferred_element_type=F32))


def _mix_ffn_body(h_ref, y_ref, bonus_ref, gate_ref, o_ref, og_ref, gnw_ref, gnb_ref, ones_ref, wo_ref,
                  g_ref, wu_ref, wg_ref, cwu_ref, cwg_ref, cbu_ref, cbg_ref, wout_ref,
                  out_ref, xn_ref, cu_ref, cg_ref, wu_s, wg_s, act_ref):
    tm = h_ref.shape[0]
    halo = wu_s.shape[0] - tm
    ti = pl.program_id(1)
    j = pl.program_id(2)

    def first_column_tile():
        x = _mixed_residual(h_ref, y_ref, bonus_ref, gate_ref, o_ref, og_ref, gnw_ref, gnb_ref,
                            ones_ref, wo_ref)
        ms = jnp.mean(x * x, axis=-1, keepdims=True)
        xn_ref[...] = (x * lax.rsqrt(ms + NORM_EPS) * g_ref[...]).astype(BF16)
        out_ref[...] = x

    if cu_ref.shape[0] == 1:
        first_column_tile()
    else:
        pl.when(j == 0)(first_column_tile)

    @pl.when(ti == 0)
    def _():
        cu_ref[j] = jnp.zeros(cu_ref.shape[1:], F32)
        cg_ref[j] = jnp.zeros(cg_ref.shape[1:], F32)

    xn = xn_ref[...]

    def project(w_ref, carry_ref, work):
        work[0:halo] = carry_ref[j]
        work[halo:] = jnp.dot(xn, w_ref[...], preferred_element_type=F32)
        carry_ref[j] = work[tm:]

    def conv(cs, cw_ref, cb_ref, work):
        cw = cw_ref[:, cs]
        taps = [work[pl.ds(halo - (CONV_W - 1) + i, tm), cs] for i in range(CONV_W)]
        return cb_ref[:, cs] + cw[0:1, :] * taps[0] + cw[1:2, :] * taps[1] + cw[2:3, :] * taps[2]

    project(wu_ref, cu_ref, wu_s)
    project(wg_ref, cg_ref, wg_s)
    tf = wu_ref.shape[1]
    for c0 in range(0, tf, 2 * MXU_TILE):
        cs = slice(c0, min(c0 + 2 * MXU_TILE, tf))
        gt = conv(cs, cwg_ref, cbg_ref, wg_s)
        act_ref[:, cs] = (gt * _sigmoid(gt) * conv(cs, cwu_ref, cbu_ref, wu_s)).astype(BF16)
    out_ref[...] += jnp.dot(act_ref[...], wout_ref[...], preferred_element_type=F32)


def _mix_ffn(h, y, bonus, gate, o_fx, og, gn_w, gn_b, ones512, wo_bf16,
             g, w_in_bf16, conv_w, conv_b, w_out_bf16, layer, tm, tf):
    b, l, d = h.shape
    d_ff = w_out_bf16.shape[1]
    nj = d_ff // tf
    cw = jnp.pad(conv_w, ((0, SUBLANES - CONV_W), (0, 0)))
    cb = conv_b.reshape(1, 2 * d_ff)
    once = dict(pipeline_mode=pl.Buffered(1))
    full = lambda shape: pl.BlockSpec(shape, lambda bi, ti, j: (0,) * len(shape), **once)
    wmode = once if nj == 1 else {}
    rows = pl.BlockSpec((None, tm, WIDTH), lambda bi, ti, j: (bi, ti, 0))
    return pl.pallas_call(
        _mix_ffn_body,
        out_shape=jax.ShapeDtypeStruct((b, l, d), F32),
        grid=(b, l // tm, nj),
        in_specs=[pl.BlockSpec((None, tm, d), lambda bi, ti, j: (bi, ti, 0)),
                  pl.BlockSpec((None, PAIRS, tm, LANES), lambda bi, ti, j: (bi, 0, ti, 0)),
                  rows, rows, rows, rows,
                  full((1, WIDTH)), full((1, WIDTH)), full(ones512.shape),
                  pl.BlockSpec((None, 2 * WIDTH, d), lambda bi, ti, j: (layer, 0, 0), **once),
                  full((1, d)),
                  pl.BlockSpec((None, d, tf), lambda bi, ti, j: (layer, 0, j), **wmode),
                  pl.BlockSpec((None, d, tf), lambda bi, ti, j: (layer, 0, nj + j), **wmode),
                  pl.BlockSpec((SUBLANES, tf), lambda bi, ti, j: (0, j), **wmode),
                  pl.BlockSpec((SUBLANES, tf), lambda bi, ti, j: (0, nj + j), **wmode),
                  pl.BlockSpec((1, tf), lambda bi, ti, j: (0, j), **wmode),
                  pl.BlockSpec((1, tf), lambda bi, ti, j: (0, nj + j), **wmode),
                  pl.BlockSpec((None, tf, d), lambda bi, ti, j: (layer, j, 0), **wmode)],
        out_specs=pl.BlockSpec((None, tm, d), lambda bi, ti, j: (bi, ti, 0)),
        scratch_shapes=[pltpu.VMEM((tm, d), BF16),
                        pltpu.VMEM((nj, SUBLANES, tf), F32), pltpu.VMEM((nj, SUBLANES, tf), F32),
                        pltpu.VMEM((SUBLANES + tm, tf), F32), pltpu.VMEM((SUBLANES + tm, tf), F32),
                        pltpu.VMEM((tm, tf), BF16)],
        compiler_params=_cparams(("arbitrary", "arbitrary", "arbitrary")),
        name="mix_ffn",
    )(h, y, bonus, gate, o_fx, og, gn_w, gn_b, ones512, wo_bf16,
      g.reshape(1, d), w_in_bf16, w_in_bf16, cw, cw, cb, cb, w_out_bf16)


def _pad_cols(w, n):
    return jnp.pad(w, ((0, 0),) * (w.ndim - 1) + ((0, n - w.shape[-1]),))


def _layer(h, prm, stacks, layer, l_real, o_buf):
    (norm1_g, rw_mu, rw_w0, rw_w_up, rw_a0, rw_a_up, rw_g_up, rw_k_k, rw_k_a, rw_r_k,
     rw_gn_w, rw_gn_b, fx_b_f, fx_q_g, fx_k_g, norm2_g, ffn_conv_w, ffn_conv_b) = prm
    w_all, w_o, ffn_w_in, ffn_w_out = stacks
    b, l, d = h.shape

    mu_p = _pad_cols(rw_mu.reshape(1, RW_COLS), RW_PCOLS)
    wcomb = jnp.zeros((LORA_WA, 2 * WIDTH), F32)
    wcomb = wcomb.at[:DECAY_LORA, :WIDTH].set(rw_w_up).at[DECAY_LORA:, WIDTH:].set(rw_a_up)
    w0a0 = jnp.concatenate([rw_w0, rw_a0]).reshape(1, 2 * WIDTH)
    gup_p = jnp.pad(rw_g_up, ((0, GATE_PAD - GATE_LORA), (0, 0)))
    ones512 = _group_ones(MXU_TILE, HEAD_DIM)
    gain = jnp.concatenate([jnp.tile(fx_q_g, N_HEADS) * (HEAD_DIM ** -0.5 * LOG2E),
                            jnp.tile(fx_k_g, N_HEADS)]).reshape(1, 2 * WIDTH)
    bf_p = jnp.pad(fx_b_f, (0, LANES - N_HEADS)).reshape(1, LANES)

    t1 = _pick(l, (544, 384, 128))
    t2 = _pick(l, (544, 384, 256, 128))

    r, lw, k, v, a, bb, g, bonus, og, qkvb = _proj_prep(
        h, norm1_g, w_all, layer, mu_p, wcomb, w0a0, gup_p, rw_k_k.reshape(1, WIDTH),
        rw_k_a.reshape(1, WIDTH), rw_r_k.reshape(1, WIDTH), gain, bf_p, ones512, t1)

    pm = lambda z: z.reshape(b * PAIRS, l, LANES)
    y = _rwkv_chunk(pm(r), pm(lw), pm(k), pm(v), pm(a), pm(bb), npg=_pick(b * PAIRS, (16, 8, 4)))
    y = y.reshape(b, PAIRS, l, LANES)

    n_main = l // ATTN_BLOCK * ATTN_BLOCK
    o_fx = o_buf
    if n_main:
        o_fx = _fox_attn(qkvb, 0, n_main, ATTN_BLOCK, ATTN_KEYS, into=o_fx)
    if l > n_main:
        tq_tail = min(-(-max(l_real - n_main, 1) // BF16_ROWS) * BF16_ROWS, l - n_main)
        tk_tail = ATTN_KEYS if n_main % ATTN_KEYS == 0 else ATTN_BLOCK
        o_fx = _fox_attn(qkvb, n_main, tq_tail, tq_tail, tk_tail, out_rows=l - n_main, into=o_fx)

    d_ff = ffn_w_out.shape[1]
    t3 = _pick(l, (544, 384, 128))
    h = _mix_ffn(h, y, bonus, g, o_fx, og, rw_gn_w.reshape(1, WIDTH), rw_gn_b.reshape(1, WIDTH),
                 ones512, w_o, norm2_g, ffn_w_in, ffn_conv_w, ffn_conv_b, ffn_w_out, layer, t3, d_ff)
    return h, o_fx


def kernel(x, meta, norm1_g, w_in, rw_mu, rw_w0, rw_w_up, rw_a0, rw_a_up, rw_g_up, rw_k_k, rw_k_a,
           rw_r_k, rw_gn_w, rw_gn_b, fx_b_f, fx_q_g, fx_k_g, w_o, norm2_g, ffn_w_in, ffn_conv_w,
           ffn_conv_b, ffn_w_out):
    b, seq, d = x.shape
    w_all = jnp.concatenate([_pad_cols(w_in[..., :RW_COLS], RW_PCOLS),
                             _pad_cols(w_in[..., RW_COLS:], FX_PCOLS)], axis=-1).astype(BF16)
    w_o, ffn_w_in, ffn_w_out = (w.astype(BF16) for w in (w_o, ffn_w_in, ffn_w_out))
    params = (norm1_g, rw_mu, rw_w0, rw_w_up, rw_a0, rw_a_up, rw_g_up, rw_k_k, rw_k_a, rw_r_k,
              rw_gn_w, rw_gn_b, fx_b_f, fx_q_g, fx_k_g, norm2_g, ffn_conv_w, ffn_conv_b)
    stacks = (w_all, w_o, ffn_w_in, ffn_w_out)
    l = N_META + seq
    lp = -(-l // SB) * SB
    meta_b = jnp.broadcast_to(meta[None].astype(x.dtype), (b, N_META, d))
    h = jnp.concatenate([meta_b, x, jnp.zeros((b, lp - l, d), x.dtype)], axis=1)
    o_buf = jnp.zeros((b, lp, WIDTH), F32)
    for layer in range(norm1_g.shape[0]):
        h, o_buf = _layer(h, tuple(p[layer] for p in params), stacks, layer, l, o_buf)
    return h[:, N_META:l]
```
